```python
import math
import jax, jax.numpy as jnp
from jax import lax
import numpy as np

D_MODEL = 1024
BATCH = 4
SEQ = 4096
DEPTH = 1

CHUNK = 64
Q_BLOCK = 128
HEAD_DIM = 64
MIX_WIDTH = D_MODEL
RW_WIDTH = MIX_WIDTH // 2
FX_WIDTH = MIX_WIDTH - RW_WIDTH
RW_HEADS = RW_WIDTH // HEAD_DIM
FX_HEADS = FX_WIDTH // HEAD_DIM
DECAY_LORA = 64
AAA_LORA = 64
GATE_LORA = 160
D_FF = 4 * D_MODEL
N_MOD = 6
NORM_EPS = 1e-6
GN_EPS = 64e-5
NEG_INF = -1e30

RW_SPLITS = (RW_WIDTH, RW_WIDTH, RW_WIDTH, DECAY_LORA, AAA_LORA, GATE_LORA)
FX_SPLITS = (FX_WIDTH, FX_WIDTH, FX_WIDTH, FX_WIDTH, FX_HEADS)
RW_COLS = sum(RW_SPLITS)
FX_COLS = sum(FX_SPLITS)
IN_COLS = RW_COLS + FX_COLS

kernel_name = "hymba_rwkv7_fox_adaln_block"


def _split(p, sizes):
    idx = np.cumsum(np.array(sizes[:-1])).tolist()
    return jnp.split(p, idx, axis=-1)


def rmsnorm(x, g, eps=NORM_EPS):
    xf = x.astype(jnp.float32)
    y = xf * lax.rsqrt(jnp.mean(xf * xf, axis=-1, keepdims=True) + eps)
    return (y * g.astype(jnp.float32)).astype(x.dtype)


def head_rmsnorm(x, g, n_heads, eps=NORM_EPS):
    sh = x.shape
    xh = x.reshape(sh[:-1] + (n_heads, HEAD_DIM)).astype(jnp.float32)
    xh = xh * lax.rsqrt(jnp.mean(xh * xh, axis=-1, keepdims=True) + eps)
    return (xh.reshape(sh) * g.astype(jnp.float32)).astype(x.dtype)


def rwkv7_time_mix(p, mu_shift, w0, w_up_decay, a0, w_up_a, w_up_g, k_k, k_a, r_k, gn_g, gn_b):
    B, T, _ = p.shape
    prev = jnp.pad(p, ((0, 0), (1, 0), (0, 0)))[:, :-1]
    p = p + (prev - p) * mu_shift
    r, k, v, wd, ad, gd = _split(p, RW_SPLITS)
    logw = -jax.nn.softplus(-(w0 + jnp.tanh(wd) @ w_up_decay)) - 0.5
    decay = jnp.exp(-jnp.exp(logw.astype(jnp.float32)))
    a = jax.nn.sigmoid(a0 + ad @ w_up_a)
    g = jax.nn.sigmoid(gd) @ w_up_g
    kk = (k * k_k).reshape(B, T, RW_HEADS, HEAD_DIM).astype(jnp.float32)
    kk = kk / jnp.maximum(jnp.linalg.norm(kk, axis=-1, keepdims=True), 1e-12)
    k = k * (1.0 + (a - 1.0) * k_a)

    def heads(t):
        return t.reshape(B, T, RW_HEADS, HEAD_DIM).astype(jnp.float32)

    rh, kh, vh, ah, wh = heads(r), heads(k), heads(v), heads(a), heads(decay)
    seq_first = lambda t: jnp.swapaxes(t, 0, 1)

    def step(S, inp):
        r_t, w_t, k_t, v_t, kk_t, a_t = inp
        sa = jnp.einsum('bhvk,bhk->bhv', S, -kk_t)
        S = (S * w_t[:, :, None, :]
             + sa[..., None] * (kk_t * a_t)[:, :, None, :]
             + v_t[..., None] * k_t[:, :, None, :])
        return S, jnp.einsum('bhvk,bhk->bhv', S, r_t)

    S0 = jnp.zeros((B, RW_HEADS, HEAD_DIM, HEAD_DIM), jnp.float32)
    _, y = lax.scan(step, S0, (seq_first(rh), seq_first(wh), seq_first(kh),
                               seq_first(vh), seq_first(kk), seq_first(ah)))
    y = jnp.swapaxes(y, 0, 1)
    mean = jnp.mean(y, axis=-1, keepdims=True)
    var = jnp.mean(jnp.square(y - mean), axis=-1, keepdims=True)
    yn = ((y - mean) * lax.rsqrt(var + GN_EPS)).reshape(B, T, RW_WIDTH)
    yn = yn * gn_g.astype(jnp.float32) + gn_b.astype(jnp.float32)
    bonus = jnp.sum(rh * kh * r_k.astype(jnp.float32), axis=-1, keepdims=True) * vh
    out = (yn + bonus.reshape(B, T, RW_WIDTH)) * g.astype(jnp.float32)
    return out.astype(p.dtype)


def fox_attention(p, b_f, q_norm_g, k_norm_g, fox_out_g):
    B, T, _ = p.shape
    q, k, v, og, f = _split(p, FX_SPLITS)
    to_heads = lambda t: jnp.transpose(t.reshape(B, T, FX_HEADS, HEAD_DIM), (0, 2, 1, 3))
    qh = to_heads(head_rmsnorm(q, jnp.tile(q_norm_g, FX_HEADS), FX_HEADS))
    kh = to_heads(head_rmsnorm(k, jnp.tile(k_norm_g, FX_HEADS), FX_HEADS))
    vh = to_heads(v)
    logf = jax.nn.log_sigmoid(f.astype(jnp.float32) + b_f.astype(jnp.float32))
    F = jnp.transpose(jnp.cumsum(logf, axis=1), (0, 2, 1))
    scale = 1.0 / math.sqrt(HEAD_DIM)
    outs = []
    for i in range(T // Q_BLOCK):
        qs, klen = i * Q_BLOCK, (i + 1) * Q_BLOCK
        q_blk = qh[:, :, qs:klen]
        s = jnp.einsum('bhqd,bhkd->bhqk', q_blk, kh[:, :, :klen]).astype(jnp.float32) * scale
        s = s + F[:, :, qs:klen, None] - F[:, :, None, :klen]
        qidx = qs + jnp.arange(Q_BLOCK)
        kidx = jnp.arange(klen)
        s = jnp.where(kidx[None, :] <= qidx[:, None], s, NEG_INF)
        pr = jax.nn.softmax(s, axis=-1).astype(vh.dtype)
        outs.append(jnp.einsum('bhqk,bhkd->bhqd', pr, vh[:, :, :klen]))
    o = jnp.concatenate(outs, axis=2)
    o = jnp.transpose(o, (0, 2, 1, 3)).reshape(B, T, FX_WIDTH)
    o = head_rmsnorm(o, fox_out_g, FX_HEADS)
    return o * jax.nn.sigmoid(og)


def setup_inputs(seed: int = 0) -> dict:
    key = jax.random.key(seed)
    ks = jax.random.split(key, 32)
    n = lambda i, shape, s: jax.random.normal(ks[i], shape, jnp.float32) * s
    D = D_MODEL
    return {
        "x": n(0, (BATCH, SEQ, D), 1.0),
        "c": n(1, (BATCH, D), 1.0),
        "w_ada": n(2, (D, N_MOD * D), 0.5 * D ** -0.5),
        "b_ada": n(3, (N_MOD * D,), 0.1),
        "norm1_g": 1.0 + n(4, (D,), 0.05),
        "w_in": n(5, (D, IN_COLS), D ** -0.5),
        "mu_shift": jax.random.uniform(ks[6], (RW_COLS,), jnp.float32),
        "w0": -2.0 + n(7, (RW_WIDTH,), 1.0),
        "w_up_decay": n(8, (DECAY_LORA, RW_WIDTH), 0.3 * DECAY_LORA ** -0.5),
        "a0": n(9, (RW_WIDTH,), 0.1),
        "w_up_a": n(10, (AAA_LORA, RW_WIDTH), 0.3 * AAA_LORA ** -0.5),
        "w_up_g": n(11, (GATE_LORA, RW_WIDTH), GATE_LORA ** -0.5),
        "k_k": 0.85 + n(12, (RW_WIDTH,), 0.05),
        "k_a": 1.0 + n(13, (RW_WIDTH,), 0.05),
        "r_k": n(14, (RW_HEADS, HEAD_DIM), 0.3),
        "gn_g": 1.0 + n(15, (RW_WIDTH,), 0.05),
        "gn_b": n(16, (RW_WIDTH,), 0.02),
        "b_f": 3.0 + n(17, (FX_HEADS,), 0.5),
        "q_norm_g": 1.0 + n(18, (HEAD_DIM,), 0.05),
        "k_norm_g": 1.0 + n(19, (HEAD_DIM,), 0.05),
        "fox_out_g": 1.0 + n(20, (FX_WIDTH,), 0.05),
        "w_o": n(21, (MIX_WIDTH, D), MIX_WIDTH ** -0.5),
        "norm2_g": 1.0 + n(22, (D,), 0.05),
        "w_mlp1": n(23, (D, D_FF), D ** -0.5),
        "w_mlp2": n(24, (D_FF, D), D_FF ** -0.5),
        "final_g": 1.0 + n(25, (D,), 0.05),
    }


def reference(x, c, w_ada, b_ada, norm1_g, w_in, mu_shift, w0, w_up_decay, a0, w_up_a,
              w_up_g, k_k, k_a, r_k, gn_g, gn_b, b_f, q_norm_g, k_norm_g, fox_out_g,
              w_o, norm2_g, w_mlp1, w_mlp2, final_g):
    mod = jax.nn.silu(c) @ w_ada + b_ada
    shift1, scale1, gate1, shift2, scale2, gate2 = [m[:, None, :] for m in jnp.split(mod, N_MOD, axis=-1)]
    h = x
    for _ in range(DEPTH):
        u = rmsnorm(h, norm1_g) * (1.0 + scale1) + shift1
        p = u @ w_in
        p_rw, p_fx = p[..., :RW_COLS], p[..., RW_COLS:]
        y_rw = rwkv7_time_mix(p_rw, mu_shift, w0, w_up_decay, a0, w_up_a, w_up_g,
                              k_k, k_a, r_k, gn_g, gn_b)
        y_fx = fox_attention(p_fx, b_f, q_norm_g, k_norm_g, fox_out_g)
        y = jnp.concatenate([y_rw, y_fx], axis=-1) @ w_o
        h = h + gate1 * y
        u = rmsnorm(h, norm2_g) * (1.0 + scale2) + shift2
        y = jnp.square(jax.nn.relu(u @ w_mlp1)) @ w_mlp2
        h = h + gate2 * y
    return rmsnorm(h, final_g)
```

```python
import functools
import math

import jax
import jax.numpy as jnp
from jax import lax
from jax.experimental import pallas as pl
from jax.experimental.pallas import tpu as pltpu

F32 = jnp.float32
BF16 = jnp.bfloat16

HEAD_DIM = 64
LANES = 128
NORM_EPS = 1e-6
GN_EPS = 64e-5
NEG_INF = -1e30
N_MOD = 6
DECAY_LORA = 64
AAA_LORA = 64
GATE_LORA = 160
RW_CHUNK = 64
VMEM_LIMIT = 56 * 1024 * 1024

_HI = lax.Precision.HIGHEST


def _dot(a, b):
    return jnp.dot(a, b, preferred_element_type=F32)


def _dot_hi(a, b):
    return jnp.dot(a, b, preferred_element_type=F32, precision=_HI)


def _dot_nt(a, b, precision=None):
    return lax.dot_general(a, b, (((1,), (1,)), ((), ())), preferred_element_type=F32,
                           precision=precision)


def _dot_tn(a, b, precision=None):
    return lax.dot_general(a, b, (((0,), (0,)), ((), ())), preferred_element_type=F32,
                           precision=precision)


def _sigmoid(x):
    return 1.0 / (1.0 + jnp.exp(-x))


def _const_spec(shape):
    n = len(shape)
    return pl.BlockSpec(shape, lambda *_: (0,) * n)


def _ada_kernel(c_ref, w_ref, b_ref, o_ref):
    c = c_ref[...]
    o_ref[...] = _dot_hi(c * _sigmoid(c), w_ref[...]) + b_ref[...]


def _ada(c, w_ada, b_ada):
    B, D = c.shape
    n = w_ada.shape[1]
    return pl.pallas_call(
        _ada_kernel,
        out_shape=jax.ShapeDtypeStruct((B, n), F32),
        grid=(n // D,),
        in_specs=[pl.BlockSpec((B, D), lambda j: (0, 0)),
                  pl.BlockSpec((D, D), lambda j: (0, j)),
                  pl.BlockSpec((1, D), lambda j: (0, j))],
        out_specs=pl.BlockSpec((B, D), lambda j: (0, j)),
        name="ada",
    )(c, w_ada, b_ada.reshape(1, n))


def _inproj_kernel(x_ref, mod_ref, g_ref, w_ref, mu_ref, bf_ref,
                   rw_ref, fx_ref, fcum_ref, prev_scr, fcar_scr, *, rw_cols):
    t = pl.program_id(1)
    tm = x_ref.shape[1]

    @pl.when(t == 0)
    def _():
        prev_scr[...] = jnp.zeros_like(prev_scr)
        fcar_scr[...] = jnp.zeros_like(fcar_scr)

    x = x_ref[0]
    shift = mod_ref[0, 0:1, :]
    scale = mod_ref[0, 1:2, :]
    ms = jnp.mean(x * x, axis=-1, keepdims=True)
    u = (x * lax.rsqrt(ms + NORM_EPS) * g_ref[...]) * (1.0 + scale) + shift
    u = u.astype(BF16)

    p = _dot(u, w_ref[:, :rw_cols])
    row = lax.broadcasted_iota(jnp.int32, p.shape, 0)
    prev = jnp.where(row == 0, prev_scr[0:1, :], pltpu.roll(p, shift=1, axis=0))
    prev_scr[0:1, :] = p[tm - 1:tm, :]
    rw_ref[0] = p + (prev - p) * mu_ref[...]

    pf = _dot(u, w_ref[:, rw_cols:])
    fx_ref[0] = pf
    z = pf[:, pf.shape[1] - LANES:] + bf_ref[...]
    logf = jnp.minimum(z, 0.0) - jnp.log(1.0 + jnp.exp(-jnp.abs(z)))
    r2 = lax.broadcasted_iota(jnp.int32, (tm, tm), 0)
    c2 = lax.broadcasted_iota(jnp.int32, (tm, tm), 1)
    tril = (c2 <= r2).astype(F32)
    fcum = _dot_hi(tril, logf) + fcar_scr[0:1, :]
    fcum_ref[0] = fcum
    fcar_scr[0:1, :] = fcum[tm - 1:tm, :]


def _inproj(x, mod3, norm1_g, w_all, mu_p, bf_p, rw_cols, tm=256):
    B, T, D = x.shape
    ncols = w_all.shape[1]
    fx_cols = ncols - rw_cols
    return pl.pallas_call(
        functools.partial(_inproj_kernel, rw_cols=rw_cols),
        out_shape=(jax.ShapeDtypeStruct((B, T, rw_cols), F32),
                   jax.ShapeDtypeStruct((B, T, fx_cols), F32),
                   jax.ShapeDtypeStruct((B, T, LANES), F32)),
        grid=(B, T // tm),
        in_specs=[pl.BlockSpec((1, tm, D), lambda b, t: (b, t, 0)),
                  pl.BlockSpec((1, N_MOD, D), lambda b, t: (b, 0, 0)),
                  _const_spec((1, D)),
                  _const_spec((D, ncols)),
                  _const_spec((1, rw_cols)),
                  _const_spec((1, LANES))],
        out_specs=(pl.BlockSpec((1, tm, rw_cols), lambda b, t: (b, t, 0)),
                   pl.BlockSpec((1, tm, fx_cols), lambda b, t: (b, t, 0)),
                   pl.BlockSpec((1, tm, LANES), lambda b, t: (b, t, 0))),
        scratch_shapes=[pltpu.VMEM((8, rw_cols), F32), pltpu.VMEM((8, LANES), F32)],
        compiler_params=pltpu.CompilerParams(
            dimension_semantics=("arbitrary", "arbitrary"), vmem_limit_bytes=VMEM_LIMIT),
        name="inproj",
    )(x, mod3, norm1_g.reshape(1, D), w_all, mu_p, bf_p)


def _segsum(x, bd):
    hi = x.astype(BF16)
    lo = (x - hi.astype(F32)).astype(BF16)
    return _dot(hi, bd) + _dot(lo, bd)


def _rwkv_kernel(p_ref, w0_ref, wdw_ref, a0_ref, waw_ref, wgw_ref, kk_ref, ka_ref, rk_ref,
                 gng_ref, gnb_ref, o_ref, s_scr, *, width, mm_dtype, mm_prec):
    t = pl.program_id(1)
    L = p_ref.shape[1]
    W = width
    heads = W // HEAD_DIM

    @pl.when(t == 0)
    def _():
        s_scr[...] = jnp.zeros_like(s_scr)

    p = p_ref[0]
    r = p[:, 0:W]
    k = p[:, W:2 * W]
    v = p[:, 2 * W:3 * W]
    o0 = 3 * W
    wd = p[:, o0:o0 + LANES]
    ad = p[:, o0 + LANES:o0 + 2 * LANES]
    gd = p[:, o0 + 2 * LANES:o0 + 4 * LANES]

    z = w0_ref[...] + _dot(jnp.tanh(wd).astype(BF16), wdw_ref[...])
    softplus = jnp.maximum(-z, 0.0) + jnp.log(1.0 + jnp.exp(-jnp.abs(z)))
    ld = -jnp.exp(-softplus - 0.5)
    a_sig = _sigmoid(a0_ref[...] + _dot(ad.astype(BF16), waw_ref[...]))
    g = _dot(_sigmoid(gd).astype(BF16), wgw_ref[...])

    ri = lax.broadcasted_iota(jnp.int32, (W, W), 0)
    ci = lax.broadcasted_iota(jnp.int32, (W, W), 1)
    bd = ((ri // HEAD_DIM) == (ci // HEAD_DIM)).astype(BF16)

    kk = k * kk_ref[...]
    kk = kk / jnp.maximum(jnp.sqrt(_segsum(kk * kk, bd)), 1e-12)
    k2 = k * (1.0 + (a_sig - 1.0) * ka_ref[...])
    b = kk * a_sig
    bonus = _segsum(r * k2 * rk_ref[...], bd) * v

    rl = lax.broadcasted_iota(jnp.int32, (L, L), 0)
    cl = lax.broadcasted_iota(jnp.int32, (L, L), 1)
    incl = cl <= rl
    strict = cl < rl
    cum = _dot_hi(incl.astype(F32), ld)
    tot = cum[L - 1:L, :]
    e_neg = jnp.exp(-cum)
    e_rem = jnp.exp(tot - cum)
    rt = (r * jnp.exp(cum)).astype(mm_dtype)
    at = (-kk * jnp.exp(cum - ld)).astype(mm_dtype)
    bt = (b * e_neg).astype(mm_dtype)
    kt = (k2 * e_neg).astype(mm_dtype)
    bh = (b * e_rem).astype(mm_dtype)
    kh = (k2 * e_rem).astype(mm_dtype)
    vm = v.astype(mm_dtype)
    e_tot = jnp.exp(tot)

    eye = (cl == rl).astype(F32)
    ys = []
    for h in range(heads):
        sl = slice(h * HEAD_DIM, (h + 1) * HEAD_DIM)
        rt_h, at_h, bt_h, kt_h = rt[:, sl], at[:, sl], bt[:, sl], kt[:, sl]
        bh_h, kh_h, v_h = bh[:, sl], kh[:, sl], vm[:, sl]
        s = s_scr[h]
        s_m = s.astype(mm_dtype)
        a_ab = jnp.where(strict, _dot_nt(at_h, bt_h, mm_prec), 0.0)
        a_ak = jnp.where(strict, _dot_nt(at_h, kt_h, mm_prec), 0.0)
        a_rb = jnp.where(incl, _dot_nt(rt_h, bt_h, mm_prec), 0.0)
        a_rk = jnp.where(incl, _dot_nt(rt_h, kt_h, mm_prec), 0.0)
        pw = a_ab
        inv = eye + a_ab
        for _ in range(int(math.log2(L)) - 1):
            pw = _dot_hi(pw, pw)
            inv = inv + _dot_hi(inv, pw)
        rhs = _dot_nt(at_h, s_m, mm_prec) + jnp.dot(
            a_ak.astype(mm_dtype), v_h, preferred_element_type=F32, precision=mm_prec)
        u = _dot_hi(inv, rhs)
        u_m = u.astype(mm_dtype)
        y = (_dot_nt(rt_h, s_m, mm_prec)
             + jnp.dot(a_rb.astype(mm_dtype), u_m, preferred_element_type=F32, precision=mm_prec)
             + jnp.dot(a_rk.astype(mm_dtype), v_h, preferred_element_type=F32, precision=mm_prec))
        s_scr[h] = (s * e_tot[:, sl] + _dot_tn(u_m, bh_h, mm_prec) + _dot_tn(v_h, kh_h, mm_prec))
        ys.append(y)
    y = jnp.concatenate(ys, axis=1)

    inv_n = 1.0 / HEAD_DIM
    mean = _segsum(y, bd) * inv_n
    d = y - mean
    var = _segsum(d * d, bd) * inv_n
    yn = d * lax.rsqrt(var + GN_EPS) * gng_ref[...] + gnb_ref[...]
    o_ref[0] = ((yn + bonus) * g).astype(o_ref.dtype)


def _rwkv(rw, w0, wdw, a0, waw, wgw, k_k, k_a, r_k, gn_g, gn_b, width,
          mm_dtype=F32, mm_prec=_HI):
    B, T, cols = rw.shape
    L = RW_CHUNK
    row = lambda a: a.reshape(1, width)
    vec = _const_spec((1, width))
    return pl.pallas_call(
        functools.partial(_rwkv_kernel, width=width, mm_dtype=mm_dtype, mm_prec=mm_prec),
        out_shape=jax.ShapeDtypeStruct((B, T, width), BF16),
        grid=(B, T // L),
        in_specs=[pl.BlockSpec((1, L, cols), lambda b, t: (b, t, 0)),
                  vec, _const_spec(wdw.shape), vec, _const_spec(waw.shape),
                  _const_spec(wgw.shape), vec, vec, vec, vec, vec],
        out_specs=pl.BlockSpec((1, L, width), lambda b, t: (b, t, 0)),
        scratch_shapes=[pltpu.VMEM((width // HEAD_DIM, HEAD_DIM, HEAD_DIM), F32)],
        compiler_params=pltpu.CompilerParams(
            dimension_semantics=("arbitrary", "arbitrary"), vmem_limit_bytes=VMEM_LIMIT),
        name="rwkv",
    )(rw, row(w0), wdw, row(a0), waw, wgw, row(k_k), row(k_a), row(r_k), row(gn_g), row(gn_b))


def _foxprep_kernel(fx_ref, fcum_ref, qg_ref, kg_ref, qa_ref, ka_ref, v_ref, *, width):
    W = width
    heads = W // HEAD_DIM
    pf = fx_ref[0]
    tm = pf.shape[0]
    fcum = fcum_ref[0]
    f_hi = fcum.astype(BF16).astype(F32)
    r1 = fcum - f_hi
    f_mid = r1.astype(BF16).astype(F32)
    f_lo = r1 - f_mid
    lane = lax.broadcasted_iota(jnp.int32, (tm, HEAD_DIM), 1)
    scale = 1.0 / math.sqrt(HEAD_DIM)
    for h in range(heads):
        q = pf[:, h * HEAD_DIM:(h + 1) * HEAD_DIM]
        k = pf[:, W + h * HEAD_DIM:W + (h + 1) * HEAD_DIM]
        qn = q * lax.rsqrt(jnp.mean(q * q, axis=-1, keepdims=True) + NORM_EPS) * (qg_ref[...] * scale)
        kn = k * lax.rsqrt(jnp.mean(k * k, axis=-1, keepdims=True) + NORM_EPS) * kg_ref[...]
        hi, mid, lo = f_hi[:, h:h + 1], f_mid[:, h:h + 1], f_lo[:, h:h + 1]
        aq = jnp.where(lane == 0, hi, jnp.where(lane == 1, mid, jnp.where(
            lane == 2, lo, jnp.where(lane < 6, 1.0, 0.0))))
        ak = jnp.where(lane < 3, 1.0, jnp.where(lane == 3, -hi, jnp.where(
            lane == 4, -mid, jnp.where(lane == 5, -lo, 0.0))))
        qa_ref[0, h] = jnp.concatenate([qn, aq], axis=1).astype(BF16)
        ka_ref[0, h] = jnp.concatenate([kn, ak], axis=1).astype(BF16)
    v_ref[0] = pf[:, 2 * W:3 * W].astype(BF16)


def _foxprep(fx, fcum, q_norm_g, k_norm_g, width, tm=512):
    B, T, cols = fx.shape
    heads = width // HEAD_DIM
    return pl.pallas_call(
        functools.partial(_foxprep_kernel, width=width),
        out_shape=(jax.ShapeDtypeStruct((B, heads, T, LANES), BF16),
                   jax.ShapeDtypeStruct((B, heads, T, LANES), BF16),
                   jax.ShapeDtypeStruct((B, T, width), BF16)),
        grid=(B, T // tm),
        in_specs=[pl.BlockSpec((1, tm, cols), lambda b, t: (b, t, 0)),
                  pl.BlockSpec((1, tm, LANES), lambda b, t: (b, t, 0)),
                  _const_spec((1, HEAD_DIM)), _const_spec((1, HEAD_DIM))],
        out_specs=(pl.BlockSpec((1, heads, tm, LANES), lambda b, t: (b, 0, t, 0)),
                   pl.BlockSpec((1, heads, tm, LANES), lambda b, t: (b, 0, t, 0)),
                   pl.BlockSpec((1, tm, width), lambda b, t: (b, t, 0))),
        compiler_params=pltpu.CompilerParams(
            dimension_semantics=("arbitrary", "arbitrary"), vmem_limit_bytes=VMEM_LIMIT),
        name="foxprep",
    )(fx, fcum, q_norm_g.reshape(1, HEAD_DIM), k_norm_g.reshape(1, HEAD_DIM))


def _fox_kernel(qa_ref, ka_ref, v_ref, og_ref, g_ref, o_ref, m_scr, l_scr, acc_scr):
    qi = pl.program_id(2)
    ki = pl.program_id(3)
    tq = qa_ref.shape[2]
    tk = ka_ref.shape[2]

    @pl.when(ki == 0)
    def _():
        m_scr[...] = jnp.full_like(m_scr, NEG_INF)
        l_scr[...] = jnp.zeros_like(l_scr)
        acc_scr[...] = jnp.zeros_like(acc_scr)

    lane = lax.broadcasted_iota(jnp.int32, (tq, LANES), 1)
    first = lane < HEAD_DIM

    @pl.when(ki <= qi)
    def _():
        row = qi * tq + lax.broadcasted_iota(jnp.int32, (tq, tk), 0)
        col = ki * tk + lax.broadcasted_iota(jnp.int32, (tq, tk), 1)
        causal = col <= row
        v = v_ref[0]
        alphas, pvs = [], []
        for hh in range(2):
            s = _dot_nt(qa_ref[0, hh], ka_ref[0, hh])
            s = jnp.where(causal, s, NEG_INF)
            m_prev = m_scr[hh]
            m_new = jnp.maximum(m_prev, jnp.max(s, axis=1, keepdims=True))
            e = jnp.exp(s - m_new)
            alpha = jnp.exp(m_prev - m_new)
            l_scr[hh] = alpha * l_scr[hh] + jnp.sum(e, axis=1, keepdims=True)
            m_scr[hh] = m_new
            alphas.append(alpha)
            pvs.append(_dot(e.astype(BF16), v))
        alpha2 = jnp.where(first, alphas[0], alphas[1])
        acc_scr[...] = alpha2 * acc_scr[...] + jnp.where(first, pvs[0], pvs[1])

    @pl.when(ki == qi)
    def _():
        o = acc_scr[...] / jnp.where(first, l_scr[0], l_scr[1])
        o2 = o * o
        s0 = jnp.sum(jnp.where(first, o2, 0.0), axis=1, keepdims=True)
        s1 = jnp.sum(jnp.where(first, 0.0, o2), axis=1, keepdims=True)
        ms = jnp.where(first, s0, s1) * (1.0 / HEAD_DIM)
        on = o * lax.rsqrt(ms + NORM_EPS) * g_ref[...]
        o_ref[0] = (on * _sigmoid(og_ref[0])).astype(o_ref.dtype)


def _fox(qa, ka, vb, fx, fox_out_g, width, tq=512):
    B, H, T, _ = qa.shape
    tk = tq
    og_blk = 3 * width // LANES
    return pl.pallas_call(
        _fox_kernel,
        out_shape=jax.ShapeDtypeStruct((B, T, width), BF16),
        grid=(B, H // 2, T // tq, T // tk),
        in_specs=[pl.BlockSpec((1, 2, tq, LANES), lambda b, h, q, k: (b, h, q, 0)),
                  pl.BlockSpec((1, 2, tk, LANES), lambda b, h, q, k: (b, h, jnp.minimum(k, q), 0)),
                  pl.BlockSpec((1, tk, LANES), lambda b, h, q, k: (b, jnp.minimum(k, q), h)),
                  pl.BlockSpec((1, tq, LANES), lambda b, h, q, k: (b, q, og_blk + h)),
                  pl.BlockSpec((1, LANES), lambda b, h, q, k: (0, h))],
        out_specs=pl.BlockSpec((1, tq, LANES), lambda b, h, q, k: (b, q, h)),
        scratch_shapes=[pltpu.VMEM((2, tq, 1), F32), pltpu.VMEM((2, tq, 1), F32),
                        pltpu.VMEM((tq, LANES), F32)],
        compiler_params=pltpu.CompilerParams(
            dimension_semantics=("arbitrary",) * 4, vmem_limit_bytes=VMEM_LIMIT),
        name="fox",
    )(qa, ka, vb, fx, fox_out_g.reshape(1, width))


def _outmlp_kernel(x_ref, yrw_ref, yfx_ref, mod_ref, wo_ref, g2_ref, w1_ref, w2_ref, gf_ref,
                   o_ref, *, ff_tile):
    x = x_ref[0]
    half = yrw_ref.shape[2]
    gate1 = mod_ref[0, 2:3, :]
    shift2 = mod_ref[0, 3:4, :]
    scale2 = mod_ref[0, 4:5, :]
    gate2 = mod_ref[0, 5:6, :]
    y = _dot(yrw_ref[0], wo_ref[:half, :]) + _dot(yfx_ref[0], wo_ref[half:, :])
    h1 = x + gate1 * y
    ms = jnp.mean(h1 * h1, axis=-1, keepdims=True)
    u = ((h1 * lax.rsqrt(ms + NORM_EPS) * g2_ref[...]) * (1.0 + scale2) + shift2).astype(BF16)
    acc = jnp.zeros_like(x)
    for j in range(w1_ref.shape[1] // ff_tile):
        hid = jnp.maximum(_dot(u, w1_ref[:, j * ff_tile:(j + 1) * ff_tile]), 0.0)
        acc = acc + _dot((hid * hid).astype(BF16), w2_ref[j * ff_tile:(j + 1) * ff_tile, :])
    h2 = h1 + gate2 * acc
    ms2 = jnp.mean(h2 * h2, axis=-1, keepdims=True)
    o_ref[0] = h2 * lax.rsqrt(ms2 + NORM_EPS) * gf_ref[...]


def _outmlp(x, y_rw, y_fx, mod3, w_o, norm2_g, w1, w2, final_g, tm=512, ff_tile=1024):
    B, T, D = x.shape
    half = y_rw.shape[2]
    return pl.pallas_call(
        functools.partial(_outmlp_kernel, ff_tile=ff_tile),
        out_shape=jax.ShapeDtypeStruct((B, T, D), F32),
        grid=(B, T // tm),
        in_specs=[pl.BlockSpec((1, tm, D), lambda b, t: (b, t, 0)),
                  pl.BlockSpec((1, tm, half), lambda b, t: (b, t, 0)),
                  pl.BlockSpec((1, tm, half), lambda b, t: (b, t, 0)),
                  pl.BlockSpec((1, N_MOD, D), lambda b, t: (b, 0, 0)),
                  _const_spec(w_o.shape), _const_spec((1, D)),
                  _const_spec(w1.shape), _const_spec(w2.shape), _const_spec((1, D))],
        out_specs=pl.BlockSpec((1, tm, D), lambda b, t: (b, t, 0)),
        compiler_params=pltpu.CompilerParams(
            dimension_semantics=("arbitrary", "arbitrary"), vmem_limit_bytes=VMEM_LIMIT),
        name="outmlp",
    )(x, y_rw, y_fx, mod3, w_o, norm2_g.reshape(1, D), w1, w2, final_g.reshape(1, D))


def _pad_cols(a, n):
    return jnp.pad(a, ((0, 0), (0, n - a.shape[1])))


def _pad_rows(a, n):
    return jnp.pad(a, ((0, n - a.shape[0]), (0, 0)))


def _branches(x, c, w_ada, b_ada, norm1_g, w_in, mu_shift, w0, w_up_decay, a0, w_up_a, w_up_g,
              k_k, k_a, r_k, gn_g, gn_b, b_f, q_norm_g, k_norm_g, fox_out_g):
    B, T, D = x.shape
    W = w0.shape[0]
    heads = b_f.shape[0]
    rw_n = 3 * W + DECAY_LORA + AAA_LORA + GATE_LORA

    def regroup(a):
        rw, fx = a[:, :rw_n], a[:, rw_n:]
        o = 3 * W
        return jnp.concatenate([
            rw[:, :o],
            _pad_cols(rw[:, o:o + DECAY_LORA], LANES),
            _pad_cols(rw[:, o + DECAY_LORA:o + DECAY_LORA + AAA_LORA], LANES),
            _pad_cols(rw[:, o + DECAY_LORA + AAA_LORA:], 2 * LANES),
            fx[:, :4 * W],
            _pad_cols(fx[:, 4 * W:], LANES)], axis=1)

    rw_cols = 3 * W + 4 * LANES
    w_all = regroup(w_in).astype(BF16)
    mu_p = regroup(jnp.pad(mu_shift.reshape(1, rw_n), ((0, 0), (0, w_in.shape[1] - rw_n))))[:, :rw_cols]
    bf_p = _pad_cols(b_f.reshape(1, heads), LANES)
    wdw = _pad_rows(w_up_decay, LANES).astype(BF16)
    waw = _pad_rows(w_up_a, LANES).astype(BF16)
    wgw = _pad_rows(w_up_g, 2 * LANES).astype(BF16)

    mod3 = _ada(c, w_ada, b_ada).reshape(B, N_MOD, D)
    rw, fx, fcum = _inproj(x, mod3, norm1_g, w_all, mu_p, bf_p, rw_cols)
    y_rw = _rwkv(rw, w0, wdw, a0, waw, wgw, k_k, k_a, r_k.reshape(-1), gn_g, gn_b, W)
    qa, ka, vb = _foxprep(fx, fcum, q_norm_g, k_norm_g, W)
    y_fx = _fox(qa, ka, vb, fx, fox_out_g, W)
    return y_rw, y_fx, mod3


def kernel(x, c, w_ada, b_ada, norm1_g, w_in, mu_shift, w0, w_up_decay, a0, w_up_a, w_up_g,
           k_k, k_a, r_k, gn_g, gn_b, b_f, q_norm_g, k_norm_g, fox_out_g, w_o, norm2_g,
           w_mlp1, w_mlp2, final_g):
    y_rw, y_fx, mod3 = _branches(x, c, w_ada, b_ada, norm1_g, w_in, mu_shift, w0, w_up_decay,
                                 a0, w_up_a, w_up_g, k_k, k_a, r_k, gn_g, gn_b, b_f,
                                 q_norm_g, k_norm_g, fox_out_g)
    return _outmlp(x, y_rw, y_fx, mod3, w_o.astype(BF16), norm2_g,
                   w_mlp1.astype(BF16), w_mlp2.astype(BF16), final_g)
```

```python
import functools
import math

import jax
import jax.numpy as jnp
from jax import lax
from jax.experimental import pallas as pl
from jax.experimental.pallas import tpu as pltpu

F32 = jnp.float32
BF16 = jnp.bfloat16

HEAD_DIM = 64
LANES = 128
NORM_EPS = 1e-6
GN_EPS = 64e-5
NEG_INF = -1e30
N_MOD = 6
DECAY_LORA = 64
AAA_LORA = 64
GATE_LORA = 160
RW_CHUNK = 64
VMEM_LIMIT = 56 * 1024 * 1024

_HI = lax.Precision.HIGHEST


def _dot(a, b):
    return jnp.dot(a, b, preferred_element_type=F32)


def _dot_hi(a, b):
    return jnp.dot(a, b, preferred_element_type=F32, precision=_HI)


def _dot_nt(a, b):
    return lax.dot_general(a, b, (((1,), (1,)), ((), ())), preferred_element_type=F32)


def _dot_tn(a, b):
    return lax.dot_general(a, b, (((0,), (0,)), ((), ())), preferred_element_type=F32)


def _bf(x):
    return x.astype(BF16)


def _sigmoid(x):
    return 1.0 / (1.0 + jnp.exp(-x))


def _const_spec(shape):
    n = len(shape)
    return pl.BlockSpec(shape, lambda *_: (0,) * n)


def _ada_kernel(c_ref, w_ref, b_ref, o_ref):
    c = c_ref[...]
    o_ref[...] = _dot_hi(c * _sigmoid(c), w_ref[...]) + b_ref[...]


def _ada(c, w_ada, b_ada):
    B, D = c.shape
    n = w_ada.shape[1]
    return pl.pallas_call(
        _ada_kernel,
        out_shape=jax.ShapeDtypeStruct((B, n), F32),
        grid=(n // D,),
        in_specs=[pl.BlockSpec((B, D), lambda j: (0, 0)),
                  pl.BlockSpec((D, D), lambda j: (0, j)),
                  pl.BlockSpec((1, D), lambda j: (0, j))],
        out_specs=pl.BlockSpec((B, D), lambda j: (0, j)),
        name="ada",
    )(c, w_ada, b_ada.reshape(1, n))


def _inproj_kernel(x_ref, mod_ref, g_ref, w_ref, mu_ref, bf_ref,
                   rw_ref, fx_ref, fcum_ref, prev_scr, fcar_scr, *, rw_cols):
    t = pl.program_id(1)
    tm = x_ref.shape[1]

    @pl.when(t == 0)
    def _():
        prev_scr[...] = jnp.zeros_like(prev_scr)
        fcar_scr[...] = jnp.zeros_like(fcar_scr)

    x = x_ref[0]
    shift = mod_ref[0, 0:1, :]
    scale = mod_ref[0, 1:2, :]
    ms = jnp.mean(x * x, axis=-1, keepdims=True)
    u = (x * lax.rsqrt(ms + NORM_EPS) * g_ref[...]) * (1.0 + scale) + shift
    u = u.astype(BF16)

    p = _dot(u, w_ref[:, :rw_cols])
    row = lax.broadcasted_iota(jnp.int32, p.shape, 0)
    prev = jnp.where(row == 0, prev_scr[0:1, :], pltpu.roll(p, shift=1, axis=0))
    prev_scr[0:1, :] = p[tm - 1:tm, :]
    rw_ref[0] = p + (prev - p) * mu_ref[...]

    pf = _dot(u, w_ref[:, rw_cols:])
    fx_ref[0] = pf
    z = pf[:, pf.shape[1] - LANES:] + bf_ref[...]
    logf = jnp.minimum(z, 0.0) - jnp.log(1.0 + jnp.exp(-jnp.abs(z)))
    r2 = lax.broadcasted_iota(jnp.int32, (tm, tm), 0)
    c2 = lax.broadcasted_iota(jnp.int32, (tm, tm), 1)
    tril = (c2 <= r2).astype(F32)
    fcum = _dot_hi(tril, logf) + fcar_scr[0:1, :]
    fcum_ref[0] = fcum
    fcar_scr[0:1, :] = fcum[tm - 1:tm, :]


def _inproj(x, mod3, norm1_g, w_all, mu_p, bf_p, rw_cols, tm=256):
    B, T, D = x.shape
    ncols = w_all.shape[1]
    fx_cols = ncols - rw_cols
    return pl.pallas_call(
        functools.partial(_inproj_kernel, rw_cols=rw_cols),
        out_shape=(jax.ShapeDtypeStruct((B, T, rw_cols), F32),
                   jax.ShapeDtypeStruct((B, T, fx_cols), F32),
                   jax.ShapeDtypeStruct((B, T, LANES), F32)),
        grid=(B, T // tm),
        in_specs=[pl.BlockSpec((1, tm, D), lambda b, t: (b, t, 0)),
                  pl.BlockSpec((1, N_MOD, D), lambda b, t: (b, 0, 0)),
                  _const_spec((1, D)),
                  _const_spec((D, ncols)),
                  _const_spec((1, rw_cols)),
                  _const_spec((1, LANES))],
        out_specs=(pl.BlockSpec((1, tm, rw_cols), lambda b, t: (b, t, 0)),
                   pl.BlockSpec((1, tm, fx_cols), lambda b, t: (b, t, 0)),
                   pl.BlockSpec((1, tm, LANES), lambda b, t: (b, t, 0))),
        scratch_shapes=[pltpu.VMEM((8, rw_cols), F32), pltpu.VMEM((8, LANES), F32)],
        compiler_params=pltpu.CompilerParams(
            dimension_semantics=("arbitrary", "arbitrary"), vmem_limit_bytes=VMEM_LIMIT),
        name="inproj",
    )(x, mod3, norm1_g.reshape(1, D), w_all, mu_p, bf_p)


def _segsum(x, bd):
    hi = x.astype(BF16)
    lo = (x - hi.astype(F32)).astype(BF16)
    return _dot(hi, bd) + _dot(lo, bd)


def _rwkv_kernel(p_ref, w0_ref, wdw_ref, a0_ref, waw_ref, wgw_ref, kk_ref, ka_ref, rk_ref,
                 gng_ref, gnb_ref, o_ref, s_scr, *, width):
    t = pl.program_id(1)
    rows = p_ref.shape[1]
    L = RW_CHUNK
    n_chunks = rows // L
    W = width
    heads = W // HEAD_DIM

    @pl.when(t == 0)
    def _():
        s_scr[...] = jnp.zeros_like(s_scr)

    p = p_ref[0]
    r = p[:, 0:W]
    k = p[:, W:2 * W]
    v = p[:, 2 * W:3 * W]
    o0 = 3 * W
    wd = p[:, o0:o0 + LANES]
    ad = p[:, o0 + LANES:o0 + 2 * LANES]
    gd = p[:, o0 + 2 * LANES:o0 + 4 * LANES]

    z = w0_ref[...] + _dot(_bf(jnp.tanh(wd)), wdw_ref[...])
    softplus = jnp.maximum(-z, 0.0) + jnp.log(1.0 + jnp.exp(-jnp.abs(z)))
    ld = -jnp.exp(-softplus - 0.5)
    a_sig = _sigmoid(a0_ref[...] + _dot(_bf(ad), waw_ref[...]))
    g = _dot(_bf(_sigmoid(gd)), wgw_ref[...])

    ri = lax.broadcasted_iota(jnp.int32, (W, W), 0)
    ci = lax.broadcasted_iota(jnp.int32, (W, W), 1)
    bd = ((ri // HEAD_DIM) == (ci // HEAD_DIM)).astype(BF16)

    kk = k * kk_ref[...]
    kk = kk / jnp.maximum(jnp.sqrt(_segsum(kk * kk, bd)), 1e-12)
    k2 = k * (1.0 + (a_sig - 1.0) * ka_ref[...])
    b = kk * a_sig
    bonus = _segsum(r * k2 * rk_ref[...], bd) * v

    rl = lax.broadcasted_iota(jnp.int32, (L, L), 0)
    cl = lax.broadcasted_iota(jnp.int32, (L, L), 1)
    incl = cl <= rl
    strict = cl < rl
    tril = incl.astype(BF16)
    eye = (cl == rl).astype(F32)
    ld_hi = _bf(ld)
    ld_lo = _bf(ld - ld_hi.astype(F32))
    cums, tots = [], []
    for c in range(n_chunks):
        rs = slice(c * L, (c + 1) * L)
        cum_c = _dot(tril, ld_hi[rs]) + _dot(tril, ld_lo[rs])
        cums.append(cum_c)
        tots.append(cum_c[L - 1:L, :])
    cum = jnp.concatenate(cums, axis=0)
    tot_b = jnp.concatenate([jnp.broadcast_to(tc, (L, W)) for tc in tots], axis=0)

    e_neg = jnp.exp(-cum)
    e_rem = jnp.exp(tot_b - cum)
    rt = r * jnp.exp(cum)
    rt_m = _bf(rt)
    at = _bf(-kk * jnp.exp(cum - ld))
    bt = _bf(b * e_neg)
    kt = _bf(k2 * e_neg)
    bh = _bf(b * e_rem)
    kh = _bf(k2 * e_rem)
    vm = _bf(v)

    pairs = [(c, h) for c in range(n_chunks) for h in range(heads)]
    rsl = lambda c: slice(c * L, (c + 1) * L)
    hsl = lambda h: slice(h * HEAD_DIM, (h + 1) * HEAD_DIM)
    ar = [jnp.concatenate([at[rsl(c)], rt_m[rsl(c)]], axis=0) for c in range(n_chunks)]
    a_rb, a_rk, a_ak, pw, inv = {}, {}, {}, {}, {}
    for c, h in pairs:
        g_b = _dot_nt(ar[c][:, hsl(h)], bt[rsl(c), hsl(h)])
        g_k = _dot_nt(ar[c][:, hsl(h)], kt[rsl(c), hsl(h)])
        a_ab = jnp.where(strict, g_b[:L], 0.0)
        a_ak[c, h] = _bf(jnp.where(strict, g_k[:L], 0.0))
        a_rb[c, h] = _bf(jnp.where(incl, g_b[L:], 0.0))
        a_rk[c, h] = _bf(jnp.where(incl, g_k[L:], 0.0))
        pw[c, h] = _bf(a_ab)
        inv[c, h] = eye + a_ab
    for _ in range(int(math.log2(L)) - 1):
        for ch in pairs:
            pw[ch] = _bf(_dot(pw[ch], pw[ch]))
        for ch in pairs:
            inv[ch] = inv[ch] + _dot(_bf(inv[ch]), pw[ch])
    akv = {(c, h): _bf(_dot(a_ak[c, h], vm[rsl(c), hsl(h)])) for c, h in pairs}
    x = {(c, h): _bf(_dot(_bf(inv[c, h]),
                          jnp.concatenate([at[rsl(c), hsl(h)], akv[c, h]], axis=1)))
         for c, h in pairs}
    r_eff, y0, m_mat, n_mat = {}, {}, {}, {}
    for c, h in pairs:
        p1 = _dot(a_rb[c, h], x[c, h])
        r_eff[c, h] = _bf(rt[rsl(c), hsl(h)] + p1[:, :HEAD_DIM])
        y0[c, h] = p1[:, HEAD_DIM:] + _dot(a_rk[c, h], vm[rsl(c), hsl(h)])
    for c, h in pairs:
        p2 = _dot_tn(x[c, h], bh[rsl(c), hsl(h)])
        m_mat[c, h] = _bf(p2[:HEAD_DIM])
        n_mat[c, h] = p2[HEAD_DIM:] + _dot_tn(vm[rsl(c), hsl(h)], kh[rsl(c), hsl(h)])
    state = [s_scr[h] for h in range(heads)]
    y_rows = []
    for c in range(n_chunks):
        e_tot = jnp.exp(tots[c])
        s_m = [_bf(s) for s in state]
        ys = [_dot_nt(r_eff[c, h], s_m[h]) + y0[c, h] for h in range(heads)]
        state = [state[h] * e_tot[:, hsl(h)] + _dot(s_m[h], m_mat[c, h]) + n_mat[c, h]
                 for h in range(heads)]
        y_rows.append(jnp.concatenate(ys, axis=1))
    for h in range(heads):
        s_scr[h] = state[h]
    y = jnp.concatenate(y_rows, axis=0)

    inv_n = 1.0 / HEAD_DIM
    mean = _segsum(y, bd) * inv_n
    d = y - mean
    var = _segsum(d * d, bd) * inv_n
    yn = d * lax.rsqrt(var + GN_EPS) * gng_ref[...] + gnb_ref[...]
    o_ref[0] = ((yn + bonus) * g).astype(o_ref.dtype)


def _rwkv(rw, w0, wdw, a0, waw, wgw, k_k, k_a, r_k, gn_g, gn_b, width, rows=128):
    B, T, cols = rw.shape
    row = lambda a: a.reshape(1, width)
    vec = _const_spec((1, width))
    return pl.pallas_call(
        functools.partial(_rwkv_kernel, width=width),
        out_shape=jax.ShapeDtypeStruct((B, T, width), BF16),
        grid=(B, T // rows),
        in_specs=[pl.BlockSpec((1, rows, cols), lambda b, t: (b, t, 0)),
                  vec, _const_spec(wdw.shape), vec, _const_spec(waw.shape),
                  _const_spec(wgw.shape), vec, vec, vec, vec, vec],
        out_specs=pl.BlockSpec((1, rows, width), lambda b, t: (b, t, 0)),
        scratch_shapes=[pltpu.VMEM((width // HEAD_DIM, HEAD_DIM, HEAD_DIM), F32)],
        compiler_params=pltpu.CompilerParams(
            dimension_semantics=("arbitrary", "arbitrary"), vmem_limit_bytes=VMEM_LIMIT),
        name="rwkv",
    )(rw, row(w0), wdw, row(a0), waw, wgw, row(k_k), row(k_a), row(r_k), row(gn_g), row(gn_b))


def _foxprep_kernel(fx_ref, fcum_ref, qg_ref, kg_ref, qa_ref, ka_ref, v_ref, *, width):
    W = width
    heads = W // HEAD_DIM
    pf = fx_ref[0]
    tm = pf.shape[0]
    fcum = fcum_ref[0]
    f_hi = fcum.astype(BF16).astype(F32)
    r1 = fcum - f_hi
    f_mid = r1.astype(BF16).astype(F32)
    f_lo = r1 - f_mid
    lane = lax.broadcasted_iota(jnp.int32, (tm, HEAD_DIM), 1)
    scale = 1.0 / math.sqrt(HEAD_DIM)
    for h in range(heads):
        q = pf[:, h * HEAD_DIM:(h + 1) * HEAD_DIM]
        k = pf[:, W + h * HEAD_DIM:W + (h + 1) * HEAD_DIM]
        qn = q * lax.rsqrt(jnp.mean(q * q, axis=-1, keepdims=True) + NORM_EPS) * (qg_ref[...] * scale)
        kn = k * lax.rsqrt(jnp.mean(k * k, axis=-1, keepdims=True) + NORM_EPS) * kg_ref[...]
        hi, mid, lo = f_hi[:, h:h + 1], f_mid[:, h:h + 1], f_lo[:, h:h + 1]
        aq = jnp.where(lane == 0, hi, jnp.where(lane == 1, mid, jnp.where(
            lane == 2, lo, jnp.where(lane < 6, 1.0, 0.0))))
        ak = jnp.where(lane < 3, 1.0, jnp.where(lane == 3, -hi, jnp.where(
            lane == 4, -mid, jnp.where(lane == 5, -lo, 0.0))))
        qa_ref[0, h] = jnp.concatenate([qn, aq], axis=1).astype(BF16)
        ka_ref[0, h] = jnp.concatenate([kn, ak], axis=1).astype(BF16)
    v_ref[0] = pf[:, 2 * W:3 * W].astype(BF16)


def _foxprep(fx, fcum, q_norm_g, k_norm_g, width, tm=512):
    B, T, cols = fx.shape
    heads = width // HEAD_DIM
    return pl.pallas_call(
        functools.partial(_foxprep_kernel, width=width),
        out_shape=(jax.ShapeDtypeStruct((B, heads, T, LANES), BF16),
                   jax.ShapeDtypeStruct((B, heads, T, LANES), BF16),
                   jax.ShapeDtypeStruct((B, T, width), BF16)),
        grid=(B, T // tm),
        in_specs=[pl.BlockSpec((1, tm, cols), lambda b, t: (b, t, 0)),
                  pl.BlockSpec((1, tm, LANES), lambda b, t: (b, t, 0)),
                  _const_spec((1, HEAD_DIM)), _const_spec((1, HEAD_DIM))],
        out_specs=(pl.BlockSpec((1, heads, tm, LANES), lambda b, t: (b, 0, t, 0)),
                   pl.BlockSpec((1, heads, tm, LANES), lambda b, t: (b, 0, t, 0)),
                   pl.BlockSpec((1, tm, width), lambda b, t: (b, t, 0))),
        compiler_params=pltpu.CompilerParams(
            dimension_semantics=("arbitrary", "arbitrary"), vmem_limit_bytes=VMEM_LIMIT),
        name="foxprep",
    )(fx, fcum, q_norm_g.reshape(1, HEAD_DIM), k_norm_g.reshape(1, HEAD_DIM))


def _fox_kernel(qa_ref, ka_ref, v_ref, og_ref, g_ref, o_ref, m_scr, l_scr, acc_scr):
    qi = pl.program_id(2)
    ki = pl.program_id(3)
    tq = qa_ref.shape[2]
    tk = ka_ref.shape[2]

    @pl.when(ki == 0)
    def _():
        m_scr[...] = jnp.full_like(m_scr, NEG_INF)
        l_scr[...] = jnp.zeros_like(l_scr)
        acc_scr[...] = jnp.zeros_like(acc_scr)

    lane = lax.broadcasted_iota(jnp.int32, (tq, LANES), 1)
    first = lane < HEAD_DIM

    @pl.when(ki <= qi)
    def _():
        row = qi * tq + lax.broadcasted_iota(jnp.int32, (tq, tk), 0)
        col = ki * tk + lax.broadcasted_iota(jnp.int32, (tq, tk), 1)
        causal = col <= row
        v = v_ref[0]
        alphas, pvs = [], []
        for hh in range(2):
            s = _dot_nt(qa_ref[0, hh], ka_ref[0, hh])
            s = jnp.where(causal, s, NEG_INF)
            m_prev = m_scr[hh]
            m_new = jnp.maximum(m_prev, jnp.max(s, axis=1, keepdims=True))
            e = jnp.exp(s - m_new)
            alpha = jnp.exp(m_prev - m_new)
            l_scr[hh] = alpha * l_scr[hh] + jnp.sum(e, axis=1, keepdims=True)
            m_scr[hh] = m_new
            alphas.append(alpha)
            pvs.append(_dot(e.astype(BF16), v))
        alpha2 = jnp.where(first, alphas[0], alphas[1])
        acc_scr[...] = alpha2 * acc_scr[...] + jnp.where(first, pvs[0], pvs[1])

    @pl.when(ki == qi)
    def _():
        o = acc_scr[...] / jnp.where(first, l_scr[0], l_scr[1])
        o2 = o * o
        s0 = jnp.sum(jnp.where(first, o2, 0.0), axis=1, keepdims=True)
        s1 = jnp.sum(jnp.where(first, 0.0, o2), axis=1, keepdims=True)
        ms = jnp.where(first, s0, s1) * (1.0 / HEAD_DIM)
        on = o * lax.rsqrt(ms + NORM_EPS) * g_ref[...]
        o_ref[0] = (on * _sigmoid(og_ref[0])).astype(o_ref.dtype)


def _fox(qa, ka, vb, fx, fox_out_g, width, tq=512):
    B, H, T, _ = qa.shape
    tk = tq
    og_blk = 3 * width // LANES
    return pl.pallas_call(
        _fox_kernel,
        out_shape=jax.ShapeDtypeStruct((B, T, width), BF16),
        grid=(B, H // 2, T // tq, T // tk),
        in_specs=[pl.BlockSpec((1, 2, tq, LANES), lambda b, h, q, k: (b, h, q, 0)),
                  pl.BlockSpec((1, 2, tk, LANES), lambda b, h, q, k: (b, h, jnp.minimum(k, q), 0)),
                  pl.BlockSpec((1, tk, LANES), lambda b, h, q, k: (b, jnp.minimum(k, q), h)),
                  pl.BlockSpec((1, tq, LANES), lambda b, h, q, k: (b, q, og_blk + h)),
                  pl.BlockSpec((1, LANES), lambda b, h, q, k: (0, h))],
        out_specs=pl.BlockSpec((1, tq, LANES), lambda b, h, q, k: (b, q, h)),
        scratch_shapes=[pltpu.VMEM((2, tq, 1), F32), pltpu.VMEM((2, tq, 1), F32),
                        pltpu.VMEM((tq, LANES), F32)],
        compiler_params=pltpu.CompilerParams(
            dimension_semantics=("arbitrary",) * 4, vmem_limit_bytes=VMEM_LIMIT),
        name="fox",
    )(qa, ka, vb, fx, fox_out_g.reshape(1, width))


def _outmlp_kernel(x_ref, yrw_ref, yfx_ref, mod_ref, wo_ref, g2_ref, w1_ref, w2_ref, gf_ref,
                   o_ref, *, ff_tile):
    x = x_ref[0]
    half = yrw_ref.shape[2]
    gate1 = mod_ref[0, 2:3, :]
    shift2 = mod_ref[0, 3:4, :]
    scale2 = mod_ref[0, 4:5, :]
    gate2 = mod_ref[0, 5:6, :]
    y = _dot(yrw_ref[0], wo_ref[:half, :]) + _dot(yfx_ref[0], wo_ref[half:, :])
    h1 = x + gate1 * y
    ms = jnp.mean(h1 * h1, axis=-1, keepdims=True)
    u = ((h1 * lax.rsqrt(ms + NORM_EPS) * g2_ref[...]) * (1.0 + scale2) + shift2).astype(BF16)
    acc = jnp.zeros_like(x)
    for j in range(w1_ref.shape[1] // ff_tile):
        hid = jnp.maximum(_dot(u, w1_ref[:, j * ff_tile:(j + 1) * ff_tile]), 0.0)
        acc = acc + _dot((hid * hid).astype(BF16), w2_ref[j * ff_tile:(j + 1) * ff_tile, :])
    h2 = h1 + gate2 * acc
    ms2 = jnp.mean(h2 * h2, axis=-1, keepdims=True)
    o_ref[0] = h2 * lax.rsqrt(ms2 + NORM_EPS) * gf_ref[...]


def _outmlp(x, y_rw, y_fx, mod3, w_o, norm2_g, w1, w2, final_g, tm=512, ff_tile=1024):
    B, T, D = x.shape
    half = y_rw.shape[2]
    return pl.pallas_call(
        functools.partial(_outmlp_kernel, ff_tile=ff_tile),
        out_shape=jax.ShapeDtypeStruct((B, T, D), F32),
        grid=(B, T // tm),
        in_specs=[pl.BlockSpec((1, tm, D), lambda b, t: (b, t, 0)),
                  pl.BlockSpec((1, tm, half), lambda b, t: (b, t, 0)),
                  pl.BlockSpec((1, tm, half), lambda b, t: (b, t, 0)),
                  pl.BlockSpec((1, N_MOD, D), lambda b, t: (b, 0, 0)),
                  _const_spec(w_o.shape), _const_spec((1, D)),
                  _const_spec(w1.shape), _const_spec(w2.shape), _const_spec((1, D))],
        out_specs=pl.BlockSpec((1, tm, D), lambda b, t: (b, t, 0)),
        compiler_params=pltpu.CompilerParams(
            dimension_semantics=("arbitrary", "arbitrary"), vmem_limit_bytes=VMEM_LIMIT),
        name="outmlp",
    )(x, y_rw, y_fx, mod3, w_o, norm2_g.reshape(1, D), w1, w2, final_g.reshape(1, D))


def _pad_cols(a, n):
    return jnp.pad(a, ((0, 0), (0, n - a.shape[1])))


def _pad_rows(a, n):
    return jnp.pad(a, ((0, n - a.shape[0]), (0, 0)))


def _branches(x, c, w_ada, b_ada, norm1_g, w_in, mu_shift, w0, w_up_decay, a0, w_up_a, w_up_g,
              k_k, k_a, r_k, gn_g, gn_b, b_f, q_norm_g, k_norm_g, fox_out_g):
    B, T, D = x.shape
    W = w0.shape[0]
    heads = b_f.shape[0]
    rw_n = 3 * W + DECAY_LORA + AAA_LORA + GATE_LORA

    def regroup(a):
        rw, fx = a[:, :rw_n], a[:, rw_n:]
        o = 3 * W
        return jnp.concatenate([
            rw[:, :o],
            _pad_cols(rw[:, o:o + DECAY_LORA], LANES),
            _pad_cols(rw[:, o + DECAY_LORA:o + DECAY_LORA + AAA_LORA], LANES),
            _pad_cols(rw[:, o + DECAY_LORA + AAA_LORA:], 2 * LANES),
            fx[:, :4 * W],
            _pad_cols(fx[:, 4 * W:], LANES)], axis=1)

    rw_cols = 3 * W + 4 * LANES
    w_all = regroup(w_in).astype(BF16)
    mu_p = regroup(jnp.pad(mu_shift.reshape(1, rw_n), ((0, 0), (0, w_in.shape[1] - rw_n))))[:, :rw_cols]
    bf_p = _pad_cols(b_f.reshape(1, heads), LANES)
    wdw = _pad_rows(w_up_decay, LANES).astype(BF16)
    waw = _pad_rows(w_up_a, LANES).astype(BF16)
    wgw = _pad_rows(w_up_g, 2 * LANES).astype(BF16)

    mod3 = _ada(c, w_ada, b_ada).reshape(B, N_MOD, D)
    rw, fx, fcum = _inproj(x, mod3, norm1_g, w_all, mu_p, bf_p, rw_cols)
    y_rw = _rwkv(rw, w0, wdw, a0, waw, wgw, k_k, k_a, r_k.reshape(-1), gn_g, gn_b, W)
    qa, ka, vb = _foxprep(fx, fcum, q_norm_g, k_norm_g, W)
    y_fx = _fox(qa, ka, vb, fx, fox_out_g, W)
    return y_rw, y_fx, mod3


def kernel(x, c, w_ada, b_ada, norm1_g, w_in, mu_shift, w0, w_up_decay, a0, w_up_a, w_up_g,
           k_k, k_a, r_k, gn_g, gn_b, b_f, q_norm_g, k_norm_g, fox_out_g, w_o, norm2_g,
           w_mlp1, w_mlp2, final_g):
    y_rw, y_fx, mod3 = _branches(x, c, w_ada, b_ada, norm1_g, w_in, mu_shift, w0, w_up_decay,
                                 a0, w_up_a, w_up_g, k_k, k_a, r_k, gn_g, gn_b, b_f,
                                 q_norm_g, k_norm_g, fox_out_g)
    return _outmlp(x, y_rw, y_fx, mod3, w_o.astype(BF16), norm2_g,
                   w_mlp1.astype(BF16), w_mlp2.astype(BF16), final_g)
```

```python
import functools
import math

import jax
import jax.numpy as jnp
from jax import lax
from jax.experimental import pallas as pl
from jax.experimental.pallas import tpu as pltpu

F32 = jnp.float32
BF16 = jnp.bfloat16

HEAD_DIM = 64
LANES = 128
NORM_EPS = 1e-6
GN_EPS = 64e-5
NEG_INF = -1e30
LOG2E = 1.4426950408889634
N_MOD = 6
DECAY_LORA = 64
AAA_LORA = 64
GATE_LORA = 160
RW_CHUNK = 64
VMEM_LIMIT = 56 * 1024 * 1024

_HI = lax.Precision.HIGHEST


def _dot(a, b):
    return jnp.dot(a, b, preferred_element_type=F32)


def _dot_hi(a, b):
    return jnp.dot(a, b, preferred_element_type=F32, precision=_HI)


def _dot_nt(a, b):
    return lax.dot_general(a, b, (((1,), (1,)), ((), ())), preferred_element_type=F32)


def _dot_tn(a, b):
    return lax.dot_general(a, b, (((0,), (0,)), ((), ())), preferred_element_type=F32)


def _bf(x):
    return x.astype(BF16)


def _sigmoid(x):
    return 1.0 / (1.0 + jnp.exp(-x))


def _const_spec(shape):
    n = len(shape)
    return pl.BlockSpec(shape, lambda *_: (0,) * n)


def _ada_kernel(c_ref, w_ref, b_ref, o_ref):
    c = c_ref[...]
    o_ref[...] = _dot_hi(c * _sigmoid(c), w_ref[...]) + b_ref[...]


def _ada(c, w_ada, b_ada):
    B, D = c.shape
    n = w_ada.shape[1]
    return pl.pallas_call(
        _ada_kernel,
        out_shape=jax.ShapeDtypeStruct((B, n), F32),
        grid=(n // D,),
        in_specs=[pl.BlockSpec((B, D), lambda j: (0, 0)),
                  pl.BlockSpec((D, D), lambda j: (0, j)),
                  pl.BlockSpec((1, D), lambda j: (0, j))],
        out_specs=pl.BlockSpec((B, D), lambda j: (0, j)),
        name="ada",
    )(c, w_ada, b_ada.reshape(1, n))


def _inproj_kernel(x_ref, mod_ref, g_ref, w_ref, mu_ref, bf_ref,
                   rw_ref, fx_ref, fcum_ref, prev_scr, fcar_scr, *, rw_cols):
    t = pl.program_id(1)
    tm = x_ref.shape[1]

    @pl.when(t == 0)
    def _():
        prev_scr[...] = jnp.zeros_like(prev_scr)
        fcar_scr[...] = jnp.zeros_like(fcar_scr)

    x = x_ref[0]
    shift = mod_ref[0, 0:1, :]
    scale = mod_ref[0, 1:2, :]
    ms = jnp.mean(x * x, axis=-1, keepdims=True)
    u = (x * lax.rsqrt(ms + NORM_EPS) * g_ref[...]) * (1.0 + scale) + shift
    u = u.astype(BF16)

    p = _dot(u, w_ref[:, :rw_cols])
    row = lax.broadcasted_iota(jnp.int32, p.shape, 0)
    prev = jnp.where(row == 0, prev_scr[0:1, :], pltpu.roll(p, shift=1, axis=0))
    prev_scr[0:1, :] = p[tm - 1:tm, :]
    rw_ref[0] = p + (prev - p) * mu_ref[...]

    pf = _dot(u, w_ref[:, rw_cols:])
    fx_ref[0] = pf
    z = pf[:, pf.shape[1] - LANES:] + bf_ref[...]
    logf = jnp.minimum(z, 0.0) - jnp.log(1.0 + jnp.exp(-jnp.abs(z)))
    r2 = lax.broadcasted_iota(jnp.int32, (tm, tm), 0)
    c2 = lax.broadcasted_iota(jnp.int32, (tm, tm), 1)
    tril = (c2 <= r2).astype(F32)
    fcum = _dot_hi(tril, logf) + fcar_scr[0:1, :]
    fcum_ref[0] = fcum
    fcar_scr[0:1, :] = fcum[tm - 1:tm, :]


def _inproj(x, mod3, norm1_g, w_all, mu_p, bf_p, rw_cols, tm=256):
    B, T, D = x.shape
    ncols = w_all.shape[1]
    fx_cols = ncols - rw_cols
    return pl.pallas_call(
        functools.partial(_inproj_kernel, rw_cols=rw_cols),
        out_shape=(jax.ShapeDtypeStruct((B, T, rw_cols), F32),
                   jax.ShapeDtypeStruct((B, T, fx_cols), F32),
                   jax.ShapeDtypeStruct((B, T, LANES), F32)),
        grid=(B, T // tm),
        in_specs=[pl.BlockSpec((1, tm, D), lambda b, t: (b, t, 0)),
                  pl.BlockSpec((1, N_MOD, D), lambda b, t: (b, 0, 0)),
                  _const_spec((1, D)),
                  _const_spec((D, ncols)),
                  _const_spec((1, rw_cols)),
                  _const_spec((1, LANES))],
        out_specs=(pl.BlockSpec((1, tm, rw_cols), lambda b, t: (b, t, 0)),
                   pl.BlockSpec((1, tm, fx_cols), lambda b, t: (b, t, 0)),
                   pl.BlockSpec((1, tm, LANES), lambda b, t: (b, t, 0))),
        scratch_shapes=[pltpu.VMEM((8, rw_cols), F32), pltpu.VMEM((8, LANES), F32)],
        compiler_params=pltpu.CompilerParams(
            dimension_semantics=("arbitrary", "arbitrary"), vmem_limit_bytes=VMEM_LIMIT),
        name="inproj",
    )(x, mod3, norm1_g.reshape(1, D), w_all, mu_p, bf_p)


def _segsum(x, bd):
    hi = x.astype(BF16)
    lo = (x - hi.astype(F32)).astype(BF16)
    return _dot(hi, bd) + _dot(lo, bd)


def _rwkv_kernel(p_ref, w0_ref, wdw_ref, a0_ref, waw_ref, wgw_ref, kk_ref, ka_ref, rk_ref,
                 gng_ref, gnb_ref, o_ref, s_scr, *, width):
    t = pl.program_id(1)
    rows = p_ref.shape[1]
    L = RW_CHUNK
    n_chunks = rows // L
    W = width
    heads = W // HEAD_DIM

    @pl.when(t == 0)
    def _():
        s_scr[...] = jnp.zeros_like(s_scr)

    p = p_ref[0]
    r = p[:, 0:W]
    k = p[:, W:2 * W]
    v = p[:, 2 * W:3 * W]
    o0 = 3 * W
    wd = p[:, o0:o0 + LANES]
    ad = p[:, o0 + LANES:o0 + 2 * LANES]
    gd = p[:, o0 + 2 * LANES:o0 + 4 * LANES]

    z = w0_ref[...] + _dot(_bf(jnp.tanh(wd)), wdw_ref[...])
    softplus = jnp.maximum(-z, 0.0) + jnp.log(1.0 + jnp.exp(-jnp.abs(z)))
    ld = -jnp.exp(-softplus - 0.5)
    a_sig = _sigmoid(a0_ref[...] + _dot(_bf(ad), waw_ref[...]))
    g = _dot(_bf(_sigmoid(gd)), wgw_ref[...])

    ri = lax.broadcasted_iota(jnp.int32, (W, W), 0)
    ci = lax.broadcasted_iota(jnp.int32, (W, W), 1)
    bd = ((ri // HEAD_DIM) == (ci // HEAD_DIM)).astype(BF16)

    kk = k * kk_ref[...]
    kk = kk / jnp.maximum(jnp.sqrt(_segsum(kk * kk, bd)), 1e-12)
    k2 = k * (1.0 + (a_sig - 1.0) * ka_ref[...])
    b = kk * a_sig
    bonus = _segsum(r * k2 * rk_ref[...], bd) * v

    rl = lax.broadcasted_iota(jnp.int32, (L, L), 0)
    cl = lax.broadcasted_iota(jnp.int32, (L, L), 1)
    incl = cl <= rl
    strict = cl < rl
    tril = incl.astype(BF16)
    eye = (cl == rl).astype(F32)
    ld_hi = _bf(ld)
    ld_lo = _bf(ld - ld_hi.astype(F32))
    cums, tots = [], []
    for c in range(n_chunks):
        rs = slice(c * L, (c + 1) * L)
        cum_c = _dot(tril, ld_hi[rs]) + _dot(tril, ld_lo[rs])
        cums.append(cum_c)
        tots.append(cum_c[L - 1:L, :])
    cum = jnp.concatenate(cums, axis=0)
    tot_b = jnp.concatenate([jnp.broadcast_to(tc, (L, W)) for tc in tots], axis=0)

    e_neg = jnp.exp(-cum)
    e_rem = jnp.exp(tot_b - cum)
    rt = r * jnp.exp(cum)
    rt_m = _bf(rt)
    at = _bf(-kk * jnp.exp(cum - ld))
    bt = _bf(b * e_neg)
    kt = _bf(k2 * e_neg)
    bh = _bf(b * e_rem)
    kh = _bf(k2 * e_rem)
    vm = _bf(v)

    pairs = [(c, h) for c in range(n_chunks) for h in range(heads)]
    rsl = lambda c: slice(c * L, (c + 1) * L)
    hsl = lambda h: slice(h * HEAD_DIM, (h + 1) * HEAD_DIM)
    ar = [jnp.concatenate([at[rsl(c)], rt_m[rsl(c)]], axis=0) for c in range(n_chunks)]
    a_rb, a_rk, a_ak, pw, inv = {}, {}, {}, {}, {}
    for c, h in pairs:
        g_b = _dot_nt(ar[c][:, hsl(h)], bt[rsl(c), hsl(h)])
        g_k = _dot_nt(ar[c][:, hsl(h)], kt[rsl(c), hsl(h)])
        a_ab = jnp.where(strict, g_b[:L], 0.0)
        a_ak[c, h] = _bf(jnp.where(strict, g_k[:L], 0.0))
        a_rb[c, h] = _bf(jnp.where(incl, g_b[L:], 0.0))
        a_rk[c, h] = _bf(jnp.where(incl, g_k[L:], 0.0))
        pw[c, h] = _bf(a_ab)
        inv[c, h] = eye + a_ab
    for _ in range(int(math.log2(L)) - 1):
        for ch in pairs:
            pw[ch] = _bf(_dot(pw[ch], pw[ch]))
        for ch in pairs:
            inv[ch] = inv[ch] + _dot(_bf(inv[ch]), pw[ch])
    akv = {(c, h): _bf(_dot(a_ak[c, h], vm[rsl(c), hsl(h)])) for c, h in pairs}
    x = {(c, h): _bf(_dot(_bf(inv[c, h]),
                          jnp.concatenate([at[rsl(c), hsl(h)], akv[c, h]], axis=1)))
         for c, h in pairs}
    r_eff, y0, m_mat, n_mat = {}, {}, {}, {}
    for c, h in pairs:
        p1 = _dot(a_rb[c, h], x[c, h])
        r_eff[c, h] = _bf(rt[rsl(c), hsl(h)] + p1[:, :HEAD_DIM])
        y0[c, h] = p1[:, HEAD_DIM:] + _dot(a_rk[c, h], vm[rsl(c), hsl(h)])
    for c, h in pairs:
        p2 = _dot_tn(x[c, h], bh[rsl(c), hsl(h)])
        m_mat[c, h] = _bf(p2[:HEAD_DIM])
        n_mat[c, h] = p2[HEAD_DIM:] + _dot_tn(vm[rsl(c), hsl(h)], kh[rsl(c), hsl(h)])
    state = [s_scr[h] for h in range(heads)]
    y_rows = []
    for c in range(n_chunks):
        e_tot = jnp.exp(tots[c])
        s_m = [_bf(s) for s in state]
        ys = [_dot_nt(r_eff[c, h], s_m[h]) + y0[c, h] for h in range(heads)]
        state = [state[h] * e_tot[:, hsl(h)] + _dot(s_m[h], m_mat[c, h]) + n_mat[c, h]
                 for h in range(heads)]
        y_rows.append(jnp.concatenate(ys, axis=1))
    for h in range(heads):
        s_scr[h] = state[h]
    y = jnp.concatenate(y_rows, axis=0)

    inv_n = 1.0 / HEAD_DIM
    mean = _segsum(y, bd) * inv_n
    d = y - mean
    var = _segsum(d * d, bd) * inv_n
    yn = d * lax.rsqrt(var + GN_EPS) * gng_ref[...] + gnb_ref[...]
    o_ref[0] = ((yn + bonus) * g).astype(o_ref.dtype)


def _rwkv(rw, w0, wdw, a0, waw, wgw, k_k, k_a, r_k, gn_g, gn_b, width, rows=128):
    B, T, cols = rw.shape
    row = lambda a: a.reshape(1, width)
    vec = _const_spec((1, width))
    return pl.pallas_call(
        functools.partial(_rwkv_kernel, width=width),
        out_shape=jax.ShapeDtypeStruct((B, T, width), BF16),
        grid=(B, T // rows),
        in_specs=[pl.BlockSpec((1, rows, cols), lambda b, t: (b, t, 0)),
                  vec, _const_spec(wdw.shape), vec, _const_spec(waw.shape),
                  _const_spec(wgw.shape), vec, vec, vec, vec, vec],
        out_specs=pl.BlockSpec((1, rows, width), lambda b, t: (b, t, 0)),
        scratch_shapes=[pltpu.VMEM((width // HEAD_DIM, HEAD_DIM, HEAD_DIM), F32)],
        compiler_params=pltpu.CompilerParams(
            dimension_semantics=("arbitrary", "arbitrary"), vmem_limit_bytes=VMEM_LIMIT),
        name="rwkv",
    )(rw, row(w0), wdw, row(a0), waw, wgw, row(k_k), row(k_a), row(r_k), row(gn_g), row(gn_b))


def _foxprep_kernel(fx_ref, fcum_ref, qg_ref, kg_ref, qa_ref, ka_ref, v_ref, *, width):
    W = width
    heads = W // HEAD_DIM
    pf = fx_ref[0]
    tm = pf.shape[0]
    fcum = fcum_ref[0] * LOG2E
    f_hi = fcum.astype(BF16).astype(F32)
    r1 = fcum - f_hi
    f_mid = r1.astype(BF16).astype(F32)
    f_lo = r1 - f_mid
    lane = lax.broadcasted_iota(jnp.int32, (tm, HEAD_DIM), 1)
    scale = LOG2E / math.sqrt(HEAD_DIM)
    for h in range(heads):
        q = pf[:, h * HEAD_DIM:(h + 1) * HEAD_DIM]
        k = pf[:, W + h * HEAD_DIM:W + (h + 1) * HEAD_DIM]
        qn = q * lax.rsqrt(jnp.mean(q * q, axis=-1, keepdims=True) + NORM_EPS) * (qg_ref[...] * scale)
        kn = k * lax.rsqrt(jnp.mean(k * k, axis=-1, keepdims=True) + NORM_EPS) * kg_ref[...]
        hi, mid, lo = f_hi[:, h:h + 1], f_mid[:, h:h + 1], f_lo[:, h:h + 1]
        aq = jnp.where(lane == 0, hi, jnp.where(lane == 1, mid, jnp.where(
            lane == 2, lo, jnp.where(lane < 6, 1.0, 0.0))))
        ak = jnp.where(lane < 3, 1.0, jnp.where(lane == 3, -hi, jnp.where(
            lane == 4, -mid, jnp.where(lane == 5, -lo, 0.0))))
        qa_ref[0, h] = jnp.concatenate([qn, aq], axis=1).astype(BF16)
        ka_ref[0, h] = jnp.concatenate([kn, ak], axis=1).astype(BF16)
    v_ref[0] = pf[:, 2 * W:3 * W].astype(BF16)


def _foxprep(fx, fcum, q_norm_g, k_norm_g, width, tm=512):
    B, T, cols = fx.shape
    heads = width // HEAD_DIM
    return pl.pallas_call(
        functools.partial(_foxprep_kernel, width=width),
        out_shape=(jax.ShapeDtypeStruct((B, heads, T, LANES), BF16),
                   jax.ShapeDtypeStruct((B, heads, T, LANES), BF16),
                   jax.ShapeDtypeStruct((B, T, width), BF16)),
        grid=(B, T // tm),
        in_specs=[pl.BlockSpec((1, tm, cols), lambda b, t: (b, t, 0)),
                  pl.BlockSpec((1, tm, LANES), lambda b, t: (b, t, 0)),
                  _const_spec((1, HEAD_DIM)), _const_spec((1, HEAD_DIM))],
        out_specs=(pl.BlockSpec((1, heads, tm, LANES), lambda b, t: (b, 0, t, 0)),
                   pl.BlockSpec((1, heads, tm, LANES), lambda b, t: (b, 0, t, 0)),
                   pl.BlockSpec((1, tm, width), lambda b, t: (b, t, 0))),
        compiler_params=pltpu.CompilerParams(
            dimension_semantics=("arbitrary", "arbitrary"), vmem_limit_bytes=VMEM_LIMIT),
        name="foxprep",
    )(fx, fcum, q_norm_g.reshape(1, HEAD_DIM), k_norm_g.reshape(1, HEAD_DIM))


def _fox_kernel(qa_ref, ka_ref, v_ref, og_ref, g_ref, o_ref, m_scr, l_scr, acc_scr, *, sub):
    qi = pl.program_id(2)
    tq = qa_ref.shape[2]
    tk = tq
    chains = [(rb, hh) for rb in range(tq // sub) for hh in range(2)]
    rsl = lambda rb: slice(rb * sub, (rb + 1) * sub)
    first = lax.broadcasted_iota(jnp.int32, (sub, LANES), 1) < HEAD_DIM
    ones = jnp.ones((tk, LANES), BF16)

    m_scr[...] = jnp.full_like(m_scr, NEG_INF)
    l_scr[...] = jnp.zeros_like(l_scr)
    acc_scr[...] = jnp.zeros_like(acc_scr)

    def block(k_start, masked):
        v_aug = jnp.concatenate([v_ref[0, pl.ds(k_start, tk), :], ones], axis=1)
        ks = [ka_ref[0, hh, pl.ds(k_start, tk), :] for hh in range(2)]
        s, m_new, alpha, p, pv = {}, {}, {}, {}, {}
        for rb, hh in chains:
            s_ = _dot_nt(qa_ref[0, hh, rsl(rb), :], ks[hh])
            if masked:
                row = rb * sub + lax.broadcasted_iota(jnp.int32, (sub, tk), 0)
                col = lax.broadcasted_iota(jnp.int32, (sub, tk), 1)
                s_ = jnp.where(col <= row, s_, NEG_INF)
            s[rb, hh] = s_
        for rb, hh in chains:
            m_prev = m_scr[hh, rsl(rb), :]
            m_new[rb, hh] = jnp.maximum(m_prev, jnp.max(s[rb, hh], axis=1, keepdims=True))
            alpha[rb, hh] = jnp.exp2(m_prev - m_new[rb, hh])
            m_scr[hh, rsl(rb), :] = m_new[rb, hh]
        for ch in chains:
            p[ch] = jnp.exp2(s[ch] - jnp.tile(m_new[ch], (1, tk // LANES))).astype(BF16)
        for ch in chains:
            pv[ch] = _dot(p[ch], v_aug)
        for rb in range(tq // sub):
            a2 = jnp.where(first, alpha[rb, 0], alpha[rb, 1])
            acc_scr[rsl(rb), :] = a2 * acc_scr[rsl(rb), :] + jnp.where(
                first, pv[rb, 0][:, :LANES], pv[rb, 1][:, :LANES])
            for hh in range(2):
                l_scr[hh, rsl(rb), :] = alpha[rb, hh] * l_scr[hh, rsl(rb), :] + pv[rb, hh][:, LANES:]

    def off_diagonal(i, carry):
        block(pl.multiple_of(i * tk, tk), masked=False)
        return carry

    lax.fori_loop(0, qi, off_diagonal, 0)
    block(pl.multiple_of(qi * tk, tk), masked=True)

    first_q = lax.broadcasted_iota(jnp.int32, (tq, LANES), 1) < HEAD_DIM
    o = acc_scr[...] / jnp.where(first_q, l_scr[0], l_scr[1])
    o2 = o * o
    s0 = jnp.sum(jnp.where(first_q, o2, 0.0), axis=1, keepdims=True)
    s1 = jnp.sum(jnp.where(first_q, 0.0, o2), axis=1, keepdims=True)
    ms = jnp.where(first_q, s0, s1) * (1.0 / HEAD_DIM)
    on = o * lax.rsqrt(ms + NORM_EPS) * g_ref[...]
    o_ref[0] = (on * _sigmoid(og_ref[0])).astype(o_ref.dtype)


def _fox(qa, ka, vb, fx, fox_out_g, width, tq=512, sub=128):
    B, H, T, _ = qa.shape
    og_blk = 3 * width // LANES
    return pl.pallas_call(
        functools.partial(_fox_kernel, sub=sub),
        out_shape=jax.ShapeDtypeStruct((B, T, width), BF16),
        grid=(B, H // 2, T // tq),
        in_specs=[pl.BlockSpec((1, 2, tq, LANES), lambda b, h, q: (b, h, q, 0)),
                  pl.BlockSpec((1, 2, T, LANES), lambda b, h, q: (b, h, 0, 0)),
                  pl.BlockSpec((1, T, LANES), lambda b, h, q: (b, 0, h)),
                  pl.BlockSpec((1, tq, LANES), lambda b, h, q: (b, q, og_blk + h)),
                  pl.BlockSpec((1, LANES), lambda b, h, q: (0, h))],
        out_specs=pl.BlockSpec((1, tq, LANES), lambda b, h, q: (b, q, h)),
        scratch_shapes=[pltpu.VMEM((2, tq, LANES), F32), pltpu.VMEM((2, tq, LANES), F32),
                        pltpu.VMEM((tq, LANES), F32)],
        compiler_params=pltpu.CompilerParams(
            dimension_semantics=("arbitrary",) * 3, vmem_limit_bytes=VMEM_LIMIT),
        name="fox",
    )(qa, ka, vb, fx, fox_out_g.reshape(1, width))


def _outmlp_kernel(x_ref, yrw_ref, yfx_ref, mod_ref, wo_ref, g2_ref, w1_ref, w2_ref, gf_ref,
                   o_ref, *, ff_tile):
    x = x_ref[0]
    half = yrw_ref.shape[2]
    gate1 = mod_ref[0, 2:3, :]
    shift2 = mod_ref[0, 3:4, :]
    scale2 = mod_ref[0, 4:5, :]
    gate2 = mod_ref[0, 5:6, :]
    y = _dot(yrw_ref[0], wo_ref[:half, :]) + _dot(yfx_ref[0], wo_ref[half:, :])
    h1 = x + gate1 * y
    ms = jnp.mean(h1 * h1, axis=-1, keepdims=True)
    u = ((h1 * lax.rsqrt(ms + NORM_EPS) * g2_ref[...]) * (1.0 + scale2) + shift2).astype(BF16)
    acc = jnp.zeros_like(x)
    for j in range(w1_ref.shape[1] // ff_tile):
        hid = jnp.maximum(_dot(u, w1_ref[:, j * ff_tile:(j + 1) * ff_tile]), 0.0)
        acc = acc + _dot((hid * hid).astype(BF16), w2_ref[j * ff_tile:(j + 1) * ff_tile, :])
    h2 = h1 + gate2 * acc
    ms2 = jnp.mean(h2 * h2, axis=-1, keepdims=True)
    o_ref[0] = h2 * lax.rsqrt(ms2 + NORM_EPS) * gf_ref[...]


def _outmlp(x, y_rw, y_fx, mod3, w_o, norm2_g, w1, w2, final_g, tm=512, ff_tile=1024):
    B, T, D = x.shape
    half = y_rw.shape[2]
    return pl.pallas_call(
        functools.partial(_outmlp_kernel, ff_tile=ff_tile),
        out_shape=jax.ShapeDtypeStruct((B, T, D), F32),
        grid=(B, T // tm),
        in_specs=[pl.BlockSpec((1, tm, D), lambda b, t: (b, t, 0)),
                  pl.BlockSpec((1, tm, half), lambda b, t: (b, t, 0)),
                  pl.BlockSpec((1, tm, half), lambda b, t: (b, t, 0)),
                  pl.BlockSpec((1, N_MOD, D), lambda b, t: (b, 0, 0)),
                  _const_spec(w_o.shape), _const_spec((1, D)),
                  _const_spec(w1.shape), _const_spec(w2.shape), _const_spec((1, D))],
        out_specs=pl.BlockSpec((1, tm, D), lambda b, t: (b, t, 0)),
        compiler_params=pltpu.CompilerParams(
            dimension_semantics=("arbitrary", "arbitrary"), vmem_limit_bytes=VMEM_LIMIT),
        name="outmlp",
    )(x, y_rw, y_fx, mod3, w_o, norm2_g.reshape(1, D), w1, w2, final_g.reshape(1, D))


def _pad_cols(a, n):
    return jnp.pad(a, ((0, 0), (0, n - a.shape[1])))


def _pad_rows(a, n):
    return jnp.pad(a, ((0, n - a.shape[0]), (0, 0)))


def _branches(x, c, w_ada, b_ada, norm1_g, w_in, mu_shift, w0, w_up_decay, a0, w_up_a, w_up_g,
              k_k, k_a, r_k, gn_g, gn_b, b_f, q_norm_g, k_norm_g, fox_out_g):
    B, T, D = x.shape
    W = w0.shape[0]
    heads = b_f.shape[0]
    rw_n = 3 * W + DECAY_LORA + AAA_LORA + GATE_LORA

    def regroup(a):
        rw, fx = a[:, :rw_n], a[:, rw_n:]
        o = 3 * W
        return jnp.concatenate([
            rw[:, :o],
            _pad_cols(rw[:, o:o + DECAY_LORA], LANES),
            _pad_cols(rw[:, o + DECAY_LORA:o + DECAY_LORA + AAA_LORA], LANES),
            _pad_cols(rw[:, o + DECAY_LORA + AAA_LORA:], 2 * LANES),
            fx[:, :4 * W],
            _pad_cols(fx[:, 4 * W:], LANES)], axis=1)

    rw_cols = 3 * W + 4 * LANES
    w_all = regroup(w_in).astype(BF16)
    mu_p = regroup(jnp.pad(mu_shift.reshape(1, rw_n), ((0, 0), (0, w_in.shape[1] - rw_n))))[:, :rw_cols]
    bf_p = _pad_cols(b_f.reshape(1, heads), LANES)
    wdw = _pad_rows(w_up_decay, LANES).astype(BF16)
    waw = _pad_rows(w_up_a, LANES).astype(BF16)
    wgw = _pad_rows(w_up_g, 2 * LANES).astype(BF16)

    mod3 = _ada(c, w_ada, b_ada).reshape(B, N_MOD, D)
    rw, fx, fcum = _inproj(x, mod3, norm1_g, w_all, mu_p, bf_p, rw_cols)
    y_rw = _rwkv(rw, w0, wdw, a0, waw, wgw, k_k, k_a, r_k.reshape(-1), gn_g, gn_b, W)
    qa, ka, vb = _foxprep(fx, fcum, q_norm_g, k_norm_g, W)
    y_fx = _fox(qa, ka, vb, fx, fox_out_g, W)
    return y_rw, y_fx, mod3


def kernel(x, c, w_ada, b_ada, norm1_g, w_in, mu_shift, w0, w_up_decay, a0, w_up_a, w_up_g,
           k_k, k_a, r_k, gn_g, gn_b, b_f, q_norm_g, k_norm_g, fox_out_g, w_o, norm2_g,
           w_mlp1, w_mlp2, final_g):
    y_rw, y_fx, mod3 = _branches(x, c, w_ada, b_ada, norm1_g, w_in, mu_shift, w0, w_up_decay,
                                 a0, w_up_a, w_up_g, k_k, k_a, r_k, gn_g, gn_b, b_f,
                                 q_norm_g, k_norm_g, fox_out_g)
    return _outmlp(x, y_rw, y_fx, mod3, w_o.astype(BF16), norm2_g,
                   w_mlp1.astype(BF16), w_mlp2.astype(BF16), final_g)
```

```python
import functools
import math

import jax
import jax.numpy as jnp
from jax import lax
from jax.experimental import pallas as pl
from jax.experimental.pallas import tpu as pltpu

F32 = jnp.float32
BF16 = jnp.bfloat16

HEAD_DIM = 64
LANES = 128
NORM_EPS = 1e-6
GN_EPS = 64e-5
NEG_INF = -1e30
LOG2E = 1.4426950408889634
N_MOD = 6
DECAY_LORA = 64
AAA_LORA = 64
GATE_LORA = 160
RW_CHUNK = 64
VMEM_LIMIT = 56 * 1024 * 1024

_HI = lax.Precision.HIGHEST


def _dot(a, b):
    return jnp.dot(a, b, preferred_element_type=F32)


def _dot_hi(a, b):
    return jnp.dot(a, b, preferred_element_type=F32, precision=_HI)


def _dot_nt(a, b):
    return lax.dot_general(a, b, (((1,), (1,)), ((), ())), preferred_element_type=F32)


def _dot_tn(a, b):
    return lax.dot_general(a, b, (((0,), (0,)), ((), ())), preferred_element_type=F32)


def _bf(x):
    return x.astype(BF16)


def _sigmoid(x):
    return 1.0 / (1.0 + jnp.exp(-x))


def _const_spec(shape):
    n = len(shape)
    return pl.BlockSpec(shape, lambda *_: (0,) * n)


def _ada_kernel(c_ref, w_ref, b_ref, o_ref):
    c = c_ref[...]
    o_ref[...] = _dot_hi(c * _sigmoid(c), w_ref[...]) + b_ref[...]


def _ada(c, w_ada, b_ada):
    B, D = c.shape
    n = w_ada.shape[1]
    return pl.pallas_call(
        _ada_kernel,
        out_shape=jax.ShapeDtypeStruct((B, n), F32),
        grid=(n // D,),
        in_specs=[pl.BlockSpec((B, D), lambda j: (0, 0)),
                  pl.BlockSpec((D, D), lambda j: (0, j)),
                  pl.BlockSpec((1, D), lambda j: (0, j))],
        out_specs=pl.BlockSpec((B, D), lambda j: (0, j)),
        name="ada",
    )(c, w_ada, b_ada.reshape(1, n))


def _inproj_kernel(x_ref, mod_ref, g_ref, w_ref, mu_ref, bf_ref,
                   rw_ref, fx_ref, fcum_ref, prev_scr, fcar_scr, *, rw_cols):
    t = pl.program_id(1)
    tm = x_ref.shape[1]

    @pl.when(t == 0)
    def _():
        prev_scr[...] = jnp.zeros_like(prev_scr)
        fcar_scr[...] = jnp.zeros_like(fcar_scr)

    x = x_ref[0]
    shift = mod_ref[0, 0:1, :]
    scale = mod_ref[0, 1:2, :]
    ms = jnp.mean(x * x, axis=-1, keepdims=True)
    u = (x * lax.rsqrt(ms + NORM_EPS) * g_ref[...]) * (1.0 + scale) + shift
    u = u.astype(BF16)

    p = _dot(u, w_ref[:, :rw_cols])
    row = lax.broadcasted_iota(jnp.int32, p.shape, 0)
    prev = jnp.where(row == 0, prev_scr[0:1, :], pltpu.roll(p, shift=1, axis=0))
    prev_scr[0:1, :] = p[tm - 1:tm, :]
    rw_ref[0] = p + (prev - p) * mu_ref[...]

    pf = _dot(u, w_ref[:, rw_cols:])
    fx_ref[0] = pf
    z = pf[:, pf.shape[1] - LANES:] + bf_ref[...]
    logf = jnp.minimum(z, 0.0) - jnp.log(1.0 + jnp.exp(-jnp.abs(z)))
    r2 = lax.broadcasted_iota(jnp.int32, (tm, tm), 0)
    c2 = lax.broadcasted_iota(jnp.int32, (tm, tm), 1)
    tril = (c2 <= r2).astype(F32)
    fcum = _dot_hi(tril, logf) + fcar_scr[0:1, :]
    fcum_ref[0] = fcum
    fcar_scr[0:1, :] = fcum[tm - 1:tm, :]


def _inproj(x, mod3, norm1_g, w_all, mu_p, bf_p, rw_cols, tm=256):
    B, T, D = x.shape
    ncols = w_all.shape[1]
    fx_cols = ncols - rw_cols
    return pl.pallas_call(
        functools.partial(_inproj_kernel, rw_cols=rw_cols),
        out_shape=(jax.ShapeDtypeStruct((B, T, rw_cols), F32),
                   jax.ShapeDtypeStruct((B, T, fx_cols), F32),
                   jax.ShapeDtypeStruct((B, T, LANES), F32)),
        grid=(B, T // tm),
        in_specs=[pl.BlockSpec((1, tm, D), lambda b, t: (b, t, 0)),
                  pl.BlockSpec((1, N_MOD, D), lambda b, t: (b, 0, 0)),
                  _const_spec((1, D)),
                  _const_spec((D, ncols)),
                  _const_spec((1, rw_cols)),
                  _const_spec((1, LANES))],
        out_specs=(pl.BlockSpec((1, tm, rw_cols), lambda b, t: (b, t, 0)),
                   pl.BlockSpec((1, tm, fx_cols), lambda b, t: (b, t, 0)),
                   pl.BlockSpec((1, tm, LANES), lambda b, t: (b, t, 0))),
        scratch_shapes=[pltpu.VMEM((8, rw_cols), F32), pltpu.VMEM((8, LANES), F32)],
        compiler_params=pltpu.CompilerParams(
            dimension_semantics=("arbitrary", "arbitrary"), vmem_limit_bytes=VMEM_LIMIT),
        name="inproj",
    )(x, mod3, norm1_g.reshape(1, D), w_all, mu_p, bf_p)


def _segsum(x, bd):
    hi = x.astype(BF16)
    lo = (x - hi.astype(F32)).astype(BF16)
    return _dot(hi, bd) + _dot(lo, bd)


def _rwkv_kernel(p_ref, w0_ref, wdw_ref, a0_ref, waw_ref, wgw_ref, kk_ref, ka_ref, rk_ref,
                 gng_ref, gnb_ref, o_ref, s_scr, *, width):
    t = pl.program_id(1)
    rows = p_ref.shape[1]
    L = RW_CHUNK
    n_chunks = rows // L
    W = width
    heads = W // HEAD_DIM

    @pl.when(t == 0)
    def _():
        s_scr[...] = jnp.zeros_like(s_scr)

    p = p_ref[0]
    r = p[:, 0:W]
    k = p[:, W:2 * W]
    v = p[:, 2 * W:3 * W]
    o0 = 3 * W
    wd = p[:, o0:o0 + LANES]
    ad = p[:, o0 + LANES:o0 + 2 * LANES]
    gd = p[:, o0 + 2 * LANES:o0 + 4 * LANES]

    z = w0_ref[...] + _dot(_bf(jnp.tanh(wd)), wdw_ref[...])
    softplus = jnp.maximum(-z, 0.0) + jnp.log(1.0 + jnp.exp(-jnp.abs(z)))
    ld = -jnp.exp(-softplus - 0.5)
    a_sig = _sigmoid(a0_ref[...] + _dot(_bf(ad), waw_ref[...]))
    g = _dot(_bf(_sigmoid(gd)), wgw_ref[...])

    ri = lax.broadcasted_iota(jnp.int32, (W, W), 0)
    ci = lax.broadcasted_iota(jnp.int32, (W, W), 1)
    bd = ((ri // HEAD_DIM) == (ci // HEAD_DIM)).astype(BF16)

    kk = k * kk_ref[...]
    kk = kk / jnp.maximum(jnp.sqrt(_segsum(kk * kk, bd)), 1e-12)
    k2 = k * (1.0 + (a_sig - 1.0) * ka_ref[...])
    b = kk * a_sig
    bonus = _segsum(r * k2 * rk_ref[...], bd) * v

    rl = lax.broadcasted_iota(jnp.int32, (L, L), 0)
    cl = lax.broadcasted_iota(jnp.int32, (L, L), 1)
    incl = cl <= rl
    strict = cl < rl
    tril = incl.astype(BF16)
    eye = (cl == rl).astype(F32)
    ld_hi = _bf(ld)
    ld_lo = _bf(ld - ld_hi.astype(F32))
    cums, tots = [], []
    for c in range(n_chunks):
        rs = slice(c * L, (c + 1) * L)
        cum_c = _dot(tril, ld_hi[rs]) + _dot(tril, ld_lo[rs])
        cums.append(cum_c)
        tots.append(cum_c[L - 1:L, :])
    cum = jnp.concatenate(cums, axis=0)
    tot_b = jnp.concatenate([jnp.broadcast_to(tc, (L, W)) for tc in tots], axis=0)

    e_neg = jnp.exp(-cum)
    e_rem = jnp.exp(tot_b - cum)
    rt = r * jnp.exp(cum)
    rt_m = _bf(rt)
    at = _bf(-kk * jnp.exp(cum - ld))
    bt = _bf(b * e_neg)
    kt = _bf(k2 * e_neg)
    bh = _bf(b * e_rem)
    kh = _bf(k2 * e_rem)
    vm = _bf(v)

    pairs = [(c, h) for c in range(n_chunks) for h in range(heads)]
    rsl = lambda c: slice(c * L, (c + 1) * L)
    hsl = lambda h: slice(h * HEAD_DIM, (h + 1) * HEAD_DIM)
    ar = [jnp.concatenate([at[rsl(c)], rt_m[rsl(c)]], axis=0) for c in range(n_chunks)]
    a_rb, a_rk, a_ak, pw, inv = {}, {}, {}, {}, {}
    for c, h in pairs:
        g_b = _dot_nt(ar[c][:, hsl(h)], bt[rsl(c), hsl(h)])
        g_k = _dot_nt(ar[c][:, hsl(h)], kt[rsl(c), hsl(h)])
        a_ab = jnp.where(strict, g_b[:L], 0.0)
        a_ak[c, h] = _bf(jnp.where(strict, g_k[:L], 0.0))
        a_rb[c, h] = _bf(jnp.where(incl, g_b[L:], 0.0))
        a_rk[c, h] = _bf(jnp.where(incl, g_k[L:], 0.0))
        pw[c, h] = _bf(a_ab)
        inv[c, h] = eye + a_ab
    for _ in range(int(math.log2(L)) - 1):
        for ch in pairs:
            pw[ch] = _bf(_dot(pw[ch], pw[ch]))
        for ch in pairs:
            inv[ch] = inv[ch] + _dot(_bf(inv[ch]), pw[ch])
    akv = {(c, h): _bf(_dot(a_ak[c, h], vm[rsl(c), hsl(h)])) for c, h in pairs}
    x = {(c, h): _bf(_dot(_bf(inv[c, h]),
                          jnp.concatenate([at[rsl(c), hsl(h)], akv[c, h]], axis=1)))
         for c, h in pairs}
    r_eff, y0, m_mat, n_mat = {}, {}, {}, {}
    for c, h in pairs:
        p1 = _dot(a_rb[c, h], x[c, h])
        r_eff[c, h] = _bf(rt[rsl(c), hsl(h)] + p1[:, :HEAD_DIM])
        y0[c, h] = p1[:, HEAD_DIM:] + _dot(a_rk[c, h], vm[rsl(c), hsl(h)])
    for c, h in pairs:
        p2 = _dot_tn(x[c, h], bh[rsl(c), hsl(h)])
        m_mat[c, h] = _bf(p2[:HEAD_DIM])
        n_mat[c, h] = p2[HEAD_DIM:] + _dot_tn(vm[rsl(c), hsl(h)], kh[rsl(c), hsl(h)])
    state = [s_scr[h] for h in range(heads)]
    y_rows = []
    for c in range(n_chunks):
        e_tot = jnp.exp(tots[c])
        s_m = [_bf(s) for s in state]
        ys = [_dot_nt(r_eff[c, h], s_m[h]) + y0[c, h] for h in range(heads)]
        state = [state[h] * e_tot[:, hsl(h)] + _dot(s_m[h], m_mat[c, h]) + n_mat[c, h]
                 for h in range(heads)]
        y_rows.append(jnp.concatenate(ys, axis=1))
    for h in range(heads):
        s_scr[h] = state[h]
    y = jnp.concatenate(y_rows, axis=0)

    inv_n = 1.0 / HEAD_DIM
    mean = _segsum(y, bd) * inv_n
    d = y - mean
    var = _segsum(d * d, bd) * inv_n
    yn = d * lax.rsqrt(var + GN_EPS) * gng_ref[...] + gnb_ref[...]
    o_ref[0] = ((yn + bonus) * g).astype(o_ref.dtype)


def _rwkv(rw, w0, wdw, a0, waw, wgw, k_k, k_a, r_k, gn_g, gn_b, width, rows=256):
    B, T, cols = rw.shape
    row = lambda a: a.reshape(1, width)
    vec = _const_spec((1, width))
    return pl.pallas_call(
        functools.partial(_rwkv_kernel, width=width),
        out_shape=jax.ShapeDtypeStruct((B, T, width), BF16),
        grid=(B, T // rows),
        in_specs=[pl.BlockSpec((1, rows, cols), lambda b, t: (b, t, 0)),
                  vec, _const_spec(wdw.shape), vec, _const_spec(waw.shape),
                  _const_spec(wgw.shape), vec, vec, vec, vec, vec],
        out_specs=pl.BlockSpec((1, rows, width), lambda b, t: (b, t, 0)),
        scratch_shapes=[pltpu.VMEM((width // HEAD_DIM, HEAD_DIM, HEAD_DIM), F32)],
        compiler_params=pltpu.CompilerParams(
            dimension_semantics=("arbitrary", "arbitrary"), vmem_limit_bytes=VMEM_LIMIT),
        name="rwkv",
    )(rw, row(w0), wdw, row(a0), waw, wgw, row(k_k), row(k_a), row(r_k), row(gn_g), row(gn_b))


def _foxprep_kernel(fx_ref, fcum_ref, qg_ref, kg_ref, qa_ref, ka_ref, v_ref, *, width):
    W = width
    heads = W // HEAD_DIM
    pf = fx_ref[0]
    tm = pf.shape[0]
    fcum = fcum_ref[0] * LOG2E
    f_hi = fcum.astype(BF16).astype(F32)
    r1 = fcum - f_hi
    f_mid = r1.astype(BF16).astype(F32)
    f_lo = r1 - f_mid
    lane = lax.broadcasted_iota(jnp.int32, (tm, HEAD_DIM), 1)
    scale = LOG2E / math.sqrt(HEAD_DIM)
    for h in range(heads):
        q = pf[:, h * HEAD_DIM:(h + 1) * HEAD_DIM]
        k = pf[:, W + h * HEAD_DIM:W + (h + 1) * HEAD_DIM]
        qn = q * lax.rsqrt(jnp.mean(q * q, axis=-1, keepdims=True) + NORM_EPS) * (qg_ref[...] * scale)
        kn = k * lax.rsqrt(jnp.mean(k * k, axis=-1, keepdims=True) + NORM_EPS) * kg_ref[...]
        hi, mid, lo = f_hi[:, h:h + 1], f_mid[:, h:h + 1], f_lo[:, h:h + 1]
        aq = jnp.where(lane == 0, hi, jnp.where(lane == 1, mid, jnp.where(
            lane == 2, lo, jnp.where(lane < 6, 1.0, 0.0))))
        ak = jnp.where(lane < 3, 1.0, jnp.where(lane == 3, -hi, jnp.where(
            lane == 4, -mid, jnp.where(lane == 5, -lo, 0.0))))
        qa_ref[0, h] = jnp.concatenate([qn, aq], axis=1).astype(BF16)
        ka_ref[0, h] = jnp.concatenate([kn, ak], axis=1).astype(BF16)
    v_ref[0] = pf[:, 2 * W:3 * W].astype(BF16)


def _foxprep(fx, fcum, q_norm_g, k_norm_g, width, tm=512):
    B, T, cols = fx.shape
    heads = width // HEAD_DIM
    return pl.pallas_call(
        functools.partial(_foxprep_kernel, width=width),
        out_shape=(jax.ShapeDtypeStruct((B, heads, T, LANES), BF16),
                   jax.ShapeDtypeStruct((B, heads, T, LANES), BF16),
                   jax.ShapeDtypeStruct((B, T, width), BF16)),
        grid=(B, T // tm),
        in_specs=[pl.BlockSpec((1, tm, cols), lambda b, t: (b, t, 0)),
                  pl.BlockSpec((1, tm, LANES), lambda b, t: (b, t, 0)),
                  _const_spec((1, HEAD_DIM)), _const_spec((1, HEAD_DIM))],
        out_specs=(pl.BlockSpec((1, heads, tm, LANES), lambda b, t: (b, 0, t, 0)),
                   pl.BlockSpec((1, heads, tm, LANES), lambda b, t: (b, 0, t, 0)),
                   pl.BlockSpec((1, tm, width), lambda b, t: (b, t, 0))),
        compiler_params=pltpu.CompilerParams(
            dimension_semantics=("arbitrary", "arbitrary"), vmem_limit_bytes=VMEM_LIMIT),
        name="foxprep",
    )(fx, fcum, q_norm_g.reshape(1, HEAD_DIM), k_norm_g.reshape(1, HEAD_DIM))


def _fox_kernel(qa_ref, ka_ref, v_ref, og_ref, g_ref, o_ref, m_scr, l_scr, acc_scr, *, sub):
    qi = pl.program_id(2)
    tq = qa_ref.shape[2]
    tk = tq
    chains = [(rb, hh) for rb in range(tq // sub) for hh in range(2)]
    rsl = lambda rb: slice(rb * sub, (rb + 1) * sub)
    first = lax.broadcasted_iota(jnp.int32, (sub, LANES), 1) < HEAD_DIM
    ones = jnp.ones((tk, LANES), BF16)

    m_scr[...] = jnp.full_like(m_scr, NEG_INF)
    l_scr[...] = jnp.zeros_like(l_scr)
    acc_scr[...] = jnp.zeros_like(acc_scr)

    def block(k_start, masked):
        v_aug = jnp.concatenate([v_ref[0, pl.ds(k_start, tk), :], ones], axis=1)
        ks = [ka_ref[0, hh, pl.ds(k_start, tk), :] for hh in range(2)]
        s, m_new, alpha, p, pv = {}, {}, {}, {}, {}
        for rb, hh in chains:
            s_ = _dot_nt(qa_ref[0, hh, rsl(rb), :], ks[hh])
            if masked:
                row = rb * sub + lax.broadcasted_iota(jnp.int32, (sub, tk), 0)
                col = lax.broadcasted_iota(jnp.int32, (sub, tk), 1)
                s_ = jnp.where(col <= row, s_, NEG_INF)
            s[rb, hh] = s_
        for rb, hh in chains:
            m_prev = m_scr[hh, rsl(rb), :]
            m_new[rb, hh] = jnp.maximum(m_prev, jnp.max(s[rb, hh], axis=1, keepdims=True))
            alpha[rb, hh] = jnp.exp2(m_prev - m_new[rb, hh])
            m_scr[hh, rsl(rb), :] = m_new[rb, hh]
        for ch in chains:
            p[ch] = jnp.exp2(s[ch] - jnp.tile(m_new[ch], (1, tk // LANES))).astype(BF16)
        for ch in chains:
            pv[ch] = _dot(p[ch], v_aug)
        for rb in range(tq // sub):
            a2 = jnp.where(first, alpha[rb, 0], alpha[rb, 1])
            acc_scr[rsl(rb), :] = a2 * acc_scr[rsl(rb), :] + jnp.where(
                first, pv[rb, 0][:, :LANES], pv[rb, 1][:, :LANES])
            for hh in range(2):
                l_scr[hh, rsl(rb), :] = alpha[rb, hh] * l_scr[hh, rsl(rb), :] + pv[rb, hh][:, LANES:]

    def off_diagonal(i, carry):
        block(pl.multiple_of(i * tk, tk), masked=False)
        return carry

    lax.fori_loop(0, qi, off_diagonal, 0)
    block(pl.multiple_of(qi * tk, tk), masked=True)

    first_q = lax.broadcasted_iota(jnp.int32, (tq, LANES), 1) < HEAD_DIM
    o = acc_scr[...] / jnp.where(first_q, l_scr[0], l_scr[1])
    o2 = o * o
    s0 = jnp.sum(jnp.where(first_q, o2, 0.0), axis=1, keepdims=True)
    s1 = jnp.sum(jnp.where(first_q, 0.0, o2), axis=1, keepdims=True)
    ms = jnp.where(first_q, s0, s1) * (1.0 / HEAD_DIM)
    on = o * lax.rsqrt(ms + NORM_EPS) * g_ref[...]
    o_ref[0] = (on * _sigmoid(og_ref[0])).astype(o_ref.dtype)


def _fox(qa, ka, vb, fx, fox_out_g, width, tq=512, sub=128):
    B, H, T, _ = qa.shape
    og_blk = 3 * width // LANES
    return pl.pallas_call(
        functools.partial(_fox_kernel, sub=sub),
        out_shape=jax.ShapeDtypeStruct((B, T, width), BF16),
        grid=(B, H // 2, T // tq),
        in_specs=[pl.BlockSpec((1, 2, tq, LANES), lambda b, h, q: (b, h, q, 0)),
                  pl.BlockSpec((1, 2, T, LANES), lambda b, h, q: (b, h, 0, 0)),
                  pl.BlockSpec((1, T, LANES), lambda b, h, q: (b, 0, h)),
                  pl.BlockSpec((1, tq, LANES), lambda b, h, q: (b, q, og_blk + h)),
                  pl.BlockSpec((1, LANES), lambda b, h, q: (0, h))],
        out_specs=pl.BlockSpec((1, tq, LANES), lambda b, h, q: (b, q, h)),
        scratch_shapes=[pltpu.VMEM((2, tq, LANES), F32), pltpu.VMEM((2, tq, LANES), F32),
                        pltpu.VMEM((tq, LANES), F32)],
        compiler_params=pltpu.CompilerParams(
            dimension_semantics=("arbitrary",) * 3, vmem_limit_bytes=VMEM_LIMIT),
        name="fox",
    )(qa, ka, vb, fx, fox_out_g.reshape(1, width))


def _outmlp_kernel(x_ref, yrw_ref, yfx_ref, mod_ref, wo_ref, g2_ref, w1_ref, w2_ref, gf_ref,
                   o_ref, *, ff_tile):
    x = x_ref[0]
    half = yrw_ref.shape[2]
    gate1 = mod_ref[0, 2:3, :]
    shift2 = mod_ref[0, 3:4, :]
    scale2 = mod_ref[0, 4:5, :]
    gate2 = mod_ref[0, 5:6, :]
    y = _dot(yrw_ref[0], wo_ref[:half, :]) + _dot(yfx_ref[0], wo_ref[half:, :])
    h1 = x + gate1 * y
    ms = jnp.mean(h1 * h1, axis=-1, keepdims=True)
    u = ((h1 * lax.rsqrt(ms + NORM_EPS) * g2_ref[...]) * (1.0 + scale2) + shift2).astype(BF16)
    acc = jnp.zeros_like(x)
    for j in range(w1_ref.shape[1] // ff_tile):
        hid = jnp.maximum(_dot(u, w1_ref[:, j * ff_tile:(j + 1) * ff_tile]), 0.0)
        acc = acc + _dot((hid * hid).astype(BF16), w2_ref[j * ff_tile:(j + 1) * ff_tile, :])
    h2 = h1 + gate2 * acc
    ms2 = jnp.mean(h2 * h2, axis=-1, keepdims=True)
    o_ref[0] = h2 * lax.rsqrt(ms2 + NORM_EPS) * gf_ref[...]


def _outmlp(x, y_rw, y_fx, mod3, w_o, norm2_g, w1, w2, final_g, tm=512, ff_tile=1024):
    B, T, D = x.shape
    half = y_rw.shape[2]
    return pl.pallas_call(
        functools.partial(_outmlp_kernel, ff_tile=ff_tile),
        out_shape=jax.ShapeDtypeStruct((B, T, D), F32),
        grid=(B, T // tm),
        in_specs=[pl.BlockSpec((1, tm, D), lambda b, t: (b, t, 0)),
                  pl.BlockSpec((1, tm, half), lambda b, t: (b, t, 0)),
                  pl.BlockSpec((1, tm, half), lambda b, t: (b, t, 0)),
                  pl.BlockSpec((1, N_MOD, D), lambda b, t: (b, 0, 0)),
                  _const_spec(w_o.shape), _const_spec((1, D)),
                  _const_spec(w1.shape), _const_spec(w2.shape), _const_spec((1, D))],
        out_specs=pl.BlockSpec((1, tm, D), lambda b, t: (b, t, 0)),
        compiler_params=pltpu.CompilerParams(
            dimension_semantics=("arbitrary", "arbitrary"), vmem_limit_bytes=VMEM_LIMIT),
        name="outmlp",
    )(x, y_rw, y_fx, mod3, w_o, norm2_g.reshape(1, D), w1, w2, final_g.reshape(1, D))


def _pad_cols(a, n):
    return jnp.pad(a, ((0, 0), (0, n - a.shape[1])))


def _pad_rows(a, n):
    return jnp.pad(a, ((0, n - a.shape[0]), (0, 0)))


def _branches(x, c, w_ada, b_ada, norm1_g, w_in, mu_shift, w0, w_up_decay, a0, w_up_a, w_up_g,
              k_k, k_a, r_k, gn_g, gn_b, b_f, q_norm_g, k_norm_g, fox_out_g):
    B, T, D = x.shape
    W = w0.shape[0]
    heads = b_f.shape[0]
    rw_n = 3 * W + DECAY_LORA + AAA_LORA + GATE_LORA

    def regroup(a):
        rw, fx = a[:, :rw_n], a[:, rw_n:]
        o = 3 * W
        return jnp.concatenate([
            rw[:, :o],
            _pad_cols(rw[:, o:o + DECAY_LORA], LANES),
            _pad_cols(rw[:, o + DECAY_LORA:o + DECAY_LORA + AAA_LORA], LANES),
            _pad_cols(rw[:, o + DECAY_LORA + AAA_LORA:], 2 * LANES),
            fx[:, :4 * W],
            _pad_cols(fx[:, 4 * W:], LANES)], axis=1)

    rw_cols = 3 * W + 4 * LANES
    w_all = regroup(w_in).astype(BF16)
    mu_p = regroup(jnp.pad(mu_shift.reshape(1, rw_n), ((0, 0), (0, w_in.shape[1] - rw_n))))[:, :rw_cols]
    bf_p = _pad_cols(b_f.reshape(1, heads), LANES)
    wdw = _pad_rows(w_up_decay, LANES).astype(BF16)
    waw = _pad_rows(w_up_a, LANES).astype(BF16)
    wgw = _pad_rows(w_up_g, 2 * LANES).astype(BF16)

    mod3 = _ada(c, w_ada, b_ada).reshape(B, N_MOD, D)
    rw, fx, fcum = _inproj(x, mod3, norm1_g, w_all, mu_p, bf_p, rw_cols)
    y_rw = _rwkv(rw, w0, wdw, a0, waw, wgw, k_k, k_a, r_k.reshape(-1), gn_g, gn_b, W)
    qa, ka, vb = _foxprep(fx, fcum, q_norm_g, k_norm_g, W)
    y_fx = _fox(qa, ka, vb, fx, fox_out_g, W)
    return y_rw, y_fx, mod3


def kernel(x, c, w_ada, b_ada, norm1_g, w_in, mu_shift, w0, w_up_decay, a0, w_up_a, w_up_g,
           k_k, k_a, r_k, gn_g, gn_b, b_f, q_norm_g, k_norm_g, fox_out_g, w_o, norm2_g,
           w_mlp1, w_mlp2, final_g):
    y_rw, y_fx, mod3 = _branches(x, c, w_ada, b_ada, norm1_g, w_in, mu_shift, w0, w_up_decay,
                                 a0, w_up_a, w_up_g, k_k, k_a, r_k, gn_g, gn_b, b_f,
                                 q_norm_g, k_norm_g, fox_out_g)
    return _outmlp(x, y_rw, y_fx, mod3, w_o.astype(BF16), norm2_g,
                   w_mlp1.astype(BF16), w_mlp2.astype(BF16), final_g)
```

```python
import functools
import math

import jax
import jax.numpy as jnp
from jax import lax
from jax.experimental import pallas as pl
from jax.experimental.pallas import tpu as pltpu

F32 = jnp.float32
BF16 = jnp.bfloat16

HEAD_DIM = 64
LANES = 128
NORM_EPS = 1e-6
GN_EPS = 64e-5
NEG_INF = -1e30
LOG2E = 1.4426950408889634
N_MOD = 6
DECAY_LORA = 64
AAA_LORA = 64
GATE_LORA = 160
RW_CHUNK = 64
V_ROWS = HEAD_DIM + 16
VMEM_LIMIT = 56 * 1024 * 1024

_HI = lax.Precision.HIGHEST


def _dot(a, b):
    return jnp.dot(a, b, preferred_element_type=F32)


def _dot_hi(a, b):
    return jnp.dot(a, b, preferred_element_type=F32, precision=_HI)


def _dot_nt(a, b):
    return lax.dot_general(a, b, (((1,), (1,)), ((), ())), preferred_element_type=F32)


def _dot_tn(a, b):
    return lax.dot_general(a, b, (((0,), (0,)), ((), ())), preferred_element_type=F32)


def _bf(x):
    return x.astype(BF16)


def _sigmoid(x):
    return 1.0 / (1.0 + jnp.exp(-x))


def _const_spec(shape):
    n = len(shape)
    return pl.BlockSpec(shape, lambda *_: (0,) * n)


def _ada_kernel(c_ref, w_ref, b_ref, o_ref):
    c = c_ref[...]
    o_ref[...] = _dot_hi(c * _sigmoid(c), w_ref[...]) + b_ref[...]


def _ada(c, w_ada, b_ada):
    B, D = c.shape
    n = w_ada.shape[1]
    return pl.pallas_call(
        _ada_kernel,
        out_shape=jax.ShapeDtypeStruct((B, n), F32),
        grid=(n // D,),
        in_specs=[pl.BlockSpec((B, D), lambda j: (0, 0)),
                  pl.BlockSpec((D, D), lambda j: (0, j)),
                  pl.BlockSpec((1, D), lambda j: (0, j))],
        out_specs=pl.BlockSpec((B, D), lambda j: (0, j)),
        name="ada",
    )(c, w_ada, b_ada.reshape(1, n))


def _inproj_kernel(x_ref, mod_ref, g_ref, w_ref, mu_ref, bf_ref,
                   rw_ref, fx_ref, fcum_ref, prev_scr, fcar_scr, *, rw_cols):
    t = pl.program_id(1)
    tm = x_ref.shape[1]

    @pl.when(t == 0)
    def _():
        prev_scr[...] = jnp.zeros_like(prev_scr)
        fcar_scr[...] = jnp.zeros_like(fcar_scr)

    x = x_ref[0]
    shift = mod_ref[0, 0:1, :]
    scale = mod_ref[0, 1:2, :]
    ms = jnp.mean(x * x, axis=-1, keepdims=True)
    u = (x * lax.rsqrt(ms + NORM_EPS) * g_ref[...]) * (1.0 + scale) + shift
    u = u.astype(BF16)

    p = _dot(u, w_ref[:, :rw_cols])
    row = lax.broadcasted_iota(jnp.int32, p.shape, 0)
    prev = jnp.where(row == 0, prev_scr[0:1, :], pltpu.roll(p, shift=1, axis=0))
    prev_scr[0:1, :] = p[tm - 1:tm, :]
    rw_ref[0] = p + (prev - p) * mu_ref[...]

    pf = _dot(u, w_ref[:, rw_cols:])
    fx_ref[0] = pf
    z = pf[:, pf.shape[1] - LANES:] + bf_ref[...]
    logf = jnp.minimum(z, 0.0) - jnp.log(1.0 + jnp.exp(-jnp.abs(z)))
    r2 = lax.broadcasted_iota(jnp.int32, (tm, tm), 0)
    c2 = lax.broadcasted_iota(jnp.int32, (tm, tm), 1)
    tril = (c2 <= r2).astype(F32)
    fcum = _dot_hi(tril, logf) + fcar_scr[0:1, :]
    fcum_ref[0] = fcum
    fcar_scr[0:1, :] = fcum[tm - 1:tm, :]


def _inproj(x, mod3, norm1_g, w_all, mu_p, bf_p, rw_cols, tm=256):
    B, T, D = x.shape
    ncols = w_all.shape[1]
    fx_cols = ncols - rw_cols
    return pl.pallas_call(
        functools.partial(_inproj_kernel, rw_cols=rw_cols),
        out_shape=(jax.ShapeDtypeStruct((B, T, rw_cols), F32),
                   jax.ShapeDtypeStruct((B, T, fx_cols), F32),
                   jax.ShapeDtypeStruct((B, T, LANES), F32)),
        grid=(B, T // tm),
        in_specs=[pl.BlockSpec((1, tm, D), lambda b, t: (b, t, 0)),
                  pl.BlockSpec((1, N_MOD, D), lambda b, t: (b, 0, 0)),
                  _const_spec((1, D)),
                  _const_spec((D, ncols)),
                  _const_spec((1, rw_cols)),
                  _const_spec((1, LANES))],
        out_specs=(pl.BlockSpec((1, tm, rw_cols), lambda b, t: (b, t, 0)),
                   pl.BlockSpec((1, tm, fx_cols), lambda b, t: (b, t, 0)),
                   pl.BlockSpec((1, tm, LANES), lambda b, t: (b, t, 0))),
        scratch_shapes=[pltpu.VMEM((8, rw_cols), F32), pltpu.VMEM((8, LANES), F32)],
        compiler_params=pltpu.CompilerParams(
            dimension_semantics=("arbitrary", "arbitrary"), vmem_limit_bytes=VMEM_LIMIT),
        name="inproj",
    )(x, mod3, norm1_g.reshape(1, D), w_all, mu_p, bf_p)


def _segsum(x, bd):
    hi = x.astype(BF16)
    lo = (x - hi.astype(F32)).astype(BF16)
    return _dot(hi, bd) + _dot(lo, bd)


def _rwkv_kernel(p_ref, w0_ref, wdw_ref, a0_ref, waw_ref, wgw_ref, kk_ref, ka_ref, rk_ref,
                 gng_ref, gnb_ref, o_ref, s_scr, *, width):
    t = pl.program_id(1)
    rows = p_ref.shape[1]
    L = RW_CHUNK
    n_chunks = rows // L
    W = width
    heads = W // HEAD_DIM

    @pl.when(t == 0)
    def _():
        s_scr[...] = jnp.zeros_like(s_scr)

    p = p_ref[0]
    r = p[:, 0:W]
    k = p[:, W:2 * W]
    v = p[:, 2 * W:3 * W]
    o0 = 3 * W
    wd = p[:, o0:o0 + LANES]
    ad = p[:, o0 + LANES:o0 + 2 * LANES]
    gd = p[:, o0 + 2 * LANES:o0 + 4 * LANES]

    z = w0_ref[...] + _dot(_bf(jnp.tanh(wd)), wdw_ref[...])
    softplus = jnp.maximum(-z, 0.0) + jnp.log(1.0 + jnp.exp(-jnp.abs(z)))
    ld = -jnp.exp(-softplus - 0.5)
    a_sig = _sigmoid(a0_ref[...] + _dot(_bf(ad), waw_ref[...]))
    g = _dot(_bf(_sigmoid(gd)), wgw_ref[...])

    ri = lax.broadcasted_iota(jnp.int32, (W, W), 0)
    ci = lax.broadcasted_iota(jnp.int32, (W, W), 1)
    bd = ((ri // HEAD_DIM) == (ci // HEAD_DIM)).astype(BF16)

    kk = k * kk_ref[...]
    kk = kk / jnp.maximum(jnp.sqrt(_segsum(kk * kk, bd)), 1e-12)
    k2 = k * (1.0 + (a_sig - 1.0) * ka_ref[...])
    b = kk * a_sig
    bonus = _segsum(r * k2 * rk_ref[...], bd) * v

    rl = lax.broadcasted_iota(jnp.int32, (L, L), 0)
    cl = lax.broadcasted_iota(jnp.int32, (L, L), 1)
    incl = cl <= rl
    strict = cl < rl
    tril = incl.astype(BF16)
    eye = (cl == rl).astype(F32)
    ld_hi = _bf(ld)
    ld_lo = _bf(ld - ld_hi.astype(F32))
    cums, tots = [], []
    for c in range(n_chunks):
        rs = slice(c * L, (c + 1) * L)
        cum_c = _dot(tril, ld_hi[rs]) + _dot(tril, ld_lo[rs])
        cums.append(cum_c)
        tots.append(cum_c[L - 1:L, :])
    cum = jnp.concatenate(cums, axis=0)
    tot_b = jnp.concatenate([jnp.broadcast_to(tc, (L, W)) for tc in tots], axis=0)

    e_neg = jnp.exp(-cum)
    e_rem = jnp.exp(tot_b - cum)
    rt = r * jnp.exp(cum)
    rt_m = _bf(rt)
    at = _bf(-kk * jnp.exp(cum - ld))
    bt = _bf(b * e_neg)
    kt = _bf(k2 * e_neg)
    bh = _bf(b * e_rem)
    kh = _bf(k2 * e_rem)
    vm = _bf(v)

    pairs = [(c, h) for c in range(n_chunks) for h in range(heads)]
    rsl = lambda c: slice(c * L, (c + 1) * L)
    hsl = lambda h: slice(h * HEAD_DIM, (h + 1) * HEAD_DIM)
    ar = [jnp.concatenate([at[rsl(c)], rt_m[rsl(c)]], axis=0) for c in range(n_chunks)]
    a_rb, a_rk, a_ak, pw, inv = {}, {}, {}, {}, {}
    for c, h in pairs:
        g_b = _dot_nt(ar[c][:, hsl(h)], bt[rsl(c), hsl(h)])
        g_k = _dot_nt(ar[c][:, hsl(h)], kt[rsl(c), hsl(h)])
        a_ab = jnp.where(strict, g_b[:L], 0.0)
        a_ak[c, h] = _bf(jnp.where(strict, g_k[:L], 0.0))
        a_rb[c, h] = _bf(jnp.where(incl, g_b[L:], 0.0))
        a_rk[c, h] = _bf(jnp.where(incl, g_k[L:], 0.0))
        pw[c, h] = _bf(a_ab)
        inv[c, h] = eye + a_ab
    for _ in range(int(math.log2(L)) - 1):
        for ch in pairs:
            pw[ch] = _bf(_dot(pw[ch], pw[ch]))
        for ch in pairs:
            inv[ch] = inv[ch] + _dot(_bf(inv[ch]), pw[ch])
    akv = {(c, h): _bf(_dot(a_ak[c, h], vm[rsl(c), hsl(h)])) for c, h in pairs}
    x = {(c, h): _bf(_dot(_bf(inv[c, h]),
                          jnp.concatenate([at[rsl(c), hsl(h)], akv[c, h]], axis=1)))
         for c, h in pairs}
    r_eff, y0, m_mat, n_mat = {}, {}, {}, {}
    for c, h in pairs:
        p1 = _dot(a_rb[c, h], x[c, h])
        r_eff[c, h] = _bf(rt[rsl(c), hsl(h)] + p1[:, :HEAD_DIM])
        y0[c, h] = p1[:, HEAD_DIM:] + _dot(a_rk[c, h], vm[rsl(c), hsl(h)])
    for c, h in pairs:
        p2 = _dot_tn(x[c, h], bh[rsl(c), hsl(h)])
        m_mat[c, h] = _bf(p2[:HEAD_DIM])
        n_mat[c, h] = p2[HEAD_DIM:] + _dot_tn(vm[rsl(c), hsl(h)], kh[rsl(c), hsl(h)])
    state = [s_scr[h] for h in range(heads)]
    y_rows = []
    for c in range(n_chunks):
        e_tot = jnp.exp(tots[c])
        s_m = [_bf(s) for s in state]
        ys = [_dot_nt(r_eff[c, h], s_m[h]) + y0[c, h] for h in range(heads)]
        state = [state[h] * e_tot[:, hsl(h)] + _dot(s_m[h], m_mat[c, h]) + n_mat[c, h]
                 for h in range(heads)]
        y_rows.append(jnp.concatenate(ys, axis=1))
    for h in range(heads):
        s_scr[h] = state[h]
    y = jnp.concatenate(y_rows, axis=0)

    inv_n = 1.0 / HEAD_DIM
    mean = _segsum(y, bd) * inv_n
    d = y - mean
    var = _segsum(d * d, bd) * inv_n
    yn = d * lax.rsqrt(var + GN_EPS) * gng_ref[...] + gnb_ref[...]
    o_ref[0] = ((yn + bonus) * g).astype(o_ref.dtype)


def _rwkv(rw, w0, wdw, a0, waw, wgw, k_k, k_a, r_k, gn_g, gn_b, width, rows=256):
    B, T, cols = rw.shape
    row = lambda a: a.reshape(1, width)
    vec = _const_spec((1, width))
    return pl.pallas_call(
        functools.partial(_rwkv_kernel, width=width),
        out_shape=jax.ShapeDtypeStruct((B, T, width), BF16),
        grid=(B, T // rows),
        in_specs=[pl.BlockSpec((1, rows, cols), lambda b, t: (b, t, 0)),
                  vec, _const_spec(wdw.shape), vec, _const_spec(waw.shape),
                  _const_spec(wgw.shape), vec, vec, vec, vec, vec],
        out_specs=pl.BlockSpec((1, rows, width), lambda b, t: (b, t, 0)),
        scratch_shapes=[pltpu.VMEM((width // HEAD_DIM, HEAD_DIM, HEAD_DIM), F32)],
        compiler_params=pltpu.CompilerParams(
            dimension_semantics=("arbitrary", "arbitrary"), vmem_limit_bytes=VMEM_LIMIT),
        name="rwkv",
    )(rw, row(w0), wdw, row(a0), waw, wgw, row(k_k), row(k_a), row(r_k), row(gn_g), row(gn_b))


def _foxprep_kernel(fx_ref, fcum_ref, qg_ref, kg_ref, qa_ref, ka_ref, v_ref, *, width):
    W = width
    heads = W // HEAD_DIM
    pf = fx_ref[0]
    tm = pf.shape[0]
    fcum = fcum_ref[0] * LOG2E
    f_hi = fcum.astype(BF16).astype(F32)
    r1 = fcum - f_hi
    f_mid = r1.astype(BF16).astype(F32)
    f_lo = r1 - f_mid
    lane = lax.broadcasted_iota(jnp.int32, (tm, HEAD_DIM), 1)
    scale = LOG2E / math.sqrt(HEAD_DIM)
    for h in range(heads):
        q = pf[:, h * HEAD_DIM:(h + 1) * HEAD_DIM]
        k = pf[:, W + h * HEAD_DIM:W + (h + 1) * HEAD_DIM]
        qn = q * lax.rsqrt(jnp.mean(q * q, axis=-1, keepdims=True) + NORM_EPS) * (qg_ref[...] * scale)
        kn = k * lax.rsqrt(jnp.mean(k * k, axis=-1, keepdims=True) + NORM_EPS) * kg_ref[...]
        hi, mid, lo = f_hi[:, h:h + 1], f_mid[:, h:h + 1], f_lo[:, h:h + 1]
        aq = jnp.where(lane == 0, hi, jnp.where(lane == 1, mid, jnp.where(
            lane == 2, lo, jnp.where(lane < 6, 1.0, 0.0))))
        ak = jnp.where(lane < 3, 1.0, jnp.where(lane == 3, -hi, jnp.where(
            lane == 4, -mid, jnp.where(lane == 5, -lo, 0.0))))
        qa_ref[0, h] = jnp.concatenate([qn, aq], axis=1).astype(BF16)
        ka_ref[0, h] = jnp.concatenate([kn, ak], axis=1).astype(BF16)
    pad = (lax.broadcasted_iota(jnp.int32, (V_ROWS - HEAD_DIM, tm), 0) == 0).astype(BF16)
    for g in range(W // LANES):
        vt = pf[:, 2 * W + g * LANES:2 * W + (g + 1) * LANES].T.astype(BF16)
        for hh in range(2):
            v_ref[0, 2 * g + hh, 0, :HEAD_DIM, :] = vt[hh * HEAD_DIM:(hh + 1) * HEAD_DIM]
            v_ref[0, 2 * g + hh, 0, HEAD_DIM:, :] = pad


def _foxprep(fx, fcum, q_norm_g, k_norm_g, width, tm=512):
    B, T, cols = fx.shape
    heads = width // HEAD_DIM
    return pl.pallas_call(
        functools.partial(_foxprep_kernel, width=width),
        out_shape=(jax.ShapeDtypeStruct((B, heads, T, LANES), BF16),
                   jax.ShapeDtypeStruct((B, heads, T, LANES), BF16),
                   jax.ShapeDtypeStruct((B, heads, T // tm, V_ROWS, tm), BF16)),
        grid=(B, T // tm),
        in_specs=[pl.BlockSpec((1, tm, cols), lambda b, t: (b, t, 0)),
                  pl.BlockSpec((1, tm, LANES), lambda b, t: (b, t, 0)),
                  _const_spec((1, HEAD_DIM)), _const_spec((1, HEAD_DIM))],
        out_specs=(pl.BlockSpec((1, heads, tm, LANES), lambda b, t: (b, 0, t, 0)),
                   pl.BlockSpec((1, heads, tm, LANES), lambda b, t: (b, 0, t, 0)),
                   pl.BlockSpec((1, heads, 1, V_ROWS, tm), lambda b, t: (b, 0, t, 0, 0))),
        compiler_params=pltpu.CompilerParams(
            dimension_semantics=("arbitrary", "arbitrary"), vmem_limit_bytes=VMEM_LIMIT),
        name="foxprep",
    )(fx, fcum, q_norm_g.reshape(1, HEAD_DIM), k_norm_g.reshape(1, HEAD_DIM))


def _fox_kernel(qa_ref, ka_ref, vt_ref, og_ref, g_ref, o_ref, m_scr, acc_scr, *, cb):
    qi = pl.program_id(2)
    tq = qa_ref.shape[2]
    tk = tq
    chains = [(hh, c) for hh in range(2) for c in range(tq // cb)]
    csl = lambda c: slice(c * cb, (c + 1) * cb)

    m_scr[...] = jnp.full_like(m_scr, NEG_INF)
    acc_scr[...] = jnp.zeros_like(acc_scr)

    def block(i, masked):
        k_start = pl.multiple_of(i * tk, tk)
        ks = [ka_ref[0, hh, pl.ds(k_start, tk), :] for hh in range(2)]
        vts = [vt_ref[0, hh, i] for hh in range(2)]
        s, m_new, alpha, p = {}, {}, {}, {}
        for hh, c in chains:
            s_ = _dot_nt(ks[hh], qa_ref[0, hh, csl(c), :])
            if masked:
                key = lax.broadcasted_iota(jnp.int32, (tk, cb), 0)
                qry = c * cb + lax.broadcasted_iota(jnp.int32, (tk, cb), 1)
                s_ = jnp.where(key <= qry, s_, NEG_INF)
            s[hh, c] = s_
        for hh, c in chains:
            m_prev = m_scr[hh, :, csl(c)]
            m_new[hh, c] = jnp.maximum(m_prev, jnp.max(s[hh, c], axis=0, keepdims=True))
            alpha[hh, c] = jnp.exp2(m_prev - m_new[hh, c])
            m_scr[hh, :, csl(c)] = m_new[hh, c]
        for ch in chains:
            p[ch] = jnp.exp2(s[ch] - m_new[ch]).astype(BF16)
        for hh, c in chains:
            acc_scr[hh, :, csl(c)] = alpha[hh, c] * acc_scr[hh, :, csl(c)] + _dot(vts[hh], p[hh, c])

    def off_diagonal(i, carry):
        block(i, masked=False)
        return carry

    lax.fori_loop(0, qi, off_diagonal, 0)
    block(qi, masked=True)

    normed = []
    for hh in range(2):
        acc = acc_scr[hh]
        o = acc[:HEAD_DIM] / acc[HEAD_DIM:HEAD_DIM + 1]
        ms = jnp.mean(o * o, axis=0, keepdims=True)
        normed.append(o * lax.rsqrt(ms + NORM_EPS))
    on = jnp.concatenate(normed, axis=0).T
    o_ref[0] = (on * g_ref[...] * _sigmoid(og_ref[0])).astype(o_ref.dtype)


def _fox(qa, ka, vt, fx, fox_out_g, width, cb=256):
    B, H, T, _ = qa.shape
    tq = vt.shape[4]
    og_blk = 3 * width // LANES
    return pl.pallas_call(
        functools.partial(_fox_kernel, cb=cb),
        out_shape=jax.ShapeDtypeStruct((B, T, width), BF16),
        grid=(B, H // 2, T // tq),
        in_specs=[pl.BlockSpec((1, 2, tq, LANES), lambda b, h, q: (b, h, q, 0)),
                  pl.BlockSpec((1, 2, T, LANES), lambda b, h, q: (b, h, 0, 0)),
                  pl.BlockSpec((1, 2, T // tq, V_ROWS, tq), lambda b, h, q: (b, h, 0, 0, 0)),
                  pl.BlockSpec((1, tq, LANES), lambda b, h, q: (b, q, og_blk + h)),
                  pl.BlockSpec((1, LANES), lambda b, h, q: (0, h))],
        out_specs=pl.BlockSpec((1, tq, LANES), lambda b, h, q: (b, q, h)),
        scratch_shapes=[pltpu.VMEM((2, 1, tq), F32), pltpu.VMEM((2, V_ROWS, tq), F32)],
        compiler_params=pltpu.CompilerParams(
            dimension_semantics=("arbitrary",) * 3, vmem_limit_bytes=VMEM_LIMIT),
        name="fox",
    )(qa, ka, vt, fx, fox_out_g.reshape(1, width))


def _outmlp_kernel(x_ref, yrw_ref, yfx_ref, mod_ref, wo_ref, g2_ref, w1_ref, w2_ref, gf_ref,
                   o_ref, *, ff_tile):
    x = x_ref[0]
    half = yrw_ref.shape[2]
    gate1 = mod_ref[0, 2:3, :]
    shift2 = mod_ref[0, 3:4, :]
    scale2 = mod_ref[0, 4:5, :]
    gate2 = mod_ref[0, 5:6, :]
    y = _dot(yrw_ref[0], wo_ref[:half, :]) + _dot(yfx_ref[0], wo_ref[half:, :])
    h1 = x + gate1 * y
    ms = jnp.mean(h1 * h1, axis=-1, keepdims=True)
    u = ((h1 * lax.rsqrt(ms + NORM_EPS) * g2_ref[...]) * (1.0 + scale2) + shift2).astype(BF16)
    acc = jnp.zeros_like(x)
    for j in range(w1_ref.shape[1] // ff_tile):
        hid = jnp.maximum(_dot(u, w1_ref[:, j * ff_tile:(j + 1) * ff_tile]), 0.0)
        acc = acc + _dot((hid * hid).astype(BF16), w2_ref[j * ff_tile:(j + 1) * ff_tile, :])
    h2 = h1 + gate2 * acc
    ms2 = jnp.mean(h2 * h2, axis=-1, keepdims=True)
    o_ref[0] = h2 * lax.rsqrt(ms2 + NORM_EPS) * gf_ref[...]


def _outmlp(x, y_rw, y_fx, mod3, w_o, norm2_g, w1, w2, final_g, tm=512, ff_tile=1024):
    B, T, D = x.shape
    half = y_rw.shape[2]
    return pl.pallas_call(
        functools.partial(_outmlp_kernel, ff_tile=ff_tile),
        out_shape=jax.ShapeDtypeStruct((B, T, D), F32),
        grid=(B, T // tm),
        in_specs=[pl.BlockSpec((1, tm, D), lambda b, t: (b, t, 0)),
                  pl.BlockSpec((1, tm, half), lambda b, t: (b, t, 0)),
                  pl.BlockSpec((1, tm, half), lambda b, t: (b, t, 0)),
                  pl.BlockSpec((1, N_MOD, D), lambda b, t: (b, 0, 0)),
                  _const_spec(w_o.shape), _const_spec((1, D)),
                  _const_spec(w1.shape), _const_spec(w2.shape), _const_spec((1, D))],
        out_specs=pl.BlockSpec((1, tm, D), lambda b, t: (b, t, 0)),
        compiler_params=pltpu.CompilerParams(
            dimension_semantics=("arbitrary", "arbitrary"), vmem_limit_bytes=VMEM_LIMIT),
        name="outmlp",
    )(x, y_rw, y_fx, mod3, w_o, norm2_g.reshape(1, D), w1, w2, final_g.reshape(1, D))


def _pad_cols(a, n):
    return jnp.pad(a, ((0, 0), (0, n - a.shape[1])))


def _pad_rows(a, n):
    return jnp.pad(a, ((0, n - a.shape[0]), (0, 0)))


def _branches(x, c, w_ada, b_ada, norm1_g, w_in, mu_shift, w0, w_up_decay, a0, w_up_a, w_up_g,
              k_k, k_a, r_k, gn_g, gn_b, b_f, q_norm_g, k_norm_g, fox_out_g):
    B, T, D = x.shape
    W = w0.shape[0]
    heads = b_f.shape[0]
    rw_n = 3 * W + DECAY_LORA + AAA_LORA + GATE_LORA

    def regroup(a):
        rw, fx = a[:, :rw_n], a[:, rw_n:]
        o = 3 * W
        return jnp.concatenate([
            rw[:, :o],
            _pad_cols(rw[:, o:o + DECAY_LORA], LANES),
            _pad_cols(rw[:, o + DECAY_LORA:o + DECAY_LORA + AAA_LORA], LANES),
            _pad_cols(rw[:, o + DECAY_LORA + AAA_LORA:], 2 * LANES),
            fx[:, :4 * W],
            _pad_cols(fx[:, 4 * W:], LANES)], axis=1)

    rw_cols = 3 * W + 4 * LANES
    w_all = regroup(w_in).astype(BF16)
    mu_p = regroup(jnp.pad(mu_shift.reshape(1, rw_n), ((0, 0), (0, w_in.shape[1] - rw_n))))[:, :rw_cols]
    bf_p = _pad_cols(b_f.reshape(1, heads), LANES)
    wdw = _pad_rows(w_up_decay, LANES).astype(BF16)
    waw = _pad_rows(w_up_a, LANES).astype(BF16)
    wgw = _pad_rows(w_up_g, 2 * LANES).astype(BF16)

    mod3 = _ada(c, w_ada, b_ada).reshape(B, N_MOD, D)
    rw, fx, fcum = _inproj(x, mod3, norm1_g, w_all, mu_p, bf_p, rw_cols)
    y_rw = _rwkv(rw, w0, wdw, a0, waw, wgw, k_k, k_a, r_k.reshape(-1), gn_g, gn_b, W)
    qa, ka, vt = _foxprep(fx, fcum, q_norm_g, k_norm_g, W)
    y_fx = _fox(qa, ka, vt, fx, fox_out_g, W)
    return y_rw, y_fx, mod3


def kernel(x, c, w_ada, b_ada, norm1_g, w_in, mu_shift, w0, w_up_decay, a0, w_up_a, w_up_g,
           k_k, k_a, r_k, gn_g, gn_b, b_f, q_norm_g, k_norm_g, fox_out_g, w_o, norm2_g,
           w_mlp1, w_mlp2, final_g):
    y_rw, y_fx, mod3 = _branches(x, c, w_ada, b_ada, norm1_g, w_in, mu_shift, w0, w_up_decay,
                                 a0, w_up_a, w_up_g, k_k, k_a, r_k, gn_g, gn_b, b_f,
                                 q_norm_g, k_norm_g, fox_out_g)
    return _outmlp(x, y_rw, y_fx, mod3, w_o.astype(BF16), norm2_g,
                   w_mlp1.astype(BF16), w_mlp2.astype(BF16), final_g)
```

```python
import functools
import math

import jax
import jax.numpy as jnp
from jax import lax
from jax.experimental import pallas as pl
from jax.experimental.pallas import tpu as pltpu

F32 = jnp.float32
BF16 = jnp.bfloat16

HEAD_DIM = 64
LANES = 128
NORM_EPS = 1e-6
GN_EPS = 64e-5
NEG_INF = -1e30
LOG2E = 1.4426950408889634
N_MOD = 6
DECAY_LORA = 64
AAA_LORA = 64
GATE_LORA = 160
RW_CHUNK = 64
V_ROWS = HEAD_DIM + 16
VMEM_LIMIT = 56 * 1024 * 1024

_HI = lax.Precision.HIGHEST


def _dot(a, b):
    return jnp.dot(a, b, preferred_element_type=F32)


def _dot_hi(a, b):
    return jnp.dot(a, b, preferred_element_type=F32, precision=_HI)


def _dot_nt(a, b):
    return lax.dot_general(a, b, (((1,), (1,)), ((), ())), preferred_element_type=F32)


def _dot_tn(a, b):
    return lax.dot_general(a, b, (((0,), (0,)), ((), ())), preferred_element_type=F32)


def _bf(x):
    return x.astype(BF16)


def _sigmoid(x):
    return 1.0 / (1.0 + jnp.exp(-x))


def _const_spec(shape):
    n = len(shape)
    return pl.BlockSpec(shape, lambda *_: (0,) * n)


def _ada_kernel(c_ref, w_ref, b_ref, o_ref):
    c = c_ref[...]
    o_ref[...] = _dot_hi(c * _sigmoid(c), w_ref[...]) + b_ref[...]


def _ada(c, w_ada, b_ada):
    B, D = c.shape
    n = w_ada.shape[1]
    return pl.pallas_call(
        _ada_kernel,
        out_shape=jax.ShapeDtypeStruct((B, n), F32),
        grid=(n // D,),
        in_specs=[pl.BlockSpec((B, D), lambda j: (0, 0)),
                  pl.BlockSpec((D, D), lambda j: (0, j)),
                  pl.BlockSpec((1, D), lambda j: (0, j))],
        out_specs=pl.BlockSpec((B, D), lambda j: (0, j)),
        name="ada",
    )(c, w_ada, b_ada.reshape(1, n))


def _inproj_kernel(x_ref, mod_ref, g_ref, w_ref, mu_ref, bf_ref,
                   rw_ref, fx_ref, fcum_ref, prev_scr, fcar_scr, *, rw_cols):
    t = pl.program_id(1)
    tm = x_ref.shape[1]

    @pl.when(t == 0)
    def _():
        prev_scr[...] = jnp.zeros_like(prev_scr)
        fcar_scr[...] = jnp.zeros_like(fcar_scr)

    x = x_ref[0]
    shift = mod_ref[0, 0:1, :]
    scale = mod_ref[0, 1:2, :]
    ms = jnp.mean(x * x, axis=-1, keepdims=True)
    u = (x * lax.rsqrt(ms + NORM_EPS) * g_ref[...]) * (1.0 + scale) + shift
    u = u.astype(BF16)

    p = _dot(u, w_ref[:, :rw_cols])
    row = lax.broadcasted_iota(jnp.int32, p.shape, 0)
    prev = jnp.where(row == 0, prev_scr[0:1, :], pltpu.roll(p, shift=1, axis=0))
    prev_scr[0:1, :] = p[tm - 1:tm, :]
    rw_ref[0] = p + (prev - p) * mu_ref[...]

    pf = _dot(u, w_ref[:, rw_cols:])
    fx_ref[0] = pf
    z = pf[:, pf.shape[1] - LANES:] + bf_ref[...]
    logf = jnp.minimum(z, 0.0) - jnp.log(1.0 + jnp.exp(-jnp.abs(z)))
    r2 = lax.broadcasted_iota(jnp.int32, (tm, tm), 0)
    c2 = lax.broadcasted_iota(jnp.int32, (tm, tm), 1)
    tril = (c2 <= r2).astype(F32)
    fcum = _dot_hi(tril, logf) + fcar_scr[0:1, :]
    fcum_ref[0] = fcum
    fcar_scr[0:1, :] = fcum[tm - 1:tm, :]


def _inproj(x, mod3, norm1_g, w_all, mu_p, bf_p, rw_cols, tm=256):
    B, T, D = x.shape
    ncols = w_all.shape[1]
    fx_cols = ncols - rw_cols
    return pl.pallas_call(
        functools.partial(_inproj_kernel, rw_cols=rw_cols),
        out_shape=(jax.ShapeDtypeStruct((B, T, rw_cols), F32),
                   jax.ShapeDtypeStruct((B, T, fx_cols), F32),
                   jax.ShapeDtypeStruct((B, T, LANES), F32)),
        grid=(B, T // tm),
        in_specs=[pl.BlockSpec((1, tm, D), lambda b, t: (b, t, 0)),
                  pl.BlockSpec((1, N_MOD, D), lambda b, t: (b, 0, 0)),
                  _const_spec((1, D)),
                  _const_spec((D, ncols)),
                  _const_spec((1, rw_cols)),
                  _const_spec((1, LANES))],
        out_specs=(pl.BlockSpec((1, tm, rw_cols), lambda b, t: (b, t, 0)),
                   pl.BlockSpec((1, tm, fx_cols), lambda b, t: (b, t, 0)),
                   pl.BlockSpec((1, tm, LANES), lambda b, t: (b, t, 0))),
        scratch_shapes=[pltpu.VMEM((8, rw_cols), F32), pltpu.VMEM((8, LANES), F32)],
        compiler_params=pltpu.CompilerParams(
            dimension_semantics=("arbitrary", "arbitrary"), vmem_limit_bytes=VMEM_LIMIT),
        name="inproj",
    )(x, mod3, norm1_g.reshape(1, D), w_all, mu_p, bf_p)


def _segsum(x, bd):
    hi = x.astype(BF16)
    lo = (x - hi.astype(F32)).astype(BF16)
    return _dot(hi, bd) + _dot(lo, bd)


def _rwkv_kernel(p_ref, w0_ref, wdw_ref, a0_ref, waw_ref, wgw_ref, kk_ref, ka_ref, rk_ref,
                 gng_ref, gnb_ref, o_ref, s_scr, *, width):
    t = pl.program_id(1)
    rows = p_ref.shape[1]
    L = RW_CHUNK
    n_chunks = rows // L
    W = width
    heads = W // HEAD_DIM

    @pl.when(t == 0)
    def _():
        s_scr[...] = jnp.zeros_like(s_scr)

    p = p_ref[0]
    r = p[:, 0:W]
    k = p[:, W:2 * W]
    v = p[:, 2 * W:3 * W]
    o0 = 3 * W
    wd = p[:, o0:o0 + LANES]
    ad = p[:, o0 + LANES:o0 + 2 * LANES]
    gd = p[:, o0 + 2 * LANES:o0 + 4 * LANES]

    z = w0_ref[...] + _dot(_bf(jnp.tanh(wd)), wdw_ref[...])
    softplus = jnp.maximum(-z, 0.0) + jnp.log(1.0 + jnp.exp(-jnp.abs(z)))
    ld = -jnp.exp(-softplus - 0.5)
    a_sig = _sigmoid(a0_ref[...] + _dot(_bf(ad), waw_ref[...]))
    g = _dot(_bf(_sigmoid(gd)), wgw_ref[...])

    ri = lax.broadcasted_iota(jnp.int32, (W, W), 0)
    ci = lax.broadcasted_iota(jnp.int32, (W, W), 1)
    bd = ((ri // HEAD_DIM) == (ci // HEAD_DIM)).astype(BF16)

    kk = k * kk_ref[...]
    kk = kk / jnp.maximum(jnp.sqrt(_segsum(kk * kk, bd)), 1e-12)
    k2 = k * (1.0 + (a_sig - 1.0) * ka_ref[...])
    b = kk * a_sig
    bonus = _segsum(r * k2 * rk_ref[...], bd) * v

    rl = lax.broadcasted_iota(jnp.int32, (L, L), 0)
    cl = lax.broadcasted_iota(jnp.int32, (L, L), 1)
    incl = cl <= rl
    strict = cl < rl
    tril = incl.astype(BF16)
    eye = (cl == rl).astype(F32)
    ld_hi = _bf(ld)
    ld_lo = _bf(ld - ld_hi.astype(F32))
    cums, tots = [], []
    for c in range(n_chunks):
        rs = slice(c * L, (c + 1) * L)
        cum_c = _dot(tril, ld_hi[rs]) + _dot(tril, ld_lo[rs])
        cums.append(cum_c)
        tots.append(cum_c[L - 1:L, :])
    cum = jnp.concatenate(cums, axis=0)
    tot_b = jnp.concatenate([jnp.broadcast_to(tc, (L, W)) for tc in tots], axis=0)

    e_neg = jnp.exp(-cum)
    e_rem = jnp.exp(tot_b - cum)
    rt = r * jnp.exp(cum)
    rt_m = _bf(rt)
    at = _bf(-kk * jnp.exp(cum - ld))
    bt = _bf(b * e_neg)
    kt = _bf(k2 * e_neg)
    bh = _bf(b * e_rem)
    kh = _bf(k2 * e_rem)
    vm = _bf(v)

    pairs = [(c, h) for c in range(n_chunks) for h in range(heads)]
    rsl = lambda c: slice(c * L, (c + 1) * L)
    hsl = lambda h: slice(h * HEAD_DIM, (h + 1) * HEAD_DIM)
    ar = [jnp.concatenate([at[rsl(c)], rt_m[rsl(c)]], axis=0) for c in range(n_chunks)]
    a_rb, a_rk, a_ak, pw, inv = {}, {}, {}, {}, {}
    for c, h in pairs:
        g_b = _dot_nt(ar[c][:, hsl(h)], bt[rsl(c), hsl(h)])
        g_k = _dot_nt(ar[c][:, hsl(h)], kt[rsl(c), hsl(h)])
        a_ab = jnp.where(strict, g_b[:L], 0.0)
        a_ak[c, h] = _bf(jnp.where(strict, g_k[:L], 0.0))
        a_rb[c, h] = _bf(jnp.where(incl, g_b[L:], 0.0))
        a_rk[c, h] = _bf(jnp.where(incl, g_k[L:], 0.0))
        pw[c, h] = _bf(a_ab)
        inv[c, h] = eye + a_ab
    for _ in range(int(math.log2(L)) - 1):
        for ch in pairs:
            pw[ch] = _bf(_dot(pw[ch], pw[ch]))
        for ch in pairs:
            inv[ch] = inv[ch] + _dot(_bf(inv[ch]), pw[ch])
    akv = {(c, h): _bf(_dot(a_ak[c, h], vm[rsl(c), hsl(h)])) for c, h in pairs}
    x = {(c, h): _bf(_dot(_bf(inv[c, h]),
                          jnp.concatenate([at[rsl(c), hsl(h)], akv[c, h]], axis=1)))
         for c, h in pairs}
    r_eff, y0, m_mat, n_mat = {}, {}, {}, {}
    for c, h in pairs:
        p1 = _dot(a_rb[c, h], x[c, h])
        r_eff[c, h] = _bf(rt[rsl(c), hsl(h)] + p1[:, :HEAD_DIM])
        y0[c, h] = p1[:, HEAD_DIM:] + _dot(a_rk[c, h], vm[rsl(c), hsl(h)])
    for c, h in pairs:
        p2 = _dot_tn(x[c, h], bh[rsl(c), hsl(h)])
        m_mat[c, h] = _bf(p2[:HEAD_DIM])
        n_mat[c, h] = p2[HEAD_DIM:] + _dot_tn(vm[rsl(c), hsl(h)], kh[rsl(c), hsl(h)])
    state = [s_scr[h] for h in range(heads)]
    y_rows = []
    for c in range(n_chunks):
        e_tot = jnp.exp(tots[c])
        s_m = [_bf(s) for s in state]
        ys = [_dot_nt(r_eff[c, h], s_m[h]) + y0[c, h] for h in range(heads)]
        state = [state[h] * e_tot[:, hsl(h)] + _dot(s_m[h], m_mat[c, h]) + n_mat[c, h]
                 for h in range(heads)]
        y_rows.append(jnp.concatenate(ys, axis=1))
    for h in range(heads):
        s_scr[h] = state[h]
    y = jnp.concatenate(y_rows, axis=0)

    inv_n = 1.0 / HEAD_DIM
    mean = _segsum(y, bd) * inv_n
    d = y - mean
    var = _segsum(d * d, bd) * inv_n
    yn = d * lax.rsqrt(var + GN_EPS) * gng_ref[...] + gnb_ref[...]
    o_ref[0] = ((yn + bonus) * g).astype(o_ref.dtype)


def _rwkv(rw, w0, wdw, a0, waw, wgw, k_k, k_a, r_k, gn_g, gn_b, width, rows=256):
    B, T, cols = rw.shape
    row = lambda a: a.reshape(1, width)
    vec = _const_spec((1, width))
    return pl.pallas_call(
        functools.partial(_rwkv_kernel, width=width),
        out_shape=jax.ShapeDtypeStruct((B, T, width), BF16),
        grid=(B, T // rows),
        in_specs=[pl.BlockSpec((1, rows, cols), lambda b, t: (b, t, 0)),
                  vec, _const_spec(wdw.shape), vec, _const_spec(waw.shape),
                  _const_spec(wgw.shape), vec, vec, vec, vec, vec],
        out_specs=pl.BlockSpec((1, rows, width), lambda b, t: (b, t, 0)),
        scratch_shapes=[pltpu.VMEM((width // HEAD_DIM, HEAD_DIM, HEAD_DIM), F32)],
        compiler_params=pltpu.CompilerParams(
            dimension_semantics=("arbitrary", "arbitrary"), vmem_limit_bytes=VMEM_LIMIT),
        name="rwkv",
    )(rw, row(w0), wdw, row(a0), waw, wgw, row(k_k), row(k_a), row(r_k), row(gn_g), row(gn_b))


def _foxprep_kernel(fx_ref, fcum_ref, qg_ref, kg_ref, qa_ref, ka_ref, v_ref, *, width):
    W = width
    heads = W // HEAD_DIM
    pf = fx_ref[0]
    tm = pf.shape[0]
    fcum = fcum_ref[0] * LOG2E
    f_hi = fcum.astype(BF16).astype(F32)
    r1 = fcum - f_hi
    f_mid = r1.astype(BF16).astype(F32)
    f_lo = r1 - f_mid
    lane = lax.broadcasted_iota(jnp.int32, (tm, HEAD_DIM), 1)
    scale = LOG2E / math.sqrt(HEAD_DIM)
    for h in range(heads):
        q = pf[:, h * HEAD_DIM:(h + 1) * HEAD_DIM]
        k = pf[:, W + h * HEAD_DIM:W + (h + 1) * HEAD_DIM]
        qn = q * lax.rsqrt(jnp.mean(q * q, axis=-1, keepdims=True) + NORM_EPS) * (qg_ref[...] * scale)
        kn = k * lax.rsqrt(jnp.mean(k * k, axis=-1, keepdims=True) + NORM_EPS) * kg_ref[...]
        hi, mid, lo = f_hi[:, h:h + 1], f_mid[:, h:h + 1], f_lo[:, h:h + 1]
        aq = jnp.where(lane == 0, hi, jnp.where(lane == 1, mid, jnp.where(
            lane == 2, lo, jnp.where(lane < 6, 1.0, 0.0))))
        ak = jnp.where(lane < 3, 1.0, jnp.where(lane == 3, -hi, jnp.where(
            lane == 4, -mid, jnp.where(lane == 5, -lo, 0.0))))
        qa_ref[0, h] = jnp.concatenate([qn, aq], axis=1).astype(BF16)
        ka_ref[0, h] = jnp.concatenate([kn, ak], axis=1).astype(BF16)
    pad = (lax.broadcasted_iota(jnp.int32, (V_ROWS - HEAD_DIM, tm), 0) == 0).astype(BF16)
    for g in range(W // LANES):
        vt = pf[:, 2 * W + g * LANES:2 * W + (g + 1) * LANES].T.astype(BF16)
        for hh in range(2):
            v_ref[0, 2 * g + hh, 0, :HEAD_DIM, :] = vt[hh * HEAD_DIM:(hh + 1) * HEAD_DIM]
            v_ref[0, 2 * g + hh, 0, HEAD_DIM:, :] = pad


def _foxprep(fx, fcum, q_norm_g, k_norm_g, width, tm=512):
    B, T, cols = fx.shape
    heads = width // HEAD_DIM
    return pl.pallas_call(
        functools.partial(_foxprep_kernel, width=width),
        out_shape=(jax.ShapeDtypeStruct((B, heads, T, LANES), BF16),
                   jax.ShapeDtypeStruct((B, heads, T, LANES), BF16),
                   jax.ShapeDtypeStruct((B, heads, T // tm, V_ROWS, tm), BF16)),
        grid=(B, T // tm),
        in_specs=[pl.BlockSpec((1, tm, cols), lambda b, t: (b, t, 0)),
                  pl.BlockSpec((1, tm, LANES), lambda b, t: (b, t, 0)),
                  _const_spec((1, HEAD_DIM)), _const_spec((1, HEAD_DIM))],
        out_specs=(pl.BlockSpec((1, heads, tm, LANES), lambda b, t: (b, 0, t, 0)),
                   pl.BlockSpec((1, heads, tm, LANES), lambda b, t: (b, 0, t, 0)),
                   pl.BlockSpec((1, heads, 1, V_ROWS, tm), lambda b, t: (b, 0, t, 0, 0))),
        compiler_params=pltpu.CompilerParams(
            dimension_semantics=("arbitrary", "arbitrary"), vmem_limit_bytes=VMEM_LIMIT),
        name="foxprep",
    )(fx, fcum, q_norm_g.reshape(1, HEAD_DIM), k_norm_g.reshape(1, HEAD_DIM))


def _fox_kernel(qa_ref, ka_ref, vt_ref, og_ref, g_ref, o_ref,
                m_scr, acc_scr, s_scr, p_scr, al_scr, *, cb):
    qi = pl.program_id(2)
    tq = qa_ref.shape[2]
    tk = tq
    chains = [(hh, c) for hh in range(2) for c in range(tq // cb)]
    csl = lambda c: slice(c * cb, (c + 1) * cb)

    m_scr[...] = jnp.full_like(m_scr, NEG_INF)
    acc_scr[...] = jnp.zeros_like(acc_scr)
    p_scr[1] = jnp.zeros(p_scr.shape[1:], p_scr.dtype)
    al_scr[1] = jnp.ones(al_scr.shape[1:], al_scr.dtype)

    def logits(t, buf):
        k_start = pl.multiple_of(t * tk, tk)
        for hh, c in chains:
            s_scr[buf, hh, :, csl(c)] = _dot_nt(ka_ref[0, hh, pl.ds(k_start, tk), :],
                                                qa_ref[0, hh, csl(c), :])

    def softmax(buf, masked):
        for hh, c in chains:
            s = s_scr[buf, hh, :, csl(c)]
            if masked:
                key = lax.broadcasted_iota(jnp.int32, (tk, cb), 0)
                qry = c * cb + lax.broadcasted_iota(jnp.int32, (tk, cb), 1)
                s = jnp.where(key <= qry, s, NEG_INF)
            m_prev = m_scr[hh, :, csl(c)]
            m_new = jnp.maximum(m_prev, jnp.max(s, axis=0, keepdims=True))
            m_scr[hh, :, csl(c)] = m_new
            al_scr[buf, hh, :, csl(c)] = jnp.exp2(m_prev - m_new)
            p_scr[buf, hh, :, csl(c)] = jnp.exp2(s - m_new).astype(BF16)

    def values(t, buf):
        for hh, c in chains:
            pv = _dot(vt_ref[0, hh, jnp.maximum(t, 0)], p_scr[buf, hh, :, csl(c)])
            acc_scr[hh, :, csl(c)] = al_scr[buf, hh, :, csl(c)] * acc_scr[hh, :, csl(c)] + pv

    def step(t, buf):
        logits(t, buf)
        softmax(1 - buf, masked=False)
        values(t - 2, buf)

    def two_steps(u, carry):
        step(2 * u + 1, 1)
        step(2 * u + 2, 0)
        return carry

    def drain(buf):
        softmax(buf, masked=True)
        values(qi - 1, 1 - buf)
        values(qi, buf)

    logits(0, 0)
    lax.fori_loop(0, qi // 2, two_steps, 0)

    @pl.when(qi % 2 == 1)
    def _():
        step(qi, 1)
        drain(1)

    @pl.when(qi % 2 == 0)
    def _():
        drain(0)

    normed = []
    for hh in range(2):
        acc = acc_scr[hh]
        o = acc[:HEAD_DIM] / acc[HEAD_DIM:HEAD_DIM + 1]
        ms = jnp.mean(o * o, axis=0, keepdims=True)
        normed.append(o * lax.rsqrt(ms + NORM_EPS))
    on = jnp.concatenate(normed, axis=0).T
    o_ref[0] = (on * g_ref[...] * _sigmoid(og_ref[0])).astype(o_ref.dtype)


def _fox(qa, ka, vt, fx, fox_out_g, width, cb=256):
    B, H, T, _ = qa.shape
    tq = vt.shape[4]
    og_blk = 3 * width // LANES
    return pl.pallas_call(
        functools.partial(_fox_kernel, cb=cb),
        out_shape=jax.ShapeDtypeStruct((B, T, width), BF16),
        grid=(B, H // 2, T // tq),
        in_specs=[pl.BlockSpec((1, 2, tq, LANES), lambda b, h, q: (b, h, q, 0)),
                  pl.BlockSpec((1, 2, T, LANES), lambda b, h, q: (b, h, 0, 0)),
                  pl.BlockSpec((1, 2, T // tq, V_ROWS, tq), lambda b, h, q: (b, h, 0, 0, 0)),
                  pl.BlockSpec((1, tq, LANES), lambda b, h, q: (b, q, og_blk + h)),
                  pl.BlockSpec((1, LANES), lambda b, h, q: (0, h))],
        out_specs=pl.BlockSpec((1, tq, LANES), lambda b, h, q: (b, q, h)),
        scratch_shapes=[pltpu.VMEM((2, 1, tq), F32), pltpu.VMEM((2, V_ROWS, tq), F32),
                        pltpu.VMEM((2, 2, tq, tq), F32), pltpu.VMEM((2, 2, tq, tq), BF16),
                        pltpu.VMEM((2, 2, 1, tq), F32)],
        compiler_params=pltpu.CompilerParams(
            dimension_semantics=("arbitrary",) * 3, vmem_limit_bytes=VMEM_LIMIT),
        name="fox",
    )(qa, ka, vt, fx, fox_out_g.reshape(1, width))


def _outmlp_kernel(x_ref, yrw_ref, yfx_ref, mod_ref, wo_ref, g2_ref, w1_ref, w2_ref, gf_ref,
                   o_ref, *, ff_tile):
    x = x_ref[0]
    half = yrw_ref.shape[2]
    gate1 = mod_ref[0, 2:3, :]
    shift2 = mod_ref[0, 3:4, :]
    scale2 = mod_ref[0, 4:5, :]
    gate2 = mod_ref[0, 5:6, :]
    y = _dot(yrw_ref[0], wo_ref[:half, :]) + _dot(yfx_ref[0], wo_ref[half:, :])
    h1 = x + gate1 * y
    ms = jnp.mean(h1 * h1, axis=-1, keepdims=True)
    u = ((h1 * lax.rsqrt(ms + NORM_EPS) * g2_ref[...]) * (1.0 + scale2) + shift2).astype(BF16)
    acc = jnp.zeros_like(x)
    for j in range(w1_ref.shape[1] // ff_tile):
        hid = jnp.maximum(_dot(u, w1_ref[:, j * ff_tile:(j + 1) * ff_tile]), 0.0)
        acc = acc + _dot((hid * hid).astype(BF16), w2_ref[j * ff_tile:(j + 1) * ff_tile, :])
    h2 = h1 + gate2 * acc
    ms2 = jnp.mean(h2 * h2, axis=-1, keepdims=True)
    o_ref[0] = h2 * lax.rsqrt(ms2 + NORM_EPS) * gf_ref[...]


def _outmlp(x, y_rw, y_fx, mod3, w_o, norm2_g, w1, w2, final_g, tm=512, ff_tile=1024):
    B, T, D = x.shape
    half = y_rw.shape[2]
    return pl.pallas_call(
        functools.partial(_outmlp_kernel, ff_tile=ff_tile),
        out_shape=jax.ShapeDtypeStruct((B, T, D), F32),
        grid=(B, T // tm),
        in_specs=[pl.BlockSpec((1, tm, D), lambda b, t: (b, t, 0)),
                  pl.BlockSpec((1, tm, half), lambda b, t: (b, t, 0)),
                  pl.BlockSpec((1, tm, half), lambda b, t: (b, t, 0)),
                  pl.BlockSpec((1, N_MOD, D), lambda b, t: (b, 0, 0)),
                  _const_spec(w_o.shape), _const_spec((1, D)),
                  _const_spec(w1.shape), _const_spec(w2.shape), _const_spec((1, D))],
        out_specs=pl.BlockSpec((1, tm, D), lambda b, t: (b, t, 0)),
        compiler_params=pltpu.CompilerParams(
            dimension_semantics=("arbitrary", "arbitrary"), vmem_limit_bytes=VMEM_LIMIT),
        name="outmlp",
    )(x, y_rw, y_fx, mod3, w_o, norm2_g.reshape(1, D), w1, w2, final_g.reshape(1, D))


def _pad_cols(a, n):
    return jnp.pad(a, ((0, 0), (0, n - a.shape[1])))


def _pad_rows(a, n):
    return jnp.pad(a, ((0, n - a.shape[0]), (0, 0)))


def _branches(x, c, w_ada, b_ada, norm1_g, w_in, mu_shift, w0, w_up_decay, a0, w_up_a, w_up_g,
              k_k, k_a, r_k, gn_g, gn_b, b_f, q_norm_g, k_norm_g, fox_out_g):
    B, T, D = x.shape
    W = w0.shape[0]
    heads = b_f.shape[0]
    rw_n = 3 * W + DECAY_LORA + AAA_LORA + GATE_LORA

    def regroup(a):
        rw, fx = a[:, :rw_n], a[:, rw_n:]
        o = 3 * W
        return jnp.concatenate([
            rw[:, :o],
            _pad_cols(rw[:, o:o + DECAY_LORA], LANES),
            _pad_cols(rw[:, o + DECAY_LORA:o + DECAY_LORA + AAA_LORA], LANES),
            _pad_cols(rw[:, o + DECAY_LORA + AAA_LORA:], 2 * LANES),
            fx[:, :4 * W],
            _pad_cols(fx[:, 4 * W:], LANES)], axis=1)

    rw_cols = 3 * W + 4 * LANES
    w_all = regroup(w_in).astype(BF16)
    mu_p = regroup(jnp.pad(mu_shift.reshape(1, rw_n), ((0, 0), (0, w_in.shape[1] - rw_n))))[:, :rw_cols]
    bf_p = _pad_cols(b_f.reshape(1, heads), LANES)
    wdw = _pad_rows(w_up_decay, LANES).astype(BF16)
    waw = _pad_rows(w_up_a, LANES).astype(BF16)
    wgw = _pad_rows(w_up_g, 2 * LANES).astype(BF16)

    mod3 = _ada(c, w_ada, b_ada).reshape(B, N_MOD, D)
    rw, fx, fcum = _inproj(x, mod3, norm1_g, w_all, mu_p, bf_p, rw_cols)
    y_rw = _rwkv(rw, w0, wdw, a0, waw, wgw, k_k, k_a, r_k.reshape(-1), gn_g, gn_b, W)
    qa, ka, vt = _foxprep(fx, fcum, q_norm_g, k_norm_g, W)
    y_fx = _fox(qa, ka, vt, fx, fox_out_g, W)
    return y_rw, y_fx, mod3


def kernel(x, c, w_ada, b_ada, norm1_g, w_in, mu_shift, w0, w_up_decay, a0, w_up_a, w_up_g,
           k_k, k_a, r_k, gn_g, gn_b, b_f, q_norm_g, k_norm_g, fox_out_g, w_o, norm2_g,
           w_mlp1, w_mlp2, final_g):
    y_rw, y_fx, mod3 = _branches(x, c, w_ada, b_ada, norm1_g, w_in, mu_shift, w0, w_up_decay,
                                 a0, w_up_a, w_up_g, k_k, k_a, r_k, gn_g, gn_b, b_f,
                                 q_norm_g, k_norm_g, fox_out_g)
    return _outmlp(x, y_rw, y_fx, mod3, w_o.astype(BF16), norm2_g,
                   w_mlp1.astype(BF16), w_mlp2.astype(BF16), final_g)
```

```python
import functools
import math

import jax
import jax.numpy as jnp
import numpy as np
from jax import lax
from jax.experimental import pallas as pl
from jax.experimental.pallas import tpu as pltpu

F32 = jnp.float32
BF16 = jnp.bfloat16

HEAD_DIM = 64
LANES = 128
NORM_EPS = 1e-6
GN_EPS = 64e-5
NEG_INF = -1e30
LOG2E = 1.4426950408889634
N_MOD = 6
DECAY_LORA = 64
AAA_LORA = 64
GATE_LORA = 160
RW_CHUNK = 64
V_ROWS = HEAD_DIM + 16
VMEM_LIMIT = 56 * 1024 * 1024

_HI = lax.Precision.HIGHEST


def _dot(a, b):
    return jnp.dot(a, b, preferred_element_type=F32)


def _dot_hi(a, b):
    return jnp.dot(a, b, preferred_element_type=F32, precision=_HI)


def _dot_nt(a, b):
    return lax.dot_general(a, b, (((1,), (1,)), ((), ())), preferred_element_type=F32)


def _dot_tn(a, b):
    return lax.dot_general(a, b, (((0,), (0,)), ((), ())), preferred_element_type=F32)


def _bf(x):
    return x.astype(BF16)


def _sigmoid(x):
    return 1.0 / (1.0 + jnp.exp(-x))


def _const_spec(shape):
    n = len(shape)
    return pl.BlockSpec(shape, lambda *_: (0,) * n)


def _ada_kernel(c_ref, w_ref, b_ref, o_ref):
    c = c_ref[...]
    o_ref[...] = _dot_hi(c * _sigmoid(c), w_ref[...]) + b_ref[...]


def _ada(c, w_ada, b_ada):
    B, D = c.shape
    n = w_ada.shape[1]
    return pl.pallas_call(
        _ada_kernel,
        out_shape=jax.ShapeDtypeStruct((B, n), F32),
        grid=(n // D,),
        in_specs=[pl.BlockSpec((B, D), lambda j: (0, 0)),
                  pl.BlockSpec((D, D), lambda j: (0, j)),
                  pl.BlockSpec((1, D), lambda j: (0, j))],
        out_specs=pl.BlockSpec((B, D), lambda j: (0, j)),
        name="ada",
    )(c, w_ada, b_ada.reshape(1, n))


def _inproj_kernel(x_ref, mod_ref, g_ref, w_ref, mu_ref, bf_ref,
                   rw_ref, fx_ref, fcum_ref, prev_scr, fcar_scr, *, rw_cols):
    t = pl.program_id(1)
    tm = x_ref.shape[1]

    @pl.when(t == 0)
    def _():
        prev_scr[...] = jnp.zeros_like(prev_scr)
        fcar_scr[...] = jnp.zeros_like(fcar_scr)

    x = x_ref[0]
    shift = mod_ref[0, 0:1, :]
    scale = mod_ref[0, 1:2, :]
    ms = jnp.mean(x * x, axis=-1, keepdims=True)
    u = (x * lax.rsqrt(ms + NORM_EPS) * g_ref[...]) * (1.0 + scale) + shift
    u = u.astype(BF16)

    p = _dot(u, w_ref[:, :rw_cols])
    row = lax.broadcasted_iota(jnp.int32, p.shape, 0)
    prev = jnp.where(row == 0, prev_scr[0:1, :], pltpu.roll(p, shift=1, axis=0))
    prev_scr[0:1, :] = p[tm - 1:tm, :]
    rw_ref[0] = p + (prev - p) * mu_ref[...]

    pf = _dot(u, w_ref[:, rw_cols:])
    fx_ref[0] = pf
    z = pf[:, pf.shape[1] - LANES:] + bf_ref[...]
    logf = jnp.minimum(z, 0.0) - jnp.log(1.0 + jnp.exp(-jnp.abs(z)))
    r2 = lax.broadcasted_iota(jnp.int32, (tm, tm), 0)
    c2 = lax.broadcasted_iota(jnp.int32, (tm, tm), 1)
    tril = (c2 <= r2).astype(F32)
    fcum = _dot_hi(tril, logf) + fcar_scr[0:1, :]
    fcum_ref[0] = fcum
    fcar_scr[0:1, :] = fcum[tm - 1:tm, :]


def _inproj(x, mod3, norm1_g, w_all, mu_p, bf_p, rw_cols, tm=256):
    B, T, D = x.shape
    ncols = w_all.shape[1]
    fx_cols = ncols - rw_cols
    return pl.pallas_call(
        functools.partial(_inproj_kernel, rw_cols=rw_cols),
        out_shape=(jax.ShapeDtypeStruct((B, T, rw_cols), F32),
                   jax.ShapeDtypeStruct((B, T, fx_cols), F32),
                   jax.ShapeDtypeStruct((B, T, LANES), F32)),
        grid=(B, T // tm),
        in_specs=[pl.BlockSpec((1, tm, D), lambda b, t: (b, t, 0)),
                  pl.BlockSpec((1, N_MOD, D), lambda b, t: (b, 0, 0)),
                  _const_spec((1, D)),
                  _const_spec((D, ncols)),
                  _const_spec((1, rw_cols)),
                  _const_spec((1, LANES))],
        out_specs=(pl.BlockSpec((1, tm, rw_cols), lambda b, t: (b, t, 0)),
                   pl.BlockSpec((1, tm, fx_cols), lambda b, t: (b, t, 0)),
                   pl.BlockSpec((1, tm, LANES), lambda b, t: (b, t, 0))),
        scratch_shapes=[pltpu.VMEM((8, rw_cols), F32), pltpu.VMEM((8, LANES), F32)],
        compiler_params=pltpu.CompilerParams(
            dimension_semantics=("arbitrary", "arbitrary"), vmem_limit_bytes=VMEM_LIMIT),
        name="inproj",
    )(x, mod3, norm1_g.reshape(1, D), w_all, mu_p, bf_p)


def _segsum(x, bd):
    hi = x.astype(BF16)
    lo = (x - hi.astype(F32)).astype(BF16)
    return _dot(hi, bd) + _dot(lo, bd)


def _rwkv_kernel(p_ref, w0_ref, wdw_ref, a0_ref, waw_ref, wgw_ref, kk_ref, ka_ref, rk_ref,
                 gng_ref, gnb_ref, o_ref, s_scr, *, width):
    t = pl.program_id(1)
    rows = p_ref.shape[1]
    L = RW_CHUNK
    n_chunks = rows // L
    W = width
    heads = W // HEAD_DIM

    @pl.when(t == 0)
    def _():
        s_scr[...] = jnp.zeros_like(s_scr)

    p = p_ref[0]
    r = p[:, 0:W]
    k = p[:, W:2 * W]
    v = p[:, 2 * W:3 * W]
    o0 = 3 * W
    wd = p[:, o0:o0 + LANES]
    ad = p[:, o0 + LANES:o0 + 2 * LANES]
    gd = p[:, o0 + 2 * LANES:o0 + 4 * LANES]

    z = w0_ref[...] + _dot(_bf(jnp.tanh(wd)), wdw_ref[...])
    softplus = jnp.maximum(-z, 0.0) + jnp.log(1.0 + jnp.exp(-jnp.abs(z)))
    ld = -jnp.exp(-softplus - 0.5)
    a_sig = _sigmoid(a0_ref[...] + _dot(_bf(ad), waw_ref[...]))
    g = _dot(_bf(_sigmoid(gd)), wgw_ref[...])

    ri = lax.broadcasted_iota(jnp.int32, (W, W), 0)
    ci = lax.broadcasted_iota(jnp.int32, (W, W), 1)
    bd = ((ri // HEAD_DIM) == (ci // HEAD_DIM)).astype(BF16)

    kk = k * kk_ref[...]
    kk = kk / jnp.maximum(jnp.sqrt(_segsum(kk * kk, bd)), 1e-12)
    k2 = k * (1.0 + (a_sig - 1.0) * ka_ref[...])
    b = kk * a_sig
    bonus = _segsum(r * k2 * rk_ref[...], bd) * v

    rl = lax.broadcasted_iota(jnp.int32, (L, L), 0)
    cl = lax.broadcasted_iota(jnp.int32, (L, L), 1)
    incl = cl <= rl
    strict = cl < rl
    tril = incl.astype(BF16)
    eye = (cl == rl).astype(F32)
    ld_hi = _bf(ld)
    ld_lo = _bf(ld - ld_hi.astype(F32))
    cums, tots = [], []
    for c in range(n_chunks):
        rs = slice(c * L, (c + 1) * L)
        cum_c = _dot(tril, ld_hi[rs]) + _dot(tril, ld_lo[rs])
        cums.append(cum_c)
        tots.append(cum_c[L - 1:L, :])
    cum = jnp.concatenate(cums, axis=0)
    tot_b = jnp.concatenate([jnp.broadcast_to(tc, (L, W)) for tc in tots], axis=0)

    e_neg = jnp.exp(-cum)
    e_rem = jnp.exp(tot_b - cum)
    rt = r * jnp.exp(cum)
    rt_m = _bf(rt)
    at = _bf(-kk * jnp.exp(cum - ld))
    bt = _bf(b * e_neg)
    kt = _bf(k2 * e_neg)
    bh = _bf(b * e_rem)
    kh = _bf(k2 * e_rem)
    vm = _bf(v)

    pairs = [(c, h) for c in range(n_chunks) for h in range(heads)]
    rsl = lambda c: slice(c * L, (c + 1) * L)
    hsl = lambda h: slice(h * HEAD_DIM, (h + 1) * HEAD_DIM)
    ar = [jnp.concatenate([at[rsl(c)], rt_m[rsl(c)]], axis=0) for c in range(n_chunks)]
    a_rb, a_rk, a_ak, pw, inv = {}, {}, {}, {}, {}
    for c, h in pairs:
        g_b = _dot_nt(ar[c][:, hsl(h)], bt[rsl(c), hsl(h)])
        g_k = _dot_nt(ar[c][:, hsl(h)], kt[rsl(c), hsl(h)])
        a_ab = jnp.where(strict, g_b[:L], 0.0)
        a_ak[c, h] = _bf(jnp.where(strict, g_k[:L], 0.0))
        a_rb[c, h] = _bf(jnp.where(incl, g_b[L:], 0.0))
        a_rk[c, h] = _bf(jnp.where(incl, g_k[L:], 0.0))
        pw[c, h] = _bf(a_ab)
        inv[c, h] = eye + a_ab
    for _ in range(int(math.log2(L)) - 1):
        for ch in pairs:
            pw[ch] = _bf(_dot(pw[ch], pw[ch]))
        for ch in pairs:
            inv[ch] = inv[ch] + _dot(_bf(inv[ch]), pw[ch])
    akv = {(c, h): _bf(_dot(a_ak[c, h], vm[rsl(c), hsl(h)])) for c, h in pairs}
    x = {(c, h): _bf(_dot(_bf(inv[c, h]),
                          jnp.concatenate([at[rsl(c), hsl(h)], akv[c, h]], axis=1)))
         for c, h in pairs}
    r_eff, y0, m_mat, n_mat = {}, {}, {}, {}
    for c, h in pairs:
        p1 = _dot(a_rb[c, h], x[c, h])
        r_eff[c, h] = _bf(rt[rsl(c), hsl(h)] + p1[:, :HEAD_DIM])
        y0[c, h] = p1[:, HEAD_DIM:] + _dot(a_rk[c, h], vm[rsl(c), hsl(h)])
    for c, h in pairs:
        p2 = _dot_tn(x[c, h], bh[rsl(c), hsl(h)])
        m_mat[c, h] = _bf(p2[:HEAD_DIM])
        n_mat[c, h] = p2[HEAD_DIM:] + _dot_tn(vm[rsl(c), hsl(h)], kh[rsl(c), hsl(h)])
    state = [s_scr[h] for h in range(heads)]
    y_rows = []
    for c in range(n_chunks):
        e_tot = jnp.exp(tots[c])
        s_m = [_bf(s) for s in state]
        ys = [_dot_nt(r_eff[c, h], s_m[h]) + y0[c, h] for h in range(heads)]
        state = [state[h] * e_tot[:, hsl(h)] + _dot(s_m[h], m_mat[c, h]) + n_mat[c, h]
                 for h in range(heads)]
        y_rows.append(jnp.concatenate(ys, axis=1))
    for h in range(heads):
        s_scr[h] = state[h]
    y = jnp.concatenate(y_rows, axis=0)

    inv_n = 1.0 / HEAD_DIM
    mean = _segsum(y, bd) * inv_n
    d = y - mean
    var = _segsum(d * d, bd) * inv_n
    yn = d * lax.rsqrt(var + GN_EPS) * gng_ref[...] + gnb_ref[...]
    o_ref[0] = ((yn + bonus) * g).astype(o_ref.dtype)


def _rwkv(rw, w0, wdw, a0, waw, wgw, k_k, k_a, r_k, gn_g, gn_b, width, rows=256):
    B, T, cols = rw.shape
    row = lambda a: a.reshape(1, width)
    vec = _const_spec((1, width))
    return pl.pallas_call(
        functools.partial(_rwkv_kernel, width=width),
        out_shape=jax.ShapeDtypeStruct((B, T, width), BF16),
        grid=(B, T // rows),
        in_specs=[pl.BlockSpec((1, rows, cols), lambda b, t: (b, t, 0)),
                  vec, _const_spec(wdw.shape), vec, _const_spec(waw.shape),
                  _const_spec(wgw.shape), vec, vec, vec, vec, vec],
        out_specs=pl.BlockSpec((1, rows, width), lambda b, t: (b, t, 0)),
        scratch_shapes=[pltpu.VMEM((width // HEAD_DIM, HEAD_DIM, HEAD_DIM), F32)],
        compiler_params=pltpu.CompilerParams(
            dimension_semantics=("arbitrary", "arbitrary"), vmem_limit_bytes=VMEM_LIMIT),
        name="rwkv",
    )(rw, row(w0), wdw, row(a0), waw, wgw, row(k_k), row(k_a), row(r_k), row(gn_g), row(gn_b))


def _fox_aug_tables(heads):
    sel = np.zeros((3 * LANES, 2 * heads * LANES), np.float32)
    bias = np.zeros((1, 2 * heads * LANES), np.float32)
    for h in range(heads):
        base = h * LANES + (HEAD_DIM if h % 2 == 0 else 0)
        kbase = heads * LANES + base
        for piece in range(3):
            sel[piece * LANES + h, base + piece] = 1.0
            sel[piece * LANES + h, kbase + 3 + piece] = -1.0
            bias[0, base + 3 + piece] = 1.0
            bias[0, kbase + piece] = 1.0
    return jnp.asarray(sel, BF16), jnp.asarray(bias)


def _foxprep_kernel(fx_ref, fcum_ref, qg_ref, kg_ref, sel_ref, bias_ref, qa_ref, ka_ref, v_ref,
                    *, width):
    W = width
    heads = W // HEAD_DIM
    pf = fx_ref[0]
    tm = pf.shape[0]
    fcum = fcum_ref[0] * LOG2E
    f_hi = fcum.astype(BF16)
    r1 = fcum - f_hi.astype(F32)
    f_mid = r1.astype(BF16)
    f_lo = (r1 - f_mid.astype(F32)).astype(BF16)
    aug = _dot(jnp.concatenate([f_hi, f_mid, f_lo], axis=1), sel_ref[...]) + bias_ref[...]
    first = lax.broadcasted_iota(jnp.int32, (tm, LANES), 1) < HEAD_DIM
    for side, (gain_ref, out_ref) in enumerate(((qg_ref, qa_ref), (kg_ref, ka_ref))):
        for g in range(W // LANES):
            x = pf[:, side * W + g * LANES:side * W + (g + 1) * LANES]
            sq = x * x
            lo = jnp.sum(jnp.where(first, sq, 0.0), axis=-1, keepdims=True)
            hi = jnp.sum(jnp.where(first, 0.0, sq), axis=-1, keepdims=True)
            ms = jnp.where(first, lo, hi) * (1.0 / HEAD_DIM)
            xn = x * lax.rsqrt(ms + NORM_EPS) * gain_ref[...]
            for hh in range(2):
                h = 2 * g + hh
                a = aug[:, (side * heads + h) * LANES:(side * heads + h + 1) * LANES]
                keep = first if hh == 0 else jnp.logical_not(first)
                out_ref[0, h] = jnp.where(keep, xn, a).astype(BF16)
    pad = (lax.broadcasted_iota(jnp.int32, (V_ROWS - HEAD_DIM, tm), 0) == 0).astype(BF16)
    for g in range(W // LANES):
        vt = pf[:, 2 * W + g * LANES:2 * W + (g + 1) * LANES].T.astype(BF16)
        for hh in range(2):
            v_ref[0, 2 * g + hh, 0, :HEAD_DIM, :] = vt[hh * HEAD_DIM:(hh + 1) * HEAD_DIM]
            v_ref[0, 2 * g + hh, 0, HEAD_DIM:, :] = pad


def _foxprep(fx, fcum, q_norm_g, k_norm_g, width, tm=512):
    B, T, cols = fx.shape
    heads = width // HEAD_DIM
    sel, bias = _fox_aug_tables(heads)
    scale = LOG2E / math.sqrt(HEAD_DIM)
    qg2 = jnp.tile(q_norm_g * scale, 2).reshape(1, LANES)
    kg2 = jnp.tile(k_norm_g, 2).reshape(1, LANES)
    return pl.pallas_call(
        functools.partial(_foxprep_kernel, width=width),
        out_shape=(jax.ShapeDtypeStruct((B, heads, T, LANES), BF16),
                   jax.ShapeDtypeStruct((B, heads, T, LANES), BF16),
                   jax.ShapeDtypeStruct((B, heads, T // tm, V_ROWS, tm), BF16)),
        grid=(B, T // tm),
        in_specs=[pl.BlockSpec((1, tm, cols), lambda b, t: (b, t, 0)),
                  pl.BlockSpec((1, tm, LANES), lambda b, t: (b, t, 0)),
                  _const_spec((1, LANES)), _const_spec((1, LANES)),
                  _const_spec(sel.shape), _const_spec(bias.shape)],
        out_specs=(pl.BlockSpec((1, heads, tm, LANES), lambda b, t: (b, 0, t, 0)),
                   pl.BlockSpec((1, heads, tm, LANES), lambda b, t: (b, 0, t, 0)),
                   pl.BlockSpec((1, heads, 1, V_ROWS, tm), lambda b, t: (b, 0, t, 0, 0))),
        compiler_params=pltpu.CompilerParams(
            dimension_semantics=("arbitrary", "arbitrary"), vmem_limit_bytes=VMEM_LIMIT),
        name="foxprep",
    )(fx, fcum, qg2, kg2, sel, bias)


def _fox_kernel(qa_ref, ka_ref, vt_ref, og_ref, g_ref, o_ref,
                m_scr, acc_scr, s_scr, p_scr, al_scr, *, cb):
    qi = pl.program_id(2)
    tq = qa_ref.shape[2]
    tk = tq
    chains = [(hh, c) for hh in range(2) for c in range(tq // cb)]
    csl = lambda c: slice(c * cb, (c + 1) * cb)

    m_scr[...] = jnp.full_like(m_scr, NEG_INF)
    acc_scr[...] = jnp.zeros_like(acc_scr)
    p_scr[1] = jnp.zeros(p_scr.shape[1:], p_scr.dtype)
    al_scr[1] = jnp.ones(al_scr.shape[1:], al_scr.dtype)

    def logits(t, buf):
        k_start = pl.multiple_of(t * tk, tk)
        for hh, c in chains:
            s_scr[buf, hh, :, csl(c)] = _dot_nt(ka_ref[0, hh, pl.ds(k_start, tk), :],
                                                qa_ref[0, hh, csl(c), :])

    def softmax(buf, masked):
        for hh, c in chains:
            s = s_scr[buf, hh, :, csl(c)]
            if masked:
                key = lax.broadcasted_iota(jnp.int32, (tk, cb), 0)
                qry = c * cb + lax.broadcasted_iota(jnp.int32, (tk, cb), 1)
                s = jnp.where(key <= qry, s, NEG_INF)
            m_prev = m_scr[hh, :, csl(c)]
            m_new = jnp.maximum(m_prev, jnp.max(s, axis=0, keepdims=True))
            m_scr[hh, :, csl(c)] = m_new
            al_scr[buf, hh, :, csl(c)] = jnp.exp2(m_prev - m_new)
            p_scr[buf, hh, :, csl(c)] = jnp.exp2(s - m_new).astype(BF16)

    def values(t, buf):
        for hh, c in chains:
            pv = _dot(vt_ref[0, hh, jnp.maximum(t, 0)], p_scr[buf, hh, :, csl(c)])
            acc_scr[hh, :, csl(c)] = al_scr[buf, hh, :, csl(c)] * acc_scr[hh, :, csl(c)] + pv

    def step(t, buf):
        logits(t, buf)
        softmax(1 - buf, masked=False)
        values(t - 2, buf)

    def two_steps(u, carry):
        step(2 * u + 1, 1)
        step(2 * u + 2, 0)
        return carry

    def drain(buf):
        softmax(buf, masked=True)
        values(qi - 1, 1 - buf)
        values(qi, buf)

    logits(0, 0)
    lax.fori_loop(0, qi // 2, two_steps, 0)

    @pl.when(qi % 2 == 1)
    def _():
        step(qi, 1)
        drain(1)

    @pl.when(qi % 2 == 0)
    def _():
        drain(0)

    normed = []
    for hh in range(2):
        acc = acc_scr[hh]
        o = acc[:HEAD_DIM] / acc[HEAD_DIM:HEAD_DIM + 1]
        ms = jnp.mean(o * o, axis=0, keepdims=True)
        normed.append(o * lax.rsqrt(ms + NORM_EPS))
    on = jnp.concatenate(normed, axis=0).T
    o_ref[0] = (on * g_ref[...] * _sigmoid(og_ref[0])).astype(o_ref.dtype)


def _fox(qa, ka, vt, fx, fox_out_g, width, cb=256):
    B, H, T, _ = qa.shape
    tq = vt.shape[4]
    og_blk = 3 * width // LANES
    return pl.pallas_call(
        functools.partial(_fox_kernel, cb=cb),
        out_shape=jax.ShapeDtypeStruct((B, T, width), BF16),
        grid=(B, H // 2, T // tq),
        in_specs=[pl.BlockSpec((1, 2, tq, LANES), lambda b, h, q: (b, h, q, 0)),
                  pl.BlockSpec((1, 2, T, LANES), lambda b, h, q: (b, h, 0, 0)),
                  pl.BlockSpec((1, 2, T // tq, V_ROWS, tq), lambda b, h, q: (b, h, 0, 0, 0)),
                  pl.BlockSpec((1, tq, LANES), lambda b, h, q: (b, q, og_blk + h)),
                  pl.BlockSpec((1, LANES), lambda b, h, q: (0, h))],
        out_specs=pl.BlockSpec((1, tq, LANES), lambda b, h, q: (b, q, h)),
        scratch_shapes=[pltpu.VMEM((2, 1, tq), F32), pltpu.VMEM((2, V_ROWS, tq), F32),
                        pltpu.VMEM((2, 2, tq, tq), F32), pltpu.VMEM((2, 2, tq, tq), BF16),
                        pltpu.VMEM((2, 2, 1, tq), F32)],
        compiler_params=pltpu.CompilerParams(
            dimension_semantics=("arbitrary",) * 3, vmem_limit_bytes=VMEM_LIMIT),
        name="fox",
    )(qa, ka, vt, fx, fox_out_g.reshape(1, width))


def _outmlp_kernel(x_ref, yrw_ref, yfx_ref, mod_ref, wo_ref, g2_ref, w1_ref, w2_ref, gf_ref,
                   o_ref, *, ff_tile):
    x = x_ref[0]
    half = yrw_ref.shape[2]
    gate1 = mod_ref[0, 2:3, :]
    shift2 = mod_ref[0, 3:4, :]
    scale2 = mod_ref[0, 4:5, :]
    gate2 = mod_ref[0, 5:6, :]
    y = _dot(yrw_ref[0], wo_ref[:half, :]) + _dot(yfx_ref[0], wo_ref[half:, :])
    h1 = x + gate1 * y
    ms = jnp.mean(h1 * h1, axis=-1, keepdims=True)
    u = ((h1 * lax.rsqrt(ms + NORM_EPS) * g2_ref[...]) * (1.0 + scale2) + shift2).astype(BF16)
    acc = jnp.zeros_like(x)
    for j in range(w1_ref.shape[1] // ff_tile):
        hid = jnp.maximum(_dot(u, w1_ref[:, j * ff_tile:(j + 1) * ff_tile]), 0.0)
        acc = acc + _dot((hid * hid).astype(BF16), w2_ref[j * ff_tile:(j + 1) * ff_tile, :])
    h2 = h1 + gate2 * acc
    ms2 = jnp.mean(h2 * h2, axis=-1, keepdims=True)
    o_ref[0] = h2 * lax.rsqrt(ms2 + NORM_EPS) * gf_ref[...]


def _outmlp(x, y_rw, y_fx, mod3, w_o, norm2_g, w1, w2, final_g, tm=512, ff_tile=1024):
    B, T, D = x.shape
    half = y_rw.shape[2]
    return pl.pallas_call(
        functools.partial(_outmlp_kernel, ff_tile=ff_tile),
        out_shape=jax.ShapeDtypeStruct((B, T, D), F32),
        grid=(B, T // tm),
        in_specs=[pl.BlockSpec((1, tm, D), lambda b, t: (b, t, 0)),
                  pl.BlockSpec((1, tm, half), lambda b, t: (b, t, 0)),
                  pl.BlockSpec((1, tm, half), lambda b, t: (b, t, 0)),
                  pl.BlockSpec((1, N_MOD, D), lambda b, t: (b, 0, 0)),
                  _const_spec(w_o.shape), _const_spec((1, D)),
                  _const_spec(w1.shape), _const_spec(w2.shape), _const_spec((1, D))],
        out_specs=pl.BlockSpec((1, tm, D), lambda b, t: (b, t, 0)),
        compiler_params=pltpu.CompilerParams(
            dimension_semantics=("arbitrary", "arbitrary"), vmem_limit_bytes=VMEM_LIMIT),
        name="outmlp",
    )(x, y_rw, y_fx, mod3, w_o, norm2_g.reshape(1, D), w1, w2, final_g.reshape(1, D))


def _pad_cols(a, n):
    return jnp.pad(a, ((0, 0), (0, n - a.shape[1])))


def _pad_rows(a, n):
    return jnp.pad(a, ((0, n - a.shape[0]), (0, 0)))


def _branches(x, c, w_ada, b_ada, norm1_g, w_in, mu_shift, w0, w_up_decay, a0, w_up_a, w_up_g,
              k_k, k_a, r_k, gn_g, gn_b, b_f, q_norm_g, k_norm_g, fox_out_g):
    B, T, D = x.shape
    W = w0.shape[0]
    heads = b_f.shape[0]
    rw_n = 3 * W + DECAY_LORA + AAA_LORA + GATE_LORA

    def regroup(a):
        rw, fx = a[:, :rw_n], a[:, rw_n:]
        o = 3 * W
        return jnp.concatenate([
            rw[:, :o],
            _pad_cols(rw[:, o:o + DECAY_LORA], LANES),
            _pad_cols(rw[:, o + DECAY_LORA:o + DECAY_LORA + AAA_LORA], LANES),
            _pad_cols(rw[:, o + DECAY_LORA + AAA_LORA:], 2 * LANES),
            fx[:, :4 * W],
            _pad_cols(fx[:, 4 * W:], LANES)], axis=1)

    rw_cols = 3 * W + 4 * LANES
    w_all = regroup(w_in).astype(BF16)
    mu_p = regroup(jnp.pad(mu_shift.reshape(1, rw_n), ((0, 0), (0, w_in.shape[1] - rw_n))))[:, :rw_cols]
    bf_p = _pad_cols(b_f.reshape(1, heads), LANES)
    wdw = _pad_rows(w_up_decay, LANES).astype(BF16)
    waw = _pad_rows(w_up_a, LANES).astype(BF16)
    wgw = _pad_rows(w_up_g, 2 * LANES).astype(BF16)

    mod3 = _ada(c, w_ada, b_ada).reshape(B, N_MOD, D)
    rw, fx, fcum = _inproj(x, mod3, norm1_g, w_all, mu_p, bf_p, rw_cols)
    y_rw = _rwkv(rw, w0, wdw, a0, waw, wgw, k_k, k_a, r_k.reshape(-1), gn_g, gn_b, W)
    qa, ka, vt = _foxprep(fx, fcum, q_norm_g, k_norm_g, W)
    y_fx = _fox(qa, ka, vt, fx, fox_out_g, W)
    return y_rw, y_fx, mod3


def kernel(x, c, w_ada, b_ada, norm1_g, w_in, mu_shift, w0, w_up_decay, a0, w_up_a, w_up_g,
           k_k, k_a, r_k, gn_g, gn_b, b_f, q_norm_g, k_norm_g, fox_out_g, w_o, norm2_g,
           w_mlp1, w_mlp2, final_g):
    y_rw, y_fx, mod3 = _branches(x, c, w_ada, b_ada, norm1_g, w_in, mu_shift, w0, w_up_decay,
                                 a0, w_up_a, w_up_g, k_k, k_a, r_k, gn_g, gn_b, b_f,
                                 q_norm_g, k_norm_g, fox_out_g)
    return _outmlp(x, y_rw, y_fx, mod3, w_o.astype(BF16), norm2_g,
                   w_mlp1.astype(BF16), w_mlp2.astype(BF16), final_g)
```

```python
import functools
import math

import jax
import jax.numpy as jnp
import numpy as np
from jax import lax
from jax.experimental import pallas as pl
from jax.experimental.pallas import tpu as pltpu

F32 = jnp.float32
BF16 = jnp.bfloat16

HEAD_DIM = 64
LANES = 128
MXU_TILE = 256
NORM_EPS = 1e-6
GN_EPS = 64e-5
NEG_INF = -1e30
LOG2E = 1.4426950408889634
N_MOD = 6
DECAY_LORA = 64
AAA_LORA = 64
GATE_LORA = 160
RW_CHUNK = 64
V_ROWS = HEAD_DIM + 16
VMEM_LIMIT = 56 * 1024 * 1024

_HI = lax.Precision.HIGHEST


def _dot(a, b):
    return jnp.dot(a, b, preferred_element_type=F32)


def _dot_hi(a, b):
    return jnp.dot(a, b, preferred_element_type=F32, precision=_HI)


def _dot_nt(a, b):
    return lax.dot_general(a, b, (((1,), (1,)), ((), ())), preferred_element_type=F32)


def _dot_tn(a, b):
    return lax.dot_general(a, b, (((0,), (0,)), ((), ())), preferred_element_type=F32)


def _bf(x):
    return x.astype(BF16)


def _sigmoid(x):
    return 1.0 / (1.0 + jnp.exp(-x))


def _const_spec(shape):
    n = len(shape)
    return pl.BlockSpec(shape, lambda *_: (0,) * n)


def _ada_kernel(c_ref, w_ref, b_ref, o_ref):
    c = c_ref[...]
    o_ref[...] = _dot_hi(c * _sigmoid(c), w_ref[...]) + b_ref[...]


def _ada(c, w_ada, b_ada):
    B, D = c.shape
    n = w_ada.shape[1]
    return pl.pallas_call(
        _ada_kernel,
        out_shape=jax.ShapeDtypeStruct((B, n), F32),
        grid=(n // D,),
        in_specs=[pl.BlockSpec((B, D), lambda j: (0, 0)),
                  pl.BlockSpec((D, D), lambda j: (0, j)),
                  pl.BlockSpec((1, D), lambda j: (0, j))],
        out_specs=pl.BlockSpec((B, D), lambda j: (0, j)),
        name="ada",
    )(c, w_ada, b_ada.reshape(1, n))


def _inproj_kernel(x_ref, mod_ref, g_ref, w_ref, mu_ref, bf_ref,
                   rw_ref, fx_ref, fcum_ref, prev_scr, fcar_scr, *, rw_cols):
    t = pl.program_id(1)
    tm = x_ref.shape[1]

    @pl.when(t == 0)
    def _():
        prev_scr[...] = jnp.zeros_like(prev_scr)
        fcar_scr[...] = jnp.zeros_like(fcar_scr)

    x = x_ref[0]
    shift = mod_ref[0, 0:1, :]
    scale = mod_ref[0, 1:2, :]
    ms = jnp.mean(x * x, axis=-1, keepdims=True)
    u = (x * lax.rsqrt(ms + NORM_EPS) * g_ref[...]) * (1.0 + scale) + shift
    u = u.astype(BF16)

    p = _dot(u, w_ref[:, :rw_cols])
    row = lax.broadcasted_iota(jnp.int32, p.shape, 0)
    prev = jnp.where(row == 0, prev_scr[0:1, :], pltpu.roll(p, shift=1, axis=0))
    prev_scr[0:1, :] = p[tm - 1:tm, :]
    rw_ref[0] = p + (prev - p) * mu_ref[...]

    pf = _dot(u, w_ref[:, rw_cols:])
    fx_ref[0] = pf
    z = pf[:, pf.shape[1] - LANES:] + bf_ref[...]
    logf = jnp.minimum(z, 0.0) - jnp.log(1.0 + jnp.exp(-jnp.abs(z)))
    r2 = lax.broadcasted_iota(jnp.int32, (tm, tm), 0)
    c2 = lax.broadcasted_iota(jnp.int32, (tm, tm), 1)
    tril = (c2 <= r2).astype(F32)
    fcum = _dot_hi(tril, logf) + fcar_scr[0:1, :]
    fcum_ref[0] = fcum
    fcar_scr[0:1, :] = fcum[tm - 1:tm, :]


def _inproj(x, mod3, norm1_g, w_all, mu_p, bf_p, rw_cols, tm=256):
    B, T, D = x.shape
    ncols = w_all.shape[1]
    fx_cols = ncols - rw_cols
    return pl.pallas_call(
        functools.partial(_inproj_kernel, rw_cols=rw_cols),
        out_shape=(jax.ShapeDtypeStruct((B, T, rw_cols), F32),
                   jax.ShapeDtypeStruct((B, T, fx_cols), F32),
                   jax.ShapeDtypeStruct((B, T, LANES), F32)),
        grid=(B, T // tm),
        in_specs=[pl.BlockSpec((1, tm, D), lambda b, t: (b, t, 0)),
                  pl.BlockSpec((1, N_MOD, D), lambda b, t: (b, 0, 0)),
                  _const_spec((1, D)),
                  _const_spec((D, ncols)),
                  _const_spec((1, rw_cols)),
                  _const_spec((1, LANES))],
        out_specs=(pl.BlockSpec((1, tm, rw_cols), lambda b, t: (b, t, 0)),
                   pl.BlockSpec((1, tm, fx_cols), lambda b, t: (b, t, 0)),
                   pl.BlockSpec((1, tm, LANES), lambda b, t: (b, t, 0))),
        scratch_shapes=[pltpu.VMEM((8, rw_cols), F32), pltpu.VMEM((8, LANES), F32)],
        compiler_params=pltpu.CompilerParams(
            dimension_semantics=("arbitrary", "arbitrary"), vmem_limit_bytes=VMEM_LIMIT),
        name="inproj",
    )(x, mod3, norm1_g.reshape(1, D), w_all, mu_p, bf_p)


def _segsum(x, bd, split=False):
    tile = bd.shape[0]
    hi = x.astype(BF16)
    lo = (x - hi.astype(F32)).astype(BF16) if split else None
    cols = []
    for j in range(x.shape[1] // tile):
        sl = slice(j * tile, (j + 1) * tile)
        s = _dot(hi[:, sl], bd)
        cols.append(s + _dot(lo[:, sl], bd) if split else s)
    return jnp.concatenate(cols, axis=1)


def _rwkv_kernel(p_ref, w0_ref, wdw_ref, a0_ref, waw_ref, wgw_ref, kk_ref, ka_ref, rk_ref,
                 gng_ref, gnb_ref, o_ref, s_scr, *, width):
    t = pl.program_id(1)
    rows = p_ref.shape[1]
    L = RW_CHUNK
    n_chunks = rows // L
    W = width
    heads = W // HEAD_DIM

    @pl.when(t == 0)
    def _():
        s_scr[...] = jnp.zeros_like(s_scr)

    p = p_ref[0]
    r = p[:, 0:W]
    k = p[:, W:2 * W]
    v = p[:, 2 * W:3 * W]
    o0 = 3 * W
    wd = p[:, o0:o0 + LANES]
    ad = p[:, o0 + LANES:o0 + 2 * LANES]
    gd = p[:, o0 + 2 * LANES:o0 + 4 * LANES]

    z = w0_ref[...] + _dot(_bf(jnp.tanh(wd)), wdw_ref[...])
    softplus = jnp.maximum(-z, 0.0) + jnp.log(1.0 + jnp.exp(-jnp.abs(z)))
    ld = -jnp.exp(-softplus - 0.5)
    a_sig = _sigmoid(a0_ref[...] + _dot(_bf(ad), waw_ref[...]))
    g = _dot(_bf(_sigmoid(gd)), wgw_ref[...])

    ri = lax.broadcasted_iota(jnp.int32, (MXU_TILE, MXU_TILE), 0)
    ci = lax.broadcasted_iota(jnp.int32, (MXU_TILE, MXU_TILE), 1)
    bd = ((ri // HEAD_DIM) == (ci // HEAD_DIM)).astype(BF16)

    kk = k * kk_ref[...]
    kk = kk / jnp.maximum(jnp.sqrt(_segsum(kk * kk, bd, split=True)), 1e-12)
    k2 = k * (1.0 + (a_sig - 1.0) * ka_ref[...])
    b = kk * a_sig
    bonus = _segsum(r * k2 * rk_ref[...], bd) * v

    rl = lax.broadcasted_iota(jnp.int32, (L, L), 0)
    cl = lax.broadcasted_iota(jnp.int32, (L, L), 1)
    incl = cl <= rl
    strict = cl < rl
    tril = incl.astype(BF16)
    eye = (cl == rl).astype(F32)
    ld_hi = _bf(ld)
    ld_lo = _bf(ld - ld_hi.astype(F32))
    cums, tots = [], []
    for c in range(n_chunks):
        rs = slice(c * L, (c + 1) * L)
        cum_c = _dot(tril, ld_hi[rs]) + _dot(tril, ld_lo[rs])
        cums.append(cum_c)
        tots.append(cum_c[L - 1:L, :])
    cum = jnp.concatenate(cums, axis=0)
    tot_b = jnp.concatenate([jnp.broadcast_to(tc, (L, W)) for tc in tots], axis=0)

    e_neg = jnp.exp(-cum)
    e_rem = jnp.exp(tot_b - cum)
    rt = r * jnp.exp(cum)
    rt_m = _bf(rt)
    at = _bf(-kk * jnp.exp(cum - ld))
    bt = _bf(b * e_neg)
    kt = _bf(k2 * e_neg)
    bh = _bf(b * e_rem)
    kh = _bf(k2 * e_rem)
    vm = _bf(v)

    G2 = 2 * LANES
    groups = W // LANES
    items = [(c, j) for c in range(n_chunks) for j in range(groups)]
    rsl = lambda c: slice(c * L, (c + 1) * L)
    gsl = lambda j: slice(j * LANES, (j + 1) * LANES)
    lane1 = lax.broadcasted_iota(jnp.int32, (L, LANES), 1)
    lane2 = lax.broadcasted_iota(jnp.int32, (L, G2), 1)
    row2 = lax.broadcasted_iota(jnp.int32, (L, G2), 0)
    first1 = lane1 < HEAD_DIM
    first2 = (lane2 % LANES) < HEAD_DIM
    strict2 = (lane2 % HEAD_DIM) < row2
    incl2 = (lane2 % HEAD_DIM) <= row2
    upper2 = lane2 >= LANES
    eye2 = ((lane1 % HEAD_DIM) == lax.broadcasted_iota(jnp.int32, (L, LANES), 0)).astype(F32)
    zero1 = jnp.zeros((L, LANES), BF16)
    rr = lax.broadcasted_iota(jnp.int32, (LANES, LANES), 0) < HEAD_DIM
    cc = lax.broadcasted_iota(jnp.int32, (LANES, LANES), 1) < HEAD_DIM
    diag_blocks = rr == cc

    def split(m, mask):
        return jnp.concatenate([jnp.where(mask, m, 0), jnp.where(mask, 0, m)], axis=0)

    g_bot, akm, zz = {}, {}, {}
    for c, j in items:
        ar = jnp.concatenate([at[rsl(c), gsl(j)], rt_m[rsl(c), gsl(j)]], axis=0)
        rhs = jnp.concatenate([split(bt[rsl(c), gsl(j)], first1),
                               split(kt[rsl(c), gsl(j)], first1)], axis=0)
        gm = _dot_nt(ar, rhs)
        top = jnp.where(strict2, gm[:L], 0.0)
        g_bot[c, j] = _bf(jnp.where(incl2, gm[L:], 0.0))
        akm[c, j] = _bf(top[:, LANES:])
        zz[c, j] = jnp.concatenate([top[:, :LANES], eye2], axis=1)
    for _ in range(int(math.log2(L))):
        for it in items:
            zb = _bf(zz[it])
            out = _dot(zb[:, :LANES], split(zb, first2))
            zz[it] = out + jnp.where(upper2, zz[it], 0.0)
    vbd = {(c, j): split(vm[rsl(c), gsl(j)], first1) for c, j in items}
    akv = {it: _bf(_dot(akm[it], vbd[it])) for it in items}
    x = {}
    for c, j in items:
        zed = jnp.concatenate([at[rsl(c), gsl(j)], akv[c, j]], axis=1)
        x[c, j] = _bf(_dot(_bf(zz[c, j][:, LANES:]), split(zed, first2)))
    r_eff, y0, mt, nt = {}, {}, {}, {}
    for c, j in items:
        w4 = jnp.concatenate([split(x[c, j], first2),
                              jnp.concatenate([jnp.concatenate([zero1, zero1], axis=0),
                                               vbd[c, j]], axis=1)], axis=0)
        out = _dot(g_bot[c, j], w4)
        r_eff[c, j] = _bf(rt[rsl(c), gsl(j)] + out[:, :LANES])
        y0[c, j] = out[:, LANES:]
    for c, j in items:
        lhs = jnp.concatenate([bh[rsl(c), gsl(j)], kh[rsl(c), gsl(j)]], axis=0)
        rhs = jnp.concatenate([x[c, j], jnp.concatenate([zero1, vm[rsl(c), gsl(j)]], axis=1)],
                              axis=0)
        pt = _dot_tn(lhs, rhs)
        mt[c, j] = _bf(jnp.where(diag_blocks, pt[:, :LANES], 0.0))
        nt[c, j] = jnp.where(diag_blocks, pt[:, LANES:], 0.0)
    e_cols = []
    for j in range(groups):
        e_rows = jnp.concatenate([jnp.exp(tots[c][:, gsl(j)]) for c in range(n_chunks)]
                                 + [jnp.zeros((LANES - n_chunks, LANES), F32)], axis=0)
        e_cols.append(e_rows.T)
    state = [s_scr[j] for j in range(groups)]
    y_rows = []
    for c in range(n_chunks):
        hb = [_bf(s) for s in state]
        y_rows.append(jnp.concatenate(
            [_dot(r_eff[c, j], hb[j]) + y0[c, j] for j in range(groups)], axis=1))
        state = [state[j] * e_cols[j][:, c:c + 1] + _dot(mt[c, j], hb[j]) + nt[c, j]
                 for j in range(groups)]
    for j in range(groups):
        s_scr[j] = state[j]
    y = jnp.concatenate(y_rows, axis=0)

    inv_n = 1.0 / HEAD_DIM
    mean = _segsum(y, bd) * inv_n
    d = y - mean
    var = _segsum(d * d, bd) * inv_n
    yn = d * lax.rsqrt(var + GN_EPS) * gng_ref[...] + gnb_ref[...]
    o_ref[0] = ((yn + bonus) * g).astype(o_ref.dtype)


def _rwkv(rw, w0, wdw, a0, waw, wgw, k_k, k_a, r_k, gn_g, gn_b, width, rows=256):
    B, T, cols = rw.shape
    row = lambda a: a.reshape(1, width)
    vec = _const_spec((1, width))
    return pl.pallas_call(
        functools.partial(_rwkv_kernel, width=width),
        out_shape=jax.ShapeDtypeStruct((B, T, width), BF16),
        grid=(B, T // rows),
        in_specs=[pl.BlockSpec((1, rows, cols), lambda b, t: (b, t, 0)),
                  vec, _const_spec(wdw.shape), vec, _const_spec(waw.shape),
                  _const_spec(wgw.shape), vec, vec, vec, vec, vec],
        out_specs=pl.BlockSpec((1, rows, width), lambda b, t: (b, t, 0)),
        scratch_shapes=[pltpu.VMEM((width // LANES, LANES, LANES), F32)],
        compiler_params=pltpu.CompilerParams(
            dimension_semantics=("arbitrary", "arbitrary"), vmem_limit_bytes=VMEM_LIMIT),
        name="rwkv",
    )(rw, row(w0), wdw, row(a0), waw, wgw, row(k_k), row(k_a), row(r_k), row(gn_g), row(gn_b))


def _fox_aug_tables(heads):
    sel = np.zeros((3 * LANES, 2 * heads * LANES), np.float32)
    bias = np.zeros((1, 2 * heads * LANES), np.float32)
    for h in range(heads):
        base = h * LANES + (HEAD_DIM if h % 2 == 0 else 0)
        kbase = heads * LANES + base
        for piece in range(3):
            sel[piece * LANES + h, base + piece] = 1.0
            sel[piece * LANES + h, kbase + 3 + piece] = -1.0
            bias[0, base + 3 + piece] = 1.0
            bias[0, kbase + piece] = 1.0
    return jnp.asarray(sel, BF16), jnp.asarray(bias)


def _foxprep_kernel(fx_ref, fcum_ref, qg_ref, kg_ref, sel_ref, bias_ref, qa_ref, ka_ref, v_ref,
                    *, width):
    W = width
    heads = W // HEAD_DIM
    pf = fx_ref[0]
    tm = pf.shape[0]
    fcum = fcum_ref[0] * LOG2E
    f_hi = fcum.astype(BF16)
    r1 = fcum - f_hi.astype(F32)
    f_mid = r1.astype(BF16)
    f_lo = (r1 - f_mid.astype(F32)).astype(BF16)
    aug = _dot(jnp.concatenate([f_hi, f_mid, f_lo], axis=1), sel_ref[...]) + bias_ref[...]
    first = lax.broadcasted_iota(jnp.int32, (tm, LANES), 1) < HEAD_DIM
    for side, (gain_ref, out_ref) in enumerate(((qg_ref, qa_ref), (kg_ref, ka_ref))):
        for g in range(W // LANES):
            x = pf[:, side * W + g * LANES:side * W + (g + 1) * LANES]
            sq = x * x
            lo = jnp.sum(jnp.where(first, sq, 0.0), axis=-1, keepdims=True)
            hi = jnp.sum(jnp.where(first, 0.0, sq), axis=-1, keepdims=True)
            ms = jnp.where(first, lo, hi) * (1.0 / HEAD_DIM)
            xn = x * lax.rsqrt(ms + NORM_EPS) * gain_ref[...]
            for hh in range(2):
                h = 2 * g + hh
                a = aug[:, (side * heads + h) * LANES:(side * heads + h + 1) * LANES]
                keep = first if hh == 0 else jnp.logical_not(first)
                out_ref[0, h] = jnp.where(keep, xn, a).astype(BF16)
    pad = (lax.broadcasted_iota(jnp.int32, (V_ROWS - HEAD_DIM, tm), 0) == 0).astype(BF16)
    for g in range(W // LANES):
        vt = pf[:, 2 * W + g * LANES:2 * W + (g + 1) * LANES].T.astype(BF16)
        for hh in range(2):
            v_ref[0, 2 * g + hh, 0, :HEAD_DIM, :] = vt[hh * HEAD_DIM:(hh + 1) * HEAD_DIM]
            v_ref[0, 2 * g + hh, 0, HEAD_DIM:, :] = pad


def _foxprep(fx, fcum, q_norm_g, k_norm_g, width, tm=512):
    B, T, cols = fx.shape
    heads = width // HEAD_DIM
    sel, bias = _fox_aug_tables(heads)
    scale = LOG2E / math.sqrt(HEAD_DIM)
    qg2 = jnp.tile(q_norm_g * scale, 2).reshape(1, LANES)
    kg2 = jnp.tile(k_norm_g, 2).reshape(1, LANES)
    return pl.pallas_call(
        functools.partial(_foxprep_kernel, width=width),
        out_shape=(jax.ShapeDtypeStruct((B, heads, T, LANES), BF16),
                   jax.ShapeDtypeStruct((B, heads, T, LANES), BF16),
                   jax.ShapeDtypeStruct((B, heads, T // tm, V_ROWS, tm), BF16)),
        grid=(B, T // tm),
        in_specs=[pl.BlockSpec((1, tm, cols), lambda b, t: (b, t, 0)),
                  pl.BlockSpec((1, tm, LANES), lambda b, t: (b, t, 0)),
                  _const_spec((1, LANES)), _const_spec((1, LANES)),
                  _const_spec(sel.shape), _const_spec(bias.shape)],
        out_specs=(pl.BlockSpec((1, heads, tm, LANES), lambda b, t: (b, 0, t, 0)),
                   pl.BlockSpec((1, heads, tm, LANES), lambda b, t: (b, 0, t, 0)),
                   pl.BlockSpec((1, heads, 1, V_ROWS, tm), lambda b, t: (b, 0, t, 0, 0))),
        compiler_params=pltpu.CompilerParams(
            dimension_semantics=("arbitrary", "arbitrary"), vmem_limit_bytes=VMEM_LIMIT),
        name="foxprep",
    )(fx, fcum, qg2, kg2, sel, bias)


def _fox_kernel(qa_ref, ka_ref, vt_ref, og_ref, g_ref, o_ref,
                m_scr, acc_scr, s_scr, p_scr, al_scr, *, cb):
    qi = pl.program_id(2)
    tq = qa_ref.shape[2]
    tk = tq
    chains = [(hh, c) for hh in range(2) for c in range(tq // cb)]
    csl = lambda c: slice(c * cb, (c + 1) * cb)

    m_scr[...] = jnp.full_like(m_scr, NEG_INF)
    acc_scr[...] = jnp.zeros_like(acc_scr)
    p_scr[1] = jnp.zeros(p_scr.shape[1:], p_scr.dtype)
    al_scr[1] = jnp.ones(al_scr.shape[1:], al_scr.dtype)

    def logits(t, buf):
        k_start = pl.multiple_of(t * tk, tk)
        for hh, c in chains:
            s_scr[buf, hh, :, csl(c)] = _dot_nt(ka_ref[0, hh, pl.ds(k_start, tk), :],
                                                qa_ref[0, hh, csl(c), :])

    def softmax(buf, masked):
        for hh, c in chains:
            s = s_scr[buf, hh, :, csl(c)]
            if masked:
                key = lax.broadcasted_iota(jnp.int32, (tk, cb), 0)
                qry = c * cb + lax.broadcasted_iota(jnp.int32, (tk, cb), 1)
                s = jnp.where(key <= qry, s, NEG_INF)
            m_prev = m_scr[hh, :, csl(c)]
            m_new = jnp.maximum(m_prev, jnp.max(s, axis=0, keepdims=True))
            m_scr[hh, :, csl(c)] = m_new
            al_scr[buf, hh, :, csl(c)] = jnp.exp2(m_prev - m_new)
            p_scr[buf, hh, :, csl(c)] = jnp.exp2(s - m_new).astype(BF16)

    def values(t, buf):
        for hh, c in chains:
            pv = _dot(vt_ref[0, hh, jnp.maximum(t, 0)], p_scr[buf, hh, :, csl(c)])
            acc_scr[hh, :, csl(c)] = al_scr[buf, hh, :, csl(c)] * acc_scr[hh, :, csl(c)] + pv

    def step(t, buf):
        logits(t, buf)
        softmax(1 - buf, masked=False)
        values(t - 2, buf)

    def two_steps(u, carry):
        step(2 * u + 1, 1)
        step(2 * u + 2, 0)
        return carry

    def drain(buf):
        softmax(buf, masked=True)
        values(qi - 1, 1 - buf)
        values(qi, buf)

    logits(0, 0)
    lax.fori_loop(0, qi // 2, two_steps, 0)

    @pl.when(qi % 2 == 1)
    def _():
        step(qi, 1)
        drain(1)

    @pl.when(qi % 2 == 0)
    def _():
        drain(0)

    normed = []
    for hh in range(2):
        acc = acc_scr[hh]
        o = acc[:HEAD_DIM] / acc[HEAD_DIM:HEAD_DIM + 1]
        ms = jnp.mean(o * o, axis=0, keepdims=True)
        normed.append(o * lax.rsqrt(ms + NORM_EPS))
    on = jnp.concatenate(normed, axis=0).T
    o_ref[0] = (on * g_ref[...] * _sigmoid(og_ref[0])).astype(o_ref.dtype)


def _fox(qa, ka, vt, fx, fox_out_g, width, cb=256):
    B, H, T, _ = qa.shape
    tq = vt.shape[4]
    og_blk = 3 * width // LANES
    return pl.pallas_call(
        functools.partial(_fox_kernel, cb=cb),
        out_shape=jax.ShapeDtypeStruct((B, T, width), BF16),
        grid=(B, H // 2, T // tq),
        in_specs=[pl.BlockSpec((1, 2, tq, LANES), lambda b, h, q: (b, h, q, 0)),
                  pl.BlockSpec((1, 2, T, LANES), lambda b, h, q: (b, h, 0, 0)),
                  pl.BlockSpec((1, 2, T // tq, V_ROWS, tq), lambda b, h, q: (b, h, 0, 0, 0)),
                  pl.BlockSpec((1, tq, LANES), lambda b, h, q: (b, q, og_blk + h)),
                  pl.BlockSpec((1, LANES), lambda b, h, q: (0, h))],
        out_specs=pl.BlockSpec((1, tq, LANES), lambda b, h, q: (b, q, h)),
        scratch_shapes=[pltpu.VMEM((2, 1, tq), F32), pltpu.VMEM((2, V_ROWS, tq), F32),
                        pltpu.VMEM((2, 2, tq, tq), F32), pltpu.VMEM((2, 2, tq, tq), BF16),
                        pltpu.VMEM((2, 2, 1, tq), F32)],
        compiler_params=pltpu.CompilerParams(
            dimension_semantics=("arbitrary",) * 3, vmem_limit_bytes=VMEM_LIMIT),
        name="fox",
    )(qa, ka, vt, fx, fox_out_g.reshape(1, width))


def _outmlp_kernel(x_ref, yrw_ref, yfx_ref, mod_ref, wo_ref, g2_ref, w1_ref, w2_ref, gf_ref,
                   o_ref, *, ff_tile):
    x = x_ref[0]
    half = yrw_ref.shape[2]
    gate1 = mod_ref[0, 2:3, :]
    shift2 = mod_ref[0, 3:4, :]
    scale2 = mod_ref[0, 4:5, :]
    gate2 = mod_ref[0, 5:6, :]
    y = _dot(yrw_ref[0], wo_ref[:half, :]) + _dot(yfx_ref[0], wo_ref[half:, :])
    h1 = x + gate1 * y
    ms = jnp.mean(h1 * h1, axis=-1, keepdims=True)
    u = ((h1 * lax.rsqrt(ms + NORM_EPS) * g2_ref[...]) * (1.0 + scale2) + shift2).astype(BF16)
    acc = jnp.zeros_like(x)
    for j in range(w1_ref.shape[1] // ff_tile):
        hid = jnp.maximum(_dot(u, w1_ref[:, j * ff_tile:(j + 1) * ff_tile]), 0.0)
        acc = acc + _dot((hid * hid).astype(BF16), w2_ref[j * ff_tile:(j + 1) * ff_tile, :])
    h2 = h1 + gate2 * acc
    ms2 = jnp.mean(h2 * h2, axis=-1, keepdims=True)
    o_ref[0] = h2 * lax.rsqrt(ms2 + NORM_EPS) * gf_ref[...]


def _outmlp(x, y_rw, y_fx, mod3, w_o, norm2_g, w1, w2, final_g, tm=512, ff_tile=1024):
    B, T, D = x.shape
    half = y_rw.shape[2]
    return pl.pallas_call(
        functools.partial(_outmlp_kernel, ff_tile=ff_tile),
        out_shape=jax.ShapeDtypeStruct((B, T, D), F32),
        grid=(B, T // tm),
        in_specs=[pl.BlockSpec((1, tm, D), lambda b, t: (b, t, 0)),
                  pl.BlockSpec((1, tm, half), lambda b, t: (b, t, 0)),
                  pl.BlockSpec((1, tm, half), lambda b, t: (b, t, 0)),
                  pl.BlockSpec((1, N_MOD, D), lambda b, t: (b, 0, 0)),
                  _const_spec(w_o.shape), _const_spec((1, D)),
                  _const_spec(w1.shape), _const_spec(w2.shape), _const_spec((1, D))],
        out_specs=pl.BlockSpec((1, tm, D), lambda b, t: (b, t, 0)),
        compiler_params=pltpu.CompilerParams(
            dimension_semantics=("arbitrary", "arbitrary"), vmem_limit_bytes=VMEM_LIMIT),
        name="outmlp",
    )(x, y_rw, y_fx, mod3, w_o, norm2_g.reshape(1, D), w1, w2, final_g.reshape(1, D))


def _pad_cols(a, n):
    return jnp.pad(a, ((0, 0), (0, n - a.shape[1])))


def _pad_rows(a, n):
    return jnp.pad(a, ((0, n - a.shape[0]), (0, 0)))


def _branches(x, c, w_ada, b_ada, norm1_g, w_in, mu_shift, w0, w_up_decay, a0, w_up_a, w_up_g,
              k_k, k_a, r_k, gn_g, gn_b, b_f, q_norm_g, k_norm_g, fox_out_g):
    B, T, D = x.shape
    W = w0.shape[0]
    heads = b_f.shape[0]
    rw_n = 3 * W + DECAY_LORA + AAA_LORA + GATE_LORA

    def regroup(a):
        rw, fx = a[:, :rw_n], a[:, rw_n:]
        o = 3 * W
        return jnp.concatenate([
            rw[:, :o],
            _pad_cols(rw[:, o:o + DECAY_LORA], LANES),
            _pad_cols(rw[:, o + DECAY_LORA:o + DECAY_LORA + AAA_LORA], LANES),
            _pad_cols(rw[:, o + DECAY_LORA + AAA_LORA:], 2 * LANES),
            fx[:, :4 * W],
            _pad_cols(fx[:, 4 * W:], LANES)], axis=1)

    rw_cols = 3 * W + 4 * LANES
    w_all = regroup(w_in).astype(BF16)
    mu_p = regroup(jnp.pad(mu_shift.reshape(1, rw_n), ((0, 0), (0, w_in.shape[1] - rw_n))))[:, :rw_cols]
    bf_p = _pad_cols(b_f.reshape(1, heads), LANES)
    wdw = _pad_rows(w_up_decay, LANES).astype(BF16)
    waw = _pad_rows(w_up_a, LANES).astype(BF16)
    wgw = _pad_rows(w_up_g, 2 * LANES).astype(BF16)

    mod3 = _ada(c, w_ada, b_ada).reshape(B, N_MOD, D)
    rw, fx, fcum = _inproj(x, mod3, norm1_g, w_all, mu_p, bf_p, rw_cols)
    y_rw = _rwkv(rw, w0, wdw, a0, waw, wgw, k_k, k_a, r_k.reshape(-1), gn_g, gn_b, W)
    qa, ka, vt = _foxprep(fx, fcum, q_norm_g, k_norm_g, W)
    y_fx = _fox(qa, ka, vt, fx, fox_out_g, W)
    return y_rw, y_fx, mod3


def kernel(x, c, w_ada, b_ada, norm1_g, w_in, mu_shift, w0, w_up_decay, a0, w_up_a, w_up_g,
           k_k, k_a, r_k, gn_g, gn_b, b_f, q_norm_g, k_norm_g, fox_out_g, w_o, norm2_g,
           w_mlp1, w_mlp2, final_g):
    y_rw, y_fx, mod3 = _branches(x, c, w_ada, b_ada, norm1_g, w_in, mu_shift, w0, w_up_decay,
                                 a0, w_up_a, w_up_g, k_k, k_a, r_k, gn_g, gn_b, b_f,
                                 q_norm_g, k_norm_g, fox_out_g)
    return _outmlp(x, y_rw, y_fx, mod3, w_o.astype(BF16), norm2_g,
                   w_mlp1.astype(BF16), w_mlp2.astype(BF16), final_g)
```

```python
import functools
import math

import jax
import jax.numpy as jnp
import numpy as np
from jax import lax
from jax.experimental import pallas as pl
from jax.experimental.pallas import tpu as pltpu

F32 = jnp.float32
BF16 = jnp.bfloat16

HEAD_DIM = 64
LANES = 128
MXU_TILE = 256
NORM_EPS = 1e-6
GN_EPS = 64e-5
NEG_INF = -1e30
LOG2E = 1.4426950408889634
N_MOD = 6
DECAY_LORA = 64
AAA_LORA = 64
GATE_LORA = 160
RW_CHUNK = 64
V_ROWS = HEAD_DIM + 16
VMEM_LIMIT = 56 * 1024 * 1024

_HI = lax.Precision.HIGHEST


def _dot(a, b):
    return jnp.dot(a, b, preferred_element_type=F32)


def _dot_hi(a, b):
    return jnp.dot(a, b, preferred_element_type=F32, precision=_HI)


def _dot_nt(a, b):
    return lax.dot_general(a, b, (((1,), (1,)), ((), ())), preferred_element_type=F32)


def _dot_tn(a, b):
    return lax.dot_general(a, b, (((0,), (0,)), ((), ())), preferred_element_type=F32)


def _bf(x):
    return x.astype(BF16)


def _sigmoid(x):
    return 1.0 / (1.0 + jnp.exp(-x))


def _const_spec(shape, single_buffer=False):
    n = len(shape)
    mode = pl.Buffered(1) if single_buffer else None
    return pl.BlockSpec(shape, lambda *_: (0,) * n, pipeline_mode=mode)


def _ada_kernel(c_ref, w_ref, b_ref, o_ref):
    c = c_ref[...]
    o_ref[...] = _dot_hi(c * _sigmoid(c), w_ref[...]) + b_ref[...]


def _ada(c, w_ada, b_ada):
    B, D = c.shape
    n = w_ada.shape[1]
    return pl.pallas_call(
        _ada_kernel,
        out_shape=jax.ShapeDtypeStruct((B, n), F32),
        grid=(n // D,),
        in_specs=[pl.BlockSpec((B, D), lambda j: (0, 0)),
                  pl.BlockSpec((D, D), lambda j: (0, j)),
                  pl.BlockSpec((1, D), lambda j: (0, j))],
        out_specs=pl.BlockSpec((B, D), lambda j: (0, j)),
        name="ada",
    )(c, w_ada, b_ada.reshape(1, n))


def _inproj_kernel(x_ref, mod_ref, g_ref, w_ref, mu_ref, bf_ref,
                   rw_ref, fx_ref, fcum_ref, prev_scr, fcar_scr, *, rw_cols):
    t = pl.program_id(1)
    tm = x_ref.shape[1]

    @pl.when(t == 0)
    def _():
        prev_scr[...] = jnp.zeros_like(prev_scr)
        fcar_scr[...] = jnp.zeros_like(fcar_scr)

    x = x_ref[0]
    shift = mod_ref[0, 0:1, :]
    scale = mod_ref[0, 1:2, :]
    ms = jnp.mean(x * x, axis=-1, keepdims=True)
    u = (x * lax.rsqrt(ms + NORM_EPS) * g_ref[...]) * (1.0 + scale) + shift
    u = u.astype(BF16)

    p = _dot(u, w_ref[:, :rw_cols])
    row = lax.broadcasted_iota(jnp.int32, p.shape, 0)
    prev = jnp.where(row == 0, prev_scr[0:1, :], pltpu.roll(p, shift=1, axis=0))
    prev_scr[0:1, :] = p[tm - 1:tm, :]
    rw_ref[0] = p + (prev - p) * mu_ref[...]

    pf = _dot(u, w_ref[:, rw_cols:])
    fx_ref[0] = pf
    z = pf[:, pf.shape[1] - LANES:] + bf_ref[...]
    logf = jnp.minimum(z, 0.0) - jnp.log(1.0 + jnp.exp(-jnp.abs(z)))
    sub = min(tm, MXU_TILE)
    r2 = lax.broadcasted_iota(jnp.int32, (sub, sub), 0)
    c2 = lax.broadcasted_iota(jnp.int32, (sub, sub), 1)
    tril = (c2 <= r2).astype(BF16)
    f_hi = logf.astype(BF16)
    r1 = logf - f_hi.astype(F32)
    f_mid = r1.astype(BF16)
    f_lo = (r1 - f_mid.astype(F32)).astype(BF16)
    carry = fcar_scr[0:1, :]
    for i in range(tm // sub):
        rs = slice(i * sub, (i + 1) * sub)
        fcum = (_dot(tril, f_hi[rs]) + _dot(tril, f_mid[rs])) + _dot(tril, f_lo[rs]) + carry
        fcum_ref[0, rs, :] = fcum
        carry = fcum[sub - 1:sub, :]
    fcar_scr[0:1, :] = carry


def _inproj(x, mod3, norm1_g, w_all, mu_p, bf_p, rw_cols, tm=512):
    B, T, D = x.shape
    ncols = w_all.shape[1]
    fx_cols = ncols - rw_cols
    return pl.pallas_call(
        functools.partial(_inproj_kernel, rw_cols=rw_cols),
        out_shape=(jax.ShapeDtypeStruct((B, T, rw_cols), F32),
                   jax.ShapeDtypeStruct((B, T, fx_cols), F32),
                   jax.ShapeDtypeStruct((B, T, LANES), F32)),
        grid=(B, T // tm),
        in_specs=[pl.BlockSpec((1, tm, D), lambda b, t: (b, t, 0)),
                  pl.BlockSpec((1, N_MOD, D), lambda b, t: (b, 0, 0)),
                  _const_spec((1, D)),
                  _const_spec((D, ncols), single_buffer=True),
                  _const_spec((1, rw_cols)),
                  _const_spec((1, LANES))],
        out_specs=(pl.BlockSpec((1, tm, rw_cols), lambda b, t: (b, t, 0)),
                   pl.BlockSpec((1, tm, fx_cols), lambda b, t: (b, t, 0)),
                   pl.BlockSpec((1, tm, LANES), lambda b, t: (b, t, 0))),
        scratch_shapes=[pltpu.VMEM((8, rw_cols), F32), pltpu.VMEM((8, LANES), F32)],
        compiler_params=pltpu.CompilerParams(
            dimension_semantics=("arbitrary", "arbitrary"), vmem_limit_bytes=VMEM_LIMIT),
        name="inproj",
    )(x, mod3, norm1_g.reshape(1, D), w_all, mu_p, bf_p)


def _segsum(x, bd, split=False):
    tile = bd.shape[0]
    hi = x.astype(BF16)
    lo = (x - hi.astype(F32)).astype(BF16) if split else None
    cols = []
    for j in range(x.shape[1] // tile):
        sl = slice(j * tile, (j + 1) * tile)
        s = _dot(hi[:, sl], bd)
        cols.append(s + _dot(lo[:, sl], bd) if split else s)
    return jnp.concatenate(cols, axis=1)


def _rwkv_kernel(p_ref, w0_ref, wdw_ref, a0_ref, waw_ref, wgw_ref, kk_ref, ka_ref, rk_ref,
                 gng_ref, gnb_ref, o_ref, s_scr, *, width):
    t = pl.program_id(1)
    rows = p_ref.shape[1]
    L = RW_CHUNK
    n_chunks = rows // L
    W = width
    heads = W // HEAD_DIM

    @pl.when(t == 0)
    def _():
        s_scr[...] = jnp.zeros_like(s_scr)

    p = p_ref[0]
    r = p[:, 0:W]
    k = p[:, W:2 * W]
    v = p[:, 2 * W:3 * W]
    o0 = 3 * W
    wd = p[:, o0:o0 + LANES]
    ad = p[:, o0 + LANES:o0 + 2 * LANES]
    gd = p[:, o0 + 2 * LANES:o0 + 4 * LANES]

    z = w0_ref[...] + _dot(_bf(jnp.tanh(wd)), wdw_ref[...])
    softplus = jnp.maximum(-z, 0.0) + jnp.log(1.0 + jnp.exp(-jnp.abs(z)))
    ld = -jnp.exp(-softplus - 0.5)
    a_sig = _sigmoid(a0_ref[...] + _dot(_bf(ad), waw_ref[...]))
    g = _dot(_bf(_sigmoid(gd)), wgw_ref[...])

    ri = lax.broadcasted_iota(jnp.int32, (MXU_TILE, MXU_TILE), 0)
    ci = lax.broadcasted_iota(jnp.int32, (MXU_TILE, MXU_TILE), 1)
    bd = ((ri // HEAD_DIM) == (ci // HEAD_DIM)).astype(BF16)

    kk = k * kk_ref[...]
    kk = kk / jnp.maximum(jnp.sqrt(_segsum(kk * kk, bd, split=True)), 1e-12)
    k2 = k * (1.0 + (a_sig - 1.0) * ka_ref[...])
    b = kk * a_sig
    bonus = _segsum(r * k2 * rk_ref[...], bd) * v

    rl = lax.broadcasted_iota(jnp.int32, (L, L), 0)
    cl = lax.broadcasted_iota(jnp.int32, (L, L), 1)
    incl = cl <= rl
    strict = cl < rl
    tril = incl.astype(BF16)
    eye = (cl == rl).astype(F32)
    ld_hi = _bf(ld)
    ld_lo = _bf(ld - ld_hi.astype(F32))
    cums, tots = [], []
    for c in range(n_chunks):
        rs = slice(c * L, (c + 1) * L)
        cum_c = _dot(tril, ld_hi[rs]) + _dot(tril, ld_lo[rs])
        cums.append(cum_c)
        tots.append(cum_c[L - 1:L, :])
    cum = jnp.concatenate(cums, axis=0)
    tot_b = jnp.concatenate([jnp.broadcast_to(tc, (L, W)) for tc in tots], axis=0)

    e_neg = jnp.exp(-cum)
    e_rem = jnp.exp(tot_b - cum)
    rt = r * jnp.exp(cum)
    rt_m = _bf(rt)
    at = _bf(-kk * jnp.exp(cum - ld))
    bt = _bf(b * e_neg)
    kt = _bf(k2 * e_neg)
    bh = _bf(b * e_rem)
    kh = _bf(k2 * e_rem)
    vm = _bf(v)

    G2 = 2 * LANES
    groups = W // LANES
    items = [(c, j) for c in range(n_chunks) for j in range(groups)]
    rsl = lambda c: slice(c * L, (c + 1) * L)
    gsl = lambda j: slice(j * LANES, (j + 1) * LANES)
    lane1 = lax.broadcasted_iota(jnp.int32, (L, LANES), 1)
    lane2 = lax.broadcasted_iota(jnp.int32, (L, G2), 1)
    row2 = lax.broadcasted_iota(jnp.int32, (L, G2), 0)
    first1 = lane1 < HEAD_DIM
    first2 = (lane2 % LANES) < HEAD_DIM
    strict2 = (lane2 % HEAD_DIM) < row2
    incl2 = (lane2 % HEAD_DIM) <= row2
    upper2 = lane2 >= LANES
    eye2 = ((lane1 % HEAD_DIM) == lax.broadcasted_iota(jnp.int32, (L, LANES), 0)).astype(F32)
    zero1 = jnp.zeros((L, LANES), BF16)
    rr = lax.broadcasted_iota(jnp.int32, (LANES, LANES), 0) < HEAD_DIM
    cc = lax.broadcasted_iota(jnp.int32, (LANES, LANES), 1) < HEAD_DIM
    diag_blocks = rr == cc

    def split(m, mask):
        return jnp.concatenate([jnp.where(mask, m, 0), jnp.where(mask, 0, m)], axis=0)

    g_bot, akm, zz = {}, {}, {}
    for c, j in items:
        ar = jnp.concatenate([at[rsl(c), gsl(j)], rt_m[rsl(c), gsl(j)]], axis=0)
        rhs = jnp.concatenate([split(bt[rsl(c), gsl(j)], first1),
                               split(kt[rsl(c), gsl(j)], first1)], axis=0)
        gm = _dot_nt(ar, rhs)
        top = jnp.where(strict2, gm[:L], 0.0)
        g_bot[c, j] = _bf(jnp.where(incl2, gm[L:], 0.0))
        akm[c, j] = _bf(top[:, LANES:])
        zz[c, j] = jnp.concatenate([top[:, :LANES], eye2], axis=1)
    for _ in range(int(math.log2(L))):
        for it in items:
            zb = _bf(zz[it])
            out = _dot(zb[:, :LANES], split(zb, first2))
            zz[it] = out + jnp.where(upper2, zz[it], 0.0)
    vbd = {(c, j): split(vm[rsl(c), gsl(j)], first1) for c, j in items}
    akv = {it: _bf(_dot(akm[it], vbd[it])) for it in items}
    x = {}
    for c, j in items:
        zed = jnp.concatenate([at[rsl(c), gsl(j)], akv[c, j]], axis=1)
        x[c, j] = _bf(_dot(_bf(zz[c, j][:, LANES:]), split(zed, first2)))
    r_eff, y0, mt, nt = {}, {}, {}, {}
    for c, j in items:
        w4 = jnp.concatenate([split(x[c, j], first2),
                              jnp.concatenate([jnp.concatenate([zero1, zero1], axis=0),
                                               vbd[c, j]], axis=1)], axis=0)
        out = _dot(g_bot[c, j], w4)
        r_eff[c, j] = _bf(rt[rsl(c), gsl(j)] + out[:, :LANES])
        y0[c, j] = out[:, LANES:]
    for c, j in items:
        lhs = jnp.concatenate([bh[rsl(c), gsl(j)], kh[rsl(c), gsl(j)]], axis=0)
        rhs = jnp.concatenate([x[c, j], jnp.concatenate([zero1, vm[rsl(c), gsl(j)]], axis=1)],
                              axis=0)
        pt = _dot_tn(lhs, rhs)
        mt[c, j] = _bf(jnp.where(diag_blocks, pt[:, :LANES], 0.0))
        nt[c, j] = jnp.where(diag_blocks, pt[:, LANES:], 0.0)
    e_cols = []
    for j in range(groups):
        e_rows = jnp.concatenate([jnp.exp(tots[c][:, gsl(j)]) for c in range(n_chunks)]
                                 + [jnp.zeros((LANES - n_chunks, LANES), F32)], axis=0)
        e_cols.append(e_rows.T)
    state = [s_scr[j] for j in range(groups)]
    y_rows = []
    for c in range(n_chunks):
        hb = [_bf(s) for s in state]
        y_rows.append(jnp.concatenate(
            [_dot(r_eff[c, j], hb[j]) + y0[c, j] for j in range(groups)], axis=1))
        state = [state[j] * e_cols[j][:, c:c + 1] + _dot(mt[c, j], hb[j]) + nt[c, j]
                 for j in range(groups)]
    for j in range(groups):
        s_scr[j] = state[j]
    y = jnp.concatenate(y_rows, axis=0)

    inv_n = 1.0 / HEAD_DIM
    mean = _segsum(y, bd) * inv_n
    d = y - mean
    var = _segsum(d * d, bd) * inv_n
    yn = d * lax.rsqrt(var + GN_EPS) * gng_ref[...] + gnb_ref[...]
    o_ref[0] = ((yn + bonus) * g).astype(o_ref.dtype)


def _rwkv(rw, w0, wdw, a0, waw, wgw, k_k, k_a, r_k, gn_g, gn_b, width, rows=256):
    B, T, cols = rw.shape
    row = lambda a: a.reshape(1, width)
    vec = _const_spec((1, width))
    return pl.pallas_call(
        functools.partial(_rwkv_kernel, width=width),
        out_shape=jax.ShapeDtypeStruct((B, T, width), BF16),
        grid=(B, T // rows),
        in_specs=[pl.BlockSpec((1, rows, cols), lambda b, t: (b, t, 0)),
                  vec, _const_spec(wdw.shape), vec, _const_spec(waw.shape),
                  _const_spec(wgw.shape), vec, vec, vec, vec, vec],
        out_specs=pl.BlockSpec((1, rows, width), lambda b, t: (b, t, 0)),
        scratch_shapes=[pltpu.VMEM((width // LANES, LANES, LANES), F32)],
        compiler_params=pltpu.CompilerParams(
            dimension_semantics=("arbitrary", "arbitrary"), vmem_limit_bytes=VMEM_LIMIT),
        name="rwkv",
    )(rw, row(w0), wdw, row(a0), waw, wgw, row(k_k), row(k_a), row(r_k), row(gn_g), row(gn_b))


def _fox_aug_tables(heads):
    sel = np.zeros((3 * LANES, 2 * heads * LANES), np.float32)
    bias = np.zeros((1, 2 * heads * LANES), np.float32)
    for h in range(heads):
        base = h * LANES + (HEAD_DIM if h % 2 == 0 else 0)
        kbase = heads * LANES + base
        for piece in range(3):
            sel[piece * LANES + h, base + piece] = 1.0
            sel[piece * LANES + h, kbase + 3 + piece] = -1.0
            bias[0, base + 3 + piece] = 1.0
            bias[0, kbase + piece] = 1.0
    return jnp.asarray(sel, BF16), jnp.asarray(bias)


def _foxprep_kernel(fx_ref, fcum_ref, qg_ref, kg_ref, sel_ref, bias_ref, qa_ref, ka_ref, v_ref,
                    *, width):
    W = width
    heads = W // HEAD_DIM
    pf = fx_ref[0]
    tm = pf.shape[0]
    fcum = fcum_ref[0] * LOG2E
    f_hi = fcum.astype(BF16)
    r1 = fcum - f_hi.astype(F32)
    f_mid = r1.astype(BF16)
    f_lo = (r1 - f_mid.astype(F32)).astype(BF16)
    aug = _dot(jnp.concatenate([f_hi, f_mid, f_lo], axis=1), sel_ref[...]) + bias_ref[...]
    first = lax.broadcasted_iota(jnp.int32, (tm, LANES), 1) < HEAD_DIM
    for side, (gain_ref, out_ref) in enumerate(((qg_ref, qa_ref), (kg_ref, ka_ref))):
        for g in range(W // LANES):
            x = pf[:, side * W + g * LANES:side * W + (g + 1) * LANES]
            sq = x * x
            lo = jnp.sum(jnp.where(first, sq, 0.0), axis=-1, keepdims=True)
            hi = jnp.sum(jnp.where(first, 0.0, sq), axis=-1, keepdims=True)
            ms = jnp.where(first, lo, hi) * (1.0 / HEAD_DIM)
            xn = x * lax.rsqrt(ms + NORM_EPS) * gain_ref[...]
            for hh in range(2):
                h = 2 * g + hh
                a = aug[:, (side * heads + h) * LANES:(side * heads + h + 1) * LANES]
                keep = first if hh == 0 else jnp.logical_not(first)
                out_ref[0, h] = jnp.where(keep, xn, a).astype(BF16)
    pad = (lax.broadcasted_iota(jnp.int32, (V_ROWS - HEAD_DIM, tm), 0) == 0).astype(BF16)
    for g in range(W // LANES):
        vt = pf[:, 2 * W + g * LANES:2 * W + (g + 1) * LANES].T.astype(BF16)
        for hh in range(2):
            v_ref[0, 2 * g + hh, 0, :HEAD_DIM, :] = vt[hh * HEAD_DIM:(hh + 1) * HEAD_DIM]
            v_ref[0, 2 * g + hh, 0, HEAD_DIM:, :] = pad


def _foxprep(fx, fcum, q_norm_g, k_norm_g, width, tm=512):
    B, T, cols = fx.shape
    heads = width // HEAD_DIM
    sel, bias = _fox_aug_tables(heads)
    scale = LOG2E / math.sqrt(HEAD_DIM)
    qg2 = jnp.tile(q_norm_g * scale, 2).reshape(1, LANES)
    kg2 = jnp.tile(k_norm_g, 2).reshape(1, LANES)
    return pl.pallas_call(
        functools.partial(_foxprep_kernel, width=width),
        out_shape=(jax.ShapeDtypeStruct((B, heads, T, LANES), BF16),
                   jax.ShapeDtypeStruct((B, heads, T, LANES), BF16),
                   jax.ShapeDtypeStruct((B, heads, T // tm, V_ROWS, tm), BF16)),
        grid=(B, T // tm),
        in_specs=[pl.BlockSpec((1, tm, cols), lambda b, t: (b, t, 0)),
                  pl.BlockSpec((1, tm, LANES), lambda b, t: (b, t, 0)),
                  _const_spec((1, LANES)), _const_spec((1, LANES)),
                  _const_spec(sel.shape), _const_spec(bias.shape)],
        out_specs=(pl.BlockSpec((1, heads, tm, LANES), lambda b, t: (b, 0, t, 0)),
                   pl.BlockSpec((1, heads, tm, LANES), lambda b, t: (b, 0, t, 0)),
                   pl.BlockSpec((1, heads, 1, V_ROWS, tm), lambda b, t: (b, 0, t, 0, 0))),
        compiler_params=pltpu.CompilerParams(
            dimension_semantics=("arbitrary", "arbitrary"), vmem_limit_bytes=VMEM_LIMIT),
        name="foxprep",
    )(fx, fcum, qg2, kg2, sel, bias)


def _fox_kernel(qa_ref, ka_ref, vt_ref, og_ref, g_ref, o_ref,
                m_scr, acc_scr, s_scr, p_scr, al_scr, *, cb):
    qi = pl.program_id(2)
    tq = qa_ref.shape[2]
    tk = tq
    chains = [(hh, c) for hh in range(2) for c in range(tq // cb)]
    csl = lambda c: slice(c * cb, (c + 1) * cb)

    m_scr[...] = jnp.full_like(m_scr, NEG_INF)
    acc_scr[...] = jnp.zeros_like(acc_scr)
    p_scr[1] = jnp.zeros(p_scr.shape[1:], p_scr.dtype)
    al_scr[1] = jnp.ones(al_scr.shape[1:], al_scr.dtype)

    def logits(t, buf):
        k_start = pl.multiple_of(t * tk, tk)
        for hh, c in chains:
            s_scr[buf, hh, :, csl(c)] = _dot_nt(ka_ref[0, hh, pl.ds(k_start, tk), :],
                                                qa_ref[0, hh, csl(c), :])

    def softmax(buf, masked):
        for hh, c in chains:
            s = s_scr[buf, hh, :, csl(c)]
            if masked:
                key = lax.broadcasted_iota(jnp.int32, (tk, cb), 0)
                qry = c * cb + lax.broadcasted_iota(jnp.int32, (tk, cb), 1)
                s = jnp.where(key <= qry, s, NEG_INF)
            m_prev = m_scr[hh, :, csl(c)]
            m_new = jnp.maximum(m_prev, jnp.max(s, axis=0, keepdims=True))
            m_scr[hh, :, csl(c)] = m_new
            al_scr[buf, hh, :, csl(c)] = jnp.exp2(m_prev - m_new)
            p_scr[buf, hh, :, csl(c)] = jnp.exp2(s - m_new).astype(BF16)

    def values(t, buf):
        for hh, c in chains:
            pv = _dot(vt_ref[0, hh, jnp.maximum(t, 0)], p_scr[buf, hh, :, csl(c)])
            acc_scr[hh, :, csl(c)] = al_scr[buf, hh, :, csl(c)] * acc_scr[hh, :, csl(c)] + pv

    def step(t, buf):
        logits(t, buf)
        softmax(1 - buf, masked=False)
        values(t - 2, buf)

    def two_steps(u, carry):
        step(2 * u + 1, 1)
        step(2 * u + 2, 0)
        return carry

    def drain(buf):
        softmax(buf, masked=True)
        values(qi - 1, 1 - buf)
        values(qi, buf)

    logits(0, 0)
    lax.fori_loop(0, qi // 2, two_steps, 0)

    @pl.when(qi % 2 == 1)
    def _():
        step(qi, 1)
        drain(1)

    @pl.when(qi % 2 == 0)
    def _():
        drain(0)

    normed = []
    for hh in range(2):
        acc = acc_scr[hh]
        o = acc[:HEAD_DIM] / acc[HEAD_DIM:HEAD_DIM + 1]
        ms = jnp.mean(o * o, axis=0, keepdims=True)
        normed.append(o * lax.rsqrt(ms + NORM_EPS))
    on = jnp.concatenate(normed, axis=0).T
    o_ref[0] = (on * g_ref[...] * _sigmoid(og_ref[0])).astype(o_ref.dtype)


def _fox(qa, ka, vt, fx, fox_out_g, width, cb=256):
    B, H, T, _ = qa.shape
    tq = vt.shape[4]
    og_blk = 3 * width // LANES
    return pl.pallas_call(
        functools.partial(_fox_kernel, cb=cb),
        out_shape=jax.ShapeDtypeStruct((B, T, width), BF16),
        grid=(B, H // 2, T // tq),
        in_specs=[pl.BlockSpec((1, 2, tq, LANES), lambda b, h, q: (b, h, q, 0)),
                  pl.BlockSpec((1, 2, T, LANES), lambda b, h, q: (b, h, 0, 0)),
                  pl.BlockSpec((1, 2, T // tq, V_ROWS, tq), lambda b, h, q: (b, h, 0, 0, 0)),
                  pl.BlockSpec((1, tq, LANES), lambda b, h, q: (b, q, og_blk + h)),
                  pl.BlockSpec((1, LANES), lambda b, h, q: (0, h))],
        out_specs=pl.BlockSpec((1, tq, LANES), lambda b, h, q: (b, q, h)),
        scratch_shapes=[pltpu.VMEM((2, 1, tq), F32), pltpu.VMEM((2, V_ROWS, tq), F32),
                        pltpu.VMEM((2, 2, tq, tq), F32), pltpu.VMEM((2, 2, tq, tq), BF16),
                        pltpu.VMEM((2, 2, 1, tq), F32)],
        compiler_params=pltpu.CompilerParams(
            dimension_semantics=("arbitrary",) * 3, vmem_limit_bytes=VMEM_LIMIT),
        name="fox",
    )(qa, ka, vt, fx, fox_out_g.reshape(1, width))


def _outmlp_kernel(x_ref, yrw_ref, yfx_ref, mod_ref, wo_ref, g2_ref, w1_ref, w2_ref, gf_ref,
                   o_ref, *, ff_tile):
    x = x_ref[0]
    half = yrw_ref.shape[2]
    gate1 = mod_ref[0, 2:3, :]
    shift2 = mod_ref[0, 3:4, :]
    scale2 = mod_ref[0, 4:5, :]
    gate2 = mod_ref[0, 5:6, :]
    y = _dot(yrw_ref[0], wo_ref[:half, :]) + _dot(yfx_ref[0], wo_ref[half:, :])
    h1 = x + gate1 * y
    ms = jnp.mean(h1 * h1, axis=-1, keepdims=True)
    u = ((h1 * lax.rsqrt(ms + NORM_EPS) * g2_ref[...]) * (1.0 + scale2) + shift2).astype(BF16)
    acc = jnp.zeros_like(x)
    for j in range(w1_ref.shape[1] // ff_tile):
        hid = jnp.maximum(_dot(u, w1_ref[:, j * ff_tile:(j + 1) * ff_tile]), 0.0)
        acc = acc + _dot((hid * hid).astype(BF16), w2_ref[j * ff_tile:(j + 1) * ff_tile, :])
    h2 = h1 + gate2 * acc
    ms2 = jnp.mean(h2 * h2, axis=-1, keepdims=True)
    o_ref[0] = h2 * lax.rsqrt(ms2 + NORM_EPS) * gf_ref[...]


def _outmlp(x, y_rw, y_fx, mod3, w_o, norm2_g, w1, w2, final_g, tm=512, ff_tile=1024):
    B, T, D = x.shape
    half = y_rw.shape[2]
    return pl.pallas_call(
        functools.partial(_outmlp_kernel, ff_tile=ff_tile),
        out_shape=jax.ShapeDtypeStruct((B, T, D), F32),
        grid=(B, T // tm),
        in_specs=[pl.BlockSpec((1, tm, D), lambda b, t: (b, t, 0)),
                  pl.BlockSpec((1, tm, half), lambda b, t: (b, t, 0)),
                  pl.BlockSpec((1, tm, half), lambda b, t: (b, t, 0)),
                  pl.BlockSpec((1, N_MOD, D), lambda b, t: (b, 0, 0)),
                  _const_spec(w_o.shape, single_buffer=True), _const_spec((1, D)),
                  _const_spec(w1.shape, single_buffer=True),
                  _const_spec(w2.shape, single_buffer=True), _const_spec((1, D))],
        out_specs=pl.BlockSpec((1, tm, D), lambda b, t: (b, t, 0)),
        compiler_params=pltpu.CompilerParams(
            dimension_semantics=("arbitrary", "arbitrary"), vmem_limit_bytes=VMEM_LIMIT),
        name="outmlp",
    )(x, y_rw, y_fx, mod3, w_o, norm2_g.reshape(1, D), w1, w2, final_g.reshape(1, D))


def _pad_cols(a, n):
    return jnp.pad(a, ((0, 0), (0, n - a.shape[1])))


def _pad_rows(a, n):
    return jnp.pad(a, ((0, n - a.shape[0]), (0, 0)))


def _branches(x, c, w_ada, b_ada, norm1_g, w_in, mu_shift, w0, w_up_decay, a0, w_up_a, w_up_g,
              k_k, k_a, r_k, gn_g, gn_b, b_f, q_norm_g, k_norm_g, fox_out_g):
    B, T, D = x.shape
    W = w0.shape[0]
    heads = b_f.shape[0]
    rw_n = 3 * W + DECAY_LORA + AAA_LORA + GATE_LORA

    def regroup(a):
        rw, fx = a[:, :rw_n], a[:, rw_n:]
        o = 3 * W
        return jnp.concatenate([
            rw[:, :o],
            _pad_cols(rw[:, o:o + DECAY_LORA], LANES),
            _pad_cols(rw[:, o + DECAY_LORA:o + DECAY_LORA + AAA_LORA], LANES),
            _pad_cols(rw[:, o + DECAY_LORA + AAA_LORA:], 2 * LANES),
            fx[:, :4 * W],
            _pad_cols(fx[:, 4 * W:], LANES)], axis=1)

    rw_cols = 3 * W + 4 * LANES
    w_all = regroup(w_in).astype(BF16)
    mu_p = regroup(jnp.pad(mu_shift.reshape(1, rw_n), ((0, 0), (0, w_in.shape[1] - rw_n))))[:, :rw_cols]
    bf_p = _pad_cols(b_f.reshape(1, heads), LANES)
    wdw = _pad_rows(w_up_decay, LANES).astype(BF16)
    waw = _pad_rows(w_up_a, LANES).astype(BF16)
    wgw = _pad_rows(w_up_g, 2 * LANES).astype(BF16)

    mod3 = _ada(c, w_ada, b_ada).reshape(B, N_MOD, D)
    rw, fx, fcum = _inproj(x, mod3, norm1_g, w_all, mu_p, bf_p, rw_cols)
    y_rw = _rwkv(rw, w0, wdw, a0, waw, wgw, k_k, k_a, r_k.reshape(-1), gn_g, gn_b, W)
    qa, ka, vt = _foxprep(fx, fcum, q_norm_g, k_norm_g, W)
    y_fx = _fox(qa, ka, vt, fx, fox_out_g, W)
    return y_rw, y_fx, mod3


def kernel(x, c, w_ada, b_ada, norm1_g, w_in, mu_shift, w0, w_up_decay, a0, w_up_a, w_up_g,
           k_k, k_a, r_k, gn_g, gn_b, b_f, q_norm_g, k_norm_g, fox_out_g, w_o, norm2_g,
           w_mlp1, w_mlp2, final_g):
    y_rw, y_fx, mod3 = _branches(x, c, w_ada, b_ada, norm1_g, w_in, mu_shift, w0, w_up_decay,
                                 a0, w_up_a, w_up_g, k_k, k_a, r_k, gn_g, gn_b, b_f,
                                 q_norm_g, k_norm_g, fox_out_g)
    return _outmlp(x, y_rw, y_fx, mod3, w_o.astype(BF16), norm2_g,
                   w_mlp1.astype(BF16), w_mlp2.astype(BF16), final_g)
```

```python
import functools
import math

import jax
import jax.numpy as jnp
import numpy as np
from jax import lax
from jax.experimental import pallas as pl
from jax.experimental.pallas import tpu as pltpu

F32 = jnp.float32
BF16 = jnp.bfloat16

HEAD_DIM = 64
LANES = 128
MXU_TILE = 256
NORM_EPS = 1e-6
GN_EPS = 64e-5
NEG_INF = -1e30
LOG2E = 1.4426950408889634
N_MOD = 6
DECAY_LORA = 64
AAA_LORA = 64
GATE_LORA = 160
RW_CHUNK = 64
V_ROWS = HEAD_DIM + 16
VMEM_LIMIT = 56 * 1024 * 1024

_HI = lax.Precision.HIGHEST


def _dot(a, b):
    return jnp.dot(a, b, preferred_element_type=F32)


def _dot_hi(a, b):
    return jnp.dot(a, b, preferred_element_type=F32, precision=_HI)


def _dot_nt(a, b):
    return lax.dot_general(a, b, (((1,), (1,)), ((), ())), preferred_element_type=F32)


def _dot_tn(a, b):
    return lax.dot_general(a, b, (((0,), (0,)), ((), ())), preferred_element_type=F32)


def _bf(x):
    return x.astype(BF16)


def _sigmoid(x):
    return 1.0 / (1.0 + jnp.exp(-x))


def _const_spec(shape, single_buffer=False):
    n = len(shape)
    mode = pl.Buffered(1) if single_buffer else None
    return pl.BlockSpec(shape, lambda *_: (0,) * n, pipeline_mode=mode)


def _ada_kernel(c_ref, w_ref, b_ref, o_ref):
    c = c_ref[...]
    o_ref[...] = _dot_hi(c * _sigmoid(c), w_ref[...]) + b_ref[...]


def _ada(c, w_ada, b_ada):
    B, D = c.shape
    n = w_ada.shape[1]
    return pl.pallas_call(
        _ada_kernel,
        out_shape=jax.ShapeDtypeStruct((B, n), F32),
        grid=(n // D,),
        in_specs=[pl.BlockSpec((B, D), lambda j: (0, 0)),
                  pl.BlockSpec((D, D), lambda j: (0, j)),
                  pl.BlockSpec((1, D), lambda j: (0, j))],
        out_specs=pl.BlockSpec((B, D), lambda j: (0, j)),
        name="ada",
    )(c, w_ada, b_ada.reshape(1, n))


def _inproj_kernel(x_ref, mod_ref, g_ref, w_ref, mu_ref, bf_ref,
                   rw_ref, fx_ref, fcum_ref, prev_scr, fcar_scr, *, rw_cols):
    t = pl.program_id(1)
    tm = x_ref.shape[1]

    @pl.when(t == 0)
    def _():
        prev_scr[...] = jnp.zeros_like(prev_scr)
        fcar_scr[...] = jnp.zeros_like(fcar_scr)

    x = x_ref[0]
    shift = mod_ref[0, 0:1, :]
    scale = mod_ref[0, 1:2, :]
    ms = jnp.mean(x * x, axis=-1, keepdims=True)
    u = (x * lax.rsqrt(ms + NORM_EPS) * g_ref[...]) * (1.0 + scale) + shift
    u = u.astype(BF16)

    p = _dot(u, w_ref[:, :rw_cols])
    row = lax.broadcasted_iota(jnp.int32, p.shape, 0)
    prev = jnp.where(row == 0, prev_scr[0:1, :], pltpu.roll(p, shift=1, axis=0))
    prev_scr[0:1, :] = p[tm - 1:tm, :]
    rw_ref[0] = p + (prev - p) * mu_ref[...]

    pf = _dot(u, w_ref[:, rw_cols:])
    fx_ref[0] = pf
    z = pf[:, pf.shape[1] - LANES:] + bf_ref[...]
    logf = jnp.minimum(z, 0.0) - jnp.log(1.0 + jnp.exp(-jnp.abs(z)))
    sub = min(tm, MXU_TILE)
    r2 = lax.broadcasted_iota(jnp.int32, (sub, sub), 0)
    c2 = lax.broadcasted_iota(jnp.int32, (sub, sub), 1)
    tril = (c2 <= r2).astype(BF16)
    f_hi = logf.astype(BF16)
    r1 = logf - f_hi.astype(F32)
    f_mid = r1.astype(BF16)
    f_lo = (r1 - f_mid.astype(F32)).astype(BF16)
    carry = fcar_scr[0:1, :]
    for i in range(tm // sub):
        rs = slice(i * sub, (i + 1) * sub)
        fcum = (_dot(tril, f_hi[rs]) + _dot(tril, f_mid[rs])) + _dot(tril, f_lo[rs]) + carry
        fcum_ref[0, rs, :] = fcum
        carry = fcum[sub - 1:sub, :]
    fcar_scr[0:1, :] = carry


def _inproj(x, mod3, norm1_g, w_all, mu_p, bf_p, rw_cols, tm=512):
    B, T, D = x.shape
    ncols = w_all.shape[1]
    fx_cols = ncols - rw_cols
    return pl.pallas_call(
        functools.partial(_inproj_kernel, rw_cols=rw_cols),
        out_shape=(jax.ShapeDtypeStruct((B, T, rw_cols), F32),
                   jax.ShapeDtypeStruct((B, T, fx_cols), F32),
                   jax.ShapeDtypeStruct((B, T, LANES), F32)),
        grid=(B, T // tm),
        in_specs=[pl.BlockSpec((1, tm, D), lambda b, t: (b, t, 0)),
                  pl.BlockSpec((1, N_MOD, D), lambda b, t: (b, 0, 0)),
                  _const_spec((1, D)),
                  _const_spec((D, ncols), single_buffer=True),
                  _const_spec((1, rw_cols)),
                  _const_spec((1, LANES))],
        out_specs=(pl.BlockSpec((1, tm, rw_cols), lambda b, t: (b, t, 0)),
                   pl.BlockSpec((1, tm, fx_cols), lambda b, t: (b, t, 0)),
                   pl.BlockSpec((1, tm, LANES), lambda b, t: (b, t, 0))),
        scratch_shapes=[pltpu.VMEM((8, rw_cols), F32), pltpu.VMEM((8, LANES), F32)],
        compiler_params=pltpu.CompilerParams(
            dimension_semantics=("arbitrary", "arbitrary"), vmem_limit_bytes=VMEM_LIMIT),
        name="inproj",
    )(x, mod3, norm1_g.reshape(1, D), w_all, mu_p, bf_p)


def _segsum(x, bd, split=False):
    tile = bd.shape[0]
    hi = x.astype(BF16)
    lo = (x - hi.astype(F32)).astype(BF16) if split else None
    cols = []
    for j in range(x.shape[1] // tile):
        sl = slice(j * tile, (j + 1) * tile)
        s = _dot(hi[:, sl], bd)
        cols.append(s + _dot(lo[:, sl], bd) if split else s)
    return jnp.concatenate(cols, axis=1)


def _rwkv_kernel(p_ref, w0_ref, wdw_ref, a0_ref, waw_ref, wgw_ref, kk_ref, ka_ref, rk_ref,
                 gng_ref, gnb_ref, o_ref, s_scr, *, width):
    t = pl.program_id(1)
    rows = p_ref.shape[1]
    L = RW_CHUNK
    n_chunks = rows // L
    W = width
    heads = W // HEAD_DIM

    @pl.when(t == 0)
    def _():
        s_scr[...] = jnp.zeros_like(s_scr)

    p = p_ref[0]
    r = p[:, 0:W]
    k = p[:, W:2 * W]
    v = p[:, 2 * W:3 * W]
    o0 = 3 * W
    wd = p[:, o0:o0 + LANES]
    ad = p[:, o0 + LANES:o0 + 2 * LANES]
    gd = p[:, o0 + 2 * LANES:o0 + 4 * LANES]

    z = w0_ref[...] + _dot(_bf(jnp.tanh(wd)), wdw_ref[...])
    softplus = jnp.maximum(-z, 0.0) + jnp.log(1.0 + jnp.exp(-jnp.abs(z)))
    ld = -jnp.exp(-softplus - 0.5)
    a_sig = _sigmoid(a0_ref[...] + _dot(_bf(ad), waw_ref[...]))
    g = _dot(_bf(_sigmoid(gd)), wgw_ref[...])

    ri = lax.broadcasted_iota(jnp.int32, (MXU_TILE, MXU_TILE), 0)
    ci = lax.broadcasted_iota(jnp.int32, (MXU_TILE, MXU_TILE), 1)
    bd = ((ri // HEAD_DIM) == (ci // HEAD_DIM)).astype(BF16)

    kk = k * kk_ref[...]
    kk = kk / jnp.maximum(jnp.sqrt(_segsum(kk * kk, bd, split=True)), 1e-12)
    k2 = k * (1.0 + (a_sig - 1.0) * ka_ref[...])
    b = kk * a_sig
    bonus = _segsum(r * k2 * rk_ref[...], bd) * v

    rl = lax.broadcasted_iota(jnp.int32, (L, L), 0)
    cl = lax.broadcasted_iota(jnp.int32, (L, L), 1)
    incl = cl <= rl
    strict = cl < rl
    tril = incl.astype(BF16)
    eye = (cl == rl).astype(F32)
    ld_hi = _bf(ld)
    ld_lo = _bf(ld - ld_hi.astype(F32))
    cums, tots = [], []
    for c in range(n_chunks):
        rs = slice(c * L, (c + 1) * L)
        cum_c = _dot(tril, ld_hi[rs]) + _dot(tril, ld_lo[rs])
        cums.append(cum_c)
        tots.append(cum_c[L - 1:L, :])
    cum = jnp.concatenate(cums, axis=0)
    tot_b = jnp.concatenate([jnp.broadcast_to(tc, (L, W)) for tc in tots], axis=0)

    e_neg = jnp.exp(-cum)
    e_rem = jnp.exp(tot_b - cum)
    rt = r * jnp.exp(cum)
    rt_m = _bf(rt)
    at = _bf(-kk * jnp.exp(cum - ld))
    bt = _bf(b * e_neg)
    kt = _bf(k2 * e_neg)
    bh = _bf(b * e_rem)
    kh = _bf(k2 * e_rem)
    vm = _bf(v)

    G2 = 2 * LANES
    groups = W // LANES
    items = [(c, j) for c in range(n_chunks) for j in range(groups)]
    rsl = lambda c: slice(c * L, (c + 1) * L)
    gsl = lambda j: slice(j * LANES, (j + 1) * LANES)
    lane1 = lax.broadcasted_iota(jnp.int32, (L, LANES), 1)
    lane2 = lax.broadcasted_iota(jnp.int32, (L, G2), 1)
    row2 = lax.broadcasted_iota(jnp.int32, (L, G2), 0)
    first1 = lane1 < HEAD_DIM
    first2 = (lane2 % LANES) < HEAD_DIM
    strict2 = (lane2 % HEAD_DIM) < row2
    incl2 = (lane2 % HEAD_DIM) <= row2
    upper2 = lane2 >= LANES
    eye2 = ((lane1 % HEAD_DIM) == lax.broadcasted_iota(jnp.int32, (L, LANES), 0)).astype(F32)
    zero1 = jnp.zeros((L, LANES), BF16)
    rr = lax.broadcasted_iota(jnp.int32, (LANES, LANES), 0) < HEAD_DIM
    cc = lax.broadcasted_iota(jnp.int32, (LANES, LANES), 1) < HEAD_DIM
    diag_blocks = rr == cc

    def split(m, mask):
        return jnp.concatenate([jnp.where(mask, m, 0), jnp.where(mask, 0, m)], axis=0)

    g_bot, akm, zz = {}, {}, {}
    for c, j in items:
        ar = jnp.concatenate([at[rsl(c), gsl(j)], rt_m[rsl(c), gsl(j)]], axis=0)
        rhs = jnp.concatenate([split(bt[rsl(c), gsl(j)], first1),
                               split(kt[rsl(c), gsl(j)], first1)], axis=0)
        gm = _dot_nt(ar, rhs)
        top = jnp.where(strict2, gm[:L], 0.0)
        g_bot[c, j] = _bf(jnp.where(incl2, gm[L:], 0.0))
        akm[c, j] = _bf(top[:, LANES:])
        zz[c, j] = jnp.concatenate([top[:, :LANES], eye2], axis=1)
    for _ in range(int(math.log2(L))):
        for it in items:
            zb = _bf(zz[it])
            out = _dot(zb[:, :LANES], split(zb, first2))
            zz[it] = out + jnp.where(upper2, zz[it], 0.0)
    vbd = {(c, j): split(vm[rsl(c), gsl(j)], first1) for c, j in items}
    akv = {it: _bf(_dot(akm[it], vbd[it])) for it in items}
    x = {}
    for c, j in items:
        zed = jnp.concatenate([at[rsl(c), gsl(j)], akv[c, j]], axis=1)
        x[c, j] = _bf(_dot(_bf(zz[c, j][:, LANES:]), split(zed, first2)))
    r_eff, y0, mt, nt = {}, {}, {}, {}
    for c, j in items:
        w4 = jnp.concatenate([split(x[c, j], first2),
                              jnp.concatenate([jnp.concatenate([zero1, zero1], axis=0),
                                               vbd[c, j]], axis=1)], axis=0)
        out = _dot(g_bot[c, j], w4)
        r_eff[c, j] = _bf(rt[rsl(c), gsl(j)] + out[:, :LANES])
        y0[c, j] = out[:, LANES:]
    for c, j in items:
        lhs = jnp.concatenate([bh[rsl(c), gsl(j)], kh[rsl(c), gsl(j)]], axis=0)
        rhs = jnp.concatenate([x[c, j], jnp.concatenate([zero1, vm[rsl(c), gsl(j)]], axis=1)],
                              axis=0)
        pt = _dot_tn(lhs, rhs)
        mt[c, j] = _bf(jnp.where(diag_blocks, pt[:, :LANES], 0.0))
        nt[c, j] = jnp.where(diag_blocks, pt[:, LANES:], 0.0)
    e_cols = []
    for j in range(groups):
        e_rows = jnp.concatenate([jnp.exp(tots[c][:, gsl(j)]) for c in range(n_chunks)]
                                 + [jnp.zeros((LANES - n_chunks, LANES), F32)], axis=0)
        e_cols.append(e_rows.T)
    state = [s_scr[j] for j in range(groups)]
    y_rows = []
    for c in range(n_chunks):
        hb = [_bf(s) for s in state]
        y_rows.append(jnp.concatenate(
            [_dot(r_eff[c, j], hb[j]) + y0[c, j] for j in range(groups)], axis=1))
        state = [state[j] * e_cols[j][:, c:c + 1] + _dot(mt[c, j], hb[j]) + nt[c, j]
                 for j in range(groups)]
    for j in range(groups):
        s_scr[j] = state[j]
    y = jnp.concatenate(y_rows, axis=0)

    inv_n = 1.0 / HEAD_DIM
    mean = _segsum(y, bd) * inv_n
    d = y - mean
    var = _segsum(d * d, bd) * inv_n
    yn = d * lax.rsqrt(var + GN_EPS) * gng_ref[...] + gnb_ref[...]
    o_ref[0] = ((yn + bonus) * g).astype(o_ref.dtype)


def _rwkv(rw, w0, wdw, a0, waw, wgw, k_k, k_a, r_k, gn_g, gn_b, width, rows=256):
    B, T, cols = rw.shape
    row = lambda a: a.reshape(1, width)
    vec = _const_spec((1, width))
    return pl.pallas_call(
        functools.partial(_rwkv_kernel, width=width),
        out_shape=jax.ShapeDtypeStruct((B, T, width), BF16),
        grid=(B, T // rows),
        in_specs=[pl.BlockSpec((1, rows, cols), lambda b, t: (b, t, 0)),
                  vec, _const_spec(wdw.shape), vec, _const_spec(waw.shape),
                  _const_spec(wgw.shape), vec, vec, vec, vec, vec],
        out_specs=pl.BlockSpec((1, rows, width), lambda b, t: (b, t, 0)),
        scratch_shapes=[pltpu.VMEM((width // LANES, LANES, LANES), F32)],
        compiler_params=pltpu.CompilerParams(
            dimension_semantics=("arbitrary", "arbitrary"), vmem_limit_bytes=VMEM_LIMIT),
        name="rwkv",
    )(rw, row(w0), wdw, row(a0), waw, wgw, row(k_k), row(k_a), row(r_k), row(gn_g), row(gn_b))


def _fox_aug_tables(heads):
    sel = np.zeros((3 * LANES, 2 * heads * LANES), np.float32)
    bias = np.zeros((1, 2 * heads * LANES), np.float32)
    for h in range(heads):
        base = h * LANES + (HEAD_DIM if h % 2 == 0 else 0)
        kbase = heads * LANES + base
        for piece in range(3):
            sel[piece * LANES + h, base + piece] = 1.0
            sel[piece * LANES + h, kbase + 3 + piece] = -1.0
            bias[0, base + 3 + piece] = 1.0
            bias[0, kbase + piece] = 1.0
    return jnp.asarray(sel, BF16), jnp.asarray(bias)


def _foxprep_kernel(fx_ref, fcum_ref, qg_ref, kg_ref, sel_ref, bias_ref, qa_ref, ka_ref, v_ref,
                    *, width):
    W = width
    heads = W // HEAD_DIM
    pf = fx_ref[0]
    tm = pf.shape[0]
    fcum = fcum_ref[0] * LOG2E
    f_hi = fcum.astype(BF16)
    r1 = fcum - f_hi.astype(F32)
    f_mid = r1.astype(BF16)
    f_lo = (r1 - f_mid.astype(F32)).astype(BF16)
    aug = _dot(jnp.concatenate([f_hi, f_mid, f_lo], axis=1), sel_ref[...]) + bias_ref[...]
    first = lax.broadcasted_iota(jnp.int32, (tm, LANES), 1) < HEAD_DIM
    for side, (gain_ref, out_ref) in enumerate(((qg_ref, qa_ref), (kg_ref, ka_ref))):
        for g in range(W // LANES):
            x = pf[:, side * W + g * LANES:side * W + (g + 1) * LANES]
            sq = x * x
            lo = jnp.sum(jnp.where(first, sq, 0.0), axis=-1, keepdims=True)
            hi = jnp.sum(jnp.where(first, 0.0, sq), axis=-1, keepdims=True)
            ms = jnp.where(first, lo, hi) * (1.0 / HEAD_DIM)
            xn = x * lax.rsqrt(ms + NORM_EPS) * gain_ref[...]
            for hh in range(2):
                h = 2 * g + hh
                a = aug[:, (side * heads + h) * LANES:(side * heads + h + 1) * LANES]
                keep = first if hh == 0 else jnp.logical_not(first)
                out_ref[0, h] = jnp.where(keep, xn, a).astype(BF16)
    pad = (lax.broadcasted_iota(jnp.int32, (V_ROWS - HEAD_DIM, tm), 0) == 0).astype(BF16)
    for g in range(W // LANES):
        vt = pf[:, 2 * W + g * LANES:2 * W + (g + 1) * LANES].T.astype(BF16)
        for hh in range(2):
            v_ref[0, 2 * g + hh, 0, :HEAD_DIM, :] = vt[hh * HEAD_DIM:(hh + 1) * HEAD_DIM]
            v_ref[0, 2 * g + hh, 0, HEAD_DIM:, :] = pad


def _foxprep(fx, fcum, q_norm_g, k_norm_g, width, tm=512):
    B, T, cols = fx.shape
    heads = width // HEAD_DIM
    sel, bias = _fox_aug_tables(heads)
    scale = LOG2E / math.sqrt(HEAD_DIM)
    qg2 = jnp.tile(q_norm_g * scale, 2).reshape(1, LANES)
    kg2 = jnp.tile(k_norm_g, 2).reshape(1, LANES)
    return pl.pallas_call(
        functools.partial(_foxprep_kernel, width=width),
        out_shape=(jax.ShapeDtypeStruct((B, heads, T, LANES), BF16),
                   jax.ShapeDtypeStruct((B, heads, T, LANES), BF16),
                   jax.ShapeDtypeStruct((B, heads, T // tm, V_ROWS, tm), BF16)),
        grid=(B, T // tm),
        in_specs=[pl.BlockSpec((1, tm, cols), lambda b, t: (b, t, 0)),
                  pl.BlockSpec((1, tm, LANES), lambda b, t: (b, t, 0)),
                  _const_spec((1, LANES)), _const_spec((1, LANES)),
                  _const_spec(sel.shape), _const_spec(bias.shape)],
        out_specs=(pl.BlockSpec((1, heads, tm, LANES), lambda b, t: (b, 0, t, 0)),
                   pl.BlockSpec((1, heads, tm, LANES), lambda b, t: (b, 0, t, 0)),
                   pl.BlockSpec((1, heads, 1, V_ROWS, tm), lambda b, t: (b, 0, t, 0, 0))),
        compiler_params=pltpu.CompilerParams(
            dimension_semantics=("arbitrary", "arbitrary"), vmem_limit_bytes=VMEM_LIMIT),
        name="foxprep",
    )(fx, fcum, qg2, kg2, sel, bias)


def _fox_kernel(qtab, ktab, qa_ref, ka_ref, vt_ref, og_ref, g_ref, o_ref,
                m_all, acc_all, s_scr, p_scr, al_scr, *, blk, cb, n_off):
    n_q = qa_ref.shape[2] // blk
    n_el = n_off + n_q
    chains = [(hh, c) for hh in range(2) for c in range(blk // cb)]
    csl = lambda c: slice(c * cb, (c + 1) * cb)

    m_all[...] = jnp.full_like(m_all, NEG_INF)
    acc_all[...] = jnp.zeros_like(acc_all)

    def element(t):
        if isinstance(t, int) and t >= n_off:
            return t - n_off, t - n_off, True
        return qtab[t], ktab[t], False

    def start(i, size):
        return i * size if isinstance(i, int) else pl.multiple_of(i * size, size)

    def n_keys(c, diag):
        return (c + 1) * cb if diag else blk

    def logits(t, par):
        qi, ki, diag = element(t)
        for hh, c in chains:
            nk = n_keys(c, diag)
            s_scr[par, hh, :nk, csl(c)] = _dot_nt(
                ka_ref[0, hh, pl.ds(start(ki, blk), nk), :],
                qa_ref[0, hh, pl.ds(start(qi, blk) + c * cb, cb), :])

    def softmax(t, par):
        qi, _, diag = element(t)
        for hh, c in chains:
            nk = n_keys(c, diag)
            s = s_scr[par, hh, :nk, csl(c)]
            if diag:
                key = lax.broadcasted_iota(jnp.int32, (nk, cb), 0)
                qry = c * cb + lax.broadcasted_iota(jnp.int32, (nk, cb), 1)
                s = jnp.where(key <= qry, s, NEG_INF)
            m_prev = m_all[qi, hh, :, csl(c)]
            m_new = jnp.maximum(m_prev, jnp.max(s, axis=0, keepdims=True))
            m_all[qi, hh, :, csl(c)] = m_new
            al_scr[par, hh, :, csl(c)] = jnp.exp2(m_prev - m_new)
            p_scr[par, hh, :nk, csl(c)] = jnp.exp2(s - m_new).astype(BF16)

    def values(t, par):
        qi, ki, diag = element(t)
        for hh, c in chains:
            nk = n_keys(c, diag)
            pv = _dot(vt_ref[0, hh, ki, :, :nk], p_scr[par, hh, :nk, csl(c)])
            acc_all[qi, hh, :, csl(c)] = (al_scr[par, hh, :, csl(c)] * acc_all[qi, hh, :, csl(c)]
                                          + pv)
        if diag:
            normed = []
            for hh in range(2):
                acc = acc_all[qi, hh]
                o = acc[:HEAD_DIM] / acc[HEAD_DIM:HEAD_DIM + 1]
                ms = jnp.mean(o * o, axis=0, keepdims=True)
                normed.append(o * lax.rsqrt(ms + NORM_EPS))
            on = jnp.concatenate(normed, axis=0).T
            rows = slice(qi * blk, (qi + 1) * blk)
            o_ref[0, rows, :] = (on * g_ref[...] * _sigmoid(og_ref[0, rows, :])).astype(o_ref.dtype)

    def step(t, par):
        for lag, stage in enumerate((logits, softmax, values)):
            if not isinstance(t, int) or 0 <= t - lag < n_el:
                stage(t - lag, (par + lag) % 2)

    def two_steps(u, carry):
        step(2 * u + 2, 0)
        step(2 * u + 3, 1)
        return carry

    step(0, 0)
    step(1, 1)
    lax.fori_loop(0, (n_off - 2) // 2, two_steps, 0)
    for t in range(n_off, n_el + 2):
        step(t, t % 2)


def _fox(qa, ka, vt, fx, fox_out_g, width, cb=256):
    B, H, T, _ = qa.shape
    blk = vt.shape[4]
    n_q = T // blk
    pairs = [(qi, ki) for qi in range(n_q) for ki in range(qi)]
    n_off = len(pairs)
    assert n_off >= 2 and n_off % 2 == 0, "pipeline loop is unrolled by two"
    qtab = jnp.asarray(np.array([p[0] for p in pairs], np.int32))
    ktab = jnp.asarray(np.array([p[1] for p in pairs], np.int32))
    og_blk = 3 * width // LANES
    grid_spec = pltpu.PrefetchScalarGridSpec(
        num_scalar_prefetch=2,
        grid=(B, H // 2),
        in_specs=[pl.BlockSpec((1, 2, T, LANES), lambda b, h, qt, kt: (b, h, 0, 0)),
                  pl.BlockSpec((1, 2, T, LANES), lambda b, h, qt, kt: (b, h, 0, 0)),
                  pl.BlockSpec((1, 2, n_q, V_ROWS, blk), lambda b, h, qt, kt: (b, h, 0, 0, 0)),
                  pl.BlockSpec((1, T, LANES), lambda b, h, qt, kt: (b, 0, og_blk + h)),
                  pl.BlockSpec((1, LANES), lambda b, h, qt, kt: (0, h))],
        out_specs=pl.BlockSpec((1, T, LANES), lambda b, h, qt, kt: (b, 0, h)),
        scratch_shapes=[pltpu.VMEM((n_q, 2, 1, blk), F32), pltpu.VMEM((n_q, 2, V_ROWS, blk), F32),
                        pltpu.VMEM((2, 2, blk, blk), F32), pltpu.VMEM((2, 2, blk, blk), BF16),
                        pltpu.VMEM((2, 2, 1, blk), F32)])
    return pl.pallas_call(
        functools.partial(_fox_kernel, blk=blk, cb=cb, n_off=n_off),
        out_shape=jax.ShapeDtypeStruct((B, T, width), BF16),
        grid_spec=grid_spec,
        compiler_params=pltpu.CompilerParams(
            dimension_semantics=("arbitrary",) * 2, vmem_limit_bytes=VMEM_LIMIT),
        name="fox",
    )(qtab, ktab, qa, ka, vt, fx, fox_out_g.reshape(1, width))


def _outmlp_kernel(x_ref, yrw_ref, yfx_ref, mod_ref, wo_ref, g2_ref, w1_ref, w2_ref, gf_ref,
                   o_ref, *, ff_tile):
    x = x_ref[0]
    half = yrw_ref.shape[2]
    gate1 = mod_ref[0, 2:3, :]
    shift2 = mod_ref[0, 3:4, :]
    scale2 = mod_ref[0, 4:5, :]
    gate2 = mod_ref[0, 5:6, :]
    y = _dot(yrw_ref[0], wo_ref[:half, :]) + _dot(yfx_ref[0], wo_ref[half:, :])
    h1 = x + gate1 * y
    ms = jnp.mean(h1 * h1, axis=-1, keepdims=True)
    u = ((h1 * lax.rsqrt(ms + NORM_EPS) * g2_ref[...]) * (1.0 + scale2) + shift2).astype(BF16)
    acc = jnp.zeros_like(x)
    for j in range(w1_ref.shape[1] // ff_tile):
        hid = jnp.maximum(_dot(u, w1_ref[:, j * ff_tile:(j + 1) * ff_tile]), 0.0)
        acc = acc + _dot((hid * hid).astype(BF16), w2_ref[j * ff_tile:(j + 1) * ff_tile, :])
    h2 = h1 + gate2 * acc
    ms2 = jnp.mean(h2 * h2, axis=-1, keepdims=True)
    o_ref[0] = h2 * lax.rsqrt(ms2 + NORM_EPS) * gf_ref[...]


def _outmlp(x, y_rw, y_fx, mod3, w_o, norm2_g, w1, w2, final_g, tm=512, ff_tile=1024):
    B, T, D = x.shape
    half = y_rw.shape[2]
    return pl.pallas_call(
        functools.partial(_outmlp_kernel, ff_tile=ff_tile),
        out_shape=jax.ShapeDtypeStruct((B, T, D), F32),
        grid=(B, T // tm),
        in_specs=[pl.BlockSpec((1, tm, D), lambda b, t: (b, t, 0)),
                  pl.BlockSpec((1, tm, half), lambda b, t: (b, t, 0)),
                  pl.BlockSpec((1, tm, half), lambda b, t: (b, t, 0)),
                  pl.BlockSpec((1, N_MOD, D), lambda b, t: (b, 0, 0)),
                  _const_spec(w_o.shape, single_buffer=True), _const_spec((1, D)),
                  _const_spec(w1.shape, single_buffer=True),
                  _const_spec(w2.shape, single_buffer=True), _const_spec((1, D))],
        out_specs=pl.BlockSpec((1, tm, D), lambda b, t: (b, t, 0)),
        compiler_params=pltpu.CompilerParams(
            dimension_semantics=("arbitrary", "arbitrary"), vmem_limit_bytes=VMEM_LIMIT),
        name="outmlp",
    )(x, y_rw, y_fx, mod3, w_o, norm2_g.reshape(1, D), w1, w2, final_g.reshape(1, D))


def _pad_cols(a, n):
    return jnp.pad(a, ((0, 0), (0, n - a.shape[1])))


def _pad_rows(a, n):
    return jnp.pad(a, ((0, n - a.shape[0]), (0, 0)))


def _branches(x, c, w_ada, b_ada, norm1_g, w_in, mu_shift, w0, w_up_decay, a0, w_up_a, w_up_g,
              k_k, k_a, r_k, gn_g, gn_b, b_f, q_norm_g, k_norm_g, fox_out_g):
    B, T, D = x.shape
    W = w0.shape[0]
    heads = b_f.shape[0]
    rw_n = 3 * W + DECAY_LORA + AAA_LORA + GATE_LORA

    def regroup(a):
        rw, fx = a[:, :rw_n], a[:, rw_n:]
        o = 3 * W
        return jnp.concatenate([
            rw[:, :o],
            _pad_cols(rw[:, o:o + DECAY_LORA], LANES),
            _pad_cols(rw[:, o + DECAY_LORA:o + DECAY_LORA + AAA_LORA], LANES),
            _pad_cols(rw[:, o + DECAY_LORA + AAA_LORA:], 2 * LANES),
            fx[:, :4 * W],
            _pad_cols(fx[:, 4 * W:], LANES)], axis=1)

    rw_cols = 3 * W + 4 * LANES
    w_all = regroup(w_in).astype(BF16)
    mu_p = regroup(jnp.pad(mu_shift.reshape(1, rw_n), ((0, 0), (0, w_in.shape[1] - rw_n))))[:, :rw_cols]
    bf_p = _pad_cols(b_f.reshape(1, heads), LANES)
    wdw = _pad_rows(w_up_decay, LANES).astype(BF16)
    waw = _pad_rows(w_up_a, LANES).astype(BF16)
    wgw = _pad_rows(w_up_g, 2 * LANES).astype(BF16)

    mod3 = _ada(c, w_ada, b_ada).reshape(B, N_MOD, D)
    rw, fx, fcum = _inproj(x, mod3, norm1_g, w_all, mu_p, bf_p, rw_cols)
    y_rw = _rwkv(rw, w0, wdw, a0, waw, wgw, k_k, k_a, r_k.reshape(-1), gn_g, gn_b, W)
    qa, ka, vt = _foxprep(fx, fcum, q_norm_g, k_norm_g, W)
    y_fx = _fox(qa, ka, vt, fx, fox_out_g, W)
    return y_rw, y_fx, mod3


def kernel(x, c, w_ada, b_ada, norm1_g, w_in, mu_shift, w0, w_up_decay, a0, w_up_a, w_up_g,
           k_k, k_a, r_k, gn_g, gn_b, b_f, q_norm_g, k_norm_g, fox_out_g, w_o, norm2_g,
           w_mlp1, w_mlp2, final_g):
    y_rw, y_fx, mod3 = _branches(x, c, w_ada, b_ada, norm1_g, w_in, mu_shift, w0, w_up_decay,
                                 a0, w_up_a, w_up_g, k_k, k_a, r_k, gn_g, gn_b, b_f,
                                 q_norm_g, k_norm_g, fox_out_g)
    return _outmlp(x, y_rw, y_fx, mod3, w_o.astype(BF16), norm2_g,
                   w_mlp1.astype(BF16), w_mlp2.astype(BF16), final_g)
```

```python
import functools
import math

import jax
import jax.numpy as jnp
import numpy as np
from jax import lax
from jax.experimental import pallas as pl
from jax.experimental.pallas import tpu as pltpu

F32 = jnp.float32
BF16 = jnp.bfloat16

HEAD_DIM = 64
LANES = 128
MXU_TILE = 256
NORM_EPS = 1e-6
GN_EPS = 64e-5
NEG_INF = -1e30
LOG2E = 1.4426950408889634
N_MOD = 6
DECAY_LORA = 64
AAA_LORA = 64
GATE_LORA = 160
RW_CHUNK = 64
V_ROWS = HEAD_DIM + 16
VMEM_LIMIT = 56 * 1024 * 1024

_HI = lax.Precision.HIGHEST


def _dot(a, b):
    return jnp.dot(a, b, preferred_element_type=F32)


def _dot_hi(a, b):
    return jnp.dot(a, b, preferred_element_type=F32, precision=_HI)


def _dot_nt(a, b):
    return lax.dot_general(a, b, (((1,), (1,)), ((), ())), preferred_element_type=F32)


def _dot_tn(a, b):
    return lax.dot_general(a, b, (((0,), (0,)), ((), ())), preferred_element_type=F32)


def _bf(x):
    return x.astype(BF16)


def _sigmoid(x):
    return 1.0 / (1.0 + jnp.exp(-x))


def _const_spec(shape, single_buffer=False):
    n = len(shape)
    mode = pl.Buffered(1) if single_buffer else None
    return pl.BlockSpec(shape, lambda *_: (0,) * n, pipeline_mode=mode)


def _ada_kernel(c_ref, w_ref, b_ref, o_ref):
    c = c_ref[...]
    o_ref[...] = _dot_hi(c * _sigmoid(c), w_ref[...]) + b_ref[...]


def _ada(c, w_ada, b_ada):
    B, D = c.shape
    n = w_ada.shape[1]
    return pl.pallas_call(
        _ada_kernel,
        out_shape=jax.ShapeDtypeStruct((B, n), F32),
        grid=(n // D,),
        in_specs=[pl.BlockSpec((B, D), lambda j: (0, 0)),
                  pl.BlockSpec((D, D), lambda j: (0, j)),
                  pl.BlockSpec((1, D), lambda j: (0, j))],
        out_specs=pl.BlockSpec((B, D), lambda j: (0, j)),
        name="ada",
    )(c, w_ada, b_ada.reshape(1, n))


def _inproj_kernel(x_ref, mod_ref, g_ref, w_ref, mu_ref, bf_ref,
                   rw_ref, fx_ref, fcum_ref, prev_scr, fcar_scr, *, rw_cols):
    t = pl.program_id(1)
    tm = x_ref.shape[1]

    @pl.when(t == 0)
    def _():
        prev_scr[...] = jnp.zeros_like(prev_scr)
        fcar_scr[...] = jnp.zeros_like(fcar_scr)

    x = x_ref[0]
    shift = mod_ref[0, 0:1, :]
    scale = mod_ref[0, 1:2, :]
    ms = jnp.mean(x * x, axis=-1, keepdims=True)
    u = (x * lax.rsqrt(ms + NORM_EPS) * g_ref[...]) * (1.0 + scale) + shift
    u = u.astype(BF16)

    p = _dot(u, w_ref[:, :rw_cols])
    row = lax.broadcasted_iota(jnp.int32, p.shape, 0)
    prev = jnp.where(row == 0, prev_scr[0:1, :], pltpu.roll(p, shift=1, axis=0))
    prev_scr[0:1, :] = p[tm - 1:tm, :]
    rw_ref[0] = p + (prev - p) * mu_ref[...]

    pf = _dot(u, w_ref[:, rw_cols:])
    fx_ref[0] = pf
    z = pf[:, pf.shape[1] - LANES:] + bf_ref[...]
    logf = jnp.minimum(z, 0.0) - jnp.log(1.0 + jnp.exp(-jnp.abs(z)))
    sub = min(tm, MXU_TILE)
    r2 = lax.broadcasted_iota(jnp.int32, (sub, sub), 0)
    c2 = lax.broadcasted_iota(jnp.int32, (sub, sub), 1)
    tril = (c2 <= r2).astype(BF16)
    f_hi = logf.astype(BF16)
    r1 = logf - f_hi.astype(F32)
    f_mid = r1.astype(BF16)
    f_lo = (r1 - f_mid.astype(F32)).astype(BF16)
    carry = fcar_scr[0:1, :]
    for i in range(tm // sub):
        rs = slice(i * sub, (i + 1) * sub)
        fcum = (_dot(tril, f_hi[rs]) + _dot(tril, f_mid[rs])) + _dot(tril, f_lo[rs]) + carry
        fcum_ref[0, rs, :] = fcum
        carry = fcum[sub - 1:sub, :]
    fcar_scr[0:1, :] = carry


def _inproj(x, mod3, norm1_g, w_all, mu_p, bf_p, rw_cols, tm=512):
    B, T, D = x.shape
    ncols = w_all.shape[1]
    fx_cols = ncols - rw_cols
    return pl.pallas_call(
        functools.partial(_inproj_kernel, rw_cols=rw_cols),
        out_shape=(jax.ShapeDtypeStruct((B, T, rw_cols), F32),
                   jax.ShapeDtypeStruct((B, T, fx_cols), F32),
                   jax.ShapeDtypeStruct((B, T, LANES), F32)),
        grid=(B, T // tm),
        in_specs=[pl.BlockSpec((1, tm, D), lambda b, t: (b, t, 0)),
                  pl.BlockSpec((1, N_MOD, D), lambda b, t: (b, 0, 0)),
                  _const_spec((1, D)),
                  _const_spec((D, ncols), single_buffer=True),
                  _const_spec((1, rw_cols)),
                  _const_spec((1, LANES))],
        out_specs=(pl.BlockSpec((1, tm, rw_cols), lambda b, t: (b, t, 0)),
                   pl.BlockSpec((1, tm, fx_cols), lambda b, t: (b, t, 0)),
                   pl.BlockSpec((1, tm, LANES), lambda b, t: (b, t, 0))),
        scratch_shapes=[pltpu.VMEM((8, rw_cols), F32), pltpu.VMEM((8, LANES), F32)],
        compiler_params=pltpu.CompilerParams(
            dimension_semantics=("arbitrary", "arbitrary"), vmem_limit_bytes=VMEM_LIMIT),
        name="inproj",
    )(x, mod3, norm1_g.reshape(1, D), w_all, mu_p, bf_p)


def _segsum(x, bd, split=False):
    tile = bd.shape[0]
    hi = x.astype(BF16)
    lo = (x - hi.astype(F32)).astype(BF16) if split else None
    cols = []
    for j in range(x.shape[1] // tile):
        sl = slice(j * tile, (j + 1) * tile)
        s = _dot(hi[:, sl], bd)
        cols.append(s + _dot(lo[:, sl], bd) if split else s)
    return jnp.concatenate(cols, axis=1)


def _interleave(generators):
    live = [g for g in generators if g is not None]
    while live:
        for g in list(live):
            if next(g, StopIteration) is StopIteration:
                live.remove(g)


def _rwkv_kernel(p_ref, w0_ref, wdw_ref, a0_ref, waw_ref, wgw_ref, kk_ref, ka_ref, rk_ref,
                 gng_ref, gnb_ref, o_ref, s_scr, *, width, block_rows):
    t = pl.program_id(1)
    L = RW_CHUNK
    R = block_rows
    n_blocks = p_ref.shape[1] // R
    n_chunks = R // L
    W = width
    G2 = 2 * LANES
    groups = W // LANES
    items = [(c, j) for c in range(n_chunks) for j in range(groups)]
    rsl = lambda c: slice(c * L, (c + 1) * L)
    gsl = lambda j: slice(j * LANES, (j + 1) * LANES)

    @pl.when(t == 0)
    def _():
        s_scr[...] = jnp.zeros_like(s_scr)

    ri = lax.broadcasted_iota(jnp.int32, (MXU_TILE, MXU_TILE), 0)
    ci = lax.broadcasted_iota(jnp.int32, (MXU_TILE, MXU_TILE), 1)
    bd = ((ri // HEAD_DIM) == (ci // HEAD_DIM)).astype(BF16)
    tril = (lax.broadcasted_iota(jnp.int32, (L, L), 1)
            <= lax.broadcasted_iota(jnp.int32, (L, L), 0)).astype(BF16)
    o0 = 3 * W
    lane1 = lax.broadcasted_iota(jnp.int32, (L, LANES), 1)
    lane2 = lax.broadcasted_iota(jnp.int32, (L, G2), 1)
    row2 = lax.broadcasted_iota(jnp.int32, (L, G2), 0)
    first1 = lane1 < HEAD_DIM
    first2 = (lane2 % LANES) < HEAD_DIM
    strict2 = (lane2 % HEAD_DIM) < row2
    incl2 = (lane2 % HEAD_DIM) <= row2
    upper2 = lane2 >= LANES
    eye2 = ((lane1 % HEAD_DIM) == lax.broadcasted_iota(jnp.int32, (L, LANES), 0)).astype(F32)
    zero1 = jnp.zeros((L, LANES), BF16)
    rr = lax.broadcasted_iota(jnp.int32, (LANES, LANES), 0) < HEAD_DIM
    cc = lax.broadcasted_iota(jnp.int32, (LANES, LANES), 1) < HEAD_DIM
    diag_blocks = rr == cc

    def split(m, mask):
        return jnp.concatenate([jnp.where(mask, m, 0), jnp.where(mask, 0, m)], axis=0)

    def prepare(pb, b):
        p = p_ref[0, b * R:(b + 1) * R, :]
        r, k, v = p[:, 0:W], p[:, W:2 * W], p[:, 2 * W:3 * W]
        wd = p[:, o0:o0 + LANES]
        ad = p[:, o0 + LANES:o0 + 2 * LANES]
        gd = p[:, o0 + 2 * LANES:o0 + 4 * LANES]
        z = w0_ref[...] + _dot(_bf(jnp.tanh(wd)), wdw_ref[...])
        softplus = jnp.maximum(-z, 0.0) + jnp.log(1.0 + jnp.exp(-jnp.abs(z)))
        ld = -jnp.exp(-softplus - 0.5)
        yield
        a_sig = _sigmoid(a0_ref[...] + _dot(_bf(ad), waw_ref[...]))
        pb["gate"] = _dot(_bf(_sigmoid(gd)), wgw_ref[...])
        yield
        kk = k * kk_ref[...]
        kk = kk / jnp.maximum(jnp.sqrt(_segsum(kk * kk, bd, split=True)), 1e-12)
        yield
        k2 = k * (1.0 + (a_sig - 1.0) * ka_ref[...])
        b_ = kk * a_sig
        pb["bonus"] = _segsum(r * k2 * rk_ref[...], bd) * v
        yield
        ld_hi = _bf(ld)
        ld_lo = _bf(ld - ld_hi.astype(F32))
        cums = [_dot(tril, ld_hi[rsl(c)]) + _dot(tril, ld_lo[rsl(c)]) for c in range(n_chunks)]
        pb["tots"] = [cum_c[L - 1:L, :] for cum_c in cums]
        cum = jnp.concatenate(cums, axis=0)
        tot_b = jnp.concatenate([jnp.broadcast_to(tc, (L, W)) for tc in pb["tots"]], axis=0)
        yield
        e_neg = jnp.exp(-cum)
        rt = r * jnp.exp(cum)
        pb.update(rt=rt, rt_m=_bf(rt), at=_bf(-kk * jnp.exp(cum - ld)), bt=_bf(b_ * e_neg),
                  kt=_bf(k2 * e_neg))
        yield
        e_rem = jnp.exp(tot_b - cum)
        pb.update(bh=_bf(b_ * e_rem), kh=_bf(k2 * e_rem), vm=_bf(v))

    def chains(pb, cb):
        at, rt_m, bt, kt, vm = pb["at"], pb["rt_m"], pb["bt"], pb["kt"], pb["vm"]
        g_bot, akm, zz = {}, {}, {}
        for c, j in items:
            ar = jnp.concatenate([at[rsl(c), gsl(j)], rt_m[rsl(c), gsl(j)]], axis=0)
            rhs = jnp.concatenate([split(bt[rsl(c), gsl(j)], first1),
                                   split(kt[rsl(c), gsl(j)], first1)], axis=0)
            gm = _dot_nt(ar, rhs)
            top = jnp.where(strict2, gm[:L], 0.0)
            g_bot[c, j] = _bf(jnp.where(incl2, gm[L:], 0.0))
            akm[c, j] = _bf(top[:, LANES:])
            zz[c, j] = jnp.concatenate([top[:, :LANES], eye2], axis=1)
        yield
        for _ in range(int(math.log2(L))):
            for it in items:
                zb = _bf(zz[it])
                out = _dot(zb[:, :LANES], split(zb, first2))
                zz[it] = out + jnp.where(upper2, zz[it], 0.0)
            yield
        vbd = {(c, j): split(vm[rsl(c), gsl(j)], first1) for c, j in items}
        akv = {it: _bf(_dot(akm[it], vbd[it])) for it in items}
        yield
        x = {}
        for c, j in items:
            zed = jnp.concatenate([at[rsl(c), gsl(j)], akv[c, j]], axis=1)
            x[c, j] = _bf(_dot(_bf(zz[c, j][:, LANES:]), split(zed, first2)))
        yield
        cb.update(r_eff={}, y0={}, mt={}, nt={})
        for c, j in items:
            w4 = jnp.concatenate([split(x[c, j], first2),
                                  jnp.concatenate([jnp.concatenate([zero1, zero1], axis=0),
                                                   vbd[c, j]], axis=1)], axis=0)
            out = _dot(g_bot[c, j], w4)
            cb["r_eff"][c, j] = _bf(pb["rt"][rsl(c), gsl(j)] + out[:, :LANES])
            cb["y0"][c, j] = out[:, LANES:]
        yield
        for c, j in items:
            lhs = jnp.concatenate([pb["bh"][rsl(c), gsl(j)], pb["kh"][rsl(c), gsl(j)]], axis=0)
            rhs = jnp.concatenate([x[c, j], jnp.concatenate([zero1, vm[rsl(c), gsl(j)]], axis=1)],
                                  axis=0)
            pt = _dot_tn(lhs, rhs)
            cb["mt"][c, j] = _bf(jnp.where(diag_blocks, pt[:, :LANES], 0.0))
            cb["nt"][c, j] = jnp.where(diag_blocks, pt[:, LANES:], 0.0)
        yield
        cb["e_cols"] = [
            jnp.concatenate([jnp.exp(pb["tots"][c][:, gsl(j)]) for c in range(n_chunks)]
                            + [jnp.zeros((LANES - n_chunks, LANES), F32)], axis=0).T
            for j in range(groups)]

    state = [s_scr[j] for j in range(groups)]

    def tail(pb, cb, b):
        y_rows = []
        for c in range(n_chunks):
            hb = [_bf(s) for s in state]
            y_rows.append(jnp.concatenate(
                [_dot(cb["r_eff"][c, j], hb[j]) + cb["y0"][c, j] for j in range(groups)], axis=1))
            for j in range(groups):
                state[j] = (state[j] * cb["e_cols"][j][:, c:c + 1] + _dot(cb["mt"][c, j], hb[j])
                            + cb["nt"][c, j])
            yield
        y = jnp.concatenate(y_rows, axis=0)
        inv_n = 1.0 / HEAD_DIM
        mean = _segsum(y, bd) * inv_n
        d = y - mean
        yield
        var = _segsum(d * d, bd) * inv_n
        yn = d * lax.rsqrt(var + GN_EPS) * gng_ref[...] + gnb_ref[...]
        o_ref[0, b * R:(b + 1) * R, :] = ((yn + pb["bonus"]) * pb["gate"]).astype(o_ref.dtype)

    prep = [dict() for _ in range(n_blocks)]
    chain = [dict() for _ in range(n_blocks)]
    _interleave([prepare(prep[0], 0)])
    for b in range(n_blocks):
        _interleave([chains(prep[b], chain[b]),
                     prepare(prep[b + 1], b + 1) if b + 1 < n_blocks else None,
                     tail(prep[b - 1], chain[b - 1], b - 1) if b > 0 else None])
    _interleave([tail(prep[-1], chain[-1], n_blocks - 1)])
    for j in range(groups):
        s_scr[j] = state[j]


def _rwkv(rw, w0, wdw, a0, waw, wgw, k_k, k_a, r_k, gn_g, gn_b, width, rows=512, block_rows=256):
    B, T, cols = rw.shape
    row = lambda a: a.reshape(1, width)
    vec = _const_spec((1, width))
    return pl.pallas_call(
        functools.partial(_rwkv_kernel, width=width, block_rows=block_rows),
        out_shape=jax.ShapeDtypeStruct((B, T, width), BF16),
        grid=(B, T // rows),
        in_specs=[pl.BlockSpec((1, rows, cols), lambda b, t: (b, t, 0)),
                  vec, _const_spec(wdw.shape), vec, _const_spec(waw.shape),
                  _const_spec(wgw.shape), vec, vec, vec, vec, vec],
        out_specs=pl.BlockSpec((1, rows, width), lambda b, t: (b, t, 0)),
        scratch_shapes=[pltpu.VMEM((width // LANES, LANES, LANES), F32)],
        compiler_params=pltpu.CompilerParams(
            dimension_semantics=("arbitrary", "arbitrary"), vmem_limit_bytes=VMEM_LIMIT),
        name="rwkv",
    )(rw, row(w0), wdw, row(a0), waw, wgw, row(k_k), row(k_a), row(r_k), row(gn_g), row(gn_b))


def _fox_aug_tables(heads):
    sel = np.zeros((3 * LANES, 2 * heads * LANES), np.float32)
    bias = np.zeros((1, 2 * heads * LANES), np.float32)
    for h in range(heads):
        base = h * LANES + (HEAD_DIM if h % 2 == 0 else 0)
        kbase = heads * LANES + base
        for piece in range(3):
            sel[piece * LANES + h, base + piece] = 1.0
            sel[piece * LANES + h, kbase + 3 + piece] = -1.0
            bias[0, base + 3 + piece] = 1.0
            bias[0, kbase + piece] = 1.0
    return jnp.asarray(sel, BF16), jnp.asarray(bias)


def _foxprep_kernel(fx_ref, fcum_ref, qg_ref, kg_ref, sel_ref, bias_ref, qa_ref, ka_ref, v_ref,
                    *, width):
    W = width
    heads = W // HEAD_DIM
    pf = fx_ref[0]
    tm = pf.shape[0]
    fcum = fcum_ref[0] * LOG2E
    f_hi = fcum.astype(BF16)
    r1 = fcum - f_hi.astype(F32)
    f_mid = r1.astype(BF16)
    f_lo = (r1 - f_mid.astype(F32)).astype(BF16)
    aug = _dot(jnp.concatenate([f_hi, f_mid, f_lo], axis=1), sel_ref[...]) + bias_ref[...]
    first = lax.broadcasted_iota(jnp.int32, (tm, LANES), 1) < HEAD_DIM
    for side, (gain_ref, out_ref) in enumerate(((qg_ref, qa_ref), (kg_ref, ka_ref))):
        for g in range(W // LANES):
            x = pf[:, side * W + g * LANES:side * W + (g + 1) * LANES]
            sq = x * x
            lo = jnp.sum(jnp.where(first, sq, 0.0), axis=-1, keepdims=True)
            hi = jnp.sum(jnp.where(first, 0.0, sq), axis=-1, keepdims=True)
            ms = jnp.where(first, lo, hi) * (1.0 / HEAD_DIM)
            xn = x * lax.rsqrt(ms + NORM_EPS) * gain_ref[...]
            for hh in range(2):
                h = 2 * g + hh
                a = aug[:, (side * heads + h) * LANES:(side * heads + h + 1) * LANES]
                keep = first if hh == 0 else jnp.logical_not(first)
                out_ref[0, h] = jnp.where(keep, xn, a).astype(BF16)
    pad = (lax.broadcasted_iota(jnp.int32, (V_ROWS - HEAD_DIM, tm), 0) == 0).astype(BF16)
    for g in range(W // LANES):
        vt = pf[:, 2 * W + g * LANES:2 * W + (g + 1) * LANES].T.astype(BF16)
        for hh in range(2):
            v_ref[0, 2 * g + hh, 0, :HEAD_DIM, :] = vt[hh * HEAD_DIM:(hh + 1) * HEAD_DIM]
            v_ref[0, 2 * g + hh, 0, HEAD_DIM:, :] = pad


def _foxprep(fx, fcum, q_norm_g, k_norm_g, width, tm=512):
    B, T, cols = fx.shape
    heads = width // HEAD_DIM
    sel, bias = _fox_aug_tables(heads)
    scale = LOG2E / math.sqrt(HEAD_DIM)
    qg2 = jnp.tile(q_norm_g * scale, 2).reshape(1, LANES)
    kg2 = jnp.tile(k_norm_g, 2).reshape(1, LANES)
    return pl.pallas_call(
        functools.partial(_foxprep_kernel, width=width),
        out_shape=(jax.ShapeDtypeStruct((B, heads, T, LANES), BF16),
                   jax.ShapeDtypeStruct((B, heads, T, LANES), BF16),
                   jax.ShapeDtypeStruct((B, heads, T // tm, V_ROWS, tm), BF16)),
        grid=(B, T // tm),
        in_specs=[pl.BlockSpec((1, tm, cols), lambda b, t: (b, t, 0)),
                  pl.BlockSpec((1, tm, LANES), lambda b, t: (b, t, 0)),
                  _const_spec((1, LANES)), _const_spec((1, LANES)),
                  _const_spec(sel.shape), _const_spec(bias.shape)],
        out_specs=(pl.BlockSpec((1, heads, tm, LANES), lambda b, t: (b, 0, t, 0)),
                   pl.BlockSpec((1, heads, tm, LANES), lambda b, t: (b, 0, t, 0)),
                   pl.BlockSpec((1, heads, 1, V_ROWS, tm), lambda b, t: (b, 0, t, 0, 0))),
        compiler_params=pltpu.CompilerParams(
            dimension_semantics=("arbitrary", "arbitrary"), vmem_limit_bytes=VMEM_LIMIT),
        name="foxprep",
    )(fx, fcum, qg2, kg2, sel, bias)


def _fox_kernel(qtab, ktab, qa_ref, ka_ref, vt_ref, og_ref, g_ref, o_ref,
                m_all, acc_all, s_scr, p_scr, al_scr, *, blk, cb, n_off):
    n_q = qa_ref.shape[2] // blk
    n_el = n_off + n_q
    chains = [(hh, c) for hh in range(2) for c in range(blk // cb)]
    csl = lambda c: slice(c * cb, (c + 1) * cb)

    m_all[...] = jnp.full_like(m_all, NEG_INF)
    acc_all[...] = jnp.zeros_like(acc_all)

    def element(t):
        if isinstance(t, int) and t >= n_off:
            return t - n_off, t - n_off, True
        return qtab[t], ktab[t], False

    def start(i, size):
        return i * size if isinstance(i, int) else pl.multiple_of(i * size, size)

    def n_keys(c, diag):
        return (c + 1) * cb if diag else blk

    def logits(t, par):
        qi, ki, diag = element(t)
        for hh, c in chains:
            nk = n_keys(c, diag)
            s_scr[par, hh, :nk, csl(c)] = _dot_nt(
                ka_ref[0, hh, pl.ds(start(ki, blk), nk), :],
                qa_ref[0, hh, pl.ds(start(qi, blk) + c * cb, cb), :])

    def softmax(t, par):
        qi, _, diag = element(t)
        for hh, c in chains:
            nk = n_keys(c, diag)
            s = s_scr[par, hh, :nk, csl(c)]
            if diag:
                key = lax.broadcasted_iota(jnp.int32, (nk, cb), 0)
                qry = c * cb + lax.broadcasted_iota(jnp.int32, (nk, cb), 1)
                s = jnp.where(key <= qry, s, NEG_INF)
            m_prev = m_all[qi, hh, :, csl(c)]
            m_new = jnp.maximum(m_prev, jnp.max(s, axis=0, keepdims=True))
            m_all[qi, hh, :, csl(c)] = m_new
            al_scr[par, hh, :, csl(c)] = jnp.exp2(m_prev - m_new)
            p_scr[par, hh, :nk, csl(c)] = jnp.exp2(s - m_new).astype(BF16)

    def values(t, par):
        qi, ki, diag = element(t)
        for hh, c in chains:
            nk = n_keys(c, diag)
            pv = _dot(vt_ref[0, hh, ki, :, :nk], p_scr[par, hh, :nk, csl(c)])
            acc_all[qi, hh, :, csl(c)] = (al_scr[par, hh, :, csl(c)] * acc_all[qi, hh, :, csl(c)]
                                          + pv)
        if diag:
            normed = []
            for hh in range(2):
                acc = acc_all[qi, hh]
                o = acc[:HEAD_DIM] / acc[HEAD_DIM:HEAD_DIM + 1]
                ms = jnp.mean(o * o, axis=0, keepdims=True)
                normed.append(o * lax.rsqrt(ms + NORM_EPS))
            on = jnp.concatenate(normed, axis=0).T
            rows = slice(qi * blk, (qi + 1) * blk)
            o_ref[0, rows, :] = (on * g_ref[...] * _sigmoid(og_ref[0, rows, :])).astype(o_ref.dtype)

    def step(t, par):
        for lag, stage in enumerate((logits, softmax, values)):
            if not isinstance(t, int) or 0 <= t - lag < n_el:
                stage(t - lag, (par + lag) % 2)

    def two_steps(u, carry):
        step(2 * u + 2, 0)
        step(2 * u + 3, 1)
        return carry

    step(0, 0)
    step(1, 1)
    lax.fori_loop(0, (n_off - 2) // 2, two_steps, 0)
    for t in range(n_off, n_el + 2):
        step(t, t % 2)


def _fox(qa, ka, vt, fx, fox_out_g, width, cb=256):
    B, H, T, _ = qa.shape
    blk = vt.shape[4]
    n_q = T // blk
    pairs = [(qi, ki) for qi in range(n_q) for ki in range(qi)]
    n_off = len(pairs)
    assert n_off >= 2 and n_off % 2 == 0, "pipeline loop is unrolled by two"
    qtab = jnp.asarray(np.array([p[0] for p in pairs], np.int32))
    ktab = jnp.asarray(np.array([p[1] for p in pairs], np.int32))
    og_blk = 3 * width // LANES
    grid_spec = pltpu.PrefetchScalarGridSpec(
        num_scalar_prefetch=2,
        grid=(B, H // 2),
        in_specs=[pl.BlockSpec((1, 2, T, LANES), lambda b, h, qt, kt: (b, h, 0, 0)),
                  pl.BlockSpec((1, 2, T, LANES), lambda b, h, qt, kt: (b, h, 0, 0)),
                  pl.BlockSpec((1, 2, n_q, V_ROWS, blk), lambda b, h, qt, kt: (b, h, 0, 0, 0)),
                  pl.BlockSpec((1, T, LANES), lambda b, h, qt, kt: (b, 0, og_blk + h)),
                  pl.BlockSpec((1, LANES), lambda b, h, qt, kt: (0, h))],
        out_specs=pl.BlockSpec((1, T, LANES), lambda b, h, qt, kt: (b, 0, h)),
        scratch_shapes=[pltpu.VMEM((n_q, 2, 1, blk), F32), pltpu.VMEM((n_q, 2, V_ROWS, blk), F32),
                        pltpu.VMEM((2, 2, blk, blk), F32), pltpu.VMEM((2, 2, blk, blk), BF16),
                        pltpu.VMEM((2, 2, 1, blk), F32)])
    return pl.pallas_call(
        functools.partial(_fox_kernel, blk=blk, cb=cb, n_off=n_off),
        out_shape=jax.ShapeDtypeStruct((B, T, width), BF16),
        grid_spec=grid_spec,
        compiler_params=pltpu.CompilerParams(
            dimension_semantics=("arbitrary",) * 2, vmem_limit_bytes=VMEM_LIMIT),
        name="fox",
    )(qtab, ktab, qa, ka, vt, fx, fox_out_g.reshape(1, width))


def _outmlp_kernel(x_ref, yrw_ref, yfx_ref, mod_ref, wo_ref, g2_ref, w1_ref, w2_ref, gf_ref,
                   o_ref, *, ff_tile):
    x = x_ref[0]
    half = yrw_ref.shape[2]
    gate1 = mod_ref[0, 2:3, :]
    shift2 = mod_ref[0, 3:4, :]
    scale2 = mod_ref[0, 4:5, :]
    gate2 = mod_ref[0, 5:6, :]
    y = _dot(yrw_ref[0], wo_ref[:half, :]) + _dot(yfx_ref[0], wo_ref[half:, :])
    h1 = x + gate1 * y
    ms = jnp.mean(h1 * h1, axis=-1, keepdims=True)
    u = ((h1 * lax.rsqrt(ms + NORM_EPS) * g2_ref[...]) * (1.0 + scale2) + shift2).astype(BF16)
    acc = jnp.zeros_like(x)
    for j in range(w1_ref.shape[1] // ff_tile):
        hid = jnp.maximum(_dot(u, w1_ref[:, j * ff_tile:(j + 1) * ff_tile]), 0.0)
        acc = acc + _dot((hid * hid).astype(BF16), w2_ref[j * ff_tile:(j + 1) * ff_tile, :])
    h2 = h1 + gate2 * acc
    ms2 = jnp.mean(h2 * h2, axis=-1, keepdims=True)
    o_ref[0] = h2 * lax.rsqrt(ms2 + NORM_EPS) * gf_ref[...]


def _outmlp(x, y_rw, y_fx, mod3, w_o, norm2_g, w1, w2, final_g, tm=512, ff_tile=1024):
    B, T, D = x.shape
    half = y_rw.shape[2]
    return pl.pallas_call(
        functools.partial(_outmlp_kernel, ff_tile=ff_tile),
        out_shape=jax.ShapeDtypeStruct((B, T, D), F32),
        grid=(B, T // tm),
        in_specs=[pl.BlockSpec((1, tm, D), lambda b, t: (b, t, 0)),
                  pl.BlockSpec((1, tm, half), lambda b, t: (b, t, 0)),
                  pl.BlockSpec((1, tm, half), lambda b, t: (b, t, 0)),
                  pl.BlockSpec((1, N_MOD, D), lambda b, t: (b, 0, 0)),
                  _const_spec(w_o.shape, single_buffer=True), _const_spec((1, D)),
                  _const_spec(w1.shape, single_buffer=True),
                  _const_spec(w2.shape, single_buffer=True), _const_spec((1, D))],
        out_specs=pl.BlockSpec((1, tm, D), lambda b, t: (b, t, 0)),
        compiler_params=pltpu.CompilerParams(
            dimension_semantics=("arbitrary", "arbitrary"), vmem_limit_bytes=VMEM_LIMIT),
        name="outmlp",
    )(x, y_rw, y_fx, mod3, w_o, norm2_g.reshape(1, D), w1, w2, final_g.reshape(1, D))


def _pad_cols(a, n):
    return jnp.pad(a, ((0, 0), (0, n - a.shape[1])))


def _pad_rows(a, n):
    return jnp.pad(a, ((0, n - a.shape[0]), (0, 0)))


def _branches(x, c, w_ada, b_ada, norm1_g, w_in, mu_shift, w0, w_up_decay, a0, w_up_a, w_up_g,
              k_k, k_a, r_k, gn_g, gn_b, b_f, q_norm_g, k_norm_g, fox_out_g):
    B, T, D = x.shape
    W = w0.shape[0]
    heads = b_f.shape[0]
    rw_n = 3 * W + DECAY_LORA + AAA_LORA + GATE_LORA

    def regroup(a):
        rw, fx = a[:, :rw_n], a[:, rw_n:]
        o = 3 * W
        return jnp.concatenate([
            rw[:, :o],
            _pad_cols(rw[:, o:o + DECAY_LORA], LANES),
            _pad_cols(rw[:, o + DECAY_LORA:o + DECAY_LORA + AAA_LORA], LANES),
            _pad_cols(rw[:, o + DECAY_LORA + AAA_LORA:], 2 * LANES),
            fx[:, :4 * W],
            _pad_cols(fx[:, 4 * W:], LANES)], axis=1)

    rw_cols = 3 * W + 4 * LANES
    w_all = regroup(w_in).astype(BF16)
    mu_p = regroup(jnp.pad(mu_shift.reshape(1, rw_n), ((0, 0), (0, w_in.shape[1] - rw_n))))[:, :rw_cols]
    bf_p = _pad_cols(b_f.reshape(1, heads), LANES)
    wdw = _pad_rows(w_up_decay, LANES).astype(BF16)
    waw = _pad_rows(w_up_a, LANES).astype(BF16)
    wgw = _pad_rows(w_up_g, 2 * LANES).astype(BF16)

    mod3 = _ada(c, w_ada, b_ada).reshape(B, N_MOD, D)
    rw, fx, fcum = _inproj(x, mod3, norm1_g, w_all, mu_p, bf_p, rw_cols)
    y_rw = _rwkv(rw, w0, wdw, a0, waw, wgw, k_k, k_a, r_k.reshape(-1), gn_g, gn_b, W)
    qa, ka, vt = _foxprep(fx, fcum, q_norm_g, k_norm_g, W)
    y_fx = _fox(qa, ka, vt, fx, fox_out_g, W)
    return y_rw, y_fx, mod3


def kernel(x, c, w_ada, b_ada, norm1_g, w_in, mu_shift, w0, w_up_decay, a0, w_up_a, w_up_g,
           k_k, k_a, r_k, gn_g, gn_b, b_f, q_norm_g, k_norm_g, fox_out_g, w_o, norm2_g,
           w_mlp1, w_mlp2, final_g):
    y_rw, y_fx, mod3 = _branches(x, c, w_ada, b_ada, norm1_g, w_in, mu_shift, w0, w_up_decay,
                                 a0, w_up_a, w_up_g, k_k, k_a, r_k, gn_g, gn_b, b_f,
                                 q_norm_g, k_norm_g, fox_out_g)
    return _outmlp(x, y_rw, y_fx, mod3, w_o.astype(BF16), norm2_g,
                   w_mlp1.astype(BF16), w_mlp2.astype(BF16), final_g)
```

```python
import functools
import math

import jax
import jax.numpy as jnp
import numpy as np
from jax import lax
from jax.experimental import pallas as pl
from jax.experimental.pallas import tpu as pltpu

F32 = jnp.float32
BF16 = jnp.bfloat16

HEAD_DIM = 64
LANES = 128
MXU_TILE = 256
NORM_EPS = 1e-6
GN_EPS = 64e-5
NEG_INF = -1e30
LOG2E = 1.4426950408889634
N_MOD = 6
DECAY_LORA = 64
AAA_LORA = 64
GATE_LORA = 160
RW_CHUNK = 64
V_ROWS = HEAD_DIM + 16
VMEM_LIMIT = 56 * 1024 * 1024

_HI = lax.Precision.HIGHEST


def _dot(a, b):
    return jnp.dot(a, b, preferred_element_type=F32)


def _dot_hi(a, b):
    return jnp.dot(a, b, preferred_element_type=F32, precision=_HI)


def _dot_nt(a, b):
    return lax.dot_general(a, b, (((1,), (1,)), ((), ())), preferred_element_type=F32)


def _dot_tn(a, b):
    return lax.dot_general(a, b, (((0,), (0,)), ((), ())), preferred_element_type=F32)


def _bf(x):
    return x.astype(BF16)


def _sigmoid(x):
    return 1.0 / (1.0 + jnp.exp(-x))


def _const_spec(shape, single_buffer=False):
    n = len(shape)
    mode = pl.Buffered(1) if single_buffer else None
    return pl.BlockSpec(shape, lambda *_: (0,) * n, pipeline_mode=mode)


def _ada_kernel(c_ref, w_ref, b_ref, o_ref):
    c = c_ref[...]
    o_ref[...] = _dot_hi(c * _sigmoid(c), w_ref[...]) + b_ref[...]


def _ada(c, w_ada, b_ada):
    B, D = c.shape
    n = w_ada.shape[1]
    return pl.pallas_call(
        _ada_kernel,
        out_shape=jax.ShapeDtypeStruct((B, n), F32),
        grid=(n // D,),
        in_specs=[pl.BlockSpec((B, D), lambda j: (0, 0)),
                  pl.BlockSpec((D, D), lambda j: (0, j)),
                  pl.BlockSpec((1, D), lambda j: (0, j))],
        out_specs=pl.BlockSpec((B, D), lambda j: (0, j)),
        name="ada",
    )(c, w_ada, b_ada.reshape(1, n))


def _fox_aug_tables(heads):
    sel = np.zeros((3 * LANES, 2 * heads * LANES), np.float32)
    bias = np.zeros((1, 2 * heads * LANES), np.float32)
    for h in range(heads):
        base = h * LANES + (HEAD_DIM if h % 2 == 0 else 0)
        kbase = heads * LANES + base
        for piece in range(3):
            sel[piece * LANES + h, base + piece] = 1.0
            sel[piece * LANES + h, kbase + 3 + piece] = -1.0
            bias[0, base + 3 + piece] = 1.0
            bias[0, kbase + piece] = 1.0
    return jnp.asarray(sel, BF16), jnp.asarray(bias)


def _fox_layout(pf, fcum, qg_ref, kg_ref, sel_ref, bias_ref, qa_ref, ka_ref, v_ref, og_ref, width):
    W = width
    heads = W // HEAD_DIM
    tm = pf.shape[0]
    fcum = fcum * LOG2E
    f_hi = fcum.astype(BF16)
    r1 = fcum - f_hi.astype(F32)
    f_mid = r1.astype(BF16)
    f_lo = (r1 - f_mid.astype(F32)).astype(BF16)
    aug = _dot(jnp.concatenate([f_hi, f_mid, f_lo], axis=1), sel_ref[...]) + bias_ref[...]
    first = lax.broadcasted_iota(jnp.int32, (tm, LANES), 1) < HEAD_DIM
    for side, (gain_ref, out_ref) in enumerate(((qg_ref, qa_ref), (kg_ref, ka_ref))):
        for g in range(W // LANES):
            x = pf[:, side * W + g * LANES:side * W + (g + 1) * LANES]
            sq = x * x
            lo = jnp.sum(jnp.where(first, sq, 0.0), axis=-1, keepdims=True)
            hi = jnp.sum(jnp.where(first, 0.0, sq), axis=-1, keepdims=True)
            ms = jnp.where(first, lo, hi) * (1.0 / HEAD_DIM)
            xn = x * lax.rsqrt(ms + NORM_EPS) * gain_ref[...]
            for hh in range(2):
                h = 2 * g + hh
                a = aug[:, (side * heads + h) * LANES:(side * heads + h + 1) * LANES]
                keep = first if hh == 0 else jnp.logical_not(first)
                out_ref[0, h] = jnp.where(keep, xn, a).astype(BF16)
    pad = (lax.broadcasted_iota(jnp.int32, (V_ROWS - HEAD_DIM, tm), 0) == 0).astype(BF16)
    for g in range(W // LANES):
        vt = pf[:, 2 * W + g * LANES:2 * W + (g + 1) * LANES].T.astype(BF16)
        for hh in range(2):
            v_ref[0, 2 * g + hh, 0, :HEAD_DIM, :] = vt[hh * HEAD_DIM:(hh + 1) * HEAD_DIM]
            v_ref[0, 2 * g + hh, 0, HEAD_DIM:, :] = pad
    og_ref[0] = pf[:, 3 * W:4 * W].astype(BF16)


def _inproj_kernel(x_ref, mod_ref, g_ref, w_ref, mu_ref, bf_ref, qg_ref, kg_ref, sel_ref, bias_ref,
                   rw_ref, qa_ref, ka_ref, v_ref, og_ref, prev_scr, fcar_scr, *, rw_cols):
    t = pl.program_id(1)
    tm = x_ref.shape[1]

    @pl.when(t == 0)
    def _():
        prev_scr[...] = jnp.zeros_like(prev_scr)
        fcar_scr[...] = jnp.zeros_like(fcar_scr)

    x = x_ref[0]
    shift = mod_ref[0, 0:1, :]
    scale = mod_ref[0, 1:2, :]
    ms = jnp.mean(x * x, axis=-1, keepdims=True)
    u = (x * lax.rsqrt(ms + NORM_EPS) * g_ref[...]) * (1.0 + scale) + shift
    u = u.astype(BF16)

    p = _dot(u, w_ref[:, :rw_cols])
    row = lax.broadcasted_iota(jnp.int32, p.shape, 0)
    prev = jnp.where(row == 0, prev_scr[0:1, :], pltpu.roll(p, shift=1, axis=0))
    prev_scr[0:1, :] = p[tm - 1:tm, :]
    rw_ref[0] = p + (prev - p) * mu_ref[...]

    pf = _dot(u, w_ref[:, rw_cols:])
    z = pf[:, pf.shape[1] - LANES:] + bf_ref[...]
    logf = jnp.minimum(z, 0.0) - jnp.log(1.0 + jnp.exp(-jnp.abs(z)))
    sub = min(tm, MXU_TILE)
    r2 = lax.broadcasted_iota(jnp.int32, (sub, sub), 0)
    c2 = lax.broadcasted_iota(jnp.int32, (sub, sub), 1)
    tril = (c2 <= r2).astype(BF16)
    f_hi = logf.astype(BF16)
    r1 = logf - f_hi.astype(F32)
    f_mid = r1.astype(BF16)
    f_lo = (r1 - f_mid.astype(F32)).astype(BF16)
    carry = fcar_scr[0:1, :]
    fcums = []
    for i in range(tm // sub):
        rs = slice(i * sub, (i + 1) * sub)
        fcums.append((_dot(tril, f_hi[rs]) + _dot(tril, f_mid[rs])) + _dot(tril, f_lo[rs]) + carry)
        carry = fcums[-1][sub - 1:sub, :]
    fcar_scr[0:1, :] = carry
    _fox_layout(pf, jnp.concatenate(fcums, axis=0), qg_ref, kg_ref, sel_ref, bias_ref,
                qa_ref, ka_ref, v_ref, og_ref, (pf.shape[1] - LANES) // 4)


def _inproj(x, mod3, norm1_g, w_all, mu_p, bf_p, q_norm_g, k_norm_g, rw_cols, tm=512):
    B, T, D = x.shape
    ncols = w_all.shape[1]
    width = (ncols - rw_cols - LANES) // 4
    heads = width // HEAD_DIM
    sel, bias = _fox_aug_tables(heads)
    scale = LOG2E / math.sqrt(HEAD_DIM)
    qg2 = jnp.tile(q_norm_g * scale, 2).reshape(1, LANES)
    kg2 = jnp.tile(k_norm_g, 2).reshape(1, LANES)
    return pl.pallas_call(
        functools.partial(_inproj_kernel, rw_cols=rw_cols),
        out_shape=(jax.ShapeDtypeStruct((B, T, rw_cols), F32),
                   jax.ShapeDtypeStruct((B, heads, T, LANES), BF16),
                   jax.ShapeDtypeStruct((B, heads, T, LANES), BF16),
                   jax.ShapeDtypeStruct((B, heads, T // tm, V_ROWS, tm), BF16),
                   jax.ShapeDtypeStruct((B, T, width), BF16)),
        grid=(B, T // tm),
        in_specs=[pl.BlockSpec((1, tm, D), lambda b, t: (b, t, 0)),
                  pl.BlockSpec((1, N_MOD, D), lambda b, t: (b, 0, 0)),
                  _const_spec((1, D)),
                  _const_spec((D, ncols), single_buffer=True),
                  _const_spec((1, rw_cols)),
                  _const_spec((1, LANES)),
                  _const_spec((1, LANES)), _const_spec((1, LANES)),
                  _const_spec(sel.shape, single_buffer=True), _const_spec(bias.shape)],
        out_specs=(pl.BlockSpec((1, tm, rw_cols), lambda b, t: (b, t, 0)),
                   pl.BlockSpec((1, heads, tm, LANES), lambda b, t: (b, 0, t, 0)),
                   pl.BlockSpec((1, heads, tm, LANES), lambda b, t: (b, 0, t, 0)),
                   pl.BlockSpec((1, heads, 1, V_ROWS, tm), lambda b, t: (b, 0, t, 0, 0)),
                   pl.BlockSpec((1, tm, width), lambda b, t: (b, t, 0))),
        scratch_shapes=[pltpu.VMEM((8, rw_cols), F32), pltpu.VMEM((8, LANES), F32)],
        compiler_params=pltpu.CompilerParams(
            dimension_semantics=("arbitrary", "arbitrary"), vmem_limit_bytes=VMEM_LIMIT),
        name="inproj",
    )(x, mod3, norm1_g.reshape(1, D), w_all, mu_p, bf_p, qg2, kg2, sel, bias)


def _segsum(x, bd, split=False):
    tile = bd.shape[0]
    hi = x.astype(BF16)
    lo = (x - hi.astype(F32)).astype(BF16) if split else None
    cols = []
    for j in range(x.shape[1] // tile):
        sl = slice(j * tile, (j + 1) * tile)
        s = _dot(hi[:, sl], bd)
        cols.append(s + _dot(lo[:, sl], bd) if split else s)
    return jnp.concatenate(cols, axis=1)


def _interleave(generators):
    live = [g for g in generators if g is not None]
    while live:
        for g in list(live):
            if next(g, StopIteration) is StopIteration:
                live.remove(g)


def _rwkv_kernel(p_ref, w0_ref, wdw_ref, a0_ref, waw_ref, wgw_ref, kk_ref, ka_ref, rk_ref,
                 gng_ref, gnb_ref, o_ref, s_scr, *, width, block_rows):
    t = pl.program_id(1)
    L = RW_CHUNK
    R = block_rows
    n_blocks = p_ref.shape[1] // R
    n_chunks = R // L
    W = width
    G2 = 2 * LANES
    groups = W // LANES
    items = [(c, j) for c in range(n_chunks) for j in range(groups)]
    rsl = lambda c: slice(c * L, (c + 1) * L)
    gsl = lambda j: slice(j * LANES, (j + 1) * LANES)

    @pl.when(t == 0)
    def _():
        s_scr[...] = jnp.zeros_like(s_scr)

    ri = lax.broadcasted_iota(jnp.int32, (MXU_TILE, MXU_TILE), 0)
    ci = lax.broadcasted_iota(jnp.int32, (MXU_TILE, MXU_TILE), 1)
    bd = ((ri // HEAD_DIM) == (ci // HEAD_DIM)).astype(BF16)
    tril = (lax.broadcasted_iota(jnp.int32, (L, L), 1)
            <= lax.broadcasted_iota(jnp.int32, (L, L), 0)).astype(BF16)
    o0 = 3 * W
    lane1 = lax.broadcasted_iota(jnp.int32, (L, LANES), 1)
    lane2 = lax.broadcasted_iota(jnp.int32, (L, G2), 1)
    row2 = lax.broadcasted_iota(jnp.int32, (L, G2), 0)
    first1 = lane1 < HEAD_DIM
    first2 = (lane2 % LANES) < HEAD_DIM
    strict2 = (lane2 % HEAD_DIM) < row2
    incl2 = (lane2 % HEAD_DIM) <= row2
    upper2 = lane2 >= LANES
    eye2 = ((lane1 % HEAD_DIM) == lax.broadcasted_iota(jnp.int32, (L, LANES), 0)).astype(F32)
    zero1 = jnp.zeros((L, LANES), BF16)
    rr = lax.broadcasted_iota(jnp.int32, (LANES, LANES), 0) < HEAD_DIM
    cc = lax.broadcasted_iota(jnp.int32, (LANES, LANES), 1) < HEAD_DIM
    diag_blocks = rr == cc

    def split(m, mask):
        return jnp.concatenate([jnp.where(mask, m, 0), jnp.where(mask, 0, m)], axis=0)

    def prepare(pb, b):
        p = p_ref[0, b * R:(b + 1) * R, :]
        r, k, v = p[:, 0:W], p[:, W:2 * W], p[:, 2 * W:3 * W]
        wd = p[:, o0:o0 + LANES]
        ad = p[:, o0 + LANES:o0 + 2 * LANES]
        gd = p[:, o0 + 2 * LANES:o0 + 4 * LANES]
        z = w0_ref[...] + _dot(_bf(jnp.tanh(wd)), wdw_ref[...])
        softplus = jnp.maximum(-z, 0.0) + jnp.log(1.0 + jnp.exp(-jnp.abs(z)))
        ld = -jnp.exp(-softplus - 0.5)
        yield
        a_sig = _sigmoid(a0_ref[...] + _dot(_bf(ad), waw_ref[...]))
        pb["gate"] = _dot(_bf(_sigmoid(gd)), wgw_ref[...])
        yield
        kk = k * kk_ref[...]
        kk = kk / jnp.maximum(jnp.sqrt(_segsum(kk * kk, bd, split=True)), 1e-12)
        yield
        k2 = k * (1.0 + (a_sig - 1.0) * ka_ref[...])
        b_ = kk * a_sig
        pb["bonus"] = _segsum(r * k2 * rk_ref[...], bd) * v
        yield
        ld_hi = _bf(ld)
        ld_lo = _bf(ld - ld_hi.astype(F32))
        cums = [_dot(tril, ld_hi[rsl(c)]) + _dot(tril, ld_lo[rsl(c)]) for c in range(n_chunks)]
        pb["tots"] = [cum_c[L - 1:L, :] for cum_c in cums]
        cum = jnp.concatenate(cums, axis=0)
        tot_b = jnp.concatenate([jnp.broadcast_to(tc, (L, W)) for tc in pb["tots"]], axis=0)
        yield
        e_neg = jnp.exp(-cum)
        rt = r * jnp.exp(cum)
        pb.update(rt=rt, rt_m=_bf(rt), at=_bf(-kk * jnp.exp(cum - ld)), bt=_bf(b_ * e_neg),
                  kt=_bf(k2 * e_neg))
        yield
        e_rem = jnp.exp(tot_b - cum)
        pb.update(bh=_bf(b_ * e_rem), kh=_bf(k2 * e_rem), vm=_bf(v))

    def chains(pb, cb):
        at, rt_m, bt, kt, vm = pb["at"], pb["rt_m"], pb["bt"], pb["kt"], pb["vm"]
        g_bot, akm, zz = {}, {}, {}
        for c, j in items:
            ar = jnp.concatenate([at[rsl(c), gsl(j)], rt_m[rsl(c), gsl(j)]], axis=0)
            rhs = jnp.concatenate([split(bt[rsl(c), gsl(j)], first1),
                                   split(kt[rsl(c), gsl(j)], first1)], axis=0)
            gm = _dot_nt(ar, rhs)
            top = jnp.where(strict2, gm[:L], 0.0)
            g_bot[c, j] = _bf(jnp.where(incl2, gm[L:], 0.0))
            akm[c, j] = _bf(top[:, LANES:])
            zz[c, j] = jnp.concatenate([top[:, :LANES], eye2], axis=1)
        yield
        for _ in range(int(math.log2(L))):
            for it in items:
                zb = _bf(zz[it])
                out = _dot(zb[:, :LANES], split(zb, first2))
                zz[it] = out + jnp.where(upper2, zz[it], 0.0)
            yield
        vbd = {(c, j): split(vm[rsl(c), gsl(j)], first1) for c, j in items}
        akv = {it: _bf(_dot(akm[it], vbd[it])) for it in items}
        yield
        x = {}
        for c, j in items:
            zed = jnp.concatenate([at[rsl(c), gsl(j)], akv[c, j]], axis=1)
            x[c, j] = _bf(_dot(_bf(zz[c, j][:, LANES:]), split(zed, first2)))
        yield
        cb.update(r_eff={}, y0={}, mt={}, nt={})
        for c, j in items:
            w4 = jnp.concatenate([split(x[c, j], first2),
                                  jnp.concatenate([jnp.concatenate([zero1, zero1], axis=0),
                                                   vbd[c, j]], axis=1)], axis=0)
            out = _dot(g_bot[c, j], w4)
            cb["r_eff"][c, j] = _bf(pb["rt"][rsl(c), gsl(j)] + out[:, :LANES])
            cb["y0"][c, j] = out[:, LANES:]
        yield
        for c, j in items:
            lhs = jnp.concatenate([pb["bh"][rsl(c), gsl(j)], pb["kh"][rsl(c), gsl(j)]], axis=0)
            rhs = jnp.concatenate([x[c, j], jnp.concatenate([zero1, vm[rsl(c), gsl(j)]], axis=1)],
                                  axis=0)
            pt = _dot_tn(lhs, rhs)
            cb["mt"][c, j] = _bf(jnp.where(diag_blocks, pt[:, :LANES], 0.0))
            cb["nt"][c, j] = jnp.where(diag_blocks, pt[:, LANES:], 0.0)
        yield
        cb["e_cols"] = [
            jnp.concatenate([jnp.exp(pb["tots"][c][:, gsl(j)]) for c in range(n_chunks)]
                            + [jnp.zeros((LANES - n_chunks, LANES), F32)], axis=0).T
            for j in range(groups)]

    state = [s_scr[j] for j in range(groups)]

    def tail(pb, cb, b):
        y_rows = []
        for c in range(n_chunks):
            hb = [_bf(s) for s in state]
            y_rows.append(jnp.concatenate(
                [_dot(cb["r_eff"][c, j], hb[j]) + cb["y0"][c, j] for j in range(groups)], axis=1))
            for j in range(groups):
                state[j] = (state[j] * cb["e_cols"][j][:, c:c + 1] + _dot(cb["mt"][c, j], hb[j])
                            + cb["nt"][c, j])
            yield
        y = jnp.concatenate(y_rows, axis=0)
        inv_n = 1.0 / HEAD_DIM
        mean = _segsum(y, bd) * inv_n
        d = y - mean
        yield
        var = _segsum(d * d, bd) * inv_n
        yn = d * lax.rsqrt(var + GN_EPS) * gng_ref[...] + gnb_ref[...]
        o_ref[0, b * R:(b + 1) * R, :] = ((yn + pb["bonus"]) * pb["gate"]).astype(o_ref.dtype)

    prep = [dict() for _ in range(n_blocks)]
    chain = [dict() for _ in range(n_blocks)]
    _interleave([prepare(prep[0], 0)])
    for b in range(n_blocks):
        _interleave([chains(prep[b], chain[b]),
                     prepare(prep[b + 1], b + 1) if b + 1 < n_blocks else None,
                     tail(prep[b - 1], chain[b - 1], b - 1) if b > 0 else None])
    _interleave([tail(prep[-1], chain[-1], n_blocks - 1)])
    for j in range(groups):
        s_scr[j] = state[j]


def _rwkv(rw, w0, wdw, a0, waw, wgw, k_k, k_a, r_k, gn_g, gn_b, width, rows=512, block_rows=256):
    B, T, cols = rw.shape
    row = lambda a: a.reshape(1, width)
    vec = _const_spec((1, width))
    return pl.pallas_call(
        functools.partial(_rwkv_kernel, width=width, block_rows=block_rows),
        out_shape=jax.ShapeDtypeStruct((B, T, width), BF16),
        grid=(B, T // rows),
        in_specs=[pl.BlockSpec((1, rows, cols), lambda b, t: (b, t, 0)),
                  vec, _const_spec(wdw.shape), vec, _const_spec(waw.shape),
                  _const_spec(wgw.shape), vec, vec, vec, vec, vec],
        out_specs=pl.BlockSpec((1, rows, width), lambda b, t: (b, t, 0)),
        scratch_shapes=[pltpu.VMEM((width // LANES, LANES, LANES), F32)],
        compiler_params=pltpu.CompilerParams(
            dimension_semantics=("arbitrary", "arbitrary"), vmem_limit_bytes=VMEM_LIMIT),
        name="rwkv",
    )(rw, row(w0), wdw, row(a0), waw, wgw, row(k_k), row(k_a), row(r_k), row(gn_g), row(gn_b))


def _fox_kernel(qtab, ktab, qa_ref, ka_ref, vt_ref, og_ref, g_ref, o_ref,
                m_all, acc_all, s_scr, p_scr, al_scr, *, blk, cb, n_off):
    n_q = qa_ref.shape[2] // blk
    n_el = n_off + n_q
    chains = [(hh, c) for hh in range(2) for c in range(blk // cb)]
    csl = lambda c: slice(c * cb, (c + 1) * cb)

    m_all[...] = jnp.full_like(m_all, NEG_INF)
    acc_all[...] = jnp.zeros_like(acc_all)

    def element(t):
        if isinstance(t, int) and t >= n_off:
            return t - n_off, t - n_off, True
        return qtab[t], ktab[t], False

    def start(i, size):
        return i * size if isinstance(i, int) else pl.multiple_of(i * size, size)

    def n_keys(c, diag):
        return (c + 1) * cb if diag else blk

    def logits(t, par):
        qi, ki, diag = element(t)
        for hh, c in chains:
            nk = n_keys(c, diag)
            s_scr[par, hh, :nk, csl(c)] = _dot_nt(
                ka_ref[0, hh, pl.ds(start(ki, blk), nk), :],
                qa_ref[0, hh, pl.ds(start(qi, blk) + c * cb, cb), :])
            yield

    def softmax(t, par):
        qi, _, diag = element(t)
        for hh, c in chains:
            nk = n_keys(c, diag)
            s = s_scr[par, hh, :nk, csl(c)]
            if diag:
                key = lax.broadcasted_iota(jnp.int32, (nk, cb), 0)
                qry = c * cb + lax.broadcasted_iota(jnp.int32, (nk, cb), 1)
                s = jnp.where(key <= qry, s, NEG_INF)
            m_prev = m_all[qi, hh, :, csl(c)]
            m_new = jnp.maximum(m_prev, jnp.max(s, axis=0, keepdims=True))
            m_all[qi, hh, :, csl(c)] = m_new
            al_scr[par, hh, :, csl(c)] = jnp.exp2(m_prev - m_new)
            p_scr[par, hh, :nk, csl(c)] = jnp.exp2(s - m_new).astype(BF16)
            yield

    def values(t, par):
        qi, ki, diag = element(t)
        for hh, c in chains:
            nk = n_keys(c, diag)
            pv = _dot(vt_ref[0, hh, ki, :, :nk], p_scr[par, hh, :nk, csl(c)])
            acc_all[qi, hh, :, csl(c)] = (al_scr[par, hh, :, csl(c)] * acc_all[qi, hh, :, csl(c)]
                                          + pv)
            yield
        if diag:
            normed = []
            for hh in range(2):
                acc = acc_all[qi, hh]
                o = acc[:HEAD_DIM] / acc[HEAD_DIM:HEAD_DIM + 1]
                ms = jnp.mean(o * o, axis=0, keepdims=True)
                normed.append(o * lax.rsqrt(ms + NORM_EPS))
            on = jnp.concatenate(normed, axis=0).T
            rows = slice(qi * blk, (qi + 1) * blk)
            o_ref[0, rows, :] = (on * g_ref[...] * _sigmoid(og_ref[0, rows, :].astype(F32))).astype(o_ref.dtype)

    def step(t, par):
        _interleave([stage(t - lag, (par + lag) % 2)
                     for lag, stage in enumerate((logits, softmax, values))
                     if not isinstance(t, int) or 0 <= t - lag < n_el])

    def two_steps(u, carry):
        step(2 * u + 2, 0)
        step(2 * u + 3, 1)
        return carry

    step(0, 0)
    step(1, 1)
    lax.fori_loop(0, (n_off - 2) // 2, two_steps, 0)
    for t in range(n_off, n_el + 2):
        step(t, t % 2)


def _fox(qa, ka, vt, og, fox_out_g, width, cb=256):
    B, H, T, _ = qa.shape
    blk = vt.shape[4]
    n_q = T // blk
    pairs = [(qi, ki) for qi in range(n_q) for ki in range(qi)]
    n_off = len(pairs)
    assert n_off >= 2 and n_off % 2 == 0, "pipeline loop is unrolled by two"
    qtab = jnp.asarray(np.array([p[0] for p in pairs], np.int32))
    ktab = jnp.asarray(np.array([p[1] for p in pairs], np.int32))
    grid_spec = pltpu.PrefetchScalarGridSpec(
        num_scalar_prefetch=2,
        grid=(B, H // 2),
        in_specs=[pl.BlockSpec((1, 2, T, LANES), lambda b, h, qt, kt: (b, h, 0, 0)),
                  pl.BlockSpec((1, 2, T, LANES), lambda b, h, qt, kt: (b, h, 0, 0)),
                  pl.BlockSpec((1, 2, n_q, V_ROWS, blk), lambda b, h, qt, kt: (b, h, 0, 0, 0)),
                  pl.BlockSpec((1, T, LANES), lambda b, h, qt, kt: (b, 0, h)),
                  pl.BlockSpec((1, LANES), lambda b, h, qt, kt: (0, h))],
        out_specs=pl.BlockSpec((1, T, LANES), lambda b, h, qt, kt: (b, 0, h)),
        scratch_shapes=[pltpu.VMEM((n_q, 2, 1, blk), F32), pltpu.VMEM((n_q, 2, V_ROWS, blk), F32),
                        pltpu.VMEM((2, 2, blk, blk), F32), pltpu.VMEM((2, 2, blk, blk), BF16),
                        pltpu.VMEM((2, 2, 1, blk), F32)])
    return pl.pallas_call(
        functools.partial(_fox_kernel, blk=blk, cb=cb, n_off=n_off),
        out_shape=jax.ShapeDtypeStruct((B, T, width), BF16),
        grid_spec=grid_spec,
        compiler_params=pltpu.CompilerParams(
            dimension_semantics=("arbitrary",) * 2, vmem_limit_bytes=VMEM_LIMIT),
        name="fox",
    )(qtab, ktab, qa, ka, vt, og, fox_out_g.reshape(1, width))


def _outmlp_kernel(x_ref, yrw_ref, yfx_ref, mod_ref, wo_ref, g2_ref, w1_ref, w2_ref, gf_ref,
                   o_ref, *, ff_tile):
    x = x_ref[0]
    half = yrw_ref.shape[2]
    gate1 = mod_ref[0, 2:3, :]
    shift2 = mod_ref[0, 3:4, :]
    scale2 = mod_ref[0, 4:5, :]
    gate2 = mod_ref[0, 5:6, :]
    y = _dot(yrw_ref[0], wo_ref[:half, :]) + _dot(yfx_ref[0], wo_ref[half:, :])
    h1 = x + gate1 * y
    ms = jnp.mean(h1 * h1, axis=-1, keepdims=True)
    u = ((h1 * lax.rsqrt(ms + NORM_EPS) * g2_ref[...]) * (1.0 + scale2) + shift2).astype(BF16)
    acc = jnp.zeros_like(x)
    for j in range(w1_ref.shape[1] // ff_tile):
        hid = jnp.maximum(_dot(u, w1_ref[:, j * ff_tile:(j + 1) * ff_tile]), 0.0)
        acc = acc + _dot((hid * hid).astype(BF16), w2_ref[j * ff_tile:(j + 1) * ff_tile, :])
    h2 = h1 + gate2 * acc
    ms2 = jnp.mean(h2 * h2, axis=-1, keepdims=True)
    o_ref[0] = h2 * lax.rsqrt(ms2 + NORM_EPS) * gf_ref[...]


def _outmlp(x, y_rw, y_fx, mod3, w_o, norm2_g, w1, w2, final_g, tm=512, ff_tile=1024):
    B, T, D = x.shape
    half = y_rw.shape[2]
    return pl.pallas_call(
        functools.partial(_outmlp_kernel, ff_tile=ff_tile),
        out_shape=jax.ShapeDtypeStruct((B, T, D), F32),
        grid=(B, T // tm),
        in_specs=[pl.BlockSpec((1, tm, D), lambda b, t: (b, t, 0)),
                  pl.BlockSpec((1, tm, half), lambda b, t: (b, t, 0)),
                  pl.BlockSpec((1, tm, half), lambda b, t: (b, t, 0)),
                  pl.BlockSpec((1, N_MOD, D), lambda b, t: (b, 0, 0)),
                  _const_spec(w_o.shape, single_buffer=True), _const_spec((1, D)),
                  _const_spec(w1.shape, single_buffer=True),
                  _const_spec(w2.shape, single_buffer=True), _const_spec((1, D))],
        out_specs=pl.BlockSpec((1, tm, D), lambda b, t: (b, t, 0)),
        compiler_params=pltpu.CompilerParams(
            dimension_semantics=("arbitrary", "arbitrary"), vmem_limit_bytes=VMEM_LIMIT),
        name="outmlp",
    )(x, y_rw, y_fx, mod3, w_o, norm2_g.reshape(1, D), w1, w2, final_g.reshape(1, D))


def _pad_cols(a, n):
    return jnp.pad(a, ((0, 0), (0, n - a.shape[1])))


def _pad_rows(a, n):
    return jnp.pad(a, ((0, n - a.shape[0]), (0, 0)))


def _branches(x, c, w_ada, b_ada, norm1_g, w_in, mu_shift, w0, w_up_decay, a0, w_up_a, w_up_g,
              k_k, k_a, r_k, gn_g, gn_b, b_f, q_norm_g, k_norm_g, fox_out_g):
    B, T, D = x.shape
    W = w0.shape[0]
    heads = b_f.shape[0]
    rw_n = 3 * W + DECAY_LORA + AAA_LORA + GATE_LORA

    def regroup(a):
        rw, fx = a[:, :rw_n], a[:, rw_n:]
        o = 3 * W
        return jnp.concatenate([
            rw[:, :o],
            _pad_cols(rw[:, o:o + DECAY_LORA], LANES),
            _pad_cols(rw[:, o + DECAY_LORA:o + DECAY_LORA + AAA_LORA], LANES),
            _pad_cols(rw[:, o + DECAY_LORA + AAA_LORA:], 2 * LANES),
            fx[:, :4 * W],
            _pad_cols(fx[:, 4 * W:], LANES)], axis=1)

    rw_cols = 3 * W + 4 * LANES
    w_all = regroup(w_in).astype(BF16)
    mu_p = regroup(jnp.pad(mu_shift.reshape(1, rw_n), ((0, 0), (0, w_in.shape[1] - rw_n))))[:, :rw_cols]
    bf_p = _pad_cols(b_f.reshape(1, heads), LANES)
    wdw = _pad_rows(w_up_decay, LANES).astype(BF16)
    waw = _pad_rows(w_up_a, LANES).astype(BF16)
    wgw = _pad_rows(w_up_g, 2 * LANES).astype(BF16)

    mod3 = _ada(c, w_ada, b_ada).reshape(B, N_MOD, D)
    rw, qa, ka, vt, og = _inproj(x, mod3, norm1_g, w_all, mu_p, bf_p, q_norm_g, k_norm_g, rw_cols)
    y_rw = _rwkv(rw, w0, wdw, a0, waw, wgw, k_k, k_a, r_k.reshape(-1), gn_g, gn_b, W)
    y_fx = _fox(qa, ka, vt, og, fox_out_g, W)
    return y_rw, y_fx, mod3


def kernel(x, c, w_ada, b_ada, norm1_g, w_in, mu_shift, w0, w_up_decay, a0, w_up_a, w_up_g,
           k_k, k_a, r_k, gn_g, gn_b, b_f, q_norm_g, k_norm_g, fox_out_g, w_o, norm2_g,
           w_mlp1, w_mlp2, final_g):
    y_rw, y_fx, mod3 = _branches(x, c, w_ada, b_ada, norm1_g, w_in, mu_shift, w0, w_up_decay,
                                 a0, w_up_a, w_up_g, k_k, k_a, r_k, gn_g, gn_b, b_f,
                                 q_norm_g, k_norm_g, fox_out_g)
    return _outmlp(x, y_rw, y_fx, mod3, w_o.astype(BF16), norm2_g,
                   w_mlp1.astype(BF16), w_mlp2.astype(BF16), final_g)
```

```python
import functools
import math

import jax
import jax.numpy as jnp
import numpy as np
from jax import lax
from jax.experimental import pallas as pl
from jax.experimental.pallas import tpu as pltpu

F32 = jnp.float32
BF16 = jnp.bfloat16

HEAD_DIM = 64
LANES = 128
MXU_TILE = 256
NORM_EPS = 1e-6
GN_EPS = 64e-5
NEG_INF = -1e30
LOG2E = 1.4426950408889634
N_MOD = 6
DECAY_LORA = 64
AAA_LORA = 64
GATE_LORA = 160
RW_CHUNK = 64
V_ROWS = HEAD_DIM + 16
VMEM_LIMIT = 56 * 1024 * 1024

_HI = lax.Precision.HIGHEST


def _dot(a, b):
    return jnp.dot(a, b, preferred_element_type=F32)


def _dot_hi(a, b):
    return jnp.dot(a, b, preferred_element_type=F32, precision=_HI)


def _dot_nt(a, b):
    return lax.dot_general(a, b, (((1,), (1,)), ((), ())), preferred_element_type=F32)


def _dot_tn(a, b):
    return lax.dot_general(a, b, (((0,), (0,)), ((), ())), preferred_element_type=F32)


def _bf(x):
    return x.astype(BF16)


def _sigmoid(x):
    return 1.0 / (1.0 + jnp.exp(-x))


def _const_spec(shape, single_buffer=False):
    n = len(shape)
    mode = pl.Buffered(1) if single_buffer else None
    return pl.BlockSpec(shape, lambda *_: (0,) * n, pipeline_mode=mode)


def _ada_kernel(c_ref, w_ref, b_ref, o_ref):
    c = c_ref[...]
    o_ref[...] = _dot_hi(c * _sigmoid(c), w_ref[...]) + b_ref[...]


def _ada(c, w_ada, b_ada):
    B, D = c.shape
    n = w_ada.shape[1]
    return pl.pallas_call(
        _ada_kernel,
        out_shape=jax.ShapeDtypeStruct((B, n), F32),
        grid=(n // D,),
        in_specs=[pl.BlockSpec((B, D), lambda j: (0, 0)),
                  pl.BlockSpec((D, D), lambda j: (0, j)),
                  pl.BlockSpec((1, D), lambda j: (0, j))],
        out_specs=pl.BlockSpec((B, D), lambda j: (0, j)),
        name="ada",
    )(c, w_ada, b_ada.reshape(1, n))


N_PIECES = 3


def _fox_aug_tables(heads):
    groups = heads // 2
    sel = np.zeros((LANES, 2 * groups * LANES), np.float32)
    bias = np.zeros((1, 2 * groups * LANES), np.float32)
    for h in range(heads):
        base = (h // 2) * LANES + (HEAD_DIM if h % 2 == 0 else 0)
        kbase = groups * LANES + base
        for piece in range(N_PIECES):
            sel[piece * heads + h, base + piece] = 1.0
            sel[piece * heads + h, kbase + N_PIECES + piece] = -1.0
            bias[0, base + N_PIECES + piece] = 1.0
            bias[0, kbase + piece] = 1.0
    return jnp.asarray(sel, BF16), jnp.asarray(bias)


def _fox_layout(pf, fcum, qg_ref, kg_ref, sel_ref, bias_ref, qa_ref, ka_ref, v_ref, og_ref, width):
    W = width
    heads = W // HEAD_DIM
    groups = W // LANES
    tm = pf.shape[0]
    lane = lax.broadcasted_iota(jnp.int32, (tm, LANES), 1)
    fcum = fcum * LOG2E
    f_hi = fcum.astype(BF16).astype(F32)
    r1 = fcum - f_hi
    f_mid = r1.astype(BF16).astype(F32)
    f_lo = r1 - f_mid
    packed = jnp.where(lane < heads, f_hi, jnp.where(
        lane < 2 * heads, pltpu.roll(f_mid, heads, axis=1), jnp.where(
            lane < 3 * heads, pltpu.roll(f_lo, 2 * heads, axis=1), 0.0)))
    aug = _dot(packed.astype(BF16), sel_ref[...]) + bias_ref[...]
    first = lane < HEAD_DIM
    for side, (gain_ref, out_ref) in enumerate(((qg_ref, qa_ref), (kg_ref, ka_ref))):
        for g in range(groups):
            x = pf[:, side * W + g * LANES:side * W + (g + 1) * LANES]
            sq = x * x
            lo = jnp.sum(jnp.where(first, sq, 0.0), axis=-1, keepdims=True)
            hi = jnp.sum(jnp.where(first, 0.0, sq), axis=-1, keepdims=True)
            ms = jnp.where(first, lo, hi) * (1.0 / HEAD_DIM)
            xn = x * lax.rsqrt(ms + NORM_EPS) * gain_ref[...]
            a = aug[:, (side * groups + g) * LANES:(side * groups + g + 1) * LANES]
            out_ref[0, 2 * g] = jnp.where(first, xn, a).astype(BF16)
            out_ref[0, 2 * g + 1] = jnp.where(first, a, xn).astype(BF16)
    pad = (lax.broadcasted_iota(jnp.int32, (V_ROWS - HEAD_DIM, tm), 0) == 0).astype(BF16)
    for g in range(groups):
        vt = pf[:, 2 * W + g * LANES:2 * W + (g + 1) * LANES].T.astype(BF16)
        for hh in range(2):
            v_ref[0, 2 * g + hh, 0, :HEAD_DIM, :] = vt[hh * HEAD_DIM:(hh + 1) * HEAD_DIM]
            v_ref[0, 2 * g + hh, 0, HEAD_DIM:, :] = pad
    og_ref[0] = pf[:, 3 * W:4 * W].astype(BF16)


def _inproj_kernel(x_ref, mod_ref, g_ref, w_ref, mu_ref, bf_ref, qg_ref, kg_ref, sel_ref, bias_ref,
                   rw_ref, qa_ref, ka_ref, v_ref, og_ref, prev_scr, fcar_scr, *, rw_cols):
    t = pl.program_id(1)
    tm = x_ref.shape[1]

    @pl.when(t == 0)
    def _():
        prev_scr[...] = jnp.zeros_like(prev_scr)
        fcar_scr[...] = jnp.zeros_like(fcar_scr)

    x = x_ref[0]
    shift = mod_ref[0, 0:1, :]
    scale = mod_ref[0, 1:2, :]
    ms = jnp.mean(x * x, axis=-1, keepdims=True)
    u = (x * lax.rsqrt(ms + NORM_EPS) * g_ref[...]) * (1.0 + scale) + shift
    u = u.astype(BF16)

    p = _dot(u, w_ref[:, :rw_cols])
    row = lax.broadcasted_iota(jnp.int32, p.shape, 0)
    prev = jnp.where(row == 0, prev_scr[0:1, :], pltpu.roll(p, shift=1, axis=0))
    prev_scr[0:1, :] = p[tm - 1:tm, :]
    rw_ref[0] = p + (prev - p) * mu_ref[...]

    pf = _dot(u, w_ref[:, rw_cols:])
    z = pf[:, pf.shape[1] - LANES:] + bf_ref[...]
    logf = jnp.minimum(z, 0.0) - jnp.log(1.0 + jnp.exp(-jnp.abs(z)))
    sub = min(tm, MXU_TILE)
    r2 = lax.broadcasted_iota(jnp.int32, (sub, sub), 0)
    c2 = lax.broadcasted_iota(jnp.int32, (sub, sub), 1)
    tril = (c2 <= r2).astype(BF16)
    f_hi = logf.astype(BF16)
    r1 = logf - f_hi.astype(F32)
    f_mid = r1.astype(BF16)
    f_lo = (r1 - f_mid.astype(F32)).astype(BF16)
    carry = fcar_scr[0:1, :]
    fcums = []
    for i in range(tm // sub):
        rs = slice(i * sub, (i + 1) * sub)
        fcums.append((_dot(tril, f_hi[rs]) + _dot(tril, f_mid[rs])) + _dot(tril, f_lo[rs]) + carry)
        carry = fcums[-1][sub - 1:sub, :]
    fcar_scr[0:1, :] = carry
    _fox_layout(pf, jnp.concatenate(fcums, axis=0), qg_ref, kg_ref, sel_ref, bias_ref,
                qa_ref, ka_ref, v_ref, og_ref, (pf.shape[1] - LANES) // 4)


def _inproj(x, mod3, norm1_g, w_all, mu_p, bf_p, q_norm_g, k_norm_g, rw_cols, tm=512):
    B, T, D = x.shape
    ncols = w_all.shape[1]
    width = (ncols - rw_cols - LANES) // 4
    heads = width // HEAD_DIM
    sel, bias = _fox_aug_tables(heads)
    scale = LOG2E / math.sqrt(HEAD_DIM)
    qg2 = jnp.tile(q_norm_g * scale, 2).reshape(1, LANES)
    kg2 = jnp.tile(k_norm_g, 2).reshape(1, LANES)
    return pl.pallas_call(
        functools.partial(_inproj_kernel, rw_cols=rw_cols),
        out_shape=(jax.ShapeDtypeStruct((B, T, rw_cols), F32),
                   jax.ShapeDtypeStruct((B, heads, T, LANES), BF16),
                   jax.ShapeDtypeStruct((B, heads, T, LANES), BF16),
                   jax.ShapeDtypeStruct((B, heads, T // tm, V_ROWS, tm), BF16),
                   jax.ShapeDtypeStruct((B, T, width), BF16)),
        grid=(B, T // tm),
        in_specs=[pl.BlockSpec((1, tm, D), lambda b, t: (b, t, 0)),
                  pl.BlockSpec((1, N_MOD, D), lambda b, t: (b, 0, 0)),
                  _const_spec((1, D)),
                  _const_spec((D, ncols), single_buffer=True),
                  _const_spec((1, rw_cols)),
                  _const_spec((1, LANES)),
                  _const_spec((1, LANES)), _const_spec((1, LANES)),
                  _const_spec(sel.shape, single_buffer=True), _const_spec(bias.shape)],
        out_specs=(pl.BlockSpec((1, tm, rw_cols), lambda b, t: (b, t, 0)),
                   pl.BlockSpec((1, heads, tm, LANES), lambda b, t: (b, 0, t, 0)),
                   pl.BlockSpec((1, heads, tm, LANES), lambda b, t: (b, 0, t, 0)),
                   pl.BlockSpec((1, heads, 1, V_ROWS, tm), lambda b, t: (b, 0, t, 0, 0)),
                   pl.BlockSpec((1, tm, width), lambda b, t: (b, t, 0))),
        scratch_shapes=[pltpu.VMEM((8, rw_cols), F32), pltpu.VMEM((8, LANES), F32)],
        compiler_params=pltpu.CompilerParams(
            dimension_semantics=("arbitrary", "arbitrary"), vmem_limit_bytes=VMEM_LIMIT),
        name="inproj",
    )(x, mod3, norm1_g.reshape(1, D), w_all, mu_p, bf_p, qg2, kg2, sel, bias)


def _segsum(x, bd, split=False):
    tile = bd.shape[0]
    hi = x.astype(BF16)
    lo = (x - hi.astype(F32)).astype(BF16) if split else None
    cols = []
    for j in range(x.shape[1] // tile):
        sl = slice(j * tile, (j + 1) * tile)
        s = _dot(hi[:, sl], bd)
        cols.append(s + _dot(lo[:, sl], bd) if split else s)
    return jnp.concatenate(cols, axis=1)


def _interleave(generators):
    live = [g for g in generators if g is not None]
    while live:
        for g in list(live):
            if next(g, StopIteration) is StopIteration:
                live.remove(g)


def _rwkv_kernel(p_ref, w0_ref, wdw_ref, a0_ref, waw_ref, wgw_ref, kk_ref, ka_ref, rk_ref,
                 gng_ref, gnb_ref, o_ref, s_scr, *, width, block_rows):
    t = pl.program_id(1)
    L = RW_CHUNK
    R = block_rows
    n_blocks = p_ref.shape[1] // R
    n_chunks = R // L
    W = width
    G2 = 2 * LANES
    groups = W // LANES
    items = [(c, j) for c in range(n_chunks) for j in range(groups)]
    rsl = lambda c: slice(c * L, (c + 1) * L)
    gsl = lambda j: slice(j * LANES, (j + 1) * LANES)

    @pl.when(t == 0)
    def _():
        s_scr[...] = jnp.zeros_like(s_scr)

    ri = lax.broadcasted_iota(jnp.int32, (MXU_TILE, MXU_TILE), 0)
    ci = lax.broadcasted_iota(jnp.int32, (MXU_TILE, MXU_TILE), 1)
    bd = ((ri // HEAD_DIM) == (ci // HEAD_DIM)).astype(BF16)
    tril = (lax.broadcasted_iota(jnp.int32, (L, L), 1)
            <= lax.broadcasted_iota(jnp.int32, (L, L), 0)).astype(BF16)
    o0 = 3 * W
    lane1 = lax.broadcasted_iota(jnp.int32, (L, LANES), 1)
    lane2 = lax.broadcasted_iota(jnp.int32, (L, G2), 1)
    row2 = lax.broadcasted_iota(jnp.int32, (L, G2), 0)
    first1 = lane1 < HEAD_DIM
    first2 = (lane2 % LANES) < HEAD_DIM
    strict2 = (lane2 % HEAD_DIM) < row2
    incl2 = (lane2 % HEAD_DIM) <= row2
    upper2 = lane2 >= LANES
    eye2 = ((lane1 % HEAD_DIM) == lax.broadcasted_iota(jnp.int32, (L, LANES), 0)).astype(F32)
    zero1 = jnp.zeros((L, LANES), BF16)
    rr = lax.broadcasted_iota(jnp.int32, (LANES, LANES), 0) < HEAD_DIM
    cc = lax.broadcasted_iota(jnp.int32, (LANES, LANES), 1) < HEAD_DIM
    diag_blocks = rr == cc

    def split(m, mask):
        return jnp.concatenate([jnp.where(mask, m, 0), jnp.where(mask, 0, m)], axis=0)

    def prepare(pb, b):
        p = p_ref[0, b * R:(b + 1) * R, :]
        r, k, v = p[:, 0:W], p[:, W:2 * W], p[:, 2 * W:3 * W]
        wd = p[:, o0:o0 + LANES]
        ad = p[:, o0 + LANES:o0 + 2 * LANES]
        gd = p[:, o0 + 2 * LANES:o0 + 4 * LANES]
        z = w0_ref[...] + _dot(_bf(jnp.tanh(wd)), wdw_ref[...])
        softplus = jnp.maximum(-z, 0.0) + jnp.log(1.0 + jnp.exp(-jnp.abs(z)))
        ld = -jnp.exp(-softplus - 0.5)
        yield
        a_sig = _sigmoid(a0_ref[...] + _dot(_bf(ad), waw_ref[...]))
        pb["gate"] = _dot(_bf(_sigmoid(gd)), wgw_ref[...])
        yield
        kk = k * kk_ref[...]
        kk = kk / jnp.maximum(jnp.sqrt(_segsum(kk * kk, bd, split=True)), 1e-12)
        yield
        k2 = k * (1.0 + (a_sig - 1.0) * ka_ref[...])
        b_ = kk * a_sig
        pb["bonus"] = _segsum(r * k2 * rk_ref[...], bd) * v
        yield
        ld_hi = _bf(ld)
        ld_lo = _bf(ld - ld_hi.astype(F32))
        cums = [_dot(tril, ld_hi[rsl(c)]) + _dot(tril, ld_lo[rsl(c)]) for c in range(n_chunks)]
        pb["tots"] = [cum_c[L - 1:L, :] for cum_c in cums]
        cum = jnp.concatenate(cums, axis=0)
        tot_b = jnp.concatenate([jnp.broadcast_to(tc, (L, W)) for tc in pb["tots"]], axis=0)
        yield
        e_neg = jnp.exp(-cum)
        rt = r * jnp.exp(cum)
        pb.update(rt=rt, rt_m=_bf(rt), at=_bf(-kk * jnp.exp(cum - ld)), bt=_bf(b_ * e_neg),
                  kt=_bf(k2 * e_neg))
        yield
        e_rem = jnp.exp(tot_b - cum)
        pb.update(bh=_bf(b_ * e_rem), kh=_bf(k2 * e_rem), vm=_bf(v))

    def chains(pb, cb):
        at, rt_m, bt, kt, vm = pb["at"], pb["rt_m"], pb["bt"], pb["kt"], pb["vm"]
        g_bot, akm, zz = {}, {}, {}
        for c, j in items:
            ar = jnp.concatenate([at[rsl(c), gsl(j)], rt_m[rsl(c), gsl(j)]], axis=0)
            rhs = jnp.concatenate([split(bt[rsl(c), gsl(j)], first1),
                                   split(kt[rsl(c), gsl(j)], first1)], axis=0)
            gm = _dot_nt(ar, rhs)
            top = jnp.where(strict2, gm[:L], 0.0)
            g_bot[c, j] = _bf(jnp.where(incl2, gm[L:], 0.0))
            akm[c, j] = _bf(top[:, LANES:])
            zz[c, j] = jnp.concatenate([top[:, :LANES], eye2], axis=1)
        yield
        for _ in range(int(math.log2(L))):
            for it in items:
                zb = _bf(zz[it])
                out = _dot(zb[:, :LANES], split(zb, first2))
                zz[it] = out + jnp.where(upper2, zz[it], 0.0)
            yield
        vbd = {(c, j): split(vm[rsl(c), gsl(j)], first1) for c, j in items}
        akv = {it: _bf(_dot(akm[it], vbd[it])) for it in items}
        yield
        x = {}
        for c, j in items:
            zed = jnp.concatenate([at[rsl(c), gsl(j)], akv[c, j]], axis=1)
            x[c, j] = _bf(_dot(_bf(zz[c, j][:, LANES:]), split(zed, first2)))
        yield
        cb.update(r_eff={}, y0={}, mt={}, nt={})
        for c, j in items:
            w4 = jnp.concatenate([split(x[c, j], first2),
                                  jnp.concatenate([jnp.concatenate([zero1, zero1], axis=0),
                                                   vbd[c, j]], axis=1)], axis=0)
            out = _dot(g_bot[c, j], w4)
            cb["r_eff"][c, j] = _bf(pb["rt"][rsl(c), gsl(j)] + out[:, :LANES])
            cb["y0"][c, j] = out[:, LANES:]
        yield
        for c, j in items:
            lhs = jnp.concatenate([pb["bh"][rsl(c), gsl(j)], pb["kh"][rsl(c), gsl(j)]], axis=0)
            rhs = jnp.concatenate([x[c, j], jnp.concatenate([zero1, vm[rsl(c), gsl(j)]], axis=1)],
                                  axis=0)
            pt = _dot_tn(lhs, rhs)
            cb["mt"][c, j] = _bf(jnp.where(diag_blocks, pt[:, :LANES], 0.0))
            cb["nt"][c, j] = jnp.where(diag_blocks, pt[:, LANES:], 0.0)
        yield
        cb["e_cols"] = [
            jnp.concatenate([jnp.exp(pb["tots"][c][:, gsl(j)]) for c in range(n_chunks)]
                            + [jnp.zeros((LANES - n_chunks, LANES), F32)], axis=0).T
            for j in range(groups)]

    state = [s_scr[j] for j in range(groups)]

    def tail(pb, cb, b):
        y_rows = []
        for c in range(n_chunks):
            hb = [_bf(s) for s in state]
            y_rows.append(jnp.concatenate(
                [_dot(cb["r_eff"][c, j], hb[j]) + cb["y0"][c, j] for j in range(groups)], axis=1))
            for j in range(groups):
                state[j] = (state[j] * cb["e_cols"][j][:, c:c + 1] + _dot(cb["mt"][c, j], hb[j])
                            + cb["nt"][c, j])
            yield
        y = jnp.concatenate(y_rows, axis=0)
        inv_n = 1.0 / HEAD_DIM
        mean = _segsum(y, bd) * inv_n
        d = y - mean
        yield
        var = _segsum(d * d, bd) * inv_n
        yn = d * lax.rsqrt(var + GN_EPS) * gng_ref[...] + gnb_ref[...]
        o_ref[0, b * R:(b + 1) * R, :] = ((yn + pb["bonus"]) * pb["gate"]).astype(o_ref.dtype)

    prep = [dict() for _ in range(n_blocks)]
    chain = [dict() for _ in range(n_blocks)]
    _interleave([prepare(prep[0], 0)])
    for b in range(n_blocks):
        _interleave([chains(prep[b], chain[b]),
                     prepare(prep[b + 1], b + 1) if b + 1 < n_blocks else None,
                     tail(prep[b - 1], chain[b - 1], b - 1) if b > 0 else None])
    _interleave([tail(prep[-1], chain[-1], n_blocks - 1)])
    for j in range(groups):
        s_scr[j] = state[j]


def _rwkv(rw, w0, wdw, a0, waw, wgw, k_k, k_a, r_k, gn_g, gn_b, width, rows=512, block_rows=256):
    B, T, cols = rw.shape
    row = lambda a: a.reshape(1, width)
    vec = _const_spec((1, width))
    return pl.pallas_call(
        functools.partial(_rwkv_kernel, width=width, block_rows=block_rows),
        out_shape=jax.ShapeDtypeStruct((B, T, width), BF16),
        grid=(B, T // rows),
        in_specs=[pl.BlockSpec((1, rows, cols), lambda b, t: (b, t, 0)),
                  vec, _const_spec(wdw.shape), vec, _const_spec(waw.shape),
                  _const_spec(wgw.shape), vec, vec, vec, vec, vec],
        out_specs=pl.BlockSpec((1, rows, width), lambda b, t: (b, t, 0)),
        scratch_shapes=[pltpu.VMEM((width // LANES, LANES, LANES), F32)],
        compiler_params=pltpu.CompilerParams(
            dimension_semantics=("arbitrary", "arbitrary"), vmem_limit_bytes=VMEM_LIMIT),
        name="rwkv",
    )(rw, row(w0), wdw, row(a0), waw, wgw, row(k_k), row(k_a), row(r_k), row(gn_g), row(gn_b))


def _fox_kernel(qtab, ktab, qa_ref, ka_ref, vt_ref, og_ref, g_ref, o_ref,
                m_all, acc_all, s_scr, p_scr, al_scr, *, blk, cb, n_off):
    n_q = qa_ref.shape[2] // blk
    n_el = n_off + n_q
    chains = [(hh, c) for hh in range(2) for c in range(blk // cb)]
    csl = lambda c: slice(c * cb, (c + 1) * cb)

    m_all[...] = jnp.full_like(m_all, NEG_INF)
    acc_all[...] = jnp.zeros_like(acc_all)

    def element(t):
        if isinstance(t, int) and t >= n_off:
            return t - n_off, t - n_off, True
        return qtab[t], ktab[t], False

    def start(i, size):
        return i * size if isinstance(i, int) else pl.multiple_of(i * size, size)

    def n_keys(c, diag):
        return (c + 1) * cb if diag else blk

    def logits(t, par):
        qi, ki, diag = element(t)
        for hh, c in chains:
            nk = n_keys(c, diag)
            s_scr[par, hh, :nk, csl(c)] = _dot_nt(
                ka_ref[0, hh, pl.ds(start(ki, blk), nk), :],
                qa_ref[0, hh, pl.ds(start(qi, blk) + c * cb, cb), :])
            yield

    def softmax(t, par):
        qi, _, diag = element(t)
        for hh, c in chains:
            nk = n_keys(c, diag)
            s = s_scr[par, hh, :nk, csl(c)]
            if diag:
                key = lax.broadcasted_iota(jnp.int32, (nk, cb), 0)
                qry = c * cb + lax.broadcasted_iota(jnp.int32, (nk, cb), 1)
                s = jnp.where(key <= qry, s, NEG_INF)
            m_prev = m_all[qi, hh, :, csl(c)]
            m_new = jnp.maximum(m_prev, jnp.max(s, axis=0, keepdims=True))
            m_all[qi, hh, :, csl(c)] = m_new
            al_scr[par, hh, :, csl(c)] = jnp.exp2(m_prev - m_new)
            p_scr[par, hh, :nk, csl(c)] = jnp.exp2(s - m_new).astype(BF16)
            yield

    def values(t, par):
        qi, ki, diag = element(t)
        for hh, c in chains:
            nk = n_keys(c, diag)
            pv = _dot(vt_ref[0, hh, ki, :, :nk], p_scr[par, hh, :nk, csl(c)])
            acc_all[qi, hh, :, csl(c)] = (al_scr[par, hh, :, csl(c)] * acc_all[qi, hh, :, csl(c)]
                                          + pv)
            yield
        if diag:
            normed = []
            for hh in range(2):
                acc = acc_all[qi, hh]
                o = acc[:HEAD_DIM] / acc[HEAD_DIM:HEAD_DIM + 1]
                ms = jnp.mean(o * o, axis=0, keepdims=True)
                normed.append(o * lax.rsqrt(ms + NORM_EPS))
            on = jnp.concatenate(normed, axis=0).T
            rows = slice(qi * blk, (qi + 1) * blk)
            o_ref[0, rows, :] = (on * g_ref[...] * _sigmoid(og_ref[0, rows, :].astype(F32))).astype(o_ref.dtype)

    def step(t, par):
        _interleave([stage(t - lag, (par + lag) % 2)
                     for lag, stage in enumerate((logits, softmax, values))
                     if not isinstance(t, int) or 0 <= t - lag < n_el])

    def two_steps(u, carry):
        step(2 * u + 2, 0)
        step(2 * u + 3, 1)
        return carry

    step(0, 0)
    step(1, 1)
    lax.fori_loop(0, (n_off - 2) // 2, two_steps, 0)
    for t in range(n_off, n_el + 2):
        step(t, t % 2)


def _fox(qa, ka, vt, og, fox_out_g, width, cb=256):
    B, H, T, _ = qa.shape
    blk = vt.shape[4]
    n_q = T // blk
    pairs = [(qi, ki) for qi in range(n_q) for ki in range(qi)]
    n_off = len(pairs)
    assert n_off >= 2 and n_off % 2 == 0, "pipeline loop is unrolled by two"
    qtab = jnp.asarray(np.array([p[0] for p in pairs], np.int32))
    ktab = jnp.asarray(np.array([p[1] for p in pairs], np.int32))
    grid_spec = pltpu.PrefetchScalarGridSpec(
        num_scalar_prefetch=2,
        grid=(B, H // 2),
        in_specs=[pl.BlockSpec((1, 2, T, LANES), lambda b, h, qt, kt: (b, h, 0, 0)),
                  pl.BlockSpec((1, 2, T, LANES), lambda b, h, qt, kt: (b, h, 0, 0)),
                  pl.BlockSpec((1, 2, n_q, V_ROWS, blk), lambda b, h, qt, kt: (b, h, 0, 0, 0)),
                  pl.BlockSpec((1, T, LANES), lambda b, h, qt, kt: (b, 0, h)),
                  pl.BlockSpec((1, LANES), lambda b, h, qt, kt: (0, h))],
        out_specs=pl.BlockSpec((1, T, LANES), lambda b, h, qt, kt: (b, 0, h)),
        scratch_shapes=[pltpu.VMEM((n_q, 2, 1, blk), F32), pltpu.VMEM((n_q, 2, V_ROWS, blk), F32),
                        pltpu.VMEM((2, 2, blk, blk), F32), pltpu.VMEM((2, 2, blk, blk), BF16),
                        pltpu.VMEM((2, 2, 1, blk), F32)])
    return pl.pallas_call(
        functools.partial(_fox_kernel, blk=blk, cb=cb, n_off=n_off),
        out_shape=jax.ShapeDtypeStruct((B, T, width), BF16),
        grid_spec=grid_spec,
        compiler_params=pltpu.CompilerParams(
            dimension_semantics=("arbitrary",) * 2, vmem_limit_bytes=VMEM_LIMIT),
        name="fox",
    )(qtab, ktab, qa, ka, vt, og, fox_out_g.reshape(1, width))


def _outmlp_kernel(x_ref, yrw_ref, yfx_ref, mod_ref, wo_ref, g2_ref, w1_ref, w2_ref, gf_ref,
                   o_ref, *, ff_tile):
    x = x_ref[0]
    half = yrw_ref.shape[2]
    gate1 = mod_ref[0, 2:3, :]
    shift2 = mod_ref[0, 3:4, :]
    scale2 = mod_ref[0, 4:5, :]
    gate2 = mod_ref[0, 5:6, :]
    y = _dot(yrw_ref[0], wo_ref[:half, :]) + _dot(yfx_ref[0], wo_ref[half:, :])
    h1 = x + gate1 * y
    ms = jnp.mean(h1 * h1, axis=-1, keepdims=True)
    u = ((h1 * lax.rsqrt(ms + NORM_EPS) * g2_ref[...]) * (1.0 + scale2) + shift2).astype(BF16)
    acc = jnp.zeros_like(x)
    for j in range(w1_ref.shape[1] // ff_tile):
        hid = jnp.maximum(_dot(u, w1_ref[:, j * ff_tile:(j + 1) * ff_tile]), 0.0)
        acc = acc + _dot((hid * hid).astype(BF16), w2_ref[j * ff_tile:(j + 1) * ff_tile, :])
    h2 = h1 + gate2 * acc
    ms2 = jnp.mean(h2 * h2, axis=-1, keepdims=True)
    o_ref[0] = h2 * lax.rsqrt(ms2 + NORM_EPS) * gf_ref[...]


def _outmlp(x, y_rw, y_fx, mod3, w_o, norm2_g, w1, w2, final_g, tm=512, ff_tile=1024):
    B, T, D = x.shape
    half = y_rw.shape[2]
    return pl.pallas_call(
        functools.partial(_outmlp_kernel, ff_tile=ff_tile),
        out_shape=jax.ShapeDtypeStruct((B, T, D), F32),
        grid=(B, T // tm),
        in_specs=[pl.BlockSpec((1, tm, D), lambda b, t: (b, t, 0)),
                  pl.BlockSpec((1, tm, half), lambda b, t: (b, t, 0)),
                  pl.BlockSpec((1, tm, half), lambda b, t: (b, t, 0)),
                  pl.BlockSpec((1, N_MOD, D), lambda b, t: (b, 0, 0)),
                  _const_spec(w_o.shape, single_buffer=True), _const_spec((1, D)),
                  _const_spec(w1.shape, single_buffer=True),
                  _const_spec(w2.shape, single_buffer=True), _const_spec((1, D))],
        out_specs=pl.BlockSpec((1, tm, D), lambda b, t: (b, t, 0)),
        compiler_params=pltpu.CompilerParams(
            dimension_semantics=("arbitrary", "arbitrary"), vmem_limit_bytes=VMEM_LIMIT),
        name="outmlp",
    )(x, y_rw, y_fx, mod3, w_o, norm2_g.reshape(1, D), w1, w2, final_g.reshape(1, D))


def _pad_cols(a, n):
    return jnp.pad(a, ((0, 0), (0, n - a.shape[1])))


def _pad_rows(a, n):
    return jnp.pad(a, ((0, n - a.shape[0]), (0, 0)))


def _branches(x, c, w_ada, b_ada, norm1_g, w_in, mu_shift, w0, w_up_decay, a0, w_up_a, w_up_g,
              k_k, k_a, r_k, gn_g, gn_b, b_f, q_norm_g, k_norm_g, fox_out_g):
    B, T, D = x.shape
    W = w0.shape[0]
    heads = b_f.shape[0]
    rw_n = 3 * W + DECAY_LORA + AAA_LORA + GATE_LORA

    def regroup(a):
        rw, fx = a[:, :rw_n], a[:, rw_n:]
        o = 3 * W
        return jnp.concatenate([
            rw[:, :o],
            _pad_cols(rw[:, o:o + DECAY_LORA], LANES),
            _pad_cols(rw[:, o + DECAY_LORA:o + DECAY_LORA + AAA_LORA], LANES),
            _pad_cols(rw[:, o + DECAY_LORA + AAA_LORA:], 2 * LANES),
            fx[:, :4 * W],
            _pad_cols(fx[:, 4 * W:], LANES)], axis=1)

    rw_cols = 3 * W + 4 * LANES
    w_all = regroup(w_in.astype(BF16))
    mu_p = regroup(jnp.pad(mu_shift.reshape(1, rw_n), ((0, 0), (0, w_in.shape[1] - rw_n))))[:, :rw_cols]
    bf_p = _pad_cols(b_f.reshape(1, heads), LANES)
    wdw = _pad_rows(w_up_decay, LANES).astype(BF16)
    waw = _pad_rows(w_up_a, LANES).astype(BF16)
    wgw = _pad_rows(w_up_g, 2 * LANES).astype(BF16)

    mod3 = _ada(c, w_ada, b_ada).reshape(B, N_MOD, D)
    rw, qa, ka, vt, og = _inproj(x, mod3, norm1_g, w_all, mu_p, bf_p, q_norm_g, k_norm_g, rw_cols)
    y_rw = _rwkv(rw, w0, wdw, a0, waw, wgw, k_k, k_a, r_k.reshape(-1), gn_g, gn_b, W)
    y_fx = _fox(qa, ka, vt, og, fox_out_g, W)
    return y_rw, y_fx, mod3


def kernel(x, c, w_ada, b_ada, norm1_g, w_in, mu_shift, w0, w_up_decay, a0, w_up_a, w_up_g,
           k_k, k_a, r_k, gn_g, gn_b, b_f, q_norm_g, k_norm_g, fox_out_g, w_o, norm2_g,
           w_mlp1, w_mlp2, final_g):
    y_rw, y_fx, mod3 = _branches(x, c, w_ada, b_ada, norm1_g, w_in, mu_shift, w0, w_up_decay,
                                 a0, w_up_a, w_up_g, k_k, k_a, r_k, gn_g, gn_b, b_f,
                                 q_norm_g, k_norm_g, fox_out_g)
    return _outmlp(x, y_rw, y_fx, mod3, w_o.astype(BF16), norm2_g,
                   w_mlp1.astype(BF16), w_mlp2.astype(BF16), final_g)
```

```python
import functools
import math

import jax
import jax.numpy as jnp
import numpy as np
from jax import lax
from jax.experimental import pallas as pl
from jax.experimental.pallas import tpu as pltpu

F32 = jnp.float32
BF16 = jnp.bfloat16

HEAD_DIM = 64
LANES = 128
MXU_TILE = 256
NORM_EPS = 1e-6
GN_EPS = 64e-5
NEG_INF = -1e30
LOG2E = 1.4426950408889634
N_MOD = 6
DECAY_LORA = 64
AAA_LORA = 64
GATE_LORA = 160
RW_CHUNK = 64
V_ROWS = HEAD_DIM + 16
VMEM_LIMIT = 56 * 1024 * 1024

_HI = lax.Precision.HIGHEST


def _dot(a, b):
    return jnp.dot(a, b, preferred_element_type=F32)


def _dot_hi(a, b):
    return jnp.dot(a, b, preferred_element_type=F32, precision=_HI)


def _dot_nt(a, b):
    return lax.dot_general(a, b, (((1,), (1,)), ((), ())), preferred_element_type=F32)


def _dot_tn(a, b):
    return lax.dot_general(a, b, (((0,), (0,)), ((), ())), preferred_element_type=F32)


def _bf(x):
    return x.astype(BF16)


def _sigmoid(x):
    return 1.0 / (1.0 + jnp.exp(-x))


def _const_spec(shape, single_buffer=False):
    n = len(shape)
    mode = pl.Buffered(1) if single_buffer else None
    return pl.BlockSpec(shape, lambda *_: (0,) * n, pipeline_mode=mode)


def _ada_kernel(c_ref, w_ref, b_ref, o_ref):
    c = c_ref[...]
    o_ref[...] = _dot_hi(c * _sigmoid(c), w_ref[...]) + b_ref[...]


def _ada(c, w_ada, b_ada):
    B, D = c.shape
    n = w_ada.shape[1]
    return pl.pallas_call(
        _ada_kernel,
        out_shape=jax.ShapeDtypeStruct((B, n), F32),
        grid=(n // D,),
        in_specs=[pl.BlockSpec((B, D), lambda j: (0, 0)),
                  pl.BlockSpec((D, D), lambda j: (0, j)),
                  pl.BlockSpec((1, D), lambda j: (0, j))],
        out_specs=pl.BlockSpec((B, D), lambda j: (0, j)),
        name="ada",
    )(c, w_ada, b_ada.reshape(1, n))


N_PIECES = 3


def _fox_aug_tables(heads):
    groups = heads // 2
    sel = np.zeros((LANES, 2 * groups * LANES), np.float32)
    bias = np.zeros((1, 2 * groups * LANES), np.float32)
    for h in range(heads):
        base = (h // 2) * LANES + (HEAD_DIM if h % 2 == 0 else 0)
        kbase = groups * LANES + base
        for piece in range(N_PIECES):
            sel[piece * heads + h, base + piece] = 1.0
            sel[piece * heads + h, kbase + N_PIECES + piece] = -1.0
            bias[0, base + N_PIECES + piece] = 1.0
            bias[0, kbase + piece] = 1.0
    return jnp.asarray(sel, BF16), jnp.asarray(bias)


def _fox_layout(pf, fcum, qg_ref, kg_ref, sel_ref, bias_ref, qa_ref, ka_ref, v_ref, og_ref, width):
    W = width
    heads = W // HEAD_DIM
    groups = W // LANES
    tm = pf.shape[0]
    lane = lax.broadcasted_iota(jnp.int32, (tm, LANES), 1)
    fcum = fcum * LOG2E
    f_hi = fcum.astype(BF16).astype(F32)
    r1 = fcum - f_hi
    f_mid = r1.astype(BF16).astype(F32)
    f_lo = r1 - f_mid
    packed = jnp.where(lane < heads, f_hi, jnp.where(
        lane < 2 * heads, pltpu.roll(f_mid, heads, axis=1), jnp.where(
            lane < 3 * heads, pltpu.roll(f_lo, 2 * heads, axis=1), 0.0)))
    aug = _dot(packed.astype(BF16), sel_ref[...]) + bias_ref[...]
    first = lane < HEAD_DIM
    for side, (gain_ref, out_ref) in enumerate(((qg_ref, qa_ref), (kg_ref, ka_ref))):
        for g in range(groups):
            x = pf[:, side * W + g * LANES:side * W + (g + 1) * LANES]
            sq = x * x
            lo = jnp.sum(jnp.where(first, sq, 0.0), axis=-1, keepdims=True)
            hi = jnp.sum(jnp.where(first, 0.0, sq), axis=-1, keepdims=True)
            ms = jnp.where(first, lo, hi) * (1.0 / HEAD_DIM)
            xn = x * lax.rsqrt(ms + NORM_EPS) * gain_ref[...]
            a = aug[:, (side * groups + g) * LANES:(side * groups + g + 1) * LANES]
            out_ref[0, 2 * g] = jnp.where(first, xn, a).astype(BF16)
            out_ref[0, 2 * g + 1] = jnp.where(first, a, xn).astype(BF16)
    pad = (lax.broadcasted_iota(jnp.int32, (V_ROWS - HEAD_DIM, tm), 0) == 0).astype(BF16)
    for g in range(groups):
        vt = pf[:, 2 * W + g * LANES:2 * W + (g + 1) * LANES].T.astype(BF16)
        for hh in range(2):
            v_ref[0, 2 * g + hh, 0, :HEAD_DIM, :] = vt[hh * HEAD_DIM:(hh + 1) * HEAD_DIM]
            v_ref[0, 2 * g + hh, 0, HEAD_DIM:, :] = pad
    og_ref[0] = pf[:, 3 * W:4 * W].astype(BF16)


def _inproj_kernel(x_ref, mod_ref, g_ref, w_ref, mu_ref, bf_ref, qg_ref, kg_ref, sel_ref, bias_ref,
                   rw_ref, qa_ref, ka_ref, v_ref, og_ref, prev_scr, fcar_scr, *, rw_cols):
    t = pl.program_id(1)
    tm = x_ref.shape[1]

    @pl.when(t == 0)
    def _():
        prev_scr[...] = jnp.zeros_like(prev_scr)
        fcar_scr[...] = jnp.zeros_like(fcar_scr)

    x = x_ref[0]
    shift = mod_ref[0, 0:1, :]
    scale = mod_ref[0, 1:2, :]
    ms = jnp.mean(x * x, axis=-1, keepdims=True)
    u = (x * lax.rsqrt(ms + NORM_EPS) * g_ref[...]) * (1.0 + scale) + shift
    u = u.astype(BF16)

    p = _dot(u, w_ref[:, :rw_cols])
    row = lax.broadcasted_iota(jnp.int32, p.shape, 0)
    prev = jnp.where(row == 0, prev_scr[0:1, :], pltpu.roll(p, shift=1, axis=0))
    prev_scr[0:1, :] = p[tm - 1:tm, :]
    rw_ref[0] = p + (prev - p) * mu_ref[...]

    pf = _dot(u, w_ref[:, rw_cols:])
    z = pf[:, pf.shape[1] - LANES:] + bf_ref[...]
    logf = jnp.minimum(z, 0.0) - jnp.log(1.0 + jnp.exp(-jnp.abs(z)))
    sub = min(tm, MXU_TILE)
    r2 = lax.broadcasted_iota(jnp.int32, (sub, sub), 0)
    c2 = lax.broadcasted_iota(jnp.int32, (sub, sub), 1)
    tril = (c2 <= r2).astype(BF16)
    f_hi = logf.astype(BF16)
    r1 = logf - f_hi.astype(F32)
    f_mid = r1.astype(BF16)
    f_lo = (r1 - f_mid.astype(F32)).astype(BF16)
    carry = fcar_scr[0:1, :]
    fcums = []
    for i in range(tm // sub):
        rs = slice(i * sub, (i + 1) * sub)
        fcums.append((_dot(tril, f_hi[rs]) + _dot(tril, f_mid[rs])) + _dot(tril, f_lo[rs]) + carry)
        carry = fcums[-1][sub - 1:sub, :]
    fcar_scr[0:1, :] = carry
    _fox_layout(pf, jnp.concatenate(fcums, axis=0), qg_ref, kg_ref, sel_ref, bias_ref,
                qa_ref, ka_ref, v_ref, og_ref, (pf.shape[1] - LANES) // 4)


def _inproj(x, mod3, norm1_g, w_all, mu_p, bf_p, q_norm_g, k_norm_g, rw_cols, tm=512):
    B, T, D = x.shape
    ncols = w_all.shape[1]
    width = (ncols - rw_cols - LANES) // 4
    heads = width // HEAD_DIM
    sel, bias = _fox_aug_tables(heads)
    scale = LOG2E / math.sqrt(HEAD_DIM)
    qg2 = jnp.tile(q_norm_g * scale, 2).reshape(1, LANES)
    kg2 = jnp.tile(k_norm_g, 2).reshape(1, LANES)
    return pl.pallas_call(
        functools.partial(_inproj_kernel, rw_cols=rw_cols),
        out_shape=(jax.ShapeDtypeStruct((B, T, rw_cols), F32),
                   jax.ShapeDtypeStruct((B, heads, T, LANES), BF16),
                   jax.ShapeDtypeStruct((B, heads, T, LANES), BF16),
                   jax.ShapeDtypeStruct((B, heads, T // tm, V_ROWS, tm), BF16),
                   jax.ShapeDtypeStruct((B, T, width), BF16)),
        grid=(B, T // tm),
        in_specs=[pl.BlockSpec((1, tm, D), lambda b, t: (b, t, 0)),
                  pl.BlockSpec((1, N_MOD, D), lambda b, t: (b, 0, 0)),
                  _const_spec((1, D)),
                  _const_spec((D, ncols), single_buffer=True),
                  _const_spec((1, rw_cols)),
                  _const_spec((1, LANES)),
                  _const_spec((1, LANES)), _const_spec((1, LANES)),
                  _const_spec(sel.shape, single_buffer=True), _const_spec(bias.shape)],
        out_specs=(pl.BlockSpec((1, tm, rw_cols), lambda b, t: (b, t, 0)),
                   pl.BlockSpec((1, heads, tm, LANES), lambda b, t: (b, 0, t, 0)),
                   pl.BlockSpec((1, heads, tm, LANES), lambda b, t: (b, 0, t, 0)),
                   pl.BlockSpec((1, heads, 1, V_ROWS, tm), lambda b, t: (b, 0, t, 0, 0)),
                   pl.BlockSpec((1, tm, width), lambda b, t: (b, t, 0))),
        scratch_shapes=[pltpu.VMEM((8, rw_cols), F32), pltpu.VMEM((8, LANES), F32)],
        compiler_params=pltpu.CompilerParams(
            dimension_semantics=("arbitrary", "arbitrary"), vmem_limit_bytes=VMEM_LIMIT),
        name="inproj",
    )(x, mod3, norm1_g.reshape(1, D), w_all, mu_p, bf_p, qg2, kg2, sel, bias)


def _segsum(x, bd, split=False):
    tile = bd.shape[0]
    hi = x.astype(BF16)
    lo = (x - hi.astype(F32)).astype(BF16) if split else None
    cols = []
    for j in range(x.shape[1] // tile):
        sl = slice(j * tile, (j + 1) * tile)
        s = _dot(hi[:, sl], bd)
        cols.append(s + _dot(lo[:, sl], bd) if split else s)
    return jnp.concatenate(cols, axis=1)


def _interleave(generators):
    live = [g for g in generators if g is not None]
    while live:
        for g in list(live):
            if next(g, StopIteration) is StopIteration:
                live.remove(g)


def _rwkv_kernel(p_ref, w0_ref, wdw_ref, a0_ref, waw_ref, wgw_ref, kk_ref, ka_ref, rk_ref,
                 gng_ref, gnb_ref, o_ref, s_scr, *, width, block_rows):
    t = pl.program_id(1)
    L = RW_CHUNK
    R = block_rows
    n_blocks = p_ref.shape[1] // R
    n_chunks = R // L
    W = width
    G2 = 2 * LANES
    groups = W // LANES
    items = [(c, j) for c in range(n_chunks) for j in range(groups)]
    rsl = lambda c: slice(c * L, (c + 1) * L)
    gsl = lambda j: slice(j * LANES, (j + 1) * LANES)

    @pl.when(t == 0)
    def _():
        s_scr[...] = jnp.zeros_like(s_scr)

    ri = lax.broadcasted_iota(jnp.int32, (MXU_TILE, MXU_TILE), 0)
    ci = lax.broadcasted_iota(jnp.int32, (MXU_TILE, MXU_TILE), 1)
    bd = ((ri // HEAD_DIM) == (ci // HEAD_DIM)).astype(BF16)
    tril = (lax.broadcasted_iota(jnp.int32, (L, L), 1)
            <= lax.broadcasted_iota(jnp.int32, (L, L), 0)).astype(BF16)
    o0 = 3 * W
    lane1 = lax.broadcasted_iota(jnp.int32, (L, LANES), 1)
    lane2 = lax.broadcasted_iota(jnp.int32, (L, G2), 1)
    row2 = lax.broadcasted_iota(jnp.int32, (L, G2), 0)
    first1 = lane1 < HEAD_DIM
    first2 = (lane2 % LANES) < HEAD_DIM
    strict2 = (lane2 % HEAD_DIM) < row2
    incl2 = (lane2 % HEAD_DIM) <= row2
    upper2 = lane2 >= LANES
    eye2 = ((lane1 % HEAD_DIM) == lax.broadcasted_iota(jnp.int32, (L, LANES), 0)).astype(F32)
    zero1 = jnp.zeros((L, LANES), BF16)
    rr = lax.broadcasted_iota(jnp.int32, (LANES, LANES), 0) < HEAD_DIM
    cc = lax.broadcasted_iota(jnp.int32, (LANES, LANES), 1) < HEAD_DIM
    diag_blocks = rr == cc

    def split(m, mask):
        return jnp.concatenate([jnp.where(mask, m, 0), jnp.where(mask, 0, m)], axis=0)

    def prepare(pb, b):
        p = p_ref[0, b * R:(b + 1) * R, :]
        r, k, v = p[:, 0:W], p[:, W:2 * W], p[:, 2 * W:3 * W]
        wd = p[:, o0:o0 + LANES]
        ad = p[:, o0 + LANES:o0 + 2 * LANES]
        gd = p[:, o0 + 2 * LANES:o0 + 4 * LANES]
        z = w0_ref[...] + _dot(_bf(jnp.tanh(wd)), wdw_ref[...])
        softplus = jnp.maximum(-z, 0.0) + jnp.log(1.0 + jnp.exp(-jnp.abs(z)))
        ld = -jnp.exp(-softplus - 0.5)
        yield
        a_sig = _sigmoid(a0_ref[...] + _dot(_bf(ad), waw_ref[...]))
        pb["gate"] = _dot(_bf(_sigmoid(gd)), wgw_ref[...])
        yield
        kk = k * kk_ref[...]
        kk = kk / jnp.maximum(jnp.sqrt(_segsum(kk * kk, bd, split=True)), 1e-12)
        yield
        k2 = k * (1.0 + (a_sig - 1.0) * ka_ref[...])
        b_ = kk * a_sig
        pb["bonus"] = _segsum(r * k2 * rk_ref[...], bd) * v
        yield
        ld_hi = _bf(ld)
        ld_lo = _bf(ld - ld_hi.astype(F32))
        cums = [_dot(tril, ld_hi[rsl(c)]) + _dot(tril, ld_lo[rsl(c)]) for c in range(n_chunks)]
        pb["tots"] = [cum_c[L - 1:L, :] for cum_c in cums]
        cum = jnp.concatenate(cums, axis=0)
        tot_b = jnp.concatenate([jnp.broadcast_to(tc, (L, W)) for tc in pb["tots"]], axis=0)
        yield
        e_neg = jnp.exp(-cum)
        rt = r * jnp.exp(cum)
        pb.update(rt=rt, rt_m=_bf(rt), at=_bf(-kk * jnp.exp(cum - ld)), bt=_bf(b_ * e_neg),
                  kt=_bf(k2 * e_neg))
        yield
        e_rem = jnp.exp(tot_b - cum)
        pb.update(bh=_bf(b_ * e_rem), kh=_bf(k2 * e_rem), vm=_bf(v))

    def chains(pb, cb):
        at, rt_m, bt, kt, vm = pb["at"], pb["rt_m"], pb["bt"], pb["kt"], pb["vm"]
        g_bot, akm, zz = {}, {}, {}
        for c, j in items:
            ar = jnp.concatenate([at[rsl(c), gsl(j)], rt_m[rsl(c), gsl(j)]], axis=0)
            rhs = jnp.concatenate([split(bt[rsl(c), gsl(j)], first1),
                                   split(kt[rsl(c), gsl(j)], first1)], axis=0)
            gm = _dot_nt(ar, rhs)
            top = jnp.where(strict2, gm[:L], 0.0)
            g_bot[c, j] = _bf(jnp.where(incl2, gm[L:], 0.0))
            akm[c, j] = _bf(top[:, LANES:])
            zz[c, j] = jnp.concatenate([top[:, :LANES], eye2], axis=1)
        yield
        for _ in range(int(math.log2(L))):
            for it in items:
                zb = _bf(zz[it])
                out = _dot(zb[:, :LANES], split(zb, first2))
                zz[it] = out + jnp.where(upper2, zz[it], 0.0)
            yield
        vbd = {(c, j): split(vm[rsl(c), gsl(j)], first1) for c, j in items}
        akv = {it: _bf(_dot(akm[it], vbd[it])) for it in items}
        yield
        x = {}
        for c, j in items:
            zed = jnp.concatenate([at[rsl(c), gsl(j)], akv[c, j]], axis=1)
            x[c, j] = _bf(_dot(_bf(zz[c, j][:, LANES:]), split(zed, first2)))
        yield
        cb.update(r_eff={}, y0={}, mt={}, nt={})
        for c, j in items:
            w4 = jnp.concatenate([split(x[c, j], first2),
                                  jnp.concatenate([jnp.concatenate([zero1, zero1], axis=0),
                                                   vbd[c, j]], axis=1)], axis=0)
            out = _dot(g_bot[c, j], w4)
            cb["r_eff"][c, j] = _bf(pb["rt"][rsl(c), gsl(j)] + out[:, :LANES])
            cb["y0"][c, j] = out[:, LANES:]
        yield
        for c, j in items:
            lhs = jnp.concatenate([pb["bh"][rsl(c), gsl(j)], pb["kh"][rsl(c), gsl(j)]], axis=0)
            rhs = jnp.concatenate([x[c, j], jnp.concatenate([zero1, vm[rsl(c), gsl(j)]], axis=1)],
                                  axis=0)
            pt = _dot_tn(lhs, rhs)
            cb["mt"][c, j] = _bf(jnp.where(diag_blocks, pt[:, :LANES], 0.0))
            cb["nt"][c, j] = jnp.where(diag_blocks, pt[:, LANES:], 0.0)
        yield
        cb["e_cols"] = [
            jnp.concatenate([jnp.exp(pb["tots"][c][:, gsl(j)]) for c in range(n_chunks)]
                            + [jnp.zeros((LANES - n_chunks, LANES), F32)], axis=0).T
            for j in range(groups)]

    state = [s_scr[j] for j in range(groups)]

    def tail(pb, cb, b):
        y_rows = []
        for c in range(n_chunks):
            hb = [_bf(s) for s in state]
            y_rows.append(jnp.concatenate(
                [_dot(cb["r_eff"][c, j], hb[j]) + cb["y0"][c, j] for j in range(groups)], axis=1))
            for j in range(groups):
                state[j] = (state[j] * cb["e_cols"][j][:, c:c + 1] + _dot(cb["mt"][c, j], hb[j])
                            + cb["nt"][c, j])
            yield
        y = jnp.concatenate(y_rows, axis=0)
        inv_n = 1.0 / HEAD_DIM
        mean = _segsum(y, bd) * inv_n
        d = y - mean
        yield
        var = _segsum(d * d, bd) * inv_n
        yn = d * lax.rsqrt(var + GN_EPS) * gng_ref[...] + gnb_ref[...]
        o_ref[0, b * R:(b + 1) * R, :] = ((yn + pb["bonus"]) * pb["gate"]).astype(o_ref.dtype)

    prep = [dict() for _ in range(n_blocks)]
    chain = [dict() for _ in range(n_blocks)]
    _interleave([prepare(prep[0], 0)])
    for b in range(n_blocks):
        _interleave([chains(prep[b], chain[b]),
                     prepare(prep[b + 1], b + 1) if b + 1 < n_blocks else None,
                     tail(prep[b - 1], chain[b - 1], b - 1) if b > 0 else None])
    _interleave([tail(prep[-1], chain[-1], n_blocks - 1)])
    for j in range(groups):
        s_scr[j] = state[j]


def _rwkv(rw, w0, wdw, a0, waw, wgw, k_k, k_a, r_k, gn_g, gn_b, width, rows=512, block_rows=256):
    B, T, cols = rw.shape
    row = lambda a: a.reshape(1, width)
    vec = _const_spec((1, width))
    return pl.pallas_call(
        functools.partial(_rwkv_kernel, width=width, block_rows=block_rows),
        out_shape=jax.ShapeDtypeStruct((B, T, width), BF16),
        grid=(B, T // rows),
        in_specs=[pl.BlockSpec((1, rows, cols), lambda b, t: (b, t, 0)),
                  vec, _const_spec(wdw.shape), vec, _const_spec(waw.shape),
                  _const_spec(wgw.shape), vec, vec, vec, vec, vec],
        out_specs=pl.BlockSpec((1, rows, width), lambda b, t: (b, t, 0)),
        scratch_shapes=[pltpu.VMEM((width // LANES, LANES, LANES), F32)],
        compiler_params=pltpu.CompilerParams(
            dimension_semantics=("arbitrary", "arbitrary"), vmem_limit_bytes=VMEM_LIMIT),
        name="rwkv",
    )(rw, row(w0), wdw, row(a0), waw, wgw, row(k_k), row(k_a), row(r_k), row(gn_g), row(gn_b))


def _fox_kernel(qtab, ktab, qa_ref, ka_ref, vt_ref, og_ref, g_ref, o_ref,
                m_all, acc_all, s_scr, p_scr, al_scr, *, blk, cb, n_off, hps):
    n_q = qa_ref.shape[2] // blk
    n_el = n_off + n_q
    chains = [(hh, c) for hh in range(hps) for c in range(blk // cb)]
    csl = lambda c: slice(c * cb, (c + 1) * cb)

    m_all[...] = jnp.full_like(m_all, NEG_INF)
    acc_all[...] = jnp.zeros_like(acc_all)

    def element(t):
        if isinstance(t, int) and t >= n_off:
            return t - n_off, t - n_off, True
        return qtab[t], ktab[t], False

    def start(i, size):
        return i * size if isinstance(i, int) else pl.multiple_of(i * size, size)

    def n_keys(c, diag):
        return (c + 1) * cb if diag else blk

    def logits(t, par):
        qi, ki, diag = element(t)
        for hh, c in chains:
            nk = n_keys(c, diag)
            s_scr[par, hh, :nk, csl(c)] = _dot_nt(
                ka_ref[0, hh, pl.ds(start(ki, blk), nk), :],
                qa_ref[0, hh, pl.ds(start(qi, blk) + c * cb, cb), :])
            yield

    def softmax(t, par):
        qi, _, diag = element(t)
        for hh, c in chains:
            nk = n_keys(c, diag)
            s = s_scr[par, hh, :nk, csl(c)]
            if diag:
                key = lax.broadcasted_iota(jnp.int32, (nk, cb), 0)
                qry = c * cb + lax.broadcasted_iota(jnp.int32, (nk, cb), 1)
                s = jnp.where(key <= qry, s, NEG_INF)
            m_prev = m_all[qi, hh, :, csl(c)]
            m_new = jnp.maximum(m_prev, jnp.max(s, axis=0, keepdims=True))
            m_all[qi, hh, :, csl(c)] = m_new
            al_scr[par, hh, :, csl(c)] = jnp.exp2(m_prev - m_new)
            p_scr[par, hh, :nk, csl(c)] = jnp.exp2(s - m_new).astype(BF16)
            yield

    def values(t, par):
        qi, ki, diag = element(t)
        for hh, c in chains:
            nk = n_keys(c, diag)
            pv = _dot(vt_ref[0, hh, ki, :, :nk], p_scr[par, hh, :nk, csl(c)])
            acc_all[qi, hh, :, csl(c)] = (al_scr[par, hh, :, csl(c)] * acc_all[qi, hh, :, csl(c)]
                                          + pv)
            yield
        if diag:
            rows = slice(qi * blk, (qi + 1) * blk)
            for pr in range(hps // 2):
                normed = []
                for hh in (2 * pr, 2 * pr + 1):
                    acc = acc_all[qi, hh]
                    o = acc[:HEAD_DIM] / acc[HEAD_DIM:HEAD_DIM + 1]
                    ms = jnp.mean(o * o, axis=0, keepdims=True)
                    normed.append(o * lax.rsqrt(ms + NORM_EPS))
                on = jnp.concatenate(normed, axis=0).T
                ls = slice(pr * LANES, (pr + 1) * LANES)
                o_ref[0, rows, ls] = (on * g_ref[:, ls]
                                      * _sigmoid(og_ref[0, rows, ls].astype(F32))).astype(o_ref.dtype)

    def step(t, par):
        _interleave([stage(t - lag, (par + lag) % 2)
                     for lag, stage in enumerate((logits, softmax, values))
                     if not isinstance(t, int) or 0 <= t - lag < n_el])

    def two_steps(u, carry):
        step(2 * u + 2, 0)
        step(2 * u + 3, 1)
        return carry

    step(0, 0)
    step(1, 1)
    lax.fori_loop(0, (n_off - 2) // 2, two_steps, 0)
    for t in range(n_off, n_el + 2):
        step(t, t % 2)


def _fox(qa, ka, vt, og, fox_out_g, width, cb=256, hps=4):
    B, H, T, _ = qa.shape
    blk = vt.shape[4]
    n_q = T // blk
    pairs = [(qi, ki) for qi in range(n_q) for ki in range(qi)]
    n_off = len(pairs)
    assert n_off >= 2 and n_off % 2 == 0, "pipeline loop is unrolled by two"
    qtab = jnp.asarray(np.array([p[0] for p in pairs], np.int32))
    ktab = jnp.asarray(np.array([p[1] for p in pairs], np.int32))
    grid_spec = pltpu.PrefetchScalarGridSpec(
        num_scalar_prefetch=2,
        grid=(B, H // hps),
        in_specs=[pl.BlockSpec((1, hps, T, LANES), lambda b, h, qt, kt: (b, h, 0, 0)),
                  pl.BlockSpec((1, hps, T, LANES), lambda b, h, qt, kt: (b, h, 0, 0)),
                  pl.BlockSpec((1, hps, n_q, V_ROWS, blk), lambda b, h, qt, kt: (b, h, 0, 0, 0)),
                  pl.BlockSpec((1, T, hps * HEAD_DIM), lambda b, h, qt, kt: (b, 0, h)),
                  pl.BlockSpec((1, hps * HEAD_DIM), lambda b, h, qt, kt: (0, h))],
        out_specs=pl.BlockSpec((1, T, hps * HEAD_DIM), lambda b, h, qt, kt: (b, 0, h)),
        scratch_shapes=[pltpu.VMEM((n_q, hps, 1, blk), F32),
                        pltpu.VMEM((n_q, hps, V_ROWS, blk), F32),
                        pltpu.VMEM((2, hps, blk, blk), F32), pltpu.VMEM((2, hps, blk, blk), BF16),
                        pltpu.VMEM((2, hps, 1, blk), F32)])
    return pl.pallas_call(
        functools.partial(_fox_kernel, blk=blk, cb=cb, n_off=n_off, hps=hps),
        out_shape=jax.ShapeDtypeStruct((B, T, width), BF16),
        grid_spec=grid_spec,
        compiler_params=pltpu.CompilerParams(
            dimension_semantics=("arbitrary",) * 2, vmem_limit_bytes=VMEM_LIMIT),
        name="fox",
    )(qtab, ktab, qa, ka, vt, og, fox_out_g.reshape(1, width))


def _outmlp_kernel(x_ref, yrw_ref, yfx_ref, mod_ref, wo_ref, g2_ref, w1_ref, w2_ref, gf_ref,
                   o_ref, *, ff_tile):
    x = x_ref[0]
    half = yrw_ref.shape[2]
    gate1 = mod_ref[0, 2:3, :]
    shift2 = mod_ref[0, 3:4, :]
    scale2 = mod_ref[0, 4:5, :]
    gate2 = mod_ref[0, 5:6, :]
    y = _dot(yrw_ref[0], wo_ref[:half, :]) + _dot(yfx_ref[0], wo_ref[half:, :])
    h1 = x + gate1 * y
    ms = jnp.mean(h1 * h1, axis=-1, keepdims=True)
    u = ((h1 * lax.rsqrt(ms + NORM_EPS) * g2_ref[...]) * (1.0 + scale2) + shift2).astype(BF16)
    acc = jnp.zeros_like(x)
    for j in range(w1_ref.shape[1] // ff_tile):
        hid = jnp.maximum(_dot(u, w1_ref[:, j * ff_tile:(j + 1) * ff_tile]), 0.0)
        acc = acc + _dot((hid * hid).astype(BF16), w2_ref[j * ff_tile:(j + 1) * ff_tile, :])
    h2 = h1 + gate2 * acc
    ms2 = jnp.mean(h2 * h2, axis=-1, keepdims=True)
    o_ref[0] = h2 * lax.rsqrt(ms2 + NORM_EPS) * gf_ref[...]


def _outmlp(x, y_rw, y_fx, mod3, w_o, norm2_g, w1, w2, final_g, tm=512, ff_tile=1024):
    B, T, D = x.shape
    half = y_rw.shape[2]
    return pl.pallas_call(
        functools.partial(_outmlp_kernel, ff_tile=ff_tile),
        out_shape=jax.ShapeDtypeStruct((B, T, D), F32),
        grid=(B, T // tm),
        in_specs=[pl.BlockSpec((1, tm, D), lambda b, t: (b, t, 0)),
                  pl.BlockSpec((1, tm, half), lambda b, t: (b, t, 0)),
                  pl.BlockSpec((1, tm, half), lambda b, t: (b, t, 0)),
                  pl.BlockSpec((1, N_MOD, D), lambda b, t: (b, 0, 0)),
                  _const_spec(w_o.shape, single_buffer=True), _const_spec((1, D)),
                  _const_spec(w1.shape, single_buffer=True),
                  _const_spec(w2.shape, single_buffer=True), _const_spec((1, D))],
        out_specs=pl.BlockSpec((1, tm, D), lambda b, t: (b, t, 0)),
        compiler_params=pltpu.CompilerParams(
            dimension_semantics=("arbitrary", "arbitrary"), vmem_limit_bytes=VMEM_LIMIT),
        name="outmlp",
    )(x, y_rw, y_fx, mod3, w_o, norm2_g.reshape(1, D), w1, w2, final_g.reshape(1, D))


def _pad_cols(a, n):
    return jnp.pad(a, ((0, 0), (0, n - a.shape[1])))


def _pad_rows(a, n):
    return jnp.pad(a, ((0, n - a.shape[0]), (0, 0)))


def _branches(x, c, w_ada, b_ada, norm1_g, w_in, mu_shift, w0, w_up_decay, a0, w_up_a, w_up_g,
              k_k, k_a, r_k, gn_g, gn_b, b_f, q_norm_g, k_norm_g, fox_out_g):
    B, T, D = x.shape
    W = w0.shape[0]
    heads = b_f.shape[0]
    rw_n = 3 * W + DECAY_LORA + AAA_LORA + GATE_LORA

    def regroup(a):
        rw, fx = a[:, :rw_n], a[:, rw_n:]
        o = 3 * W
        return jnp.concatenate([
            rw[:, :o],
            _pad_cols(rw[:, o:o + DECAY_LORA], LANES),
            _pad_cols(rw[:, o + DECAY_LORA:o + DECAY_LORA + AAA_LORA], LANES),
            _pad_cols(rw[:, o + DECAY_LORA + AAA_LORA:], 2 * LANES),
            fx[:, :4 * W],
            _pad_cols(fx[:, 4 * W:], LANES)], axis=1)

    rw_cols = 3 * W + 4 * LANES
    w_all = regroup(w_in.astype(BF16))
    mu_p = regroup(jnp.pad(mu_shift.reshape(1, rw_n), ((0, 0), (0, w_in.shape[1] - rw_n))))[:, :rw_cols]
    bf_p = _pad_cols(b_f.reshape(1, heads), LANES)
    wdw = _pad_rows(w_up_decay, LANES).astype(BF16)
    waw = _pad_rows(w_up_a, LANES).astype(BF16)
    wgw = _pad_rows(w_up_g, 2 * LANES).astype(BF16)

    mod3 = _ada(c, w_ada, b_ada).reshape(B, N_MOD, D)
    rw, qa, ka, vt, og = _inproj(x, mod3, norm1_g, w_all, mu_p, bf_p, q_norm_g, k_norm_g, rw_cols)
    y_rw = _rwkv(rw, w0, wdw, a0, waw, wgw, k_k, k_a, r_k.reshape(-1), gn_g, gn_b, W)
    y_fx = _fox(qa, ka, vt, og, fox_out_g, W)
    return y_rw, y_fx, mod3


def kernel(x, c, w_ada, b_ada, norm1_g, w_in, mu_shift, w0, w_up_decay, a0, w_up_a, w_up_g,
           k_k, k_a, r_k, gn_g, gn_b, b_f, q_norm_g, k_norm_g, fox_out_g, w_o, norm2_g,
           w_mlp1, w_mlp2, final_g):
    y_rw, y_fx, mod3 = _branches(x, c, w_ada, b_ada, norm1_g, w_in, mu_shift, w0, w_up_decay,
                                 a0, w_up_a, w_up_g, k_k, k_a, r_k, gn_g, gn_b, b_f,
                                 q_norm_g, k_norm_g, fox_out_g)
    return _outmlp(x, y_rw, y_fx, mod3, w_o.astype(BF16), norm2_g,
                   w_mlp1.astype(BF16), w_mlp2.astype(BF16), final_g)
```

```python
import functools
import math

import jax
import jax.numpy as jnp
import numpy as np
from jax import lax
from jax.experimental import pallas as pl
from jax.experimental.pallas import tpu as pltpu

F32 = jnp.float32
BF16 = jnp.bfloat16

HEAD_DIM = 64
LANES = 128
MXU_TILE = 256
NORM_EPS = 1e-6
GN_EPS = 64e-5
NEG_INF = -1e30
LOG2E = 1.4426950408889634
N_MOD = 6
DECAY_LORA = 64
AAA_LORA = 64
GATE_LORA = 160
RW_CHUNK = 64
V_ROWS = HEAD_DIM + 16
VMEM_LIMIT = 56 * 1024 * 1024


def _dot(a, b):
    return jnp.dot(a, b, preferred_element_type=F32)


def _dot_nt(a, b):
    return lax.dot_general(a, b, (((1,), (1,)), ((), ())), preferred_element_type=F32)


def _dot_tn(a, b):
    return lax.dot_general(a, b, (((0,), (0,)), ((), ())), preferred_element_type=F32)


def _bf(x):
    return x.astype(BF16)


def _sigmoid(x):
    return 1.0 / (1.0 + jnp.exp(-x))


def _const_spec(shape, single_buffer=False):
    n = len(shape)
    mode = pl.Buffered(1) if single_buffer else None
    return pl.BlockSpec(shape, lambda *_: (0,) * n, pipeline_mode=mode)


def _ada_kernel(c_ref, w_ref, b_ref, o_ref):
    c = c_ref[...]
    a = c * _sigmoid(c)
    w = w_ref[...]
    a_hi, w_hi = _bf(a), _bf(w)
    a_lo, w_lo = _bf(a - a_hi.astype(F32)), _bf(w - w_hi.astype(F32))
    o_ref[...] = (_dot(a_hi, w_hi) + (_dot(a_hi, w_lo) + _dot(a_lo, w_hi))) + b_ref[...]


def _ada(c, w_ada, b_ada, tn=2048):
    B, D = c.shape
    n = w_ada.shape[1]
    return pl.pallas_call(
        _ada_kernel,
        out_shape=jax.ShapeDtypeStruct((B, n), F32),
        grid=(n // tn,),
        in_specs=[pl.BlockSpec((B, D), lambda j: (0, 0)),
                  pl.BlockSpec((D, tn), lambda j: (0, j)),
                  pl.BlockSpec((1, tn), lambda j: (0, j))],
        out_specs=pl.BlockSpec((B, tn), lambda j: (0, j)),
        name="ada",
    )(c, w_ada, b_ada.reshape(1, n))


N_PIECES = 3


def _fox_aug_tables(heads, f_off):
    groups = heads // 2
    sel = np.zeros((LANES, 2 * groups * LANES), np.float32)
    bias = np.zeros((1, 2 * groups * LANES), np.float32)
    for h in range(heads):
        base = (h // 2) * LANES + (HEAD_DIM if h % 2 == 0 else 0)
        kbase = groups * LANES + base
        for piece in range(N_PIECES):
            sel[f_off + piece * heads + h, base + piece] = 1.0
            sel[f_off + piece * heads + h, kbase + N_PIECES + piece] = -1.0
            bias[0, base + N_PIECES + piece] = 1.0
            bias[0, kbase + piece] = 1.0
    return jnp.asarray(sel, BF16), jnp.asarray(bias)


def _fox_layout(pf, fcum, f_off, qg_ref, kg_ref, sel_ref, bias_ref, qa_ref, ka_ref, v_ref, og_ref):
    W = pf.shape[1] // 4
    heads = W // HEAD_DIM
    groups = W // LANES
    tm = pf.shape[0]
    lane = lax.broadcasted_iota(jnp.int32, (tm, LANES), 1)
    lane_f = lane - f_off
    fcum = fcum * LOG2E
    f_hi = fcum.astype(BF16).astype(F32)
    r1 = fcum - f_hi
    f_mid = r1.astype(BF16).astype(F32)
    f_lo = r1 - f_mid
    packed = jnp.where(lane_f < 0, 0.0, jnp.where(lane_f < heads, f_hi, jnp.where(
        lane_f < 2 * heads, pltpu.roll(f_mid, heads, axis=1), jnp.where(
            lane_f < 3 * heads, pltpu.roll(f_lo, 2 * heads, axis=1), 0.0))))
    aug = _dot(packed.astype(BF16), sel_ref[...]) + bias_ref[...]
    first = lane < HEAD_DIM
    for side, (gain_ref, out_ref) in enumerate(((qg_ref, qa_ref), (kg_ref, ka_ref))):
        for g in range(groups):
            x = pf[:, side * W + g * LANES:side * W + (g + 1) * LANES]
            sq = x * x
            lo = jnp.sum(jnp.where(first, sq, 0.0), axis=-1, keepdims=True)
            hi = jnp.sum(jnp.where(first, 0.0, sq), axis=-1, keepdims=True)
            ms = jnp.where(first, lo, hi) * (1.0 / HEAD_DIM)
            xn = x * lax.rsqrt(ms + NORM_EPS) * gain_ref[...]
            a = aug[:, (side * groups + g) * LANES:(side * groups + g + 1) * LANES]
            out_ref[0, 2 * g] = jnp.where(first, xn, a).astype(BF16)
            out_ref[0, 2 * g + 1] = jnp.where(first, a, xn).astype(BF16)
    pad = (lax.broadcasted_iota(jnp.int32, (V_ROWS - HEAD_DIM, tm), 0) == 0).astype(BF16)
    for g in range(groups):
        vt = pf[:, 2 * W + g * LANES:2 * W + (g + 1) * LANES].T.astype(BF16)
        for hh in range(2):
            v_ref[0, 2 * g + hh, 0, :HEAD_DIM, :] = vt[hh * HEAD_DIM:(hh + 1) * HEAD_DIM]
            v_ref[0, 2 * g + hh, 0, HEAD_DIM:, :] = pad
    og_ref[0] = pf[:, 3 * W:4 * W].astype(BF16)


def _inproj_kernel(x_ref, mod_ref, g_ref, w_ref, mu_ref, bf_ref, qg_ref, kg_ref, sel_ref, bias_ref,
                   rw_ref, qa_ref, ka_ref, v_ref, og_ref, prev_scr, fcar_scr, *, rw_cols, f_off):
    t = pl.program_id(1)
    tm = x_ref.shape[1]

    @pl.when(t == 0)
    def _():
        prev_scr[...] = jnp.zeros_like(prev_scr)
        fcar_scr[...] = jnp.zeros_like(fcar_scr)

    x = x_ref[0]
    shift = mod_ref[0, 0:1, :]
    scale = mod_ref[0, 1:2, :]
    ms = jnp.mean(x * x, axis=-1, keepdims=True)
    u = (x * lax.rsqrt(ms + NORM_EPS) * g_ref[...]) * (1.0 + scale) + shift
    u = u.astype(BF16)

    p = _dot(u, w_ref[:, :rw_cols])
    row = lax.broadcasted_iota(jnp.int32, p.shape, 0)
    prev = jnp.where(row == 0, prev_scr[0:1, :], pltpu.roll(p, shift=1, axis=0))
    prev_scr[0:1, :] = p[tm - 1:tm, :]
    rw_ref[0] = p + (prev - p) * mu_ref[...]

    pf = _dot(u, w_ref[:, rw_cols:])
    z = p[:, rw_cols - LANES:] + bf_ref[...]
    logf = jnp.minimum(z, 0.0) - jnp.log(1.0 + jnp.exp(-jnp.abs(z)))
    sub = min(tm, MXU_TILE)
    r2 = lax.broadcasted_iota(jnp.int32, (sub, sub), 0)
    c2 = lax.broadcasted_iota(jnp.int32, (sub, sub), 1)
    tril = (c2 <= r2).astype(BF16)
    f_hi = logf.astype(BF16)
    r1 = logf - f_hi.astype(F32)
    f_mid = r1.astype(BF16)
    f_lo = (r1 - f_mid.astype(F32)).astype(BF16)
    carry = fcar_scr[0:1, :]
    fcums = []
    for i in range(tm // sub):
        rs = slice(i * sub, (i + 1) * sub)
        fcums.append((_dot(tril, f_hi[rs]) + _dot(tril, f_mid[rs])) + _dot(tril, f_lo[rs]) + carry)
        carry = fcums[-1][sub - 1:sub, :]
    fcar_scr[0:1, :] = carry
    _fox_layout(pf, jnp.concatenate(fcums, axis=0), f_off, qg_ref, kg_ref, sel_ref, bias_ref,
                qa_ref, ka_ref, v_ref, og_ref)


def _inproj(x, mod3, norm1_g, w_all, mu_p, bf_p, q_norm_g, k_norm_g, rw_cols, f_off, tm=512):
    B, T, D = x.shape
    ncols = w_all.shape[1]
    width = (ncols - rw_cols) // 4
    heads = width // HEAD_DIM
    sel, bias = _fox_aug_tables(heads, f_off)
    scale = LOG2E / math.sqrt(HEAD_DIM)
    qg2 = jnp.tile(q_norm_g * scale, 2).reshape(1, LANES)
    kg2 = jnp.tile(k_norm_g, 2).reshape(1, LANES)
    return pl.pallas_call(
        functools.partial(_inproj_kernel, rw_cols=rw_cols, f_off=f_off),
        out_shape=(jax.ShapeDtypeStruct((B, T, rw_cols), F32),
                   jax.ShapeDtypeStruct((B, heads, T, LANES), BF16),
                   jax.ShapeDtypeStruct((B, heads, T, LANES), BF16),
                   jax.ShapeDtypeStruct((B, heads, T // tm, V_ROWS, tm), BF16),
                   jax.ShapeDtypeStruct((B, T, width), BF16)),
        grid=(B, T // tm),
        in_specs=[pl.BlockSpec((1, tm, D), lambda b, t: (b, t, 0)),
                  pl.BlockSpec((1, N_MOD, D), lambda b, t: (b, 0, 0)),
                  _const_spec((1, D)),
                  _const_spec((D, ncols), single_buffer=True),
                  _const_spec((1, rw_cols)),
                  _const_spec((1, LANES)),
                  _const_spec((1, LANES)), _const_spec((1, LANES)),
                  _const_spec(sel.shape, single_buffer=True), _const_spec(bias.shape)],
        out_specs=(pl.BlockSpec((1, tm, rw_cols), lambda b, t: (b, t, 0)),
                   pl.BlockSpec((1, heads, tm, LANES), lambda b, t: (b, 0, t, 0)),
                   pl.BlockSpec((1, heads, tm, LANES), lambda b, t: (b, 0, t, 0)),
                   pl.BlockSpec((1, heads, 1, V_ROWS, tm), lambda b, t: (b, 0, t, 0, 0)),
                   pl.BlockSpec((1, tm, width), lambda b, t: (b, t, 0))),
        scratch_shapes=[pltpu.VMEM((8, rw_cols), F32), pltpu.VMEM((8, LANES), F32)],
        compiler_params=pltpu.CompilerParams(
            dimension_semantics=("arbitrary", "arbitrary"), vmem_limit_bytes=VMEM_LIMIT),
        name="inproj",
    )(x, mod3, norm1_g.reshape(1, D), w_all, mu_p, bf_p, qg2, kg2, sel, bias)


def _segsum(x, bd, split=False):
    tile = bd.shape[0]
    hi = x.astype(BF16)
    lo = (x - hi.astype(F32)).astype(BF16) if split else None
    cols = []
    for j in range(x.shape[1] // tile):
        sl = slice(j * tile, (j + 1) * tile)
        s = _dot(hi[:, sl], bd)
        cols.append(s + _dot(lo[:, sl], bd) if split else s)
    return jnp.concatenate(cols, axis=1)


def _interleave(generators):
    live = [g for g in generators if g is not None]
    while live:
        for g in list(live):
            if next(g, StopIteration) is StopIteration:
                live.remove(g)


def _rwkv_kernel(p_ref, w0_ref, wdw_ref, a0_ref, waw_ref, wgw_ref, kk_ref, ka_ref, rk_ref,
                 gng_ref, gnb_ref, o_ref, s_scr, *, width, block_rows):
    t = pl.program_id(1)
    L = RW_CHUNK
    R = block_rows
    n_blocks = p_ref.shape[1] // R
    n_chunks = R // L
    W = width
    G2 = 2 * LANES
    groups = W // LANES
    items = [(c, j) for c in range(n_chunks) for j in range(groups)]
    rsl = lambda c: slice(c * L, (c + 1) * L)
    gsl = lambda j: slice(j * LANES, (j + 1) * LANES)

    @pl.when(t == 0)
    def _():
        s_scr[...] = jnp.zeros_like(s_scr)

    ri = lax.broadcasted_iota(jnp.int32, (MXU_TILE, MXU_TILE), 0)
    ci = lax.broadcasted_iota(jnp.int32, (MXU_TILE, MXU_TILE), 1)
    bd = ((ri // HEAD_DIM) == (ci // HEAD_DIM)).astype(BF16)
    tril = (lax.broadcasted_iota(jnp.int32, (L, L), 1)
            <= lax.broadcasted_iota(jnp.int32, (L, L), 0)).astype(BF16)
    o0 = 3 * W
    lane1 = lax.broadcasted_iota(jnp.int32, (L, LANES), 1)
    lane2 = lax.broadcasted_iota(jnp.int32, (L, G2), 1)
    row2 = lax.broadcasted_iota(jnp.int32, (L, G2), 0)
    first1 = lane1 < HEAD_DIM
    first2 = (lane2 % LANES) < HEAD_DIM
    strict2 = (lane2 % HEAD_DIM) < row2
    incl2 = (lane2 % HEAD_DIM) <= row2
    upper2 = lane2 >= LANES
    eye2 = ((lane1 % HEAD_DIM) == lax.broadcasted_iota(jnp.int32, (L, LANES), 0)).astype(F32)
    zero1 = jnp.zeros((L, LANES), BF16)
    rr = lax.broadcasted_iota(jnp.int32, (LANES, LANES), 0) < HEAD_DIM
    cc = lax.broadcasted_iota(jnp.int32, (LANES, LANES), 1) < HEAD_DIM
    diag_blocks = rr == cc

    def split(m, mask):
        return jnp.concatenate([jnp.where(mask, m, 0), jnp.where(mask, 0, m)], axis=0)

    def prepare(pb, b):
        p = p_ref[0, b * R:(b + 1) * R, :]
        r, k, v = p[:, 0:W], p[:, W:2 * W], p[:, 2 * W:3 * W]
        wd = p[:, o0:o0 + LANES]
        ad = p[:, o0 + LANES:o0 + 2 * LANES]
        gd = p[:, o0 + 2 * LANES:o0 + 4 * LANES]
        z = w0_ref[...] + _dot(_bf(jnp.tanh(wd)), wdw_ref[...])
        softplus = jnp.maximum(-z, 0.0) + jnp.log(1.0 + jnp.exp(-jnp.abs(z)))
        ld = -jnp.exp(-softplus - 0.5)
        yield
        a_sig = _sigmoid(a0_ref[...] + _dot(_bf(ad), waw_ref[...]))
        pb["gate"] = _dot(_bf(_sigmoid(gd)), wgw_ref[...])
        yield
        kk = k * kk_ref[...]
        kk = kk / jnp.maximum(jnp.sqrt(_segsum(kk * kk, bd, split=True)), 1e-12)
        yield
        k2 = k * (1.0 + (a_sig - 1.0) * ka_ref[...])
        b_ = kk * a_sig
        pb["bonus"] = _segsum(r * k2 * rk_ref[...], bd) * v
        yield
        ld_hi = _bf(ld)
        ld_lo = _bf(ld - ld_hi.astype(F32))
        cums = [_dot(tril, ld_hi[rsl(c)]) + _dot(tril, ld_lo[rsl(c)]) for c in range(n_chunks)]
        pb["tots"] = [cum_c[L - 1:L, :] for cum_c in cums]
        cum = jnp.concatenate(cums, axis=0)
        tot_b = jnp.concatenate([jnp.broadcast_to(tc, (L, W)) for tc in pb["tots"]], axis=0)
        yield
        e_neg = jnp.exp(-cum)
        rt = r * jnp.exp(cum)
        pb.update(rt=rt, rt_m=_bf(rt), at=_bf(-kk * jnp.exp(cum - ld)), bt=_bf(b_ * e_neg),
                  kt=_bf(k2 * e_neg))
        yield
        e_rem = jnp.exp(tot_b - cum)
        pb.update(bh=_bf(b_ * e_rem), kh=_bf(k2 * e_rem), vm=_bf(v))

    def chains(pb, cb):
        at, rt_m, bt, kt, vm = pb["at"], pb["rt_m"], pb["bt"], pb["kt"], pb["vm"]
        g_bot, akm, zz = {}, {}, {}
        for c, j in items:
            ar = jnp.concatenate([at[rsl(c), gsl(j)], rt_m[rsl(c), gsl(j)]], axis=0)
            rhs = jnp.concatenate([split(bt[rsl(c), gsl(j)], first1),
                                   split(kt[rsl(c), gsl(j)], first1)], axis=0)
            gm = _dot_nt(ar, rhs)
            top = jnp.where(strict2, gm[:L], 0.0)
            g_bot[c, j] = _bf(jnp.where(incl2, gm[L:], 0.0))
            akm[c, j] = _bf(top[:, LANES:])
            zz[c, j] = jnp.concatenate([top[:, :LANES], eye2], axis=1)
        yield
        for _ in range(int(math.log2(L))):
            for it in items:
                zb = _bf(zz[it])
                out = _dot(zb[:, :LANES], split(zb, first2))
                zz[it] = out + jnp.where(upper2, zz[it], 0.0)
            yield
        vbd = {(c, j): split(vm[rsl(c), gsl(j)], first1) for c, j in items}
        akv = {it: _bf(_dot(akm[it], vbd[it])) for it in items}
        yield
        x = {}
        for c, j in items:
            zed = jnp.concatenate([at[rsl(c), gsl(j)], akv[c, j]], axis=1)
            x[c, j] = _bf(_dot(_bf(zz[c, j][:, LANES:]), split(zed, first2)))
        yield
        cb.update(r_eff={}, y0={}, mt={}, nt={})
        for c, j in items:
            w4 = jnp.concatenate([split(x[c, j], first2),
                                  jnp.concatenate([jnp.concatenate([zero1, zero1], axis=0),
                                                   vbd[c, j]], axis=1)], axis=0)
            out = _dot(g_bot[c, j], w4)
            cb["r_eff"][c, j] = _bf(pb["rt"][rsl(c), gsl(j)] + out[:, :LANES])
            cb["y0"][c, j] = out[:, LANES:]
        yield
        for c, j in items:
            lhs = jnp.concatenate([pb["bh"][rsl(c), gsl(j)], pb["kh"][rsl(c), gsl(j)]], axis=0)
            rhs = jnp.concatenate([x[c, j], jnp.concatenate([zero1, vm[rsl(c), gsl(j)]], axis=1)],
                                  axis=0)
            pt = _dot_tn(lhs, rhs)
            cb["mt"][c, j] = _bf(jnp.where(diag_blocks, pt[:, :LANES], 0.0))
            cb["nt"][c, j] = jnp.where(diag_blocks, pt[:, LANES:], 0.0)
        yield
        cb["e_cols"] = [
            jnp.concatenate([jnp.exp(pb["tots"][c][:, gsl(j)]) for c in range(n_chunks)]
                            + [jnp.zeros((LANES - n_chunks, LANES), F32)], axis=0).T
            for j in range(groups)]

    state = [s_scr[j] for j in range(groups)]

    def tail(pb, cb, b):
        y_rows = []
        for c in range(n_chunks):
            hb = [_bf(s) for s in state]
            y_rows.append(jnp.concatenate(
                [_dot(cb["r_eff"][c, j], hb[j]) + cb["y0"][c, j] for j in range(groups)], axis=1))
            for j in range(groups):
                state[j] = (state[j] * cb["e_cols"][j][:, c:c + 1] + _dot(cb["mt"][c, j], hb[j])
                            + cb["nt"][c, j])
            yield
        y = jnp.concatenate(y_rows, axis=0)
        inv_n = 1.0 / HEAD_DIM
        mean = _segsum(y, bd) * inv_n
        d = y - mean
        yield
        var = _segsum(d * d, bd) * inv_n
        yn = d * lax.rsqrt(var + GN_EPS) * gng_ref[...] + gnb_ref[...]
        o_ref[0, b * R:(b + 1) * R, :] = ((yn + pb["bonus"]) * pb["gate"]).astype(o_ref.dtype)

    prep = [dict() for _ in range(n_blocks)]
    chain = [dict() for _ in range(n_blocks)]
    _interleave([prepare(prep[0], 0)])
    for b in range(n_blocks):
        _interleave([chains(prep[b], chain[b]),
                     prepare(prep[b + 1], b + 1) if b + 1 < n_blocks else None,
                     tail(prep[b - 1], chain[b - 1], b - 1) if b > 0 else None])
    _interleave([tail(prep[-1], chain[-1], n_blocks - 1)])
    for j in range(groups):
        s_scr[j] = state[j]


def _rwkv(rw, w0, wdw, a0, waw, wgw, k_k, k_a, r_k, gn_g, gn_b, width, rows=512, block_rows=256):
    B, T, cols = rw.shape
    row = lambda a: a.reshape(1, width)
    vec = _const_spec((1, width))
    return pl.pallas_call(
        functools.partial(_rwkv_kernel, width=width, block_rows=block_rows),
        out_shape=jax.ShapeDtypeStruct((B, T, width), BF16),
        grid=(B, T // rows),
        in_specs=[pl.BlockSpec((1, rows, cols), lambda b, t: (b, t, 0)),
                  vec, _const_spec(wdw.shape), vec, _const_spec(waw.shape),
                  _const_spec(wgw.shape), vec, vec, vec, vec, vec],
        out_specs=pl.BlockSpec((1, rows, width), lambda b, t: (b, t, 0)),
        scratch_shapes=[pltpu.VMEM((width // LANES, LANES, LANES), F32)],
        compiler_params=pltpu.CompilerParams(
            dimension_semantics=("arbitrary", "arbitrary"), vmem_limit_bytes=VMEM_LIMIT),
        name="rwkv",
    )(rw, row(w0), wdw, row(a0), waw, wgw, row(k_k), row(k_a), row(r_k), row(gn_g), row(gn_b))


def _fox_kernel(qtab, ktab, qa_ref, ka_ref, vt_ref, og_ref, g_ref, o_ref,
                m_all, acc_all, s_scr, p_scr, al_scr, *, blk, cb, n_off, hps):
    n_q = qa_ref.shape[2] // blk
    n_el = n_off + n_q
    chains = [(hh, c) for hh in range(hps) for c in range(blk // cb)]
    csl = lambda c: slice(c * cb, (c + 1) * cb)

    m_all[...] = jnp.full_like(m_all, NEG_INF)
    acc_all[...] = jnp.zeros_like(acc_all)

    def element(t):
        if isinstance(t, int) and t >= n_off:
            return t - n_off, t - n_off, True
        return qtab[t], ktab[t], False

    def start(i, size):
        return i * size if isinstance(i, int) else pl.multiple_of(i * size, size)

    def n_keys(c, diag):
        return (c + 1) * cb if diag else blk

    def logits(t, par):
        qi, ki, diag = element(t)
        for hh, c in chains:
            nk = n_keys(c, diag)
            s_scr[par, hh, :nk, csl(c)] = _dot_nt(
                ka_ref[0, hh, pl.ds(start(ki, blk), nk), :],
                qa_ref[0, hh, pl.ds(start(qi, blk) + c * cb, cb), :])
            yield

    def softmax(t, par):
        qi, _, diag = element(t)
        for hh, c in chains:
            nk = n_keys(c, diag)
            s = s_scr[par, hh, :nk, csl(c)]
            if diag:
                key = lax.broadcasted_iota(jnp.int32, (nk, cb), 0)
                qry = c * cb + lax.broadcasted_iota(jnp.int32, (nk, cb), 1)
                s = jnp.where(key <= qry, s, NEG_INF)
            m_prev = m_all[qi, hh, :, csl(c)]
            m_new = jnp.maximum(m_prev, jnp.max(s, axis=0, keepdims=True))
            m_all[qi, hh, :, csl(c)] = m_new
            al_scr[par, hh, :, csl(c)] = jnp.exp2(m_prev - m_new)
            p_scr[par, hh, :nk, csl(c)] = jnp.exp2(s - m_new).astype(BF16)
            yield

    def values(t, par):
        qi, ki, diag = element(t)
        for hh, c in chains:
            nk = n_keys(c, diag)
            pv = _dot(vt_ref[0, hh, ki, :, :nk], p_scr[par, hh, :nk, csl(c)])
            acc_all[qi, hh, :, csl(c)] = (al_scr[par, hh, :, csl(c)] * acc_all[qi, hh, :, csl(c)]
                                          + pv)
            yield
        if diag:
            rows = slice(qi * blk, (qi + 1) * blk)
            for pr in range(hps // 2):
                normed = []
                for hh in (2 * pr, 2 * pr + 1):
                    acc = acc_all[qi, hh]
                    o = acc[:HEAD_DIM] / acc[HEAD_DIM:HEAD_DIM + 1]
                    ms = jnp.mean(o * o, axis=0, keepdims=True)
                    normed.append(o * lax.rsqrt(ms + NORM_EPS))
                on = jnp.concatenate(normed, axis=0).T
                ls = slice(pr * LANES, (pr + 1) * LANES)
                o_ref[0, rows, ls] = (on * g_ref[:, ls]
                                      * _sigmoid(og_ref[0, rows, ls].astype(F32))).astype(o_ref.dtype)

    def step(t, par):
        _interleave([stage(t - lag, (par + lag) % 2)
                     for lag, stage in enumerate((logits, softmax, values))
                     if not isinstance(t, int) or 0 <= t - lag < n_el])

    def two_steps(u, carry):
        step(2 * u + 2, 0)
        step(2 * u + 3, 1)
        return carry

    step(0, 0)
    step(1, 1)
    lax.fori_loop(0, (n_off - 2) // 2, two_steps, 0)
    for t in range(n_off, n_el + 2):
        step(t, t % 2)


def _fox(qa, ka, vt, og, fox_out_g, width, cb=256, hps=4):
    B, H, T, _ = qa.shape
    blk = vt.shape[4]
    n_q = T // blk
    pairs = [(qi, ki) for qi in range(n_q) for ki in range(qi)]
    n_off = len(pairs)
    assert n_off >= 2 and n_off % 2 == 0, "pipeline loop is unrolled by two"
    qtab = jnp.asarray(np.array([p[0] for p in pairs], np.int32))
    ktab = jnp.asarray(np.array([p[1] for p in pairs], np.int32))
    grid_spec = pltpu.PrefetchScalarGridSpec(
        num_scalar_prefetch=2,
        grid=(B, H // hps),
        in_specs=[pl.BlockSpec((1, hps, T, LANES), lambda b, h, qt, kt: (b, h, 0, 0)),
                  pl.BlockSpec((1, hps, T, LANES), lambda b, h, qt, kt: (b, h, 0, 0)),
                  pl.BlockSpec((1, hps, n_q, V_ROWS, blk), lambda b, h, qt, kt: (b, h, 0, 0, 0)),
                  pl.BlockSpec((1, T, hps * HEAD_DIM), lambda b, h, qt, kt: (b, 0, h)),
                  pl.BlockSpec((1, hps * HEAD_DIM), lambda b, h, qt, kt: (0, h))],
        out_specs=pl.BlockSpec((1, T, hps * HEAD_DIM), lambda b, h, qt, kt: (b, 0, h)),
        scratch_shapes=[pltpu.VMEM((n_q, hps, 1, blk), F32),
                        pltpu.VMEM((n_q, hps, V_ROWS, blk), F32),
                        pltpu.VMEM((2, hps, blk, blk), F32), pltpu.VMEM((2, hps, blk, blk), BF16),
                        pltpu.VMEM((2, hps, 1, blk), F32)])
    return pl.pallas_call(
        functools.partial(_fox_kernel, blk=blk, cb=cb, n_off=n_off, hps=hps),
        out_shape=jax.ShapeDtypeStruct((B, T, width), BF16),
        grid_spec=grid_spec,
        compiler_params=pltpu.CompilerParams(
            dimension_semantics=("arbitrary",) * 2, vmem_limit_bytes=VMEM_LIMIT),
        name="fox",
    )(qtab, ktab, qa, ka, vt, og, fox_out_g.reshape(1, width))


def _outmlp_kernel(x_ref, yrw_ref, yfx_ref, mod_ref, wo_ref, g2_ref, w1_ref, w2_ref, gf_ref,
                   o_ref, *, ff_tile):
    x = x_ref[0]
    half = yrw_ref.shape[2]
    gate1 = mod_ref[0, 2:3, :]
    shift2 = mod_ref[0, 3:4, :]
    scale2 = mod_ref[0, 4:5, :]
    gate2 = mod_ref[0, 5:6, :]
    y = _dot(yrw_ref[0], wo_ref[:half, :]) + _dot(yfx_ref[0], wo_ref[half:, :])
    h1 = x + gate1 * y
    ms = jnp.mean(h1 * h1, axis=-1, keepdims=True)
    u = ((h1 * lax.rsqrt(ms + NORM_EPS) * g2_ref[...]) * (1.0 + scale2) + shift2).astype(BF16)
    acc = jnp.zeros_like(x)
    for j in range(w1_ref.shape[1] // ff_tile):
        hid = jnp.maximum(_dot(u, w1_ref[:, j * ff_tile:(j + 1) * ff_tile]), 0.0)
        acc = acc + _dot((hid * hid).astype(BF16), w2_ref[j * ff_tile:(j + 1) * ff_tile, :])
    h2 = h1 + gate2 * acc
    ms2 = jnp.mean(h2 * h2, axis=-1, keepdims=True)
    o_ref[0] = h2 * lax.rsqrt(ms2 + NORM_EPS) * gf_ref[...]


def _outmlp(x, y_rw, y_fx, mod3, w_o, norm2_g, w1, w2, final_g, tm=512, ff_tile=1024):
    B, T, D = x.shape
    half = y_rw.shape[2]
    return pl.pallas_call(
        functools.partial(_outmlp_kernel, ff_tile=ff_tile),
        out_shape=jax.ShapeDtypeStruct((B, T, D), F32),
        grid=(B, T // tm),
        in_specs=[pl.BlockSpec((1, tm, D), lambda b, t: (b, t, 0)),
                  pl.BlockSpec((1, tm, half), lambda b, t: (b, t, 0)),
                  pl.BlockSpec((1, tm, half), lambda b, t: (b, t, 0)),
                  pl.BlockSpec((1, N_MOD, D), lambda b, t: (b, 0, 0)),
                  _const_spec(w_o.shape, single_buffer=True), _const_spec((1, D)),
                  _const_spec(w1.shape, single_buffer=True),
                  _const_spec(w2.shape, single_buffer=True), _const_spec((1, D))],
        out_specs=pl.BlockSpec((1, tm, D), lambda b, t: (b, t, 0)),
        compiler_params=pltpu.CompilerParams(
            dimension_semantics=("arbitrary", "arbitrary"), vmem_limit_bytes=VMEM_LIMIT),
        name="outmlp",
    )(x, y_rw, y_fx, mod3, w_o, norm2_g.reshape(1, D), w1, w2, final_g.reshape(1, D))


def _pad_cols(a, n):
    return jnp.pad(a, ((0, 0), (0, n - a.shape[1])))


def _pad_rows(a, n):
    return jnp.pad(a, ((0, n - a.shape[0]), (0, 0)))


def _branches(x, c, w_ada, b_ada, norm1_g, w_in, mu_shift, w0, w_up_decay, a0, w_up_a, w_up_g,
              k_k, k_a, r_k, gn_g, gn_b, b_f, q_norm_g, k_norm_g, fox_out_g):
    B, T, D = x.shape
    W = w0.shape[0]
    heads = b_f.shape[0]
    rw_n = 3 * W + DECAY_LORA + AAA_LORA + GATE_LORA

    def regroup(a):
        rw, fx = a[:, :rw_n], a[:, rw_n:]
        o = 3 * W
        return jnp.concatenate([
            rw[:, :o],
            _pad_cols(rw[:, o:o + DECAY_LORA], LANES),
            _pad_cols(rw[:, o + DECAY_LORA:o + DECAY_LORA + AAA_LORA], LANES),
            _pad_cols(jnp.concatenate([rw[:, o + DECAY_LORA + AAA_LORA:], fx[:, 4 * W:]], axis=1),
                      2 * LANES),
            fx[:, :4 * W]], axis=1)

    rw_cols = 3 * W + 4 * LANES
    w_all = regroup(w_in.astype(BF16))
    mu_p = regroup(jnp.pad(mu_shift.reshape(1, rw_n), ((0, 0), (0, w_in.shape[1] - rw_n))))[:, :rw_cols]
    f_off = GATE_LORA - LANES
    bf_p = jnp.pad(b_f.reshape(1, heads), ((0, 0), (f_off, LANES - f_off - heads)))
    wdw = _pad_rows(w_up_decay, LANES).astype(BF16)
    waw = _pad_rows(w_up_a, LANES).astype(BF16)
    wgw = _pad_rows(w_up_g, 2 * LANES).astype(BF16)

    mod3 = _ada(c, w_ada, b_ada).reshape(B, N_MOD, D)
    rw, qa, ka, vt, og = _inproj(x, mod3, norm1_g, w_all, mu_p, bf_p, q_norm_g, k_norm_g,
                                 rw_cols, f_off)
    y_rw = _rwkv(rw, w0, wdw, a0, waw, wgw, k_k, k_a, r_k.reshape(-1), gn_g, gn_b, W)
    y_fx = _fox(qa, ka, vt, og, fox_out_g, W)
    return y_rw, y_fx, mod3


def kernel(x, c, w_ada, b_ada, norm1_g, w_in, mu_shift, w0, w_up_decay, a0, w_up_a, w_up_g,
           k_k, k_a, r_k, gn_g, gn_b, b_f, q_norm_g, k_norm_g, fox_out_g, w_o, norm2_g,
           w_mlp1, w_mlp2, final_g):
    y_rw, y_fx, mod3 = _branches(x, c, w_ada, b_ada, norm1_g, w_in, mu_shift, w0, w_up_decay,
                                 a0, w_up_a, w_up_g, k_k, k_a, r_k, gn_g, gn_b, b_f,
                                 q_norm_g, k_norm_g, fox_out_g)
    return _outmlp(x, y_rw, y_fx, mod3, w_o.astype(BF16), norm2_g,
                   w_mlp1.astype(BF16), w_mlp2.astype(BF16), final_g)
```

```python
import functools
import math

import jax
import jax.numpy as jnp
import numpy as np
from jax import lax
from jax.experimental import pallas as pl
from jax.experimental.pallas import tpu as pltpu

F32 = jnp.float32
BF16 = jnp.bfloat16

HEAD_DIM = 64
LANES = 128
MXU_TILE = 256
NORM_EPS = 1e-6
GN_EPS = 64e-5
NEG_INF = -1e30
LOG2E = 1.4426950408889634
N_MOD = 6
DECAY_LORA = 64
AAA_LORA = 64
GATE_LORA = 160
RW_CHUNK = 64
V_ROWS = HEAD_DIM + 16
VMEM_LIMIT = 56 * 1024 * 1024


def _dot(a, b):
    return jnp.dot(a, b, preferred_element_type=F32)


def _dot_nt(a, b):
    return lax.dot_general(a, b, (((1,), (1,)), ((), ())), preferred_element_type=F32)


def _dot_tn(a, b):
    return lax.dot_general(a, b, (((0,), (0,)), ((), ())), preferred_element_type=F32)


def _bf(x):
    return x.astype(BF16)


def _sigmoid(x):
    return 1.0 / (1.0 + jnp.exp(-x))


def _const_spec(shape, single_buffer=False):
    n = len(shape)
    mode = pl.Buffered(1) if single_buffer else None
    return pl.BlockSpec(shape, lambda *_: (0,) * n, pipeline_mode=mode)


def _ada_kernel(c_ref, w_ref, b_ref, o_ref):
    c = c_ref[...]
    a = c * _sigmoid(c)
    w = w_ref[...]
    a_hi, w_hi = _bf(a), _bf(w)
    a_lo, w_lo = _bf(a - a_hi.astype(F32)), _bf(w - w_hi.astype(F32))
    o_ref[...] = (_dot(a_hi, w_hi) + (_dot(a_hi, w_lo) + _dot(a_lo, w_hi))) + b_ref[...]


def _ada(c, w_ada, b_ada, tn=2048):
    B, D = c.shape
    n = w_ada.shape[1]
    return pl.pallas_call(
        _ada_kernel,
        out_shape=jax.ShapeDtypeStruct((B, n), F32),
        grid=(n // tn,),
        in_specs=[pl.BlockSpec((B, D), lambda j: (0, 0)),
                  pl.BlockSpec((D, tn), lambda j: (0, j)),
                  pl.BlockSpec((1, tn), lambda j: (0, j))],
        out_specs=pl.BlockSpec((B, tn), lambda j: (0, j)),
        name="ada",
    )(c, w_ada, b_ada.reshape(1, n))


N_PIECES = 3


def _fox_aug_tables(heads, f_off):
    groups = heads // 2
    sel = np.zeros((LANES, 2 * groups * LANES), np.float32)
    bias = np.zeros((1, 2 * groups * LANES), np.float32)
    for h in range(heads):
        base = (h // 2) * LANES + (HEAD_DIM if h % 2 == 0 else 0)
        kbase = groups * LANES + base
        for piece in range(N_PIECES):
            sel[f_off + piece * heads + h, base + piece] = 1.0
            sel[f_off + piece * heads + h, kbase + N_PIECES + piece] = -1.0
            bias[0, base + N_PIECES + piece] = 1.0
            bias[0, kbase + piece] = 1.0
    return jnp.asarray(sel, BF16), jnp.asarray(bias)


def _fox_layout(pf, fcum, f_off, qg_ref, kg_ref, sel_ref, bias_ref, qa_ref, ka_ref, v_ref, og_ref):
    W = pf.shape[1] // 4
    heads = W // HEAD_DIM
    groups = W // LANES
    tm = pf.shape[0]
    lane = lax.broadcasted_iota(jnp.int32, (tm, LANES), 1)
    lane_f = lane - f_off
    fcum = fcum * LOG2E
    f_hi = fcum.astype(BF16).astype(F32)
    r1 = fcum - f_hi
    f_mid = r1.astype(BF16).astype(F32)
    f_lo = r1 - f_mid
    packed = jnp.where(lane_f < 0, 0.0, jnp.where(lane_f < heads, f_hi, jnp.where(
        lane_f < 2 * heads, pltpu.roll(f_mid, heads, axis=1), jnp.where(
            lane_f < 3 * heads, pltpu.roll(f_lo, 2 * heads, axis=1), 0.0))))
    aug = _dot(packed.astype(BF16), sel_ref[...]) + bias_ref[...]
    first = lane < HEAD_DIM
    for side, (gain_ref, out_ref) in enumerate(((qg_ref, qa_ref), (kg_ref, ka_ref))):
        for g in range(groups):
            x = pf[:, side * W + g * LANES:side * W + (g + 1) * LANES]
            sq = x * x
            lo = jnp.sum(jnp.where(first, sq, 0.0), axis=-1, keepdims=True)
            hi = jnp.sum(jnp.where(first, 0.0, sq), axis=-1, keepdims=True)
            ms = jnp.where(first, lo, hi) * (1.0 / HEAD_DIM)
            xn = x * lax.rsqrt(ms + NORM_EPS) * gain_ref[...]
            a = aug[:, (side * groups + g) * LANES:(side * groups + g + 1) * LANES]
            out_ref[0, 2 * g] = jnp.where(first, xn, a).astype(BF16)
            out_ref[0, 2 * g + 1] = jnp.where(first, a, xn).astype(BF16)
    pad = (lax.broadcasted_iota(jnp.int32, (V_ROWS - HEAD_DIM, tm), 0) == 0).astype(BF16)
    for g in range(groups):
        vt = pf[:, 2 * W + g * LANES:2 * W + (g + 1) * LANES].T.astype(BF16)
        for hh in range(2):
            v_ref[0, 2 * g + hh, 0, :HEAD_DIM, :] = vt[hh * HEAD_DIM:(hh + 1) * HEAD_DIM]
            v_ref[0, 2 * g + hh, 0, HEAD_DIM:, :] = pad
    og_ref[0] = pf[:, 3 * W:4 * W].astype(BF16)


def _inproj_kernel(x_ref, mod_ref, g_ref, w_ref, mu_ref, bf_ref, qg_ref, kg_ref, sel_ref, bias_ref,
                   rw_ref, qa_ref, ka_ref, v_ref, og_ref, prev_scr, fcar_scr, *, rw_cols, f_off):
    t = pl.program_id(1)
    tm = x_ref.shape[1]

    @pl.when(t == 0)
    def _():
        prev_scr[...] = jnp.zeros_like(prev_scr)
        fcar_scr[...] = jnp.zeros_like(fcar_scr)

    x = x_ref[0]
    shift = mod_ref[0, 0:1, :]
    scale = mod_ref[0, 1:2, :]
    ms = jnp.mean(x * x, axis=-1, keepdims=True)
    u = (x * lax.rsqrt(ms + NORM_EPS) * g_ref[...]) * (1.0 + scale) + shift
    u = u.astype(BF16)

    p = _dot(u, w_ref[:, :rw_cols])
    row = lax.broadcasted_iota(jnp.int32, p.shape, 0)
    prev = jnp.where(row == 0, prev_scr[0:1, :], pltpu.roll(p, shift=1, axis=0))
    prev_scr[0:1, :] = p[tm - 1:tm, :]
    rw_ref[0] = p + (prev - p) * mu_ref[...]

    pf = _dot(u, w_ref[:, rw_cols:])
    z = p[:, rw_cols - LANES:] + bf_ref[...]
    logf = jnp.minimum(z, 0.0) - jnp.log(1.0 + jnp.exp(-jnp.abs(z)))
    sub = min(tm, MXU_TILE)
    r2 = lax.broadcasted_iota(jnp.int32, (sub, sub), 0)
    c2 = lax.broadcasted_iota(jnp.int32, (sub, sub), 1)
    tril = (c2 <= r2).astype(BF16)
    f_hi = logf.astype(BF16)
    r1 = logf - f_hi.astype(F32)
    f_mid = r1.astype(BF16)
    f_lo = (r1 - f_mid.astype(F32)).astype(BF16)
    carry = fcar_scr[0:1, :]
    fcums = []
    for i in range(tm // sub):
        rs = slice(i * sub, (i + 1) * sub)
        fcums.append((_dot(tril, f_hi[rs]) + _dot(tril, f_mid[rs])) + _dot(tril, f_lo[rs]) + carry)
        carry = fcums[-1][sub - 1:sub, :]
    fcar_scr[0:1, :] = carry
    _fox_layout(pf, jnp.concatenate(fcums, axis=0), f_off, qg_ref, kg_ref, sel_ref, bias_ref,
                qa_ref, ka_ref, v_ref, og_ref)


def _inproj(x, mod3, norm1_g, w_all, mu_p, bf_p, q_norm_g, k_norm_g, rw_cols, f_off, tm=512):
    B, T, D = x.shape
    ncols = w_all.shape[1]
    width = (ncols - rw_cols) // 4
    heads = width // HEAD_DIM
    sel, bias = _fox_aug_tables(heads, f_off)
    scale = LOG2E / math.sqrt(HEAD_DIM)
    qg2 = jnp.tile(q_norm_g * scale, 2).reshape(1, LANES)
    kg2 = jnp.tile(k_norm_g, 2).reshape(1, LANES)
    return pl.pallas_call(
        functools.partial(_inproj_kernel, rw_cols=rw_cols, f_off=f_off),
        out_shape=(jax.ShapeDtypeStruct((B, T, rw_cols), F32),
                   jax.ShapeDtypeStruct((B, heads, T, LANES), BF16),
                   jax.ShapeDtypeStruct((B, heads, T, LANES), BF16),
                   jax.ShapeDtypeStruct((B, heads, T // tm, V_ROWS, tm), BF16),
                   jax.ShapeDtypeStruct((B, T, width), BF16)),
        grid=(B, T // tm),
        in_specs=[pl.BlockSpec((1, tm, D), lambda b, t: (b, t, 0)),
                  pl.BlockSpec((1, N_MOD, D), lambda b, t: (b, 0, 0)),
                  _const_spec((1, D)),
                  _const_spec((D, ncols), single_buffer=True),
                  _const_spec((1, rw_cols)),
                  _const_spec((1, LANES)),
                  _const_spec((1, LANES)), _const_spec((1, LANES)),
                  _const_spec(sel.shape, single_buffer=True), _const_spec(bias.shape)],
        out_specs=(pl.BlockSpec((1, tm, rw_cols), lambda b, t: (b, t, 0)),
                   pl.BlockSpec((1, heads, tm, LANES), lambda b, t: (b, 0, t, 0)),
                   pl.BlockSpec((1, heads, tm, LANES), lambda b, t: (b, 0, t, 0)),
                   pl.BlockSpec((1, heads, 1, V_ROWS, tm), lambda b, t: (b, 0, t, 0, 0)),
                   pl.BlockSpec((1, tm, width), lambda b, t: (b, t, 0))),
        scratch_shapes=[pltpu.VMEM((8, rw_cols), F32), pltpu.VMEM((8, LANES), F32)],
        compiler_params=pltpu.CompilerParams(
            dimension_semantics=("arbitrary", "arbitrary"), vmem_limit_bytes=VMEM_LIMIT),
        name="inproj",
    )(x, mod3, norm1_g.reshape(1, D), w_all, mu_p, bf_p, qg2, kg2, sel, bias)


def _segsum_lanes(x):
    first = lax.broadcasted_iota(jnp.int32, (x.shape[0], LANES), 1) < HEAD_DIM
    cols = []
    for j in range(x.shape[1] // LANES):
        xg = x[:, j * LANES:(j + 1) * LANES]
        lo = jnp.sum(jnp.where(first, xg, 0.0), axis=-1, keepdims=True)
        hi = jnp.sum(jnp.where(first, 0.0, xg), axis=-1, keepdims=True)
        cols.append(jnp.where(first, lo, hi))
    return jnp.concatenate(cols, axis=1)


def _segsum(x, bd):
    tile = bd.shape[0]
    xb = x.astype(BF16)
    return jnp.concatenate([_dot(xb[:, j * tile:(j + 1) * tile], bd)
                            for j in range(x.shape[1] // tile)], axis=1)


def _interleave(generators):
    live = [g for g in generators if g is not None]
    while live:
        for g in list(live):
            if next(g, StopIteration) is StopIteration:
                live.remove(g)


def _rwkv_kernel(p_ref, w0_ref, wdw_ref, a0_ref, waw_ref, wgw_ref, kk_ref, ka_ref, rk_ref,
                 gng_ref, gnb_ref, o_ref, s_scr, *, width, block_rows):
    t = pl.program_id(1)
    L = RW_CHUNK
    R = block_rows
    n_blocks = p_ref.shape[1] // R
    n_chunks = R // L
    W = width
    G2 = 2 * LANES
    groups = W // LANES
    items = [(c, j) for c in range(n_chunks) for j in range(groups)]
    rsl = lambda c: slice(c * L, (c + 1) * L)
    gsl = lambda j: slice(j * LANES, (j + 1) * LANES)

    @pl.when(t == 0)
    def _():
        s_scr[...] = jnp.zeros_like(s_scr)

    ri = lax.broadcasted_iota(jnp.int32, (MXU_TILE, MXU_TILE), 0)
    ci = lax.broadcasted_iota(jnp.int32, (MXU_TILE, MXU_TILE), 1)
    bd = ((ri // HEAD_DIM) == (ci // HEAD_DIM)).astype(BF16)
    tril = (lax.broadcasted_iota(jnp.int32, (L, L), 1)
            <= lax.broadcasted_iota(jnp.int32, (L, L), 0)).astype(BF16)
    o0 = 3 * W
    lane1 = lax.broadcasted_iota(jnp.int32, (L, LANES), 1)
    lane2 = lax.broadcasted_iota(jnp.int32, (L, G2), 1)
    row2 = lax.broadcasted_iota(jnp.int32, (L, G2), 0)
    first1 = lane1 < HEAD_DIM
    first2 = (lane2 % LANES) < HEAD_DIM
    strict2 = (lane2 % HEAD_DIM) < row2
    incl2 = (lane2 % HEAD_DIM) <= row2
    upper2 = lane2 >= LANES
    eye2 = ((lane1 % HEAD_DIM) == lax.broadcasted_iota(jnp.int32, (L, LANES), 0)).astype(F32)
    zero1 = jnp.zeros((L, LANES), BF16)
    rr = lax.broadcasted_iota(jnp.int32, (LANES, LANES), 0) < HEAD_DIM
    cc = lax.broadcasted_iota(jnp.int32, (LANES, LANES), 1) < HEAD_DIM
    diag_blocks = rr == cc

    def split(m, mask):
        return jnp.concatenate([jnp.where(mask, m, 0), jnp.where(mask, 0, m)], axis=0)

    def prepare(pb, b):
        p = p_ref[0, b * R:(b + 1) * R, :]
        r, k, v = p[:, 0:W], p[:, W:2 * W], p[:, 2 * W:3 * W]
        wd = p[:, o0:o0 + LANES]
        ad = p[:, o0 + LANES:o0 + 2 * LANES]
        gd = p[:, o0 + 2 * LANES:o0 + 4 * LANES]
        z = w0_ref[...] + _dot(_bf(jnp.tanh(wd)), wdw_ref[...])
        softplus = jnp.maximum(-z, 0.0) + jnp.log(1.0 + jnp.exp(-jnp.abs(z)))
        ld = -jnp.exp(-softplus - 0.5)
        yield
        a_sig = _sigmoid(a0_ref[...] + _dot(_bf(ad), waw_ref[...]))
        pb["gate"] = _dot(_bf(_sigmoid(gd)), wgw_ref[...])
        yield
        kk = k * kk_ref[...]
        kk = kk / jnp.maximum(jnp.sqrt(_segsum_lanes(kk * kk)), 1e-12)
        yield
        k2 = k * (1.0 + (a_sig - 1.0) * ka_ref[...])
        b_ = kk * a_sig
        pb["bonus"] = _segsum(r * k2 * rk_ref[...], bd) * v
        yield
        ld_hi = _bf(ld)
        ld_lo = _bf(ld - ld_hi.astype(F32))
        cums = [_dot(tril, ld_hi[rsl(c)]) + _dot(tril, ld_lo[rsl(c)]) for c in range(n_chunks)]
        pb["tots"] = [cum_c[L - 1:L, :] for cum_c in cums]
        cum = jnp.concatenate(cums, axis=0)
        tot_b = jnp.concatenate([jnp.broadcast_to(tc, (L, W)) for tc in pb["tots"]], axis=0)
        yield
        e_neg = jnp.exp(-cum)
        rt = r * jnp.exp(cum)
        pb.update(rt=rt, rt_m=_bf(rt), at=_bf(-kk * jnp.exp(cum - ld)), bt=_bf(b_ * e_neg),
                  kt=_bf(k2 * e_neg))
        yield
        e_rem = jnp.exp(tot_b - cum)
        pb.update(bh=_bf(b_ * e_rem), kh=_bf(k2 * e_rem), vm=_bf(v))

    def chains(pb, cb):
        at, rt_m, bt, kt, vm = pb["at"], pb["rt_m"], pb["bt"], pb["kt"], pb["vm"]
        g_bot, akm, zz = {}, {}, {}
        for c, j in items:
            ar = jnp.concatenate([at[rsl(c), gsl(j)], rt_m[rsl(c), gsl(j)]], axis=0)
            rhs = jnp.concatenate([split(bt[rsl(c), gsl(j)], first1),
                                   split(kt[rsl(c), gsl(j)], first1)], axis=0)
            gm = _dot_nt(ar, rhs)
            top = jnp.where(strict2, gm[:L], 0.0)
            g_bot[c, j] = _bf(jnp.where(incl2, gm[L:], 0.0))
            akm[c, j] = _bf(top[:, LANES:])
            zz[c, j] = jnp.concatenate([top[:, :LANES], eye2], axis=1)
        yield
        for _ in range(int(math.log2(L))):
            for it in items:
                zb = _bf(zz[it])
                out = _dot(zb[:, :LANES], split(zb, first2))
                zz[it] = out + jnp.where(upper2, zz[it], 0.0)
            yield
        vbd = {(c, j): split(vm[rsl(c), gsl(j)], first1) for c, j in items}
        akv = {it: _bf(_dot(akm[it], vbd[it])) for it in items}
        yield
        x = {}
        for c, j in items:
            zed = jnp.concatenate([at[rsl(c), gsl(j)], akv[c, j]], axis=1)
            x[c, j] = _bf(_dot(_bf(zz[c, j][:, LANES:]), split(zed, first2)))
        yield
        cb.update(r_eff={}, y0={}, mt={}, nt={})
        for c, j in items:
            w4 = jnp.concatenate([split(x[c, j], first2),
                                  jnp.concatenate([jnp.concatenate([zero1, zero1], axis=0),
                                                   vbd[c, j]], axis=1)], axis=0)
            out = _dot(g_bot[c, j], w4)
            cb["r_eff"][c, j] = _bf(pb["rt"][rsl(c), gsl(j)] + out[:, :LANES])
            cb["y0"][c, j] = out[:, LANES:]
        yield
        for c, j in items:
            lhs = jnp.concatenate([pb["bh"][rsl(c), gsl(j)], pb["kh"][rsl(c), gsl(j)]], axis=0)
            rhs = jnp.concatenate([x[c, j], jnp.concatenate([zero1, vm[rsl(c), gsl(j)]], axis=1)],
                                  axis=0)
            pt = _dot_tn(lhs, rhs)
            cb["mt"][c, j] = _bf(jnp.where(diag_blocks, pt[:, :LANES], 0.0))
            cb["nt"][c, j] = jnp.where(diag_blocks, pt[:, LANES:], 0.0)
        yield
        cb["e_cols"] = [
            jnp.concatenate([jnp.exp(pb["tots"][c][:, gsl(j)]) for c in range(n_chunks)]
                            + [jnp.zeros((LANES - n_chunks, LANES), F32)], axis=0).T
            for j in range(groups)]

    state = [s_scr[j] for j in range(groups)]

    def tail(pb, cb, b):
        y_rows = []
        for c in range(n_chunks):
            hb = [_bf(s) for s in state]
            y_rows.append(jnp.concatenate(
                [_dot(cb["r_eff"][c, j], hb[j]) + cb["y0"][c, j] for j in range(groups)], axis=1))
            for j in range(groups):
                state[j] = (state[j] * cb["e_cols"][j][:, c:c + 1] + _dot(cb["mt"][c, j], hb[j])
                            + cb["nt"][c, j])
            yield
        y = jnp.concatenate(y_rows, axis=0)
        inv_n = 1.0 / HEAD_DIM
        mean = _segsum_lanes(y) * inv_n
        d = y - mean
        yield
        var = _segsum_lanes(d * d) * inv_n
        yn = d * lax.rsqrt(var + GN_EPS) * gng_ref[...] + gnb_ref[...]
        o_ref[0, b * R:(b + 1) * R, :] = ((yn + pb["bonus"]) * pb["gate"]).astype(o_ref.dtype)

    prep = [dict() for _ in range(n_blocks)]
    chain = [dict() for _ in range(n_blocks)]
    _interleave([prepare(prep[0], 0)])
    for b in range(n_blocks):
        _interleave([chains(prep[b], chain[b]),
                     prepare(prep[b + 1], b + 1) if b + 1 < n_blocks else None,
                     tail(prep[b - 1], chain[b - 1], b - 1) if b > 0 else None])
    _interleave([tail(prep[-1], chain[-1], n_blocks - 1)])
    for j in range(groups):
        s_scr[j] = state[j]


def _rwkv(rw, w0, wdw, a0, waw, wgw, k_k, k_a, r_k, gn_g, gn_b, width, rows=512, block_rows=256):
    B, T, cols = rw.shape
    row = lambda a: a.reshape(1, width)
    vec = _const_spec((1, width))
    return pl.pallas_call(
        functools.partial(_rwkv_kernel, width=width, block_rows=block_rows),
        out_shape=jax.ShapeDtypeStruct((B, T, width), BF16),
        grid=(B, T // rows),
        in_specs=[pl.BlockSpec((1, rows, cols), lambda b, t: (b, t, 0)),
                  vec, _const_spec(wdw.shape), vec, _const_spec(waw.shape),
                  _const_spec(wgw.shape), vec, vec, vec, vec, vec],
        out_specs=pl.BlockSpec((1, rows, width), lambda b, t: (b, t, 0)),
        scratch_shapes=[pltpu.VMEM((width // LANES, LANES, LANES), F32)],
        compiler_params=pltpu.CompilerParams(
            dimension_semantics=("arbitrary", "arbitrary"), vmem_limit_bytes=VMEM_LIMIT),
        name="rwkv",
    )(rw, row(w0), wdw, row(a0), waw, wgw, row(k_k), row(k_a), row(r_k), row(gn_g), row(gn_b))


def _fox_kernel(qtab, ktab, qa_ref, ka_ref, vt_ref, og_ref, g_ref, o_ref,
                m_all, acc_all, s_scr, p_scr, al_scr, *, blk, cb, n_off, hps):
    n_q = qa_ref.shape[2] // blk
    n_el = n_off + n_q
    chains = [(hh, c) for hh in range(hps) for c in range(blk // cb)]
    csl = lambda c: slice(c * cb, (c + 1) * cb)

    m_all[...] = jnp.full_like(m_all, NEG_INF)
    acc_all[...] = jnp.zeros_like(acc_all)

    def element(t):
        if isinstance(t, int) and t >= n_off:
            return t - n_off, t - n_off, True
        return qtab[t], ktab[t], False

    def start(i, size):
        return i * size if isinstance(i, int) else pl.multiple_of(i * size, size)

    def n_keys(c, diag):
        return (c + 1) * cb if diag else blk

    def logits(t, par):
        qi, ki, diag = element(t)
        for hh, c in chains:
            nk = n_keys(c, diag)
            s_scr[par, hh, :nk, csl(c)] = _dot_nt(
                ka_ref[0, hh, pl.ds(start(ki, blk), nk), :],
                qa_ref[0, hh, pl.ds(start(qi, blk) + c * cb, cb), :])
            yield

    def softmax(t, par):
        qi, _, diag = element(t)
        for hh, c in chains:
            nk = n_keys(c, diag)
            s = s_scr[par, hh, :nk, csl(c)]
            if diag:
                key = lax.broadcasted_iota(jnp.int32, (nk, cb), 0)
                qry = c * cb + lax.broadcasted_iota(jnp.int32, (nk, cb), 1)
                s = jnp.where(key <= qry, s, NEG_INF)
            m_prev = m_all[qi, hh, :, csl(c)]
            m_new = jnp.maximum(m_prev, jnp.max(s, axis=0, keepdims=True))
            m_all[qi, hh, :, csl(c)] = m_new
            al_scr[par, hh, :, csl(c)] = jnp.exp2(m_prev - m_new)
            p_scr[par, hh, :nk, csl(c)] = jnp.exp2(s - m_new).astype(BF16)
            yield

    def values(t, par):
        qi, ki, diag = element(t)
        for hh, c in chains:
            nk = n_keys(c, diag)
            pv = _dot(vt_ref[0, hh, ki, :, :nk], p_scr[par, hh, :nk, csl(c)])
            acc_all[qi, hh, :, csl(c)] = (al_scr[par, hh, :, csl(c)] * acc_all[qi, hh, :, csl(c)]
                                          + pv)
            yield
        if diag:
            rows = slice(qi * blk, (qi + 1) * blk)
            for pr in range(hps // 2):
                normed = []
                for hh in (2 * pr, 2 * pr + 1):
                    acc = acc_all[qi, hh]
                    o = acc[:HEAD_DIM] / acc[HEAD_DIM:HEAD_DIM + 1]
                    ms = jnp.mean(o * o, axis=0, keepdims=True)
                    normed.append(o * lax.rsqrt(ms + NORM_EPS))
                on = jnp.concatenate(normed, axis=0).T
                ls = slice(pr * LANES, (pr + 1) * LANES)
                o_ref[0, rows, ls] = (on * g_ref[:, ls]
                                      * _sigmoid(og_ref[0, rows, ls].astype(F32))).astype(o_ref.dtype)

    def step(t, par):
        _interleave([stage(t - lag, (par + lag) % 2)
                     for lag, stage in enumerate((logits, softmax, values))
                     if not isinstance(t, int) or 0 <= t - lag < n_el])

    def two_steps(u, carry):
        step(2 * u + 2, 0)
        step(2 * u + 3, 1)
        return carry

    step(0, 0)
    step(1, 1)
    lax.fori_loop(0, (n_off - 2) // 2, two_steps, 0)
    for t in range(n_off, n_el + 2):
        step(t, t % 2)


def _fox(qa, ka, vt, og, fox_out_g, width, cb=256, hps=4):
    B, H, T, _ = qa.shape
    blk = vt.shape[4]
    n_q = T // blk
    pairs = [(qi, ki) for qi in range(n_q) for ki in range(qi)]
    n_off = len(pairs)
    assert n_off >= 2 and n_off % 2 == 0, "pipeline loop is unrolled by two"
    qtab = jnp.asarray(np.array([p[0] for p in pairs], np.int32))
    ktab = jnp.asarray(np.array([p[1] for p in pairs], np.int32))
    grid_spec = pltpu.PrefetchScalarGridSpec(
        num_scalar_prefetch=2,
        grid=(B, H // hps),
        in_specs=[pl.BlockSpec((1, hps, T, LANES), lambda b, h, qt, kt: (b, h, 0, 0)),
                  pl.BlockSpec((1, hps, T, LANES), lambda b, h, qt, kt: (b, h, 0, 0)),
                  pl.BlockSpec((1, hps, n_q, V_ROWS, blk), lambda b, h, qt, kt: (b, h, 0, 0, 0)),
                  pl.BlockSpec((1, T, hps * HEAD_DIM), lambda b, h, qt, kt: (b, 0, h)),
                  pl.BlockSpec((1, hps * HEAD_DIM), lambda b, h, qt, kt: (0, h))],
        out_specs=pl.BlockSpec((1, T, hps * HEAD_DIM), lambda b, h, qt, kt: (b, 0, h)),
        scratch_shapes=[pltpu.VMEM((n_q, hps, 1, blk), F32),
                        pltpu.VMEM((n_q, hps, V_ROWS, blk), F32),
                        pltpu.VMEM((2, hps, blk, blk), F32), pltpu.VMEM((2, hps, blk, blk), BF16),
                        pltpu.VMEM((2, hps, 1, blk), F32)])
    return pl.pallas_call(
        functools.partial(_fox_kernel, blk=blk, cb=cb, n_off=n_off, hps=hps),
        out_shape=jax.ShapeDtypeStruct((B, T, width), BF16),
        grid_spec=grid_spec,
        compiler_params=pltpu.CompilerParams(
            dimension_semantics=("arbitrary",) * 2, vmem_limit_bytes=VMEM_LIMIT),
        name="fox",
    )(qtab, ktab, qa, ka, vt, og, fox_out_g.reshape(1, width))


def _outmlp_kernel(x_ref, yrw_ref, yfx_ref, mod_ref, wo_ref, g2_ref, w1_ref, w2_ref, gf_ref,
                   o_ref, *, ff_tile):
    x = x_ref[0]
    half = yrw_ref.shape[2]
    gate1 = mod_ref[0, 2:3, :]
    shift2 = mod_ref[0, 3:4, :]
    scale2 = mod_ref[0, 4:5, :]
    gate2 = mod_ref[0, 5:6, :]
    y = _dot(yrw_ref[0], wo_ref[:half, :]) + _dot(yfx_ref[0], wo_ref[half:, :])
    h1 = x + gate1 * y
    ms = jnp.mean(h1 * h1, axis=-1, keepdims=True)
    u = ((h1 * lax.rsqrt(ms + NORM_EPS) * g2_ref[...]) * (1.0 + scale2) + shift2).astype(BF16)
    acc = jnp.zeros_like(x)
    for j in range(w1_ref.shape[1] // ff_tile):
        hid = jnp.maximum(_dot(u, w1_ref[:, j * ff_tile:(j + 1) * ff_tile]), 0.0)
        acc = acc + _dot((hid * hid).astype(BF16), w2_ref[j * ff_tile:(j + 1) * ff_tile, :])
    h2 = h1 + gate2 * acc
    ms2 = jnp.mean(h2 * h2, axis=-1, keepdims=True)
    o_ref[0] = h2 * lax.rsqrt(ms2 + NORM_EPS) * gf_ref[...]


def _outmlp(x, y_rw, y_fx, mod3, w_o, norm2_g, w1, w2, final_g, tm=512, ff_tile=1024):
    B, T, D = x.shape
    half = y_rw.shape[2]
    return pl.pallas_call(
        functools.partial(_outmlp_kernel, ff_tile=ff_tile),
        out_shape=jax.ShapeDtypeStruct((B, T, D), F32),
        grid=(B, T // tm),
        in_specs=[pl.BlockSpec((1, tm, D), lambda b, t: (b, t, 0)),
                  pl.BlockSpec((1, tm, half), lambda b, t: (b, t, 0)),
                  pl.BlockSpec((1, tm, half), lambda b, t: (b, t, 0)),
                  pl.BlockSpec((1, N_MOD, D), lambda b, t: (b, 0, 0)),
                  _const_spec(w_o.shape, single_buffer=True), _const_spec((1, D)),
                  _const_spec(w1.shape, single_buffer=True),
                  _const_spec(w2.shape, single_buffer=True), _const_spec((1, D))],
        out_specs=pl.BlockSpec((1, tm, D), lambda b, t: (b, t, 0)),
        compiler_params=pltpu.CompilerParams(
            dimension_semantics=("arbitrary", "arbitrary"), vmem_limit_bytes=VMEM_LIMIT),
        name="outmlp",
    )(x, y_rw, y_fx, mod3, w_o, norm2_g.reshape(1, D), w1, w2, final_g.reshape(1, D))


def _pad_cols(a, n):
    return jnp.pad(a, ((0, 0), (0, n - a.shape[1])))


def _pad_rows(a, n):
    return jnp.pad(a, ((0, n - a.shape[0]), (0, 0)))


def _branches(x, c, w_ada, b_ada, norm1_g, w_in, mu_shift, w0, w_up_decay, a0, w_up_a, w_up_g,
              k_k, k_a, r_k, gn_g, gn_b, b_f, q_norm_g, k_norm_g, fox_out_g):
    B, T, D = x.shape
    W = w0.shape[0]
    heads = b_f.shape[0]
    rw_n = 3 * W + DECAY_LORA + AAA_LORA + GATE_LORA

    def regroup(a):
        rw, fx = a[:, :rw_n], a[:, rw_n:]
        o = 3 * W
        return jnp.concatenate([
            rw[:, :o],
            _pad_cols(rw[:, o:o + DECAY_LORA], LANES),
            _pad_cols(rw[:, o + DECAY_LORA:o + DECAY_LORA + AAA_LORA], LANES),
            _pad_cols(jnp.concatenate([rw[:, o + DECAY_LORA + AAA_LORA:], fx[:, 4 * W:]], axis=1),
                      2 * LANES),
            fx[:, :4 * W]], axis=1)

    rw_cols = 3 * W + 4 * LANES
    w_all = regroup(w_in.astype(BF16))
    mu_p = regroup(jnp.pad(mu_shift.reshape(1, rw_n), ((0, 0), (0, w_in.shape[1] - rw_n))))[:, :rw_cols]
    f_off = GATE_LORA - LANES
    bf_p = jnp.pad(b_f.reshape(1, heads), ((0, 0), (f_off, LANES - f_off - heads)))
    wdw = _pad_rows(w_up_decay, LANES).astype(BF16)
    waw = _pad_rows(w_up_a, LANES).astype(BF16)
    wgw = _pad_rows(w_up_g, 2 * LANES).astype(BF16)

    mod3 = _ada(c, w_ada, b_ada).reshape(B, N_MOD, D)
    rw, qa, ka, vt, og = _inproj(x, mod3, norm1_g, w_all, mu_p, bf_p, q_norm_g, k_norm_g,
                                 rw_cols, f_off)
    y_rw = _rwkv(rw, w0, wdw, a0, waw, wgw, k_k, k_a, r_k.reshape(-1), gn_g, gn_b, W)
    y_fx = _fox(qa, ka, vt, og, fox_out_g, W)
    return y_rw, y_fx, mod3


def kernel(x, c, w_ada, b_ada, norm1_g, w_in, mu_shift, w0, w_up_decay, a0, w_up_a, w_up_g,
           k_k, k_a, r_k, gn_g, gn_b, b_f, q_norm_g, k_norm_g, fox_out_g, w_o, norm2_g,
           w_mlp1, w_mlp2, final_g):
    y_rw, y_fx, mod3 = _branches(x, c, w_ada, b_ada, norm1_g, w_in, mu_shift, w0, w_up_decay,
                                 a0, w_up_a, w_up_g, k_k, k_a, r_k, gn_g, gn_b, b_f,
                                 q_norm_g, k_norm_g, fox_out_g)
    return _outmlp(x, y_rw, y_fx, mod3, w_o.astype(BF16), norm2_g,
                   w_mlp1.astype(BF16), w_mlp2.astype(BF16), final_g)
```

```python
import functools
import math

import jax
import jax.numpy as jnp
import numpy as np
from jax import lax
from jax.experimental import pallas as pl
from jax.experimental.pallas import tpu as pltpu

F32 = jnp.float32
BF16 = jnp.bfloat16

HEAD_DIM = 64
LANES = 128
MXU_TILE = 256
NORM_EPS = 1e-6
GN_EPS = 64e-5
NEG_INF = -1e30
LOG2E = 1.4426950408889634
N_MOD = 6
DECAY_LORA = 64
AAA_LORA = 64
GATE_LORA = 160
RW_CHUNK = 64
V_ROWS = HEAD_DIM + 16
VMEM_LIMIT = 56 * 1024 * 1024


def _dot(a, b):
    return jnp.dot(a, b, preferred_element_type=F32)


def _dot_nt(a, b):
    return lax.dot_general(a, b, (((1,), (1,)), ((), ())), preferred_element_type=F32)


def _dot_tn(a, b):
    return lax.dot_general(a, b, (((0,), (0,)), ((), ())), preferred_element_type=F32)


def _bf(x):
    return x.astype(BF16)


def _sigmoid(x):
    return 1.0 / (1.0 + jnp.exp(-x))


def _const_spec(shape, single_buffer=False):
    n = len(shape)
    mode = pl.Buffered(1) if single_buffer else None
    return pl.BlockSpec(shape, lambda *_: (0,) * n, pipeline_mode=mode)


def _ada_kernel(c_ref, w_ref, b_ref, o_ref):
    c = c_ref[...]
    a = c * _sigmoid(c)
    w = w_ref[...]
    a_hi, w_hi = _bf(a), _bf(w)
    a_lo, w_lo = _bf(a - a_hi.astype(F32)), _bf(w - w_hi.astype(F32))
    o_ref[...] = (_dot(a_hi, w_hi) + (_dot(a_hi, w_lo) + _dot(a_lo, w_hi))) + b_ref[...]


def _ada(c, w_ada, b_ada, tn=2048):
    B, D = c.shape
    n = w_ada.shape[1]
    return pl.pallas_call(
        _ada_kernel,
        out_shape=jax.ShapeDtypeStruct((B, n), F32),
        grid=(n // tn,),
        in_specs=[pl.BlockSpec((B, D), lambda j: (0, 0)),
                  pl.BlockSpec((D, tn), lambda j: (0, j)),
                  pl.BlockSpec((1, tn), lambda j: (0, j))],
        out_specs=pl.BlockSpec((B, tn), lambda j: (0, j)),
        name="ada",
    )(c, w_ada, b_ada.reshape(1, n))


N_PIECES = 3


def _fox_aug_tables(heads, f_off):
    groups = heads // 2
    sel = np.zeros((LANES, 2 * groups * LANES), np.float32)
    bias = np.zeros((1, 2 * groups * LANES), np.float32)
    for h in range(heads):
        base = (h // 2) * LANES + (HEAD_DIM if h % 2 == 0 else 0)
        kbase = groups * LANES + base
        for piece in range(N_PIECES):
            sel[f_off + piece * heads + h, base + piece] = 1.0
            sel[f_off + piece * heads + h, kbase + N_PIECES + piece] = -1.0
            bias[0, base + N_PIECES + piece] = 1.0
            bias[0, kbase + piece] = 1.0
    return jnp.asarray(sel, BF16), jnp.asarray(bias)


def _fox_layout(pf, fcum, f_off, qg_ref, kg_ref, sel_ref, bias_ref, qa_ref, ka_ref, v_ref, og_ref):
    W = pf.shape[1] // 4
    heads = W // HEAD_DIM
    groups = W // LANES
    tm = pf.shape[0]
    lane = lax.broadcasted_iota(jnp.int32, (tm, LANES), 1)
    lane_f = lane - f_off
    fcum = fcum * LOG2E
    f_hi = fcum.astype(BF16).astype(F32)
    r1 = fcum - f_hi
    f_mid = r1.astype(BF16).astype(F32)
    f_lo = r1 - f_mid
    packed = jnp.where(lane_f < 0, 0.0, jnp.where(lane_f < heads, f_hi, jnp.where(
        lane_f < 2 * heads, pltpu.roll(f_mid, heads, axis=1), jnp.where(
            lane_f < 3 * heads, pltpu.roll(f_lo, 2 * heads, axis=1), 0.0))))
    aug = _dot(packed.astype(BF16), sel_ref[...]) + bias_ref[...]
    first = lane < HEAD_DIM
    for side, (gain_ref, out_ref) in enumerate(((qg_ref, qa_ref), (kg_ref, ka_ref))):
        for g in range(groups):
            x = pf[:, side * W + g * LANES:side * W + (g + 1) * LANES]
            sq = x * x
            lo = jnp.sum(jnp.where(first, sq, 0.0), axis=-1, keepdims=True)
            hi = jnp.sum(jnp.where(first, 0.0, sq), axis=-1, keepdims=True)
            ms = jnp.where(first, lo, hi) * (1.0 / HEAD_DIM)
            xn = x * lax.rsqrt(ms + NORM_EPS) * gain_ref[...]
            a = aug[:, (side * groups + g) * LANES:(side * groups + g + 1) * LANES]
            out_ref[0, 2 * g] = jnp.where(first, xn, a).astype(BF16)
            out_ref[0, 2 * g + 1] = jnp.where(first, a, xn).astype(BF16)
    pad = (lax.broadcasted_iota(jnp.int32, (V_ROWS - HEAD_DIM, tm), 0) == 0).astype(BF16)
    for g in range(groups):
        vt = pf[:, 2 * W + g * LANES:2 * W + (g + 1) * LANES].T.astype(BF16)
        for hh in range(2):
            v_ref[0, 2 * g + hh, 0, :HEAD_DIM, :] = vt[hh * HEAD_DIM:(hh + 1) * HEAD_DIM]
            v_ref[0, 2 * g + hh, 0, HEAD_DIM:, :] = pad
    og_ref[0] = pf[:, 3 * W:4 * W].astype(BF16)


def _inproj_kernel(x_ref, mod_ref, g_ref, w_ref, mu_ref, bf_ref, qg_ref, kg_ref, sel_ref, bias_ref,
                   rw_ref, qa_ref, ka_ref, v_ref, og_ref, prev_scr, fcar_scr, *, rw_cols, f_off):
    t = pl.program_id(1)
    tm = x_ref.shape[1]

    @pl.when(t == 0)
    def _():
        prev_scr[...] = jnp.zeros_like(prev_scr)
        fcar_scr[...] = jnp.zeros_like(fcar_scr)

    x = x_ref[0]
    shift = mod_ref[0, 0:1, :]
    scale = mod_ref[0, 1:2, :]
    ms = jnp.mean(x * x, axis=-1, keepdims=True)
    u = (x * lax.rsqrt(ms + NORM_EPS) * g_ref[...]) * (1.0 + scale) + shift
    u = u.astype(BF16)

    p = _dot(u, w_ref[:, :rw_cols])
    row = lax.broadcasted_iota(jnp.int32, p.shape, 0)
    prev = jnp.where(row == 0, prev_scr[0:1, :], pltpu.roll(p, shift=1, axis=0))
    prev_scr[0:1, :] = p[tm - 1:tm, :]
    rw_ref[0] = p + (prev - p) * mu_ref[...]

    pf = _dot(u, w_ref[:, rw_cols:])
    z = p[:, rw_cols - LANES:] + bf_ref[...]
    logf = jnp.minimum(z, 0.0) - jnp.log(1.0 + jnp.exp(-jnp.abs(z)))
    frow = lax.broadcasted_iota(jnp.int32, logf.shape, 0)
    fcum = logf
    shift = 1
    while shift < tm:
        fcum = fcum + jnp.where(frow >= shift, pltpu.roll(fcum, shift, axis=0), 0.0)
        shift *= 2
    fcum = fcum + fcar_scr[0:1, :]
    fcar_scr[0:1, :] = fcum[tm - 1:tm, :]
    _fox_layout(pf, fcum, f_off, qg_ref, kg_ref, sel_ref, bias_ref, qa_ref, ka_ref, v_ref, og_ref)


def _inproj(x, mod3, norm1_g, w_all, mu_p, bf_p, q_norm_g, k_norm_g, rw_cols, f_off, tm=512):
    B, T, D = x.shape
    ncols = w_all.shape[1]
    width = (ncols - rw_cols) // 4
    heads = width // HEAD_DIM
    sel, bias = _fox_aug_tables(heads, f_off)
    scale = LOG2E / math.sqrt(HEAD_DIM)
    qg2 = jnp.tile(q_norm_g * scale, 2).reshape(1, LANES)
    kg2 = jnp.tile(k_norm_g, 2).reshape(1, LANES)
    return pl.pallas_call(
        functools.partial(_inproj_kernel, rw_cols=rw_cols, f_off=f_off),
        out_shape=(jax.ShapeDtypeStruct((B, T, rw_cols), F32),
                   jax.ShapeDtypeStruct((B, heads, T, LANES), BF16),
                   jax.ShapeDtypeStruct((B, heads, T, LANES), BF16),
                   jax.ShapeDtypeStruct((B, heads, T // tm, V_ROWS, tm), BF16),
                   jax.ShapeDtypeStruct((B, T, width), BF16)),
        grid=(B, T // tm),
        in_specs=[pl.BlockSpec((1, tm, D), lambda b, t: (b, t, 0)),
                  pl.BlockSpec((1, N_MOD, D), lambda b, t: (b, 0, 0)),
                  _const_spec((1, D)),
                  _const_spec((D, ncols), single_buffer=True),
                  _const_spec((1, rw_cols)),
                  _const_spec((1, LANES)),
                  _const_spec((1, LANES)), _const_spec((1, LANES)),
                  _const_spec(sel.shape, single_buffer=True), _const_spec(bias.shape)],
        out_specs=(pl.BlockSpec((1, tm, rw_cols), lambda b, t: (b, t, 0)),
                   pl.BlockSpec((1, heads, tm, LANES), lambda b, t: (b, 0, t, 0)),
                   pl.BlockSpec((1, heads, tm, LANES), lambda b, t: (b, 0, t, 0)),
                   pl.BlockSpec((1, heads, 1, V_ROWS, tm), lambda b, t: (b, 0, t, 0, 0)),
                   pl.BlockSpec((1, tm, width), lambda b, t: (b, t, 0))),
        scratch_shapes=[pltpu.VMEM((8, rw_cols), F32), pltpu.VMEM((8, LANES), F32)],
        compiler_params=pltpu.CompilerParams(
            dimension_semantics=("arbitrary", "arbitrary"), vmem_limit_bytes=VMEM_LIMIT),
        name="inproj",
    )(x, mod3, norm1_g.reshape(1, D), w_all, mu_p, bf_p, qg2, kg2, sel, bias)


def _segsum_lanes(x):
    first = lax.broadcasted_iota(jnp.int32, (x.shape[0], LANES), 1) < HEAD_DIM
    cols = []
    for j in range(x.shape[1] // LANES):
        xg = x[:, j * LANES:(j + 1) * LANES]
        lo = jnp.sum(jnp.where(first, xg, 0.0), axis=-1, keepdims=True)
        hi = jnp.sum(jnp.where(first, 0.0, xg), axis=-1, keepdims=True)
        cols.append(jnp.where(first, lo, hi))
    return jnp.concatenate(cols, axis=1)


def _segsum(x, bd):
    tile = bd.shape[0]
    xb = x.astype(BF16)
    return jnp.concatenate([_dot(xb[:, j * tile:(j + 1) * tile], bd)
                            for j in range(x.shape[1] // tile)], axis=1)


def _interleave(generators):
    live = [g for g in generators if g is not None]
    while live:
        for g in list(live):
            if next(g, StopIteration) is StopIteration:
                live.remove(g)


def _rwkv_kernel(p_ref, w0_ref, wdw_ref, a0_ref, waw_ref, wgw_ref, kk_ref, ka_ref, rk_ref,
                 gng_ref, gnb_ref, o_ref, s_scr, *, width, block_rows):
    t = pl.program_id(1)
    L = RW_CHUNK
    R = block_rows
    n_blocks = p_ref.shape[1] // R
    n_chunks = R // L
    W = width
    G2 = 2 * LANES
    groups = W // LANES
    items = [(c, j) for c in range(n_chunks) for j in range(groups)]
    rsl = lambda c: slice(c * L, (c + 1) * L)
    gsl = lambda j: slice(j * LANES, (j + 1) * LANES)

    @pl.when(t == 0)
    def _():
        s_scr[...] = jnp.zeros_like(s_scr)

    ri = lax.broadcasted_iota(jnp.int32, (MXU_TILE, MXU_TILE), 0)
    ci = lax.broadcasted_iota(jnp.int32, (MXU_TILE, MXU_TILE), 1)
    bd = ((ri // HEAD_DIM) == (ci // HEAD_DIM)).astype(BF16)
    tril = (lax.broadcasted_iota(jnp.int32, (L, L), 1)
            <= lax.broadcasted_iota(jnp.int32, (L, L), 0)).astype(BF16)
    o0 = 3 * W
    lane1 = lax.broadcasted_iota(jnp.int32, (L, LANES), 1)
    lane2 = lax.broadcasted_iota(jnp.int32, (L, G2), 1)
    row2 = lax.broadcasted_iota(jnp.int32, (L, G2), 0)
    first1 = lane1 < HEAD_DIM
    first2 = (lane2 % LANES) < HEAD_DIM
    strict2 = (lane2 % HEAD_DIM) < row2
    incl2 = (lane2 % HEAD_DIM) <= row2
    upper2 = lane2 >= LANES
    eye2 = ((lane1 % HEAD_DIM) == lax.broadcasted_iota(jnp.int32, (L, LANES), 0)).astype(F32)
    zero1 = jnp.zeros((L, LANES), BF16)
    rr = lax.broadcasted_iota(jnp.int32, (LANES, LANES), 0) < HEAD_DIM
    cc = lax.broadcasted_iota(jnp.int32, (LANES, LANES), 1) < HEAD_DIM
    diag_blocks = rr == cc

    def split(m, mask):
        return jnp.concatenate([jnp.where(mask, m, 0), jnp.where(mask, 0, m)], axis=0)

    def prepare(pb, b):
        p = p_ref[0, b * R:(b + 1) * R, :]
        r, k, v = p[:, 0:W], p[:, W:2 * W], p[:, 2 * W:3 * W]
        wd = p[:, o0:o0 + LANES]
        ad = p[:, o0 + LANES:o0 + 2 * LANES]
        gd = p[:, o0 + 2 * LANES:o0 + 4 * LANES]
        z = w0_ref[...] + _dot(_bf(jnp.tanh(wd)), wdw_ref[...])
        softplus = jnp.maximum(-z, 0.0) + jnp.log(1.0 + jnp.exp(-jnp.abs(z)))
        ld = -jnp.exp(-softplus - 0.5)
        yield
        a_sig = _sigmoid(a0_ref[...] + _dot(_bf(ad), waw_ref[...]))
        pb["gate"] = _dot(_bf(_sigmoid(gd)), wgw_ref[...])
        yield
        kk = k * kk_ref[...]
        kk = kk / jnp.maximum(jnp.sqrt(_segsum_lanes(kk * kk)), 1e-12)
        yield
        k2 = k * (1.0 + (a_sig - 1.0) * ka_ref[...])
        b_ = kk * a_sig
        pb["bonus"] = _segsum(r * k2 * rk_ref[...], bd) * v
        yield
        ld_hi = _bf(ld)
        ld_lo = _bf(ld - ld_hi.astype(F32))
        cums = [_dot(tril, ld_hi[rsl(c)]) + _dot(tril, ld_lo[rsl(c)]) for c in range(n_chunks)]
        pb["tots"] = [cum_c[L - 1:L, :] for cum_c in cums]
        cum = jnp.concatenate(cums, axis=0)
        tot_b = jnp.concatenate([jnp.broadcast_to(tc, (L, W)) for tc in pb["tots"]], axis=0)
        yield
        e_neg = jnp.exp(-cum)
        rt = r * jnp.exp(cum)
        pb.update(rt=rt, rt_m=_bf(rt), at=_bf(-kk * jnp.exp(cum - ld)), bt=_bf(b_ * e_neg),
                  kt=_bf(k2 * e_neg))
        yield
        e_rem = jnp.exp(tot_b - cum)
        pb.update(bh=_bf(b_ * e_rem), kh=_bf(k2 * e_rem), vm=_bf(v))

    def chains(pb, cb):
        at, rt_m, bt, kt, vm = pb["at"], pb["rt_m"], pb["bt"], pb["kt"], pb["vm"]
        g_bot, akm, zz = {}, {}, {}
        for c, j in items:
            ar = jnp.concatenate([at[rsl(c), gsl(j)], rt_m[rsl(c), gsl(j)]], axis=0)
            rhs = jnp.concatenate([split(bt[rsl(c), gsl(j)], first1),
                                   split(kt[rsl(c), gsl(j)], first1)], axis=0)
            gm = _dot_nt(ar, rhs)
            top = jnp.where(strict2, gm[:L], 0.0)
            g_bot[c, j] = _bf(jnp.where(incl2, gm[L:], 0.0))
            akm[c, j] = _bf(top[:, LANES:])
            zz[c, j] = jnp.concatenate([top[:, :LANES], eye2], axis=1)
        yield
        for _ in range(int(math.log2(L))):
            for it in items:
                zb = _bf(zz[it])
                out = _dot(zb[:, :LANES], split(zb, first2))
                zz[it] = out + jnp.where(upper2, zz[it], 0.0)
            yield
        vbd = {(c, j): split(vm[rsl(c), gsl(j)], first1) for c, j in items}
        akv = {it: _bf(_dot(akm[it], vbd[it])) for it in items}
        yield
        x = {}
        for c, j in items:
            zed = jnp.concatenate([at[rsl(c), gsl(j)], akv[c, j]], axis=1)
            x[c, j] = _bf(_dot(_bf(zz[c, j][:, LANES:]), split(zed, first2)))
        yield
        cb.update(r_eff={}, y0={}, mt={}, nt={})
        for c, j in items:
            w4 = jnp.concatenate([split(x[c, j], first2),
                                  jnp.concatenate([jnp.concatenate([zero1, zero1], axis=0),
                                                   vbd[c, j]], axis=1)], axis=0)
            out = _dot(g_bot[c, j], w4)
            cb["r_eff"][c, j] = _bf(pb["rt"][rsl(c), gsl(j)] + out[:, :LANES])
            cb["y0"][c, j] = out[:, LANES:]
        yield
        for c, j in items:
            lhs = jnp.concatenate([pb["bh"][rsl(c), gsl(j)], pb["kh"][rsl(c), gsl(j)]], axis=0)
            rhs = jnp.concatenate([x[c, j], jnp.concatenate([zero1, vm[rsl(c), gsl(j)]], axis=1)],
                                  axis=0)
            pt = _dot_tn(lhs, rhs)
            cb["mt"][c, j] = _bf(jnp.where(diag_blocks, pt[:, :LANES], 0.0))
            cb["nt"][c, j] = jnp.where(diag_blocks, pt[:, LANES:], 0.0)
        yield
        cb["e_cols"] = [
            jnp.concatenate([jnp.exp(pb["tots"][c][:, gsl(j)]) for c in range(n_chunks)]
                            + [jnp.zeros((LANES - n_chunks, LANES), F32)], axis=0).T
            for j in range(groups)]

    state = [s_scr[j] for j in range(groups)]

    def tail(pb, cb, b):
        y_rows = []
        for c in range(n_chunks):
            hb = [_bf(s) for s in state]
            y_rows.append(jnp.concatenate(
                [_dot(cb["r_eff"][c, j], hb[j]) + cb["y0"][c, j] for j in range(groups)], axis=1))
            for j in range(groups):
                state[j] = (state[j] * cb["e_cols"][j][:, c:c + 1] + _dot(cb["mt"][c, j], hb[j])
                            + cb["nt"][c, j])
            yield
        y = jnp.concatenate(y_rows, axis=0)
        inv_n = 1.0 / HEAD_DIM
        mean = _segsum_lanes(y) * inv_n
        d = y - mean
        yield
        var = _segsum_lanes(d * d) * inv_n
        yn = d * lax.rsqrt(var + GN_EPS) * gng_ref[...] + gnb_ref[...]
        o_ref[0, b * R:(b + 1) * R, :] = ((yn + pb["bonus"]) * pb["gate"]).astype(o_ref.dtype)

    prep = [dict() for _ in range(n_blocks)]
    chain = [dict() for _ in range(n_blocks)]
    _interleave([prepare(prep[0], 0)])
    for b in range(n_blocks):
        _interleave([chains(prep[b], chain[b]),
                     prepare(prep[b + 1], b + 1) if b + 1 < n_blocks else None,
                     tail(prep[b - 1], chain[b - 1], b - 1) if b > 0 else None])
    _interleave([tail(prep[-1], chain[-1], n_blocks - 1)])
    for j in range(groups):
        s_scr[j] = state[j]


def _rwkv(rw, w0, wdw, a0, waw, wgw, k_k, k_a, r_k, gn_g, gn_b, width, rows=512, block_rows=256):
    B, T, cols = rw.shape
    row = lambda a: a.reshape(1, width)
    vec = _const_spec((1, width))
    return pl.pallas_call(
        functools.partial(_rwkv_kernel, width=width, block_rows=block_rows),
        out_shape=jax.ShapeDtypeStruct((B, T, width), BF16),
        grid=(B, T // rows),
        in_specs=[pl.BlockSpec((1, rows, cols), lambda b, t: (b, t, 0)),
                  vec, _const_spec(wdw.shape), vec, _const_spec(waw.shape),
                  _const_spec(wgw.shape), vec, vec, vec, vec, vec],
        out_specs=pl.BlockSpec((1, rows, width), lambda b, t: (b, t, 0)),
        scratch_shapes=[pltpu.VMEM((width // LANES, LANES, LANES), F32)],
        compiler_params=pltpu.CompilerParams(
            dimension_semantics=("arbitrary", "arbitrary"), vmem_limit_bytes=VMEM_LIMIT),
        name="rwkv",
    )(rw, row(w0), wdw, row(a0), waw, wgw, row(k_k), row(k_a), row(r_k), row(gn_g), row(gn_b))


def _fox_kernel(qtab, ktab, qa_ref, ka_ref, vt_ref, og_ref, g_ref, o_ref,
                m_all, acc_all, s_scr, p_scr, al_scr, *, blk, cb, n_off, hps):
    n_q = qa_ref.shape[2] // blk
    n_el = n_off + n_q
    chains = [(hh, c) for hh in range(hps) for c in range(blk // cb)]
    csl = lambda c: slice(c * cb, (c + 1) * cb)

    m_all[...] = jnp.full_like(m_all, NEG_INF)
    acc_all[...] = jnp.zeros_like(acc_all)

    def element(t):
        if isinstance(t, int) and t >= n_off:
            return t - n_off, t - n_off, True
        return qtab[t], ktab[t], False

    def start(i, size):
        return i * size if isinstance(i, int) else pl.multiple_of(i * size, size)

    def n_keys(c, diag):
        return (c + 1) * cb if diag else blk

    def logits(t, par):
        qi, ki, diag = element(t)
        for hh, c in chains:
            nk = n_keys(c, diag)
            s_scr[par, hh, :nk, csl(c)] = _dot_nt(
                ka_ref[0, hh, pl.ds(start(ki, blk), nk), :],
                qa_ref[0, hh, pl.ds(start(qi, blk) + c * cb, cb), :])
            yield

    def softmax(t, par):
        qi, _, diag = element(t)
        for hh, c in chains:
            nk = n_keys(c, diag)
            s = s_scr[par, hh, :nk, csl(c)]
            if diag:
                key = lax.broadcasted_iota(jnp.int32, (nk, cb), 0)
                qry = c * cb + lax.broadcasted_iota(jnp.int32, (nk, cb), 1)
                s = jnp.where(key <= qry, s, NEG_INF)
            m_prev = m_all[qi, hh, :, csl(c)]
            m_new = jnp.maximum(m_prev, jnp.max(s, axis=0, keepdims=True))
            m_all[qi, hh, :, csl(c)] = m_new
            al_scr[par, hh, :, csl(c)] = jnp.exp2(m_prev - m_new)
            p_scr[par, hh, :nk, csl(c)] = jnp.exp2(s - m_new).astype(BF16)
            yield

    def values(t, par):
        qi, ki, diag = element(t)
        for hh, c in chains:
            nk = n_keys(c, diag)
            pv = _dot(vt_ref[0, hh, ki, :, :nk], p_scr[par, hh, :nk, csl(c)])
            acc_all[qi, hh, :, csl(c)] = (al_scr[par, hh, :, csl(c)] * acc_all[qi, hh, :, csl(c)]
                                          + pv)
            yield
        if diag:
            rows = slice(qi * blk, (qi + 1) * blk)
            for pr in range(hps // 2):
                normed = []
                for hh in (2 * pr, 2 * pr + 1):
                    acc = acc_all[qi, hh]
                    o = acc[:HEAD_DIM] / acc[HEAD_DIM:HEAD_DIM + 1]
                    ms = jnp.mean(o * o, axis=0, keepdims=True)
                    normed.append(o * lax.rsqrt(ms + NORM_EPS))
                on = jnp.concatenate(normed, axis=0).T
                ls = slice(pr * LANES, (pr + 1) * LANES)
                o_ref[0, rows, ls] = (on * g_ref[:, ls]
                                      * _sigmoid(og_ref[0, rows, ls].astype(F32))).astype(o_ref.dtype)

    def step(t, par):
        _interleave([stage(t - lag, (par + lag) % 2)
                     for lag, stage in enumerate((logits, softmax, values))
                     if not isinstance(t, int) or 0 <= t - lag < n_el])

    def two_steps(u, carry):
        step(2 * u + 2, 0)
        step(2 * u + 3, 1)
        return carry

    step(0, 0)
    step(1, 1)
    lax.fori_loop(0, (n_off - 2) // 2, two_steps, 0)
    for t in range(n_off, n_el + 2):
        step(t, t % 2)


def _fox(qa, ka, vt, og, fox_out_g, width, cb=256, hps=4):
    B, H, T, _ = qa.shape
    blk = vt.shape[4]
    n_q = T // blk
    pairs = [(qi, ki) for qi in range(n_q) for ki in range(qi)]
    n_off = len(pairs)
    assert n_off >= 2 and n_off % 2 == 0, "pipeline loop is unrolled by two"
    qtab = jnp.asarray(np.array([p[0] for p in pairs], np.int32))
    ktab = jnp.asarray(np.array([p[1] for p in pairs], np.int32))
    grid_spec = pltpu.PrefetchScalarGridSpec(
        num_scalar_prefetch=2,
        grid=(B, H // hps),
        in_specs=[pl.BlockSpec((1, hps, T, LANES), lambda b, h, qt, kt: (b, h, 0, 0)),
                  pl.BlockSpec((1, hps, T, LANES), lambda b, h, qt, kt: (b, h, 0, 0)),
                  pl.BlockSpec((1, hps, n_q, V_ROWS, blk), lambda b, h, qt, kt: (b, h, 0, 0, 0)),
                  pl.BlockSpec((1, T, hps * HEAD_DIM), lambda b, h, qt, kt: (b, 0, h)),
                  pl.BlockSpec((1, hps * HEAD_DIM), lambda b, h, qt, kt: (0, h))],
        out_specs=pl.BlockSpec((1, T, hps * HEAD_DIM), lambda b, h, qt, kt: (b, 0, h)),
        scratch_shapes=[pltpu.VMEM((n_q, hps, 1, blk), F32),
                        pltpu.VMEM((n_q, hps, V_ROWS, blk), F32),
                        pltpu.VMEM((2, hps, blk, blk), F32), pltpu.VMEM((2, hps, blk, blk), BF16),
                        pltpu.VMEM((2, hps, 1, blk), F32)])
    return pl.pallas_call(
        functools.partial(_fox_kernel, blk=blk, cb=cb, n_off=n_off, hps=hps),
        out_shape=jax.ShapeDtypeStruct((B, T, width), BF16),
        grid_spec=grid_spec,
        compiler_params=pltpu.CompilerParams(
            dimension_semantics=("arbitrary",) * 2, vmem_limit_bytes=VMEM_LIMIT),
        name="fox",
    )(qtab, ktab, qa, ka, vt, og, fox_out_g.reshape(1, width))


def _outmlp_kernel(x_ref, yrw_ref, yfx_ref, mod_ref, wo_ref, g2_ref, w1_ref, w2_ref, gf_ref,
                   o_ref, *, ff_tile):
    x = x_ref[0]
    half = yrw_ref.shape[2]
    gate1 = mod_ref[0, 2:3, :]
    shift2 = mod_ref[0, 3:4, :]
    scale2 = mod_ref[0, 4:5, :]
    gate2 = mod_ref[0, 5:6, :]
    y = _dot(yrw_ref[0], wo_ref[:half, :]) + _dot(yfx_ref[0], wo_ref[half:, :])
    h1 = x + gate1 * y
    ms = jnp.mean(h1 * h1, axis=-1, keepdims=True)
    u = ((h1 * lax.rsqrt(ms + NORM_EPS) * g2_ref[...]) * (1.0 + scale2) + shift2).astype(BF16)
    acc = jnp.zeros_like(x)
    for j in range(w1_ref.shape[1] // ff_tile):
        hid = jnp.maximum(_dot(u, w1_ref[:, j * ff_tile:(j + 1) * ff_tile]), 0.0)
        acc = acc + _dot((hid * hid).astype(BF16), w2_ref[j * ff_tile:(j + 1) * ff_tile, :])
    h2 = h1 + gate2 * acc
    ms2 = jnp.mean(h2 * h2, axis=-1, keepdims=True)
    o_ref[0] = h2 * lax.rsqrt(ms2 + NORM_EPS) * gf_ref[...]


def _outmlp(x, y_rw, y_fx, mod3, w_o, norm2_g, w1, w2, final_g, tm=512, ff_tile=1024):
    B, T, D = x.shape
    half = y_rw.shape[2]
    return pl.pallas_call(
        functools.partial(_outmlp_kernel, ff_tile=ff_tile),
        out_shape=jax.ShapeDtypeStruct((B, T, D), F32),
        grid=(B, T // tm),
        in_specs=[pl.BlockSpec((1, tm, D), lambda b, t: (b, t, 0)),
                  pl.BlockSpec((1, tm, half), lambda b, t: (b, t, 0)),
                  pl.BlockSpec((1, tm, half), lambda b, t: (b, t, 0)),
                  pl.BlockSpec((1, N_MOD, D), lambda b, t: (b, 0, 0)),
                  _const_spec(w_o.shape, single_buffer=True), _const_spec((1, D)),
                  _const_spec(w1.shape, single_buffer=True),
                  _const_spec(w2.shape, single_buffer=True), _const_spec((1, D))],
        out_specs=pl.BlockSpec((1, tm, D), lambda b, t: (b, t, 0)),
        compiler_params=pltpu.CompilerParams(
            dimension_semantics=("arbitrary", "arbitrary"), vmem_limit_bytes=VMEM_LIMIT),
        name="outmlp",
    )(x, y_rw, y_fx, mod3, w_o, norm2_g.reshape(1, D), w1, w2, final_g.reshape(1, D))


def _pad_cols(a, n):
    return jnp.pad(a, ((0, 0), (0, n - a.shape[1])))


def _pad_rows(a, n):
    return jnp.pad(a, ((0, n - a.shape[0]), (0, 0)))


def _branches(x, c, w_ada, b_ada, norm1_g, w_in, mu_shift, w0, w_up_decay, a0, w_up_a, w_up_g,
              k_k, k_a, r_k, gn_g, gn_b, b_f, q_norm_g, k_norm_g, fox_out_g):
    B, T, D = x.shape
    W = w0.shape[0]
    heads = b_f.shape[0]
    rw_n = 3 * W + DECAY_LORA + AAA_LORA + GATE_LORA

    def regroup(a):
        rw, fx = a[:, :rw_n], a[:, rw_n:]
        o = 3 * W
        return jnp.concatenate([
            rw[:, :o],
            _pad_cols(rw[:, o:o + DECAY_LORA], LANES),
            _pad_cols(rw[:, o + DECAY_LORA:o + DECAY_LORA + AAA_LORA], LANES),
            _pad_cols(jnp.concatenate([rw[:, o + DECAY_LORA + AAA_LORA:], fx[:, 4 * W:]], axis=1),
                      2 * LANES),
            fx[:, :4 * W]], axis=1)

    rw_cols = 3 * W + 4 * LANES
    w_all = regroup(w_in.astype(BF16))
    mu_p = regroup(jnp.pad(mu_shift.reshape(1, rw_n), ((0, 0), (0, w_in.shape[1] - rw_n))))[:, :rw_cols]
    f_off = GATE_LORA - LANES
    bf_p = jnp.pad(b_f.reshape(1, heads), ((0, 0), (f_off, LANES - f_off - heads)))
    wdw = _pad_rows(w_up_decay, LANES).astype(BF16)
    waw = _pad_rows(w_up_a, LANES).astype(BF16)
    wgw = _pad_rows(w_up_g, 2 * LANES).astype(BF16)

    mod3 = _ada(c, w_ada, b_ada).reshape(B, N_MOD, D)
    rw, qa, ka, vt, og = _inproj(x, mod3, norm1_g, w_all, mu_p, bf_p, q_norm_g, k_norm_g,
                                 rw_cols, f_off)
    y_rw = _rwkv(rw, w0, wdw, a0, waw, wgw, k_k, k_a, r_k.reshape(-1), gn_g, gn_b, W)
    y_fx = _fox(qa, ka, vt, og, fox_out_g, W)
    return y_rw, y_fx, mod3


def kernel(x, c, w_ada, b_ada, norm1_g, w_in, mu_shift, w0, w_up_decay, a0, w_up_a, w_up_g,
           k_k, k_a, r_k, gn_g, gn_b, b_f, q_norm_g, k_norm_g, fox_out_g, w_o, norm2_g,
           w_mlp1, w_mlp2, final_g):
    y_rw, y_fx, mod3 = _branches(x, c, w_ada, b_ada, norm1_g, w_in, mu_shift, w0, w_up_decay,
                                 a0, w_up_a, w_up_g, k_k, k_a, r_k, gn_g, gn_b, b_f,
                                 q_norm_g, k_norm_g, fox_out_g)
    return _outmlp(x, y_rw, y_fx, mod3, w_o.astype(BF16), norm2_g,
                   w_mlp1.astype(BF16), w_mlp2.astype(BF16), final_g)
```

```python
import functools
import math

import jax
import jax.numpy as jnp
import numpy as np
from jax import lax
from jax.experimental import pallas as pl
from jax.experimental.pallas import tpu as pltpu

F32 = jnp.float32
BF16 = jnp.bfloat16

HEAD_DIM = 64
LANES = 128
MXU_TILE = 256
NORM_EPS = 1e-6
GN_EPS = 64e-5
NEG_INF = -1e30
LOG2E = 1.4426950408889634
N_MOD = 6
DECAY_LORA = 64
AAA_LORA = 64
GATE_LORA = 160
RW_CHUNK = 64
V_ROWS = HEAD_DIM + 16
VMEM_LIMIT = 56 * 1024 * 1024


def _dot(a, b):
    return jnp.dot(a, b, preferred_element_type=F32)


def _dot_nt(a, b):
    return lax.dot_general(a, b, (((1,), (1,)), ((), ())), preferred_element_type=F32)


def _dot_tn(a, b):
    return lax.dot_general(a, b, (((0,), (0,)), ((), ())), preferred_element_type=F32)


def _bf(x):
    return x.astype(BF16)


def _sigmoid(x):
    return 1.0 / (1.0 + jnp.exp(-x))


def _const_spec(shape, single_buffer=False):
    n = len(shape)
    mode = pl.Buffered(1) if single_buffer else None
    return pl.BlockSpec(shape, lambda *_: (0,) * n, pipeline_mode=mode)


def _ada_kernel(c_ref, w_ref, b_ref, o_ref):
    c = c_ref[...]
    a = c * _sigmoid(c)
    w = w_ref[...]
    a_hi, w_hi = _bf(a), _bf(w)
    a_lo, w_lo = _bf(a - a_hi.astype(F32)), _bf(w - w_hi.astype(F32))
    o_ref[...] = (_dot(a_hi, w_hi) + (_dot(a_hi, w_lo) + _dot(a_lo, w_hi))) + b_ref[...]


def _ada(c, w_ada, b_ada, tn=2048):
    B, D = c.shape
    n = w_ada.shape[1]
    return pl.pallas_call(
        _ada_kernel,
        out_shape=jax.ShapeDtypeStruct((B, n), F32),
        grid=(n // tn,),
        in_specs=[pl.BlockSpec((B, D), lambda j: (0, 0)),
                  pl.BlockSpec((D, tn), lambda j: (0, j)),
                  pl.BlockSpec((1, tn), lambda j: (0, j))],
        out_specs=pl.BlockSpec((B, tn), lambda j: (0, j)),
        name="ada",
    )(c, w_ada, b_ada.reshape(1, n))


N_PIECES = 3


def _fox_aug_tables(heads, f_off):
    groups = heads // 2
    sel = np.zeros((LANES, 2 * groups * LANES), np.float32)
    bias = np.zeros((1, 2 * groups * LANES), np.float32)
    for h in range(heads):
        base = (h // 2) * LANES + (HEAD_DIM if h % 2 == 0 else 0)
        kbase = groups * LANES + base
        for piece in range(N_PIECES):
            sel[f_off + piece * heads + h, base + piece] = 1.0
            sel[f_off + piece * heads + h, kbase + N_PIECES + piece] = -1.0
            bias[0, base + N_PIECES + piece] = 1.0
            bias[0, kbase + piece] = 1.0
    return jnp.asarray(sel, BF16), jnp.asarray(bias)


def _fox_layout(pf, fcum, f_off, qg_ref, kg_ref, sel_ref, bias_ref, qa_ref, ka_ref, v_ref, og_ref):
    W = pf.shape[1] // 4
    heads = W // HEAD_DIM
    groups = W // LANES
    tm = pf.shape[0]
    lane = lax.broadcasted_iota(jnp.int32, (tm, LANES), 1)
    lane_f = lane - f_off
    fcum = fcum * LOG2E
    f_hi = fcum.astype(BF16).astype(F32)
    r1 = fcum - f_hi
    f_mid = r1.astype(BF16).astype(F32)
    f_lo = r1 - f_mid
    packed = jnp.where(lane_f < 0, 0.0, jnp.where(lane_f < heads, f_hi, jnp.where(
        lane_f < 2 * heads, pltpu.roll(f_mid, heads, axis=1), jnp.where(
            lane_f < 3 * heads, pltpu.roll(f_lo, 2 * heads, axis=1), 0.0))))
    aug = _dot(packed.astype(BF16), sel_ref[...]) + bias_ref[...]
    first = lane < HEAD_DIM
    for side, (gain_ref, out_ref) in enumerate(((qg_ref, qa_ref), (kg_ref, ka_ref))):
        for g in range(groups):
            x = pf[:, side * W + g * LANES:side * W + (g + 1) * LANES]
            sq = x * x
            lo = jnp.sum(jnp.where(first, sq, 0.0), axis=-1, keepdims=True)
            hi = jnp.sum(jnp.where(first, 0.0, sq), axis=-1, keepdims=True)
            ms = jnp.where(first, lo, hi) * (1.0 / HEAD_DIM)
            xn = x * lax.rsqrt(ms + NORM_EPS) * gain_ref[...]
            a = aug[:, (side * groups + g) * LANES:(side * groups + g + 1) * LANES]
            out_ref[0, 2 * g] = jnp.where(first, xn, a).astype(BF16)
            out_ref[0, 2 * g + 1] = jnp.where(first, a, xn).astype(BF16)
    pad = (lax.broadcasted_iota(jnp.int32, (V_ROWS - HEAD_DIM, tm), 0) == 0).astype(BF16)
    for g in range(groups):
        vt = pf[:, 2 * W + g * LANES:2 * W + (g + 1) * LANES].T.astype(BF16)
        for hh in range(2):
            v_ref[0, 2 * g + hh, 0, :HEAD_DIM, :] = vt[hh * HEAD_DIM:(hh + 1) * HEAD_DIM]
            v_ref[0, 2 * g + hh, 0, HEAD_DIM:, :] = pad
    og_ref[0] = pf[:, 3 * W:4 * W].astype(BF16)


def _inproj_kernel(x_ref, mod_ref, g_ref, w_ref, mu_ref, bf_ref, qg_ref, kg_ref, sel_ref, bias_ref,
                   rw_ref, qa_ref, ka_ref, v_ref, og_ref, prev_scr, fcar_scr, *, rw_cols, f_off):
    t = pl.program_id(1)
    tm = x_ref.shape[1]

    @pl.when(t == 0)
    def _():
        prev_scr[...] = jnp.zeros_like(prev_scr)
        fcar_scr[...] = jnp.zeros_like(fcar_scr)

    x = x_ref[0]
    shift = mod_ref[0, 0:1, :]
    scale = mod_ref[0, 1:2, :]
    ms = jnp.mean(x * x, axis=-1, keepdims=True)
    u = (x * lax.rsqrt(ms + NORM_EPS) * g_ref[...]) * (1.0 + scale) + shift
    u = u.astype(BF16)

    p = _dot(u, w_ref[:, :rw_cols])
    row = lax.broadcasted_iota(jnp.int32, p.shape, 0)
    prev = jnp.where(row == 0, prev_scr[0:1, :], pltpu.roll(p, shift=1, axis=0))
    prev_scr[0:1, :] = p[tm - 1:tm, :]
    rw_ref[0] = p + (prev - p) * mu_ref[...]

    pf = _dot(u, w_ref[:, rw_cols:])
    z = p[:, rw_cols - LANES:] + bf_ref[...]
    logf = jnp.minimum(z, 0.0) - jnp.log(1.0 + jnp.exp(-jnp.abs(z)))
    frow = lax.broadcasted_iota(jnp.int32, logf.shape, 0)
    fcum = logf
    shift = 1
    while shift < tm:
        fcum = fcum + jnp.where(frow >= shift, pltpu.roll(fcum, shift, axis=0), 0.0)
        shift *= 2
    fcum = fcum + fcar_scr[0:1, :]
    fcar_scr[0:1, :] = fcum[tm - 1:tm, :]
    _fox_layout(pf, fcum, f_off, qg_ref, kg_ref, sel_ref, bias_ref, qa_ref, ka_ref, v_ref, og_ref)


def _inproj(x, mod3, norm1_g, w_all, mu_p, bf_p, q_norm_g, k_norm_g, rw_cols, f_off, tm=512):
    B, T, D = x.shape
    ncols = w_all.shape[1]
    width = (ncols - rw_cols) // 4
    heads = width // HEAD_DIM
    sel, bias = _fox_aug_tables(heads, f_off)
    scale = LOG2E / math.sqrt(HEAD_DIM)
    qg2 = jnp.tile(q_norm_g * scale, 2).reshape(1, LANES)
    kg2 = jnp.tile(k_norm_g, 2).reshape(1, LANES)
    return pl.pallas_call(
        functools.partial(_inproj_kernel, rw_cols=rw_cols, f_off=f_off),
        out_shape=(jax.ShapeDtypeStruct((B, T, rw_cols), F32),
                   jax.ShapeDtypeStruct((B, heads, T, LANES), BF16),
                   jax.ShapeDtypeStruct((B, heads, T, LANES), BF16),
                   jax.ShapeDtypeStruct((B, heads, T // tm, V_ROWS, tm), BF16),
                   jax.ShapeDtypeStruct((B, T, width), BF16)),
        grid=(B, T // tm),
        in_specs=[pl.BlockSpec((1, tm, D), lambda b, t: (b, t, 0)),
                  pl.BlockSpec((1, N_MOD, D), lambda b, t: (b, 0, 0)),
                  _const_spec((1, D)),
                  _const_spec((D, ncols), single_buffer=True),
                  _const_spec((1, rw_cols)),
                  _const_spec((1, LANES)),
                  _const_spec((1, LANES)), _const_spec((1, LANES)),
                  _const_spec(sel.shape, single_buffer=True), _const_spec(bias.shape)],
        out_specs=(pl.BlockSpec((1, tm, rw_cols), lambda b, t: (b, t, 0)),
                   pl.BlockSpec((1, heads, tm, LANES), lambda b, t: (b, 0, t, 0)),
                   pl.BlockSpec((1, heads, tm, LANES), lambda b, t: (b, 0, t, 0)),
                   pl.BlockSpec((1, heads, 1, V_ROWS, tm), lambda b, t: (b, 0, t, 0, 0)),
                   pl.BlockSpec((1, tm, width), lambda b, t: (b, t, 0))),
        scratch_shapes=[pltpu.VMEM((8, rw_cols), F32), pltpu.VMEM((8, LANES), F32)],
        compiler_params=pltpu.CompilerParams(
            dimension_semantics=("arbitrary", "arbitrary"), vmem_limit_bytes=VMEM_LIMIT),
        name="inproj",
    )(x, mod3, norm1_g.reshape(1, D), w_all, mu_p, bf_p, qg2, kg2, sel, bias)


def _segsum_lanes(x):
    first = lax.broadcasted_iota(jnp.int32, (x.shape[0], LANES), 1) < HEAD_DIM
    cols = []
    for j in range(x.shape[1] // LANES):
        xg = x[:, j * LANES:(j + 1) * LANES]
        lo = jnp.sum(jnp.where(first, xg, 0.0), axis=-1, keepdims=True)
        hi = jnp.sum(jnp.where(first, 0.0, xg), axis=-1, keepdims=True)
        cols.append(jnp.where(first, lo, hi))
    return jnp.concatenate(cols, axis=1)


def _segsum(x, bd):
    tile = bd.shape[0]
    xb = x.astype(BF16)
    return jnp.concatenate([_dot(xb[:, j * tile:(j + 1) * tile], bd)
                            for j in range(x.shape[1] // tile)], axis=1)


def _interleave(generators):
    live = [g for g in generators if g is not None]
    while live:
        for g in list(live):
            if next(g, StopIteration) is StopIteration:
                live.remove(g)


def _rwkv_kernel(p_ref, w0_ref, wdw_ref, a0_ref, waw_ref, wgw_ref, kk_ref, ka_ref, rk_ref,
                 gng_ref, gnb_ref, o_ref, s_scr, *, width, block_rows):
    t = pl.program_id(1)
    L = RW_CHUNK
    R = block_rows
    n_blocks = p_ref.shape[1] // R
    n_chunks = R // L
    W = width
    G2 = 2 * LANES
    groups = W // LANES
    items = [(c, j) for c in range(n_chunks) for j in range(groups)]
    rsl = lambda c: slice(c * L, (c + 1) * L)
    gsl = lambda j: slice(j * LANES, (j + 1) * LANES)

    @pl.when(t == 0)
    def _():
        s_scr[...] = jnp.zeros_like(s_scr)

    ri = lax.broadcasted_iota(jnp.int32, (MXU_TILE, MXU_TILE), 0)
    ci = lax.broadcasted_iota(jnp.int32, (MXU_TILE, MXU_TILE), 1)
    bd = ((ri // HEAD_DIM) == (ci // HEAD_DIM)).astype(BF16)
    tril = (lax.broadcasted_iota(jnp.int32, (L, L), 1)
            <= lax.broadcasted_iota(jnp.int32, (L, L), 0)).astype(BF16)
    o0 = 3 * W
    lane1 = lax.broadcasted_iota(jnp.int32, (L, LANES), 1)
    lane2 = lax.broadcasted_iota(jnp.int32, (L, G2), 1)
    row2 = lax.broadcasted_iota(jnp.int32, (L, G2), 0)
    first1 = lane1 < HEAD_DIM
    first2 = (lane2 % LANES) < HEAD_DIM
    strict2 = (lane2 % HEAD_DIM) < row2
    incl2 = (lane2 % HEAD_DIM) <= row2
    upper2 = lane2 >= LANES
    eye2 = ((lane1 % HEAD_DIM) == lax.broadcasted_iota(jnp.int32, (L, LANES), 0)).astype(F32)
    zero1 = jnp.zeros((L, LANES), BF16)
    rr = lax.broadcasted_iota(jnp.int32, (LANES, LANES), 0) < HEAD_DIM
    cc = lax.broadcasted_iota(jnp.int32, (LANES, LANES), 1) < HEAD_DIM
    diag_blocks = rr == cc

    def split(m, mask):
        return jnp.concatenate([jnp.where(mask, m, 0), jnp.where(mask, 0, m)], axis=0)

    def prepare(pb, b):
        p = p_ref[0, b * R:(b + 1) * R, :]
        r, k, v = p[:, 0:W], p[:, W:2 * W], p[:, 2 * W:3 * W]
        wd = p[:, o0:o0 + LANES]
        ad = p[:, o0 + LANES:o0 + 2 * LANES]
        gd = p[:, o0 + 2 * LANES:o0 + 4 * LANES]
        z = w0_ref[...] + _dot(_bf(jnp.tanh(wd)), wdw_ref[...])
        softplus = jnp.maximum(-z, 0.0) + jnp.log(1.0 + jnp.exp(-jnp.abs(z)))
        ld = -jnp.exp(-softplus - 0.5)
        yield
        a_sig = _sigmoid(a0_ref[...] + _dot(_bf(ad), waw_ref[...]))
        pb["gate"] = _dot(_bf(_sigmoid(gd)), wgw_ref[...])
        yield
        kk = k * kk_ref[...]
        kk = kk / jnp.maximum(jnp.sqrt(_segsum_lanes(kk * kk)), 1e-12)
        yield
        k2 = k * (1.0 + (a_sig - 1.0) * ka_ref[...])
        b_ = kk * a_sig
        pb["bonus"] = _segsum(r * k2 * rk_ref[...], bd) * v
        yield
        ld_hi = _bf(ld)
        ld_lo = _bf(ld - ld_hi.astype(F32))
        cums = [_dot(tril, ld_hi[rsl(c)]) + _dot(tril, ld_lo[rsl(c)]) for c in range(n_chunks)]
        pb["tots"] = [cum_c[L - 1:L, :] for cum_c in cums]
        cum = jnp.concatenate(cums, axis=0)
        tot_b = jnp.concatenate([jnp.broadcast_to(tc, (L, W)) for tc in pb["tots"]], axis=0)
        yield
        e_neg = jnp.exp(-cum)
        rt = r * jnp.exp(cum)
        pb.update(rt=rt, rt_m=_bf(rt), at=_bf(-kk * jnp.exp(cum - ld)), bt=_bf(b_ * e_neg),
                  kt=_bf(k2 * e_neg))
        yield
        e_rem = jnp.exp(tot_b - cum)
        pb.update(bh=_bf(b_ * e_rem), kh=_bf(k2 * e_rem), vm=_bf(v))

    def chains(pb, cb):
        at, rt_m, bt, kt, vm = pb["at"], pb["rt_m"], pb["bt"], pb["kt"], pb["vm"]
        g_bot, akm, zz = {}, {}, {}
        for c, j in items:
            ar = jnp.concatenate([at[rsl(c), gsl(j)], rt_m[rsl(c), gsl(j)]], axis=0)
            rhs = jnp.concatenate([split(bt[rsl(c), gsl(j)], first1),
                                   split(kt[rsl(c), gsl(j)], first1)], axis=0)
            gm = _dot_nt(ar, rhs)
            top = jnp.where(strict2, gm[:L], 0.0)
            g_bot[c, j] = _bf(jnp.where(incl2, gm[L:], 0.0))
            akm[c, j] = _bf(top[:, LANES:])
            zz[c, j] = jnp.concatenate([top[:, :LANES], eye2], axis=1)
        yield
        for _ in range(int(math.log2(L))):
            for it in items:
                zb = _bf(zz[it])
                out = _dot(zb[:, :LANES], split(zb, first2))
                zz[it] = out + jnp.where(upper2, zz[it], 0.0)
            yield
        vbd = {(c, j): split(vm[rsl(c), gsl(j)], first1) for c, j in items}
        akv = {it: _bf(_dot(akm[it], vbd[it])) for it in items}
        yield
        x = {}
        for c, j in items:
            zed = jnp.concatenate([at[rsl(c), gsl(j)], akv[c, j]], axis=1)
            x[c, j] = _bf(_dot(_bf(zz[c, j][:, LANES:]), split(zed, first2)))
        yield
        cb.update(r_eff={}, y0={}, mt={}, nt={})
        for c, j in items:
            w4 = jnp.concatenate([split(x[c, j], first2),
                                  jnp.concatenate([jnp.concatenate([zero1, zero1], axis=0),
                                                   vbd[c, j]], axis=1)], axis=0)
            out = _dot(g_bot[c, j], w4)
            cb["r_eff"][c, j] = _bf(pb["rt"][rsl(c), gsl(j)] + out[:, :LANES])
            cb["y0"][c, j] = out[:, LANES:]
        yield
        for c, j in items:
            lhs = jnp.concatenate([pb["bh"][rsl(c), gsl(j)], pb["kh"][rsl(c), gsl(j)]], axis=0)
            rhs = jnp.concatenate([x[c, j], jnp.concatenate([zero1, vm[rsl(c), gsl(j)]], axis=1)],
                                  axis=0)
            pt = _dot_tn(lhs, rhs)
            cb["mt"][c, j] = _bf(jnp.where(diag_blocks, pt[:, :LANES], 0.0))
            cb["nt"][c, j] = jnp.where(diag_blocks, pt[:, LANES:], 0.0)
        yield
        cb["e_cols"] = [
            jnp.concatenate([jnp.exp(pb["tots"][c][:, gsl(j)]) for c in range(n_chunks)]
                            + [jnp.zeros((LANES - n_chunks, LANES), F32)], axis=0).T
            for j in range(groups)]

    state = [s_scr[j] for j in range(groups)]

    def tail(pb, cb, b):
        y_rows = []
        for c in range(n_chunks):
            hb = [_bf(s) for s in state]
            y_rows.append(jnp.concatenate(
                [_dot(cb["r_eff"][c, j], hb[j]) + cb["y0"][c, j] for j in range(groups)], axis=1))
            for j in range(groups):
                state[j] = (state[j] * cb["e_cols"][j][:, c:c + 1] + _dot(cb["mt"][c, j], hb[j])
                            + cb["nt"][c, j])
            yield
        y = jnp.concatenate(y_rows, axis=0)
        inv_n = 1.0 / HEAD_DIM
        mean = _segsum_lanes(y) * inv_n
        d = y - mean
        yield
        var = _segsum_lanes(d * d) * inv_n
        yn = d * lax.rsqrt(var + GN_EPS) * gng_ref[...] + gnb_ref[...]
        o_ref[0, b * R:(b + 1) * R, :] = ((yn + pb["bonus"]) * pb["gate"]).astype(o_ref.dtype)

    prep = [dict() for _ in range(n_blocks)]
    chain = [dict() for _ in range(n_blocks)]
    _interleave([prepare(prep[0], 0)])
    for b in range(n_blocks):
        _interleave([chains(prep[b], chain[b]),
                     prepare(prep[b + 1], b + 1) if b + 1 < n_blocks else None,
                     tail(prep[b - 1], chain[b - 1], b - 1) if b > 0 else None])
    _interleave([tail(prep[-1], chain[-1], n_blocks - 1)])
    for j in range(groups):
        s_scr[j] = state[j]


def _rwkv(rw, w0, wdw, a0, waw, wgw, k_k, k_a, r_k, gn_g, gn_b, width, rows=512, block_rows=256):
    B, T, cols = rw.shape
    row = lambda a: a.reshape(1, width)
    vec = _const_spec((1, width))
    return pl.pallas_call(
        functools.partial(_rwkv_kernel, width=width, block_rows=block_rows),
        out_shape=jax.ShapeDtypeStruct((B, T, width), BF16),
        grid=(B, T // rows),
        in_specs=[pl.BlockSpec((1, rows, cols), lambda b, t: (b, t, 0)),
                  vec, _const_spec(wdw.shape), vec, _const_spec(waw.shape),
                  _const_spec(wgw.shape), vec, vec, vec, vec, vec],
        out_specs=pl.BlockSpec((1, rows, width), lambda b, t: (b, t, 0)),
        scratch_shapes=[pltpu.VMEM((width // LANES, LANES, LANES), F32)],
        compiler_params=pltpu.CompilerParams(
            dimension_semantics=("arbitrary", "arbitrary"), vmem_limit_bytes=VMEM_LIMIT),
        name="rwkv",
    )(rw, row(w0), wdw, row(a0), waw, wgw, row(k_k), row(k_a), row(r_k), row(gn_g), row(gn_b))


def _fox_kernel(qtab, ktab, qa_ref, ka_ref, vt_ref, og_ref, g_ref, o_ref,
                m_all, acc_all, s_scr, p_scr, al_scr, *, blk, cb, n_off, hps):
    n_q = qa_ref.shape[2] // blk
    n_el = n_off + n_q
    chains = [(hh, c) for hh in range(hps) for c in range(blk // cb)]
    csl = lambda c: slice(c * cb, (c + 1) * cb)

    m_all[...] = jnp.full_like(m_all, NEG_INF)
    acc_all[...] = jnp.zeros_like(acc_all)

    def element(t):
        if isinstance(t, int) and t >= n_off:
            return t - n_off, t - n_off, True
        return qtab[t], ktab[t], False

    def start(i, size):
        return i * size if isinstance(i, int) else pl.multiple_of(i * size, size)

    def n_keys(c, diag):
        return (c + 1) * cb if diag else blk

    def logits(t, par):
        qi, ki, diag = element(t)
        for hh, c in chains:
            nk = n_keys(c, diag)
            s_scr[par, hh, :nk, csl(c)] = _dot_nt(
                ka_ref[0, hh, pl.ds(start(ki, blk), nk), :],
                qa_ref[0, hh, pl.ds(start(qi, blk) + c * cb, cb), :])
            yield

    def softmax(t, par):
        qi, _, diag = element(t)
        for hh, c in chains:
            nk = n_keys(c, diag)
            s = s_scr[par, hh, :nk, csl(c)]
            if diag:
                key = lax.broadcasted_iota(jnp.int32, (nk, cb), 0)
                qry = c * cb + lax.broadcasted_iota(jnp.int32, (nk, cb), 1)
                s = jnp.where(key <= qry, s, NEG_INF)
            m_prev = m_all[qi, hh, :, csl(c)]
            m_new = jnp.maximum(m_prev, jnp.max(s, axis=0, keepdims=True))
            m_all[qi, hh, :, csl(c)] = m_new
            al_scr[par, hh, :, csl(c)] = jnp.exp2(m_prev - m_new)
            p_scr[par, hh, :nk, csl(c)] = jnp.exp2(s - m_new).astype(BF16)
            yield

    def values(t, par):
        qi, ki, diag = element(t)
        for hh, c in chains:
            nk = n_keys(c, diag)
            pv = _dot(vt_ref[0, hh, ki, :, :nk], p_scr[par, hh, :nk, csl(c)])
            acc_all[qi, hh, :, csl(c)] = (al_scr[par, hh, :, csl(c)] * acc_all[qi, hh, :, csl(c)]
                                          + pv)
            yield
        if diag:
            rows = slice(qi * blk, (qi + 1) * blk)
            for pr in range(hps // 2):
                normed = []
                for hh in (2 * pr, 2 * pr + 1):
                    acc = acc_all[qi, hh]
                    o = acc[:HEAD_DIM] / acc[HEAD_DIM:HEAD_DIM + 1]
                    ms = jnp.mean(o * o, axis=0, keepdims=True)
                    normed.append(o * lax.rsqrt(ms + NORM_EPS))
                on = jnp.concatenate(normed, axis=0).T
                ls = slice(pr * LANES, (pr + 1) * LANES)
                o_ref[0, rows, ls] = (on * g_ref[:, ls]
                                      * _sigmoid(og_ref[0, rows, ls].astype(F32))).astype(o_ref.dtype)

    def step(t, par):
        _interleave([stage(t - lag, (par + lag) % 2)
                     for lag, stage in enumerate((logits, softmax, values))
                     if not isinstance(t, int) or 0 <= t - lag < n_el])

    def two_steps(u, carry):
        step(2 * u + 2, 0)
        step(2 * u + 3, 1)
        return carry

    step(0, 0)
    step(1, 1)
    lax.fori_loop(0, (n_off - 2) // 2, two_steps, 0)
    for t in range(n_off, n_el + 2):
        step(t, t % 2)


def _fox(qa, ka, vt, og, fox_out_g, width, cb=256, hps=4):
    B, H, T, _ = qa.shape
    blk = vt.shape[4]
    n_q = T // blk
    pairs = [(qi, ki) for qi in range(n_q) for ki in range(qi)]
    n_off = len(pairs)
    assert n_off >= 2 and n_off % 2 == 0, "pipeline loop is unrolled by two"
    qtab = jnp.asarray(np.array([p[0] for p in pairs], np.int32))
    ktab = jnp.asarray(np.array([p[1] for p in pairs], np.int32))
    grid_spec = pltpu.PrefetchScalarGridSpec(
        num_scalar_prefetch=2,
        grid=(B, H // hps),
        in_specs=[pl.BlockSpec((1, hps, T, LANES), lambda b, h, qt, kt: (b, h, 0, 0)),
                  pl.BlockSpec((1, hps, T, LANES), lambda b, h, qt, kt: (b, h, 0, 0)),
                  pl.BlockSpec((1, hps, n_q, V_ROWS, blk), lambda b, h, qt, kt: (b, h, 0, 0, 0)),
                  pl.BlockSpec((1, T, hps * HEAD_DIM), lambda b, h, qt, kt: (b, 0, h)),
                  pl.BlockSpec((1, hps * HEAD_DIM), lambda b, h, qt, kt: (0, h))],
        out_specs=pl.BlockSpec((1, T, hps * HEAD_DIM), lambda b, h, qt, kt: (b, 0, h)),
        scratch_shapes=[pltpu.VMEM((n_q, hps, 1, blk), F32),
                        pltpu.VMEM((n_q, hps, V_ROWS, blk), F32),
                        pltpu.VMEM((2, hps, blk, blk), F32), pltpu.VMEM((2, hps, blk, blk), BF16),
                        pltpu.VMEM((2, hps, 1, blk), F32)])
    return pl.pallas_call(
        functools.partial(_fox_kernel, blk=blk, cb=cb, n_off=n_off, hps=hps),
        out_shape=jax.ShapeDtypeStruct((B, T, width), BF16),
        grid_spec=grid_spec,
        compiler_params=pltpu.CompilerParams(
            dimension_semantics=("arbitrary",) * 2, vmem_limit_bytes=VMEM_LIMIT),
        name="fox",
    )(qtab, ktab, qa, ka, vt, og, fox_out_g.reshape(1, width))


def _outmlp_kernel(x_ref, yrw_ref, yfx_ref, mod_ref, wo_ref, g2_ref, w1_ref, w2_ref, gf_ref,
                   o_ref, wo_scr, w1_scr, w2_scr, *, n_wo, n_ff):
    s = pl.program_id(0)
    n_cast = n_wo + 2 * n_ff

    @pl.when(s < n_wo)
    def _():
        wo_scr[s] = wo_ref[...].astype(BF16)

    @pl.when((s >= n_wo) & (s < n_wo + n_ff))
    def _():
        w1_scr[s - n_wo] = w1_ref[...].astype(BF16)

    @pl.when((s >= n_wo + n_ff) & (s < n_cast))
    def _():
        w2_scr[s - n_wo - n_ff] = w2_ref[...].astype(BF16)

    @pl.when(s >= n_cast)
    def _():
        x = x_ref[0]
        gate1 = mod_ref[0, 2:3, :]
        shift2 = mod_ref[0, 3:4, :]
        scale2 = mod_ref[0, 4:5, :]
        gate2 = mod_ref[0, 5:6, :]
        y = _dot(yrw_ref[0], wo_scr[0]) + _dot(yfx_ref[0], wo_scr[1])
        h1 = x + gate1 * y
        ms = jnp.mean(h1 * h1, axis=-1, keepdims=True)
        u = ((h1 * lax.rsqrt(ms + NORM_EPS) * g2_ref[...]) * (1.0 + scale2) + shift2).astype(BF16)
        acc = jnp.zeros_like(x)
        for j in range(n_ff):
            hid = jnp.maximum(_dot(u, w1_scr[j]), 0.0)
            acc = acc + _dot((hid * hid).astype(BF16), w2_scr[j])
        h2 = h1 + gate2 * acc
        ms2 = jnp.mean(h2 * h2, axis=-1, keepdims=True)
        o_ref[0] = h2 * lax.rsqrt(ms2 + NORM_EPS) * gf_ref[...]


def _outmlp(x, y_rw, y_fx, mod3, w_o, norm2_g, w1, w2, final_g, tm=512, ff_tile=512):
    B, T, D = x.shape
    half = y_rw.shape[2]
    n_t = T // tm
    n_wo = w_o.shape[0] // half
    n_ff = w1.shape[1] // ff_tile
    n_cast = n_wo + 2 * n_ff
    assert w_o.shape[0] == n_wo * half and n_wo == 2

    def tile(s):
        r = jnp.maximum(s - n_cast, 0)
        return r // n_t, r % n_t

    rows = lambda s: (*tile(s), 0)
    return pl.pallas_call(
        functools.partial(_outmlp_kernel, n_wo=n_wo, n_ff=n_ff),
        out_shape=jax.ShapeDtypeStruct((B, T, D), F32),
        grid=(n_cast + B * n_t,),
        in_specs=[pl.BlockSpec((1, tm, D), rows),
                  pl.BlockSpec((1, tm, half), rows),
                  pl.BlockSpec((1, tm, half), rows),
                  pl.BlockSpec((1, N_MOD, D), lambda s: (tile(s)[0], 0, 0)),
                  pl.BlockSpec((half, D), lambda s: (jnp.clip(s, 0, n_wo - 1), 0)),
                  _const_spec((1, D)),
                  pl.BlockSpec((D, ff_tile), lambda s: (0, jnp.clip(s - n_wo, 0, n_ff - 1))),
                  pl.BlockSpec((ff_tile, D), lambda s: (jnp.clip(s - n_wo - n_ff, 0, n_ff - 1), 0)),
                  _const_spec((1, D))],
        out_specs=pl.BlockSpec((1, tm, D), rows),
        scratch_shapes=[pltpu.VMEM((n_wo, half, D), BF16), pltpu.VMEM((n_ff, D, ff_tile), BF16),
                        pltpu.VMEM((n_ff, ff_tile, D), BF16)],
        compiler_params=pltpu.CompilerParams(
            dimension_semantics=("arbitrary",), vmem_limit_bytes=VMEM_LIMIT),
        name="outmlp",
    )(x, y_rw, y_fx, mod3, w_o, norm2_g.reshape(1, D), w1, w2, final_g.reshape(1, D))


def _pad_cols(a, n):
    return jnp.pad(a, ((0, 0), (0, n - a.shape[1])))


def _pad_rows(a, n):
    return jnp.pad(a, ((0, n - a.shape[0]), (0, 0)))


def _branches(x, c, w_ada, b_ada, norm1_g, w_in, mu_shift, w0, w_up_decay, a0, w_up_a, w_up_g,
              k_k, k_a, r_k, gn_g, gn_b, b_f, q_norm_g, k_norm_g, fox_out_g):
    B, T, D = x.shape
    W = w0.shape[0]
    heads = b_f.shape[0]
    rw_n = 3 * W + DECAY_LORA + AAA_LORA + GATE_LORA

    def regroup(a):
        rw, fx = a[:, :rw_n], a[:, rw_n:]
        o = 3 * W
        return jnp.concatenate([
            rw[:, :o],
            _pad_cols(rw[:, o:o + DECAY_LORA], LANES),
            _pad_cols(rw[:, o + DECAY_LORA:o + DECAY_LORA + AAA_LORA], LANES),
            _pad_cols(jnp.concatenate([rw[:, o + DECAY_LORA + AAA_LORA:], fx[:, 4 * W:]], axis=1),
                      2 * LANES),
            fx[:, :4 * W]], axis=1)

    rw_cols = 3 * W + 4 * LANES
    w_all = regroup(w_in.astype(BF16))
    mu_p = regroup(jnp.pad(mu_shift.reshape(1, rw_n), ((0, 0), (0, w_in.shape[1] - rw_n))))[:, :rw_cols]
    f_off = GATE_LORA - LANES
    bf_p = jnp.pad(b_f.reshape(1, heads), ((0, 0), (f_off, LANES - f_off - heads)))
    wdw = _pad_rows(w_up_decay, LANES).astype(BF16)
    waw = _pad_rows(w_up_a, LANES).astype(BF16)
    wgw = _pad_rows(w_up_g, 2 * LANES).astype(BF16)

    mod3 = _ada(c, w_ada, b_ada).reshape(B, N_MOD, D)
    rw, qa, ka, vt, og = _inproj(x, mod3, norm1_g, w_all, mu_p, bf_p, q_norm_g, k_norm_g,
                                 rw_cols, f_off)
    y_rw = _rwkv(rw, w0, wdw, a0, waw, wgw, k_k, k_a, r_k.reshape(-1), gn_g, gn_b, W)
    y_fx = _fox(qa, ka, vt, og, fox_out_g, W)
    return y_rw, y_fx, mod3


def kernel(x, c, w_ada, b_ada, norm1_g, w_in, mu_shift, w0, w_up_decay, a0, w_up_a, w_up_g,
           k_k, k_a, r_k, gn_g, gn_b, b_f, q_norm_g, k_norm_g, fox_out_g, w_o, norm2_g,
           w_mlp1, w_mlp2, final_g):
    y_rw, y_fx, mod3 = _branches(x, c, w_ada, b_ada, norm1_g, w_in, mu_shift, w0, w_up_decay,
                                 a0, w_up_a, w_up_g, k_k, k_a, r_k, gn_g, gn_b, b_f,
                                 q_norm_g, k_norm_g, fox_out_g)
    return _outmlp(x, y_rw, y_fx, mod3, w_o, norm2_g, w_mlp1, w_mlp2, final_g)
```

```python
import functools
import math

import jax
import jax.numpy as jnp
import numpy as np
from jax import lax
from jax.experimental import pallas as pl
from jax.experimental.pallas import tpu as pltpu

F32 = jnp.float32
BF16 = jnp.bfloat16

HEAD_DIM = 64
LANES = 128
MXU_TILE = 256
NORM_EPS = 1e-6
GN_EPS = 64e-5
NEG_INF = -1e30
LOG2E = 1.4426950408889634
N_MOD = 6
DECAY_LORA = 64
AAA_LORA = 64
GATE_LORA = 160
RW_CHUNK = 64
V_ROWS = HEAD_DIM + 16
VMEM_LIMIT = 56 * 1024 * 1024


def _dot(a, b):
    return jnp.dot(a, b, preferred_element_type=F32)


def _dot_nt(a, b):
    return lax.dot_general(a, b, (((1,), (1,)), ((), ())), preferred_element_type=F32)


def _dot_tn(a, b):
    return lax.dot_general(a, b, (((0,), (0,)), ((), ())), preferred_element_type=F32)


def _bf(x):
    return x.astype(BF16)


def _sigmoid(x):
    return 1.0 / (1.0 + jnp.exp(-x))


def _const_spec(shape, single_buffer=False):
    n = len(shape)
    mode = pl.Buffered(1) if single_buffer else None
    return pl.BlockSpec(shape, lambda *_: (0,) * n, pipeline_mode=mode)


def _ada_kernel(c_ref, w_ref, b_ref, o_ref):
    c = c_ref[...]
    a = c * _sigmoid(c)
    w = w_ref[...]
    a_hi, w_hi = _bf(a), _bf(w)
    a_lo, w_lo = _bf(a - a_hi.astype(F32)), _bf(w - w_hi.astype(F32))
    o_ref[...] = (_dot(a_hi, w_hi) + (_dot(a_hi, w_lo) + _dot(a_lo, w_hi))) + b_ref[...]


def _ada(c, w_ada, b_ada, tn=2048):
    B, D = c.shape
    n = w_ada.shape[1]
    return pl.pallas_call(
        _ada_kernel,
        out_shape=jax.ShapeDtypeStruct((B, n), F32),
        grid=(n // tn,),
        in_specs=[pl.BlockSpec((B, D), lambda j: (0, 0)),
                  pl.BlockSpec((D, tn), lambda j: (0, j)),
                  pl.BlockSpec((1, tn), lambda j: (0, j))],
        out_specs=pl.BlockSpec((B, tn), lambda j: (0, j)),
        name="ada",
    )(c, w_ada, b_ada.reshape(1, n))


N_PIECES = 3


def _fox_aug_tables(heads, f_off):
    groups = heads // 2
    sel = np.zeros((LANES, 2 * groups * LANES), np.float32)
    bias = np.zeros((1, 2 * groups * LANES), np.float32)
    for h in range(heads):
        base = (h // 2) * LANES + (HEAD_DIM if h % 2 == 0 else 0)
        kbase = groups * LANES + base
        for piece in range(N_PIECES):
            sel[f_off + piece * heads + h, base + piece] = 1.0
            sel[f_off + piece * heads + h, kbase + N_PIECES + piece] = -1.0
            bias[0, base + N_PIECES + piece] = 1.0
            bias[0, kbase + piece] = 1.0
    return jnp.asarray(sel, BF16), jnp.asarray(bias)


def _fox_layout(pf, fcum, f_off, qg_ref, kg_ref, sel_ref, bias_ref, qa_ref, ka_ref, v_ref, og_ref):
    W = pf.shape[1] // 4
    heads = W // HEAD_DIM
    groups = W // LANES
    tm = pf.shape[0]
    lane = lax.broadcasted_iota(jnp.int32, (tm, LANES), 1)
    lane_f = lane - f_off
    fcum = fcum * LOG2E
    f_hi = fcum.astype(BF16).astype(F32)
    r1 = fcum - f_hi
    f_mid = r1.astype(BF16).astype(F32)
    f_lo = r1 - f_mid
    packed = jnp.where(lane_f < 0, 0.0, jnp.where(lane_f < heads, f_hi, jnp.where(
        lane_f < 2 * heads, pltpu.roll(f_mid, heads, axis=1), jnp.where(
            lane_f < 3 * heads, pltpu.roll(f_lo, 2 * heads, axis=1), 0.0))))
    aug = _dot(packed.astype(BF16), sel_ref[...]) + bias_ref[...]
    first = lane < HEAD_DIM
    for side, (gain_ref, out_ref) in enumerate(((qg_ref, qa_ref), (kg_ref, ka_ref))):
        for g in range(groups):
            x = pf[:, side * W + g * LANES:side * W + (g + 1) * LANES]
            sq = x * x
            lo = jnp.sum(jnp.where(first, sq, 0.0), axis=-1, keepdims=True)
            hi = jnp.sum(jnp.where(first, 0.0, sq), axis=-1, keepdims=True)
            ms = jnp.where(first, lo, hi) * (1.0 / HEAD_DIM)
            xn = x * lax.rsqrt(ms + NORM_EPS) * gain_ref[...]
            a = aug[:, (side * groups + g) * LANES:(side * groups + g + 1) * LANES]
            out_ref[0, 2 * g] = jnp.where(first, xn, a).astype(BF16)
            out_ref[0, 2 * g + 1] = jnp.where(first, a, xn).astype(BF16)
    pad = (lax.broadcasted_iota(jnp.int32, (V_ROWS - HEAD_DIM, tm), 0) == 0).astype(BF16)
    for g in range(groups):
        vt = pf[:, 2 * W + g * LANES:2 * W + (g + 1) * LANES].T.astype(BF16)
        for hh in range(2):
            v_ref[0, 2 * g + hh, 0, :HEAD_DIM, :] = vt[hh * HEAD_DIM:(hh + 1) * HEAD_DIM]
            v_ref[0, 2 * g + hh, 0, HEAD_DIM:, :] = pad
    og_ref[0] = pf[:, 3 * W:4 * W].astype(BF16)


def _inproj_kernel(x_ref, mod_ref, g_ref, w_ref, mu_ref, bf_ref, qg_ref, kg_ref, sel_ref, bias_ref,
                   rw_ref, qa_ref, ka_ref, v_ref, og_ref, prev_scr, fcar_scr, *, rw_cols, f_off):
    t = pl.program_id(1)
    tm = x_ref.shape[1]

    @pl.when(t == 0)
    def _():
        prev_scr[...] = jnp.zeros_like(prev_scr)
        fcar_scr[...] = jnp.zeros_like(fcar_scr)

    x = x_ref[0]
    shift = mod_ref[0, 0:1, :]
    scale = mod_ref[0, 1:2, :]
    ms = jnp.mean(x * x, axis=-1, keepdims=True)
    u = (x * lax.rsqrt(ms + NORM_EPS) * g_ref[...]) * (1.0 + scale) + shift
    u = u.astype(BF16)

    p = _dot(u, w_ref[:, :rw_cols])
    row = lax.broadcasted_iota(jnp.int32, p.shape, 0)
    prev = jnp.where(row == 0, prev_scr[0:1, :], pltpu.roll(p, shift=1, axis=0))
    prev_scr[0:1, :] = p[tm - 1:tm, :]
    rw_ref[0] = p + (prev - p) * mu_ref[...]

    pf = _dot(u, w_ref[:, rw_cols:])
    z = p[:, rw_cols - LANES:] + bf_ref[...]
    logf = jnp.minimum(z, 0.0) - jnp.log(1.0 + jnp.exp(-jnp.abs(z)))
    frow = lax.broadcasted_iota(jnp.int32, logf.shape, 0)
    fcum = logf
    shift = 1
    while shift < tm:
        fcum = fcum + jnp.where(frow >= shift, pltpu.roll(fcum, shift, axis=0), 0.0)
        shift *= 2
    fcum = fcum + fcar_scr[0:1, :]
    fcar_scr[0:1, :] = fcum[tm - 1:tm, :]
    _fox_layout(pf, fcum, f_off, qg_ref, kg_ref, sel_ref, bias_ref, qa_ref, ka_ref, v_ref, og_ref)


def _inproj(x, mod3, norm1_g, w_all, mu_p, bf_p, q_norm_g, k_norm_g, rw_cols, f_off, tm=512):
    B, T, D = x.shape
    ncols = w_all.shape[1]
    width = (ncols - rw_cols) // 4
    heads = width // HEAD_DIM
    sel, bias = _fox_aug_tables(heads, f_off)
    scale = LOG2E / math.sqrt(HEAD_DIM)
    qg2 = jnp.tile(q_norm_g * scale, 2).reshape(1, LANES)
    kg2 = jnp.tile(k_norm_g, 2).reshape(1, LANES)
    return pl.pallas_call(
        functools.partial(_inproj_kernel, rw_cols=rw_cols, f_off=f_off),
        out_shape=(jax.ShapeDtypeStruct((B, T, rw_cols), F32),
                   jax.ShapeDtypeStruct((B, heads, T, LANES), BF16),
                   jax.ShapeDtypeStruct((B, heads, T, LANES), BF16),
                   jax.ShapeDtypeStruct((B, heads, T // tm, V_ROWS, tm), BF16),
                   jax.ShapeDtypeStruct((B, T, width), BF16)),
        grid=(B, T // tm),
        in_specs=[pl.BlockSpec((1, tm, D), lambda b, t: (b, t, 0)),
                  pl.BlockSpec((1, N_MOD, D), lambda b, t: (b, 0, 0)),
                  _const_spec((1, D)),
                  _const_spec((D, ncols), single_buffer=True),
                  _const_spec((1, rw_cols)),
                  _const_spec((1, LANES)),
                  _const_spec((1, LANES)), _const_spec((1, LANES)),
                  _const_spec(sel.shape, single_buffer=True), _const_spec(bias.shape)],
        out_specs=(pl.BlockSpec((1, tm, rw_cols), lambda b, t: (b, t, 0)),
                   pl.BlockSpec((1, heads, tm, LANES), lambda b, t: (b, 0, t, 0)),
                   pl.BlockSpec((1, heads, tm, LANES), lambda b, t: (b, 0, t, 0)),
                   pl.BlockSpec((1, heads, 1, V_ROWS, tm), lambda b, t: (b, 0, t, 0, 0)),
                   pl.BlockSpec((1, tm, width), lambda b, t: (b, t, 0))),
        scratch_shapes=[pltpu.VMEM((8, rw_cols), F32), pltpu.VMEM((8, LANES), F32)],
        compiler_params=pltpu.CompilerParams(
            dimension_semantics=("arbitrary", "arbitrary"), vmem_limit_bytes=VMEM_LIMIT),
        name="inproj",
    )(x, mod3, norm1_g.reshape(1, D), w_all, mu_p, bf_p, qg2, kg2, sel, bias)


def _segsum_lanes(x):
    first = lax.broadcasted_iota(jnp.int32, (x.shape[0], LANES), 1) < HEAD_DIM
    cols = []
    for j in range(x.shape[1] // LANES):
        xg = x[:, j * LANES:(j + 1) * LANES]
        lo = jnp.sum(jnp.where(first, xg, 0.0), axis=-1, keepdims=True)
        hi = jnp.sum(jnp.where(first, 0.0, xg), axis=-1, keepdims=True)
        cols.append(jnp.where(first, lo, hi))
    return jnp.concatenate(cols, axis=1)


def _segsum(x, bd):
    tile = bd.shape[0]
    xb = x.astype(BF16)
    return jnp.concatenate([_dot(xb[:, j * tile:(j + 1) * tile], bd)
                            for j in range(x.shape[1] // tile)], axis=1)


def _interleave(generators):
    live = [g for g in generators if g is not None]
    while live:
        for g in list(live):
            if next(g, StopIteration) is StopIteration:
                live.remove(g)


def _rwkv_kernel(p_ref, w0_ref, wdw_ref, a0_ref, waw_ref, wgw_ref, kk_ref, ka_ref, rk_ref,
                 gng_ref, gnb_ref, o_ref, s_scr, *, width, block_rows):
    t = pl.program_id(1)
    L = RW_CHUNK
    R = block_rows
    n_blocks = p_ref.shape[1] // R
    n_chunks = R // L
    W = width
    G2 = 2 * LANES
    groups = W // LANES
    items = [(c, j) for c in range(n_chunks) for j in range(groups)]
    rsl = lambda c: slice(c * L, (c + 1) * L)
    gsl = lambda j: slice(j * LANES, (j + 1) * LANES)

    @pl.when(t == 0)
    def _():
        s_scr[...] = jnp.zeros_like(s_scr)

    ri = lax.broadcasted_iota(jnp.int32, (MXU_TILE, MXU_TILE), 0)
    ci = lax.broadcasted_iota(jnp.int32, (MXU_TILE, MXU_TILE), 1)
    bd = ((ri // HEAD_DIM) == (ci // HEAD_DIM)).astype(BF16)
    tril = (lax.broadcasted_iota(jnp.int32, (L, L), 1)
            <= lax.broadcasted_iota(jnp.int32, (L, L), 0)).astype(BF16)
    o0 = 3 * W
    lane1 = lax.broadcasted_iota(jnp.int32, (L, LANES), 1)
    lane2 = lax.broadcasted_iota(jnp.int32, (L, G2), 1)
    row2 = lax.broadcasted_iota(jnp.int32, (L, G2), 0)
    first1 = lane1 < HEAD_DIM
    first2 = (lane2 % LANES) < HEAD_DIM
    strict2 = (lane2 % HEAD_DIM) < row2
    incl2 = (lane2 % HEAD_DIM) <= row2
    upper2 = lane2 >= LANES
    eye2 = ((lane1 % HEAD_DIM) == lax.broadcasted_iota(jnp.int32, (L, LANES), 0)).astype(F32)
    zero1 = jnp.zeros((L, LANES), BF16)
    rr = lax.broadcasted_iota(jnp.int32, (LANES, LANES), 0) < HEAD_DIM
    cc = lax.broadcasted_iota(jnp.int32, (LANES, LANES), 1) < HEAD_DIM
    diag_blocks = rr == cc

    def split(m, mask):
        return jnp.concatenate([jnp.where(mask, m, 0), jnp.where(mask, 0, m)], axis=0)

    def prepare(pb, b):
        p = p_ref[0, b * R:(b + 1) * R, :]
        r, k, v = p[:, 0:W], p[:, W:2 * W], p[:, 2 * W:3 * W]
        wd = p[:, o0:o0 + LANES]
        ad = p[:, o0 + LANES:o0 + 2 * LANES]
        gd = p[:, o0 + 2 * LANES:o0 + 4 * LANES]
        z = w0_ref[...] + _dot(_bf(jnp.tanh(wd)), wdw_ref[...])
        softplus = jnp.maximum(-z, 0.0) + jnp.log(1.0 + jnp.exp(-jnp.abs(z)))
        ld = -jnp.exp(-softplus - 0.5)
        yield
        a_sig = _sigmoid(a0_ref[...] + _dot(_bf(ad), waw_ref[...]))
        pb["gate"] = _dot(_bf(_sigmoid(gd)), wgw_ref[...])
        yield
        kk = k * kk_ref[...]
        kk = kk / jnp.maximum(jnp.sqrt(_segsum_lanes(kk * kk)), 1e-12)
        yield
        k2 = k * (1.0 + (a_sig - 1.0) * ka_ref[...])
        b_ = kk * a_sig
        pb["bonus"] = _segsum(r * k2 * rk_ref[...], bd) * v
        yield
        ld_hi = _bf(ld)
        ld_lo = _bf(ld - ld_hi.astype(F32))
        cums = [_dot(tril, ld_hi[rsl(c)]) + _dot(tril, ld_lo[rsl(c)]) for c in range(n_chunks)]
        pb["tots"] = [cum_c[L - 1:L, :] for cum_c in cums]
        cum = jnp.concatenate(cums, axis=0)
        tot_b = jnp.concatenate([jnp.broadcast_to(tc, (L, W)) for tc in pb["tots"]], axis=0)
        yield
        e_neg = jnp.exp(-cum)
        rt = r * jnp.exp(cum)
        pb.update(rt=rt, rt_m=_bf(rt), at=_bf(-kk * jnp.exp(cum - ld)), bt=_bf(b_ * e_neg),
                  kt=_bf(k2 * e_neg))
        yield
        e_rem = jnp.exp(tot_b - cum)
        pb.update(bh=_bf(b_ * e_rem), kh=_bf(k2 * e_rem), vm=_bf(v))

    def chains(pb, cb):
        at, rt_m, bt, kt, vm = pb["at"], pb["rt_m"], pb["bt"], pb["kt"], pb["vm"]
        g_bot, akm, zz = {}, {}, {}
        for c, j in items:
            ar = jnp.concatenate([at[rsl(c), gsl(j)], rt_m[rsl(c), gsl(j)]], axis=0)
            rhs = jnp.concatenate([split(bt[rsl(c), gsl(j)], first1),
                                   split(kt[rsl(c), gsl(j)], first1)], axis=0)
            gm = _dot_nt(ar, rhs)
            top = jnp.where(strict2, gm[:L], 0.0)
            g_bot[c, j] = _bf(jnp.where(incl2, gm[L:], 0.0))
            akm[c, j] = _bf(top[:, LANES:])
            zz[c, j] = jnp.concatenate([top[:, :LANES], eye2], axis=1)
        yield
        for _ in range(int(math.log2(L))):
            for it in items:
                zb = _bf(zz[it])
                out = _dot(zb[:, :LANES], split(zb, first2))
                zz[it] = out + jnp.where(upper2, zz[it], 0.0)
            yield
        vbd = {(c, j): split(vm[rsl(c), gsl(j)], first1) for c, j in items}
        akv = {it: _bf(_dot(akm[it], vbd[it])) for it in items}
        yield
        x = {}
        for c, j in items:
            zed = jnp.concatenate([at[rsl(c), gsl(j)], akv[c, j]], axis=1)
            x[c, j] = _bf(_dot(_bf(zz[c, j][:, LANES:]), split(zed, first2)))
        yield
        cb.update(r_eff={}, y0={}, mt={}, nt={})
        for c, j in items:
            w4 = jnp.concatenate([split(x[c, j], first2),
                                  jnp.concatenate([jnp.concatenate([zero1, zero1], axis=0),
                                                   vbd[c, j]], axis=1)], axis=0)
            out = _dot(g_bot[c, j], w4)
            cb["r_eff"][c, j] = _bf(pb["rt"][rsl(c), gsl(j)] + out[:, :LANES])
            cb["y0"][c, j] = out[:, LANES:]
        yield
        for c, j in items:
            lhs = jnp.concatenate([pb["bh"][rsl(c), gsl(j)], pb["kh"][rsl(c), gsl(j)]], axis=0)
            rhs = jnp.concatenate([x[c, j], jnp.concatenate([zero1, vm[rsl(c), gsl(j)]], axis=1)],
                                  axis=0)
            pt = _dot_tn(lhs, rhs)
            cb["mt"][c, j] = _bf(jnp.where(diag_blocks, pt[:, :LANES], 0.0))
            cb["nt"][c, j] = jnp.where(diag_blocks, pt[:, LANES:], 0.0)
        yield
        cb["e_cols"] = [
            jnp.concatenate([jnp.exp(pb["tots"][c][:, gsl(j)]) for c in range(n_chunks)]
                            + [jnp.zeros((LANES - n_chunks, LANES), F32)], axis=0).T
            for j in range(groups)]

    state = [s_scr[j] for j in range(groups)]

    def tail(pb, cb, b):
        y_rows = []
        for c in range(n_chunks):
            hb = [_bf(s) for s in state]
            y_rows.append(jnp.concatenate(
                [_dot(cb["r_eff"][c, j], hb[j]) + cb["y0"][c, j] for j in range(groups)], axis=1))
            for j in range(groups):
                state[j] = (state[j] * cb["e_cols"][j][:, c:c + 1] + _dot(cb["mt"][c, j], hb[j])
                            + cb["nt"][c, j])
            yield
        y = jnp.concatenate(y_rows, axis=0)
        inv_n = 1.0 / HEAD_DIM
        mean = _segsum_lanes(y) * inv_n
        d = y - mean
        yield
        var = _segsum_lanes(d * d) * inv_n
        yn = d * lax.rsqrt(var + GN_EPS) * gng_ref[...] + gnb_ref[...]
        o_ref[0, b * R:(b + 1) * R, :] = ((yn + pb["bonus"]) * pb["gate"]).astype(o_ref.dtype)

    prep = [dict() for _ in range(n_blocks)]
    chain = [dict() for _ in range(n_blocks)]
    _interleave([prepare(prep[0], 0)])
    for b in range(n_blocks):
        _interleave([chains(prep[b], chain[b]),
                     prepare(prep[b + 1], b + 1) if b + 1 < n_blocks else None,
                     tail(prep[b - 1], chain[b - 1], b - 1) if b > 0 else None])
    _interleave([tail(prep[-1], chain[-1], n_blocks - 1)])
    for j in range(groups):
        s_scr[j] = state[j]


def _rwkv(rw, w0, wdw, a0, waw, wgw, k_k, k_a, r_k, gn_g, gn_b, width, rows=1024, block_rows=256):
    B, T, cols = rw.shape
    row = lambda a: a.reshape(1, width)
    vec = _const_spec((1, width))
    return pl.pallas_call(
        functools.partial(_rwkv_kernel, width=width, block_rows=block_rows),
        out_shape=jax.ShapeDtypeStruct((B, T, width), BF16),
        grid=(B, T // rows),
        in_specs=[pl.BlockSpec((1, rows, cols), lambda b, t: (b, t, 0)),
                  vec, _const_spec(wdw.shape), vec, _const_spec(waw.shape),
                  _const_spec(wgw.shape), vec, vec, vec, vec, vec],
        out_specs=pl.BlockSpec((1, rows, width), lambda b, t: (b, t, 0)),
        scratch_shapes=[pltpu.VMEM((width // LANES, LANES, LANES), F32)],
        compiler_params=pltpu.CompilerParams(
            dimension_semantics=("arbitrary", "arbitrary"), vmem_limit_bytes=VMEM_LIMIT),
        name="rwkv",
    )(rw, row(w0), wdw, row(a0), waw, wgw, row(k_k), row(k_a), row(r_k), row(gn_g), row(gn_b))


def _fox_kernel(qtab, ktab, qa_ref, ka_ref, vt_ref, og_ref, g_ref, o_ref,
                m_all, acc_all, s_scr, p_scr, al_scr, *, blk, cb, n_off, hps):
    n_q = qa_ref.shape[2] // blk
    n_el = n_off + n_q
    chains = [(hh, c) for hh in range(hps) for c in range(blk // cb)]
    csl = lambda c: slice(c * cb, (c + 1) * cb)

    m_all[...] = jnp.full_like(m_all, NEG_INF)
    acc_all[...] = jnp.zeros_like(acc_all)

    def element(t):
        if isinstance(t, int) and t >= n_off:
            return t - n_off, t - n_off, True
        return qtab[t], ktab[t], False

    def start(i, size):
        return i * size if isinstance(i, int) else pl.multiple_of(i * size, size)

    def n_keys(c, diag):
        return (c + 1) * cb if diag else blk

    def logits(t, par):
        qi, ki, diag = element(t)
        for hh, c in chains:
            nk = n_keys(c, diag)
            s_scr[par, hh, :nk, csl(c)] = _dot_nt(
                ka_ref[0, hh, pl.ds(start(ki, blk), nk), :],
                qa_ref[0, hh, pl.ds(start(qi, blk) + c * cb, cb), :])
            yield

    def softmax(t, par):
        qi, _, diag = element(t)
        for hh, c in chains:
            nk = n_keys(c, diag)
            s = s_scr[par, hh, :nk, csl(c)]
            if diag:
                key = lax.broadcasted_iota(jnp.int32, (nk, cb), 0)
                qry = c * cb + lax.broadcasted_iota(jnp.int32, (nk, cb), 1)
                s = jnp.where(key <= qry, s, NEG_INF)
            m_prev = m_all[qi, hh, :, csl(c)]
            m_new = jnp.maximum(m_prev, jnp.max(s, axis=0, keepdims=True))
            m_all[qi, hh, :, csl(c)] = m_new
            al_scr[par, hh, :, csl(c)] = jnp.exp2(m_prev - m_new)
            p_scr[par, hh, :nk, csl(c)] = jnp.exp2(s - m_new).astype(BF16)
            yield

    def values(t, par):
        qi, ki, diag = element(t)
        for hh, c in chains:
            nk = n_keys(c, diag)
            pv = _dot(vt_ref[0, hh, ki, :, :nk], p_scr[par, hh, :nk, csl(c)])
            acc_all[qi, hh, :, csl(c)] = (al_scr[par, hh, :, csl(c)] * acc_all[qi, hh, :, csl(c)]
                                          + pv)
            yield
        if diag:
            rows = slice(qi * blk, (qi + 1) * blk)
            for pr in range(hps // 2):
                normed = []
                for hh in (2 * pr, 2 * pr + 1):
                    acc = acc_all[qi, hh]
                    o = acc[:HEAD_DIM] / acc[HEAD_DIM:HEAD_DIM + 1]
                    ms = jnp.mean(o * o, axis=0, keepdims=True)
                    normed.append(o * lax.rsqrt(ms + NORM_EPS))
                on = jnp.concatenate(normed, axis=0).T
                ls = slice(pr * LANES, (pr + 1) * LANES)
                o_ref[0, rows, ls] = (on * g_ref[:, ls]
                                      * _sigmoid(og_ref[0, rows, ls].astype(F32))).astype(o_ref.dtype)

    def step(t, par):
        _interleave([stage(t - lag, (par + lag) % 2)
                     for lag, stage in enumerate((logits, softmax, values))
                     if not isinstance(t, int) or 0 <= t - lag < n_el])

    def two_steps(u, carry):
        step(2 * u + 2, 0)
        step(2 * u + 3, 1)
        return carry

    step(0, 0)
    step(1, 1)
    lax.fori_loop(0, (n_off - 2) // 2, two_steps, 0)
    for t in range(n_off, n_el + 2):
        step(t, t % 2)


def _fox(qa, ka, vt, og, fox_out_g, width, cb=256, hps=4):
    B, H, T, _ = qa.shape
    blk = vt.shape[4]
    n_q = T // blk
    pairs = [(qi, ki) for qi in range(n_q) for ki in range(qi)]
    n_off = len(pairs)
    assert n_off >= 2 and n_off % 2 == 0, "pipeline loop is unrolled by two"
    qtab = jnp.asarray(np.array([p[0] for p in pairs], np.int32))
    ktab = jnp.asarray(np.array([p[1] for p in pairs], np.int32))
    grid_spec = pltpu.PrefetchScalarGridSpec(
        num_scalar_prefetch=2,
        grid=(B, H // hps),
        in_specs=[pl.BlockSpec((1, hps, T, LANES), lambda b, h, qt, kt: (b, h, 0, 0)),
                  pl.BlockSpec((1, hps, T, LANES), lambda b, h, qt, kt: (b, h, 0, 0)),
                  pl.BlockSpec((1, hps, n_q, V_ROWS, blk), lambda b, h, qt, kt: (b, h, 0, 0, 0)),
                  pl.BlockSpec((1, T, hps * HEAD_DIM), lambda b, h, qt, kt: (b, 0, h)),
                  pl.BlockSpec((1, hps * HEAD_DIM), lambda b, h, qt, kt: (0, h))],
        out_specs=pl.BlockSpec((1, T, hps * HEAD_DIM), lambda b, h, qt, kt: (b, 0, h)),
        scratch_shapes=[pltpu.VMEM((n_q, hps, 1, blk), F32),
                        pltpu.VMEM((n_q, hps, V_ROWS, blk), F32),
                        pltpu.VMEM((2, hps, blk, blk), F32), pltpu.VMEM((2, hps, blk, blk), BF16),
                        pltpu.VMEM((2, hps, 1, blk), F32)])
    return pl.pallas_call(
        functools.partial(_fox_kernel, blk=blk, cb=cb, n_off=n_off, hps=hps),
        out_shape=jax.ShapeDtypeStruct((B, T, width), BF16),
        grid_spec=grid_spec,
        compiler_params=pltpu.CompilerParams(
            dimension_semantics=("arbitrary",) * 2, vmem_limit_bytes=VMEM_LIMIT),
        name="fox",
    )(qtab, ktab, qa, ka, vt, og, fox_out_g.reshape(1, width))


def _outmlp_kernel(x_ref, yrw_ref, yfx_ref, mod_ref, wo_ref, g2_ref, w1_ref, w2_ref, gf_ref,
                   o_ref, wo_scr, w1_scr, w2_scr, *, n_wo, n_ff):
    s = pl.program_id(0)
    n_cast = n_wo + 2 * n_ff

    @pl.when(s < n_wo)
    def _():
        wo_scr[s] = wo_ref[...].astype(BF16)

    @pl.when((s >= n_wo) & (s < n_wo + n_ff))
    def _():
        w1_scr[s - n_wo] = w1_ref[...].astype(BF16)

    @pl.when((s >= n_wo + n_ff) & (s < n_cast))
    def _():
        w2_scr[s - n_wo - n_ff] = w2_ref[...].astype(BF16)

    @pl.when(s >= n_cast)
    def _():
        x = x_ref[0]
        gate1 = mod_ref[0, 2:3, :]
        shift2 = mod_ref[0, 3:4, :]
        scale2 = mod_ref[0, 4:5, :]
        gate2 = mod_ref[0, 5:6, :]
        y = _dot(yrw_ref[0], wo_scr[0]) + _dot(yfx_ref[0], wo_scr[1])
        h1 = x + gate1 * y
        ms = jnp.mean(h1 * h1, axis=-1, keepdims=True)
        u = ((h1 * lax.rsqrt(ms + NORM_EPS) * g2_ref[...]) * (1.0 + scale2) + shift2).astype(BF16)
        acc = jnp.zeros_like(x)
        for j in range(n_ff):
            hid = jnp.maximum(_dot(u, w1_scr[j]), 0.0)
            acc = acc + _dot((hid * hid).astype(BF16), w2_scr[j])
        h2 = h1 + gate2 * acc
        ms2 = jnp.mean(h2 * h2, axis=-1, keepdims=True)
        o_ref[0] = h2 * lax.rsqrt(ms2 + NORM_EPS) * gf_ref[...]


def _outmlp(x, y_rw, y_fx, mod3, w_o, norm2_g, w1, w2, final_g, tm=512, ff_tile=512):
    B, T, D = x.shape
    half = y_rw.shape[2]
    n_t = T // tm
    n_wo = w_o.shape[0] // half
    n_ff = w1.shape[1] // ff_tile
    n_cast = n_wo + 2 * n_ff
    assert w_o.shape[0] == n_wo * half and n_wo == 2

    def tile(s):
        r = jnp.maximum(s - n_cast, 0)
        return r // n_t, r % n_t

    rows = lambda s: (*tile(s), 0)
    return pl.pallas_call(
        functools.partial(_outmlp_kernel, n_wo=n_wo, n_ff=n_ff),
        out_shape=jax.ShapeDtypeStruct((B, T, D), F32),
        grid=(n_cast + B * n_t,),
        in_specs=[pl.BlockSpec((1, tm, D), rows),
                  pl.BlockSpec((1, tm, half), rows),
                  pl.BlockSpec((1, tm, half), rows),
                  pl.BlockSpec((1, N_MOD, D), lambda s: (tile(s)[0], 0, 0)),
                  pl.BlockSpec((half, D), lambda s: (jnp.clip(s, 0, n_wo - 1), 0)),
                  _const_spec((1, D)),
                  pl.BlockSpec((D, ff_tile), lambda s: (0, jnp.clip(s - n_wo, 0, n_ff - 1))),
                  pl.BlockSpec((ff_tile, D), lambda s: (jnp.clip(s - n_wo - n_ff, 0, n_ff - 1), 0)),
                  _const_spec((1, D))],
        out_specs=pl.BlockSpec((1, tm, D), rows),
        scratch_shapes=[pltpu.VMEM((n_wo, half, D), BF16), pltpu.VMEM((n_ff, D, ff_tile), BF16),
                        pltpu.VMEM((n_ff, ff_tile, D), BF16)],
        compiler_params=pltpu.CompilerParams(
            dimension_semantics=("arbitrary",), vmem_limit_bytes=VMEM_LIMIT),
        name="outmlp",
    )(x, y_rw, y_fx, mod3, w_o, norm2_g.reshape(1, D), w1, w2, final_g.reshape(1, D))


def _pad_cols(a, n):
    return jnp.pad(a, ((0, 0), (0, n - a.shape[1])))


def _pad_rows(a, n):
    return jnp.pad(a, ((0, n - a.shape[0]), (0, 0)))


def _branches(x, c, w_ada, b_ada, norm1_g, w_in, mu_shift, w0, w_up_decay, a0, w_up_a, w_up_g,
              k_k, k_a, r_k, gn_g, gn_b, b_f, q_norm_g, k_norm_g, fox_out_g):
    B, T, D = x.shape
    W = w0.shape[0]
    heads = b_f.shape[0]
    rw_n = 3 * W + DECAY_LORA + AAA_LORA + GATE_LORA

    def regroup(a):
        rw, fx = a[:, :rw_n], a[:, rw_n:]
        o = 3 * W
        return jnp.concatenate([
            rw[:, :o],
            _pad_cols(rw[:, o:o + DECAY_LORA], LANES),
            _pad_cols(rw[:, o + DECAY_LORA:o + DECAY_LORA + AAA_LORA], LANES),
            _pad_cols(jnp.concatenate([rw[:, o + DECAY_LORA + AAA_LORA:], fx[:, 4 * W:]], axis=1),
                      2 * LANES),
            fx[:, :4 * W]], axis=1)

    rw_cols = 3 * W + 4 * LANES
    w_all = regroup(w_in.astype(BF16))
    mu_p = regroup(jnp.pad(mu_shift.reshape(1, rw_n), ((0, 0), (0, w_in.shape[1] - rw_n))))[:, :rw_cols]
    f_off = GATE_LORA - LANES
    bf_p = jnp.pad(b_f.reshape(1, heads), ((0, 0), (f_off, LANES - f_off - heads)))
    wdw = _pad_rows(w_up_decay, LANES).astype(BF16)
    waw = _pad_rows(w_up_a, LANES).astype(BF16)
    wgw = _pad_rows(w_up_g, 2 * LANES).astype(BF16)

    mod3 = _ada(c, w_ada, b_ada).reshape(B, N_MOD, D)
    rw, qa, ka, vt, og = _inproj(x, mod3, norm1_g, w_all, mu_p, bf_p, q_norm_g, k_norm_g,
                                 rw_cols, f_off)
    y_rw = _rwkv(rw, w0, wdw, a0, waw, wgw, k_k, k_a, r_k.reshape(-1), gn_g, gn_b, W)
    y_fx = _fox(qa, ka, vt, og, fox_out_g, W)
    return y_rw, y_fx, mod3


def kernel(x, c, w_ada, b_ada, norm1_g, w_in, mu_shift, w0, w_up_decay, a0, w_up_a, w_up_g,
           k_k, k_a, r_k, gn_g, gn_b, b_f, q_norm_g, k_norm_g, fox_out_g, w_o, norm2_g,
           w_mlp1, w_mlp2, final_g):
    y_rw, y_fx, mod3 = _branches(x, c, w_ada, b_ada, norm1_g, w_in, mu_shift, w0, w_up_decay,
                                 a0, w_up_a, w_up_g, k_k, k_a, r_k, gn_g, gn_b, b_f,
                                 q_norm_g, k_norm_g, fox_out_g)
    return _outmlp(x, y_rw, y_fx, mod3, w_o, norm2_g, w_mlp1, w_mlp2, final_g)
```

```python
import functools
import math

import jax
import jax.numpy as jnp
import numpy as np
from jax import lax
from jax.experimental import pallas as pl
from jax.experimental.pallas import tpu as pltpu

F32 = jnp.float32
BF16 = jnp.bfloat16

HEAD_DIM = 64
LANES = 128
MXU_TILE = 256
NORM_EPS = 1e-6
GN_EPS = 64e-5
NEG_INF = -1e30
LOG2E = 1.4426950408889634
N_MOD = 6
DECAY_LORA = 64
AAA_LORA = 64
GATE_LORA = 160
RW_CHUNK = 64
V_ROWS = HEAD_DIM + 16
VMEM_LIMIT = 56 * 1024 * 1024


def _dot(a, b):
    return jnp.dot(a, b, preferred_element_type=F32)


def _dot_nt(a, b):
    return lax.dot_general(a, b, (((1,), (1,)), ((), ())), preferred_element_type=F32)


def _dot_tn(a, b):
    return lax.dot_general(a, b, (((0,), (0,)), ((), ())), preferred_element_type=F32)


def _bf(x):
    return x.astype(BF16)


def _sigmoid(x):
    return 1.0 / (1.0 + jnp.exp(-x))


def _const_spec(shape, single_buffer=False):
    n = len(shape)
    mode = pl.Buffered(1) if single_buffer else None
    return pl.BlockSpec(shape, lambda *_: (0,) * n, pipeline_mode=mode)


def _ada_kernel(c_ref, w_ref, b_ref, o_ref):
    c = c_ref[...]
    a = c * _sigmoid(c)
    w = w_ref[...]
    a_hi, w_hi = _bf(a), _bf(w)
    a_lo, w_lo = _bf(a - a_hi.astype(F32)), _bf(w - w_hi.astype(F32))
    o_ref[...] = (_dot(a_hi, w_hi) + (_dot(a_hi, w_lo) + _dot(a_lo, w_hi))) + b_ref[...]


def _ada(c, w_ada, b_ada, tn=1024):
    B, D = c.shape
    n = w_ada.shape[1]
    return pl.pallas_call(
        _ada_kernel,
        out_shape=jax.ShapeDtypeStruct((B, n), F32),
        grid=(n // tn,),
        in_specs=[pl.BlockSpec((B, D), lambda j: (0, 0)),
                  pl.BlockSpec((D, tn), lambda j: (0, j)),
                  pl.BlockSpec((1, tn), lambda j: (0, j))],
        out_specs=pl.BlockSpec((B, tn), lambda j: (0, j)),
        name="ada",
    )(c, w_ada, b_ada.reshape(1, n))


N_PIECES = 3


def _fox_aug_tables(heads, f_off):
    groups = heads // 2
    sel = np.zeros((LANES, 2 * groups * LANES), np.float32)
    bias = np.zeros((1, 2 * groups * LANES), np.float32)
    for h in range(heads):
        base = (h // 2) * LANES + (HEAD_DIM if h % 2 == 0 else 0)
        kbase = groups * LANES + base
        for piece in range(N_PIECES):
            sel[f_off + piece * heads + h, base + piece] = 1.0
            sel[f_off + piece * heads + h, kbase + N_PIECES + piece] = -1.0
            bias[0, base + N_PIECES + piece] = 1.0
            bias[0, kbase + piece] = 1.0
    return jnp.asarray(sel, BF16), jnp.asarray(bias)


def _fox_layout(pf, fcum, f_off, qg_ref, kg_ref, sel_ref, bias_ref, qa_ref, ka_ref, v_ref, og_ref):
    W = pf.shape[1] // 4
    heads = W // HEAD_DIM
    groups = W // LANES
    tm = pf.shape[0]
    lane = lax.broadcasted_iota(jnp.int32, (tm, LANES), 1)
    lane_f = lane - f_off
    fcum = fcum * LOG2E
    f_hi = fcum.astype(BF16).astype(F32)
    r1 = fcum - f_hi
    f_mid = r1.astype(BF16).astype(F32)
    f_lo = r1 - f_mid
    packed = jnp.where(lane_f < 0, 0.0, jnp.where(lane_f < heads, f_hi, jnp.where(
        lane_f < 2 * heads, pltpu.roll(f_mid, heads, axis=1), jnp.where(
            lane_f < 3 * heads, pltpu.roll(f_lo, 2 * heads, axis=1), 0.0))))
    aug = _dot(packed.astype(BF16), sel_ref[...]) + bias_ref[...]
    first = lane < HEAD_DIM
    for side, (gain_ref, out_ref) in enumerate(((qg_ref, qa_ref), (kg_ref, ka_ref))):
        for g in range(groups):
            x = pf[:, side * W + g * LANES:side * W + (g + 1) * LANES]
            sq = x * x
            lo = jnp.sum(jnp.where(first, sq, 0.0), axis=-1, keepdims=True)
            hi = jnp.sum(jnp.where(first, 0.0, sq), axis=-1, keepdims=True)
            ms = jnp.where(first, lo, hi) * (1.0 / HEAD_DIM)
            xn = x * lax.rsqrt(ms + NORM_EPS) * gain_ref[...]
            a = aug[:, (side * groups + g) * LANES:(side * groups + g + 1) * LANES]
            out_ref[0, 2 * g] = jnp.where(first, xn, a).astype(BF16)
            out_ref[0, 2 * g + 1] = jnp.where(first, a, xn).astype(BF16)
    pad = (lax.broadcasted_iota(jnp.int32, (V_ROWS - HEAD_DIM, tm), 0) == 0).astype(BF16)
    for g in range(groups):
        vt = pf[:, 2 * W + g * LANES:2 * W + (g + 1) * LANES].T.astype(BF16)
        for hh in range(2):
            v_ref[0, 2 * g + hh, 0, :HEAD_DIM, :] = vt[hh * HEAD_DIM:(hh + 1) * HEAD_DIM]
            v_ref[0, 2 * g + hh, 0, HEAD_DIM:, :] = pad
    og_ref[0] = pf[:, 3 * W:4 * W].astype(BF16)


def _inproj_kernel(x_ref, mod_ref, g_ref, w_ref, mu_ref, bf_ref, qg_ref, kg_ref, sel_ref, bias_ref,
                   rw_ref, qa_ref, ka_ref, v_ref, og_ref, prev_scr, fcar_scr, *, rw_cols, f_off):
    t = pl.program_id(1)
    tm = x_ref.shape[1]

    @pl.when(t == 0)
    def _():
        prev_scr[...] = jnp.zeros_like(prev_scr)
        fcar_scr[...] = jnp.zeros_like(fcar_scr)

    x = x_ref[0]
    shift = mod_ref[0, 0:1, :]
    scale = mod_ref[0, 1:2, :]
    ms = jnp.mean(x * x, axis=-1, keepdims=True)
    u = (x * lax.rsqrt(ms + NORM_EPS) * g_ref[...]) * (1.0 + scale) + shift
    u = u.astype(BF16)

    p = _dot(u, w_ref[:, :rw_cols])
    row = lax.broadcasted_iota(jnp.int32, p.shape, 0)
    prev = jnp.where(row == 0, prev_scr[0:1, :], pltpu.roll(p, shift=1, axis=0))
    prev_scr[0:1, :] = p[tm - 1:tm, :]
    rw_ref[0] = p + (prev - p) * mu_ref[...]

    pf = _dot(u, w_ref[:, rw_cols:])
    z = p[:, rw_cols - LANES:] + bf_ref[...]
    logf = jnp.minimum(z, 0.0) - jnp.log(1.0 + jnp.exp(-jnp.abs(z)))
    frow = lax.broadcasted_iota(jnp.int32, logf.shape, 0)
    fcum = logf
    shift = 1
    while shift < tm:
        fcum = fcum + jnp.where(frow >= shift, pltpu.roll(fcum, shift, axis=0), 0.0)
        shift *= 2
    fcum = fcum + fcar_scr[0:1, :]
    fcar_scr[0:1, :] = fcum[tm - 1:tm, :]
    _fox_layout(pf, fcum, f_off, qg_ref, kg_ref, sel_ref, bias_ref, qa_ref, ka_ref, v_ref, og_ref)


def _inproj(x, mod3, norm1_g, w_all, mu_p, bf_p, q_norm_g, k_norm_g, rw_cols, f_off, tm=512):
    B, T, D = x.shape
    ncols = w_all.shape[1]
    width = (ncols - rw_cols) // 4
    heads = width // HEAD_DIM
    sel, bias = _fox_aug_tables(heads, f_off)
    scale = LOG2E / math.sqrt(HEAD_DIM)
    qg2 = jnp.tile(q_norm_g * scale, 2).reshape(1, LANES)
    kg2 = jnp.tile(k_norm_g, 2).reshape(1, LANES)
    return pl.pallas_call(
        functools.partial(_inproj_kernel, rw_cols=rw_cols, f_off=f_off),
        out_shape=(jax.ShapeDtypeStruct((B, T, rw_cols), F32),
                   jax.ShapeDtypeStruct((B, heads, T, LANES), BF16),
                   jax.ShapeDtypeStruct((B, heads, T, LANES), BF16),
                   jax.ShapeDtypeStruct((B, heads, T // tm, V_ROWS, tm), BF16),
                   jax.ShapeDtypeStruct((B, T, width), BF16)),
        grid=(B, T // tm),
        in_specs=[pl.BlockSpec((1, tm, D), lambda b, t: (b, t, 0)),
                  pl.BlockSpec((1, N_MOD, D), lambda b, t: (b, 0, 0)),
                  _const_spec((1, D)),
                  _const_spec((D, ncols), single_buffer=True),
                  _const_spec((1, rw_cols)),
                  _const_spec((1, LANES)),
                  _const_spec((1, LANES)), _const_spec((1, LANES)),
                  _const_spec(sel.shape, single_buffer=True), _const_spec(bias.shape)],
        out_specs=(pl.BlockSpec((1, tm, rw_cols), lambda b, t: (b, t, 0)),
                   pl.BlockSpec((1, heads, tm, LANES), lambda b, t: (b, 0, t, 0)),
                   pl.BlockSpec((1, heads, tm, LANES), lambda b, t: (b, 0, t, 0)),
                   pl.BlockSpec((1, heads, 1, V_ROWS, tm), lambda b, t: (b, 0, t, 0, 0)),
                   pl.BlockSpec((1, tm, width), lambda b, t: (b, t, 0))),
        scratch_shapes=[pltpu.VMEM((8, rw_cols), F32), pltpu.VMEM((8, LANES), F32)],
        compiler_params=pltpu.CompilerParams(
            dimension_semantics=("arbitrary", "arbitrary"), vmem_limit_bytes=VMEM_LIMIT),
        name="inproj",
    )(x, mod3, norm1_g.reshape(1, D), w_all, mu_p, bf_p, qg2, kg2, sel, bias)


def _segsum_lanes(x):
    first = lax.broadcasted_iota(jnp.int32, (x.shape[0], LANES), 1) < HEAD_DIM
    cols = []
    for j in range(x.shape[1] // LANES):
        xg = x[:, j * LANES:(j + 1) * LANES]
        lo = jnp.sum(jnp.where(first, xg, 0.0), axis=-1, keepdims=True)
        hi = jnp.sum(jnp.where(first, 0.0, xg), axis=-1, keepdims=True)
        cols.append(jnp.where(first, lo, hi))
    return jnp.concatenate(cols, axis=1)


def _segsum(x, bd):
    tile = bd.shape[0]
    xb = x.astype(BF16)
    return jnp.concatenate([_dot(xb[:, j * tile:(j + 1) * tile], bd)
                            for j in range(x.shape[1] // tile)], axis=1)


def _interleave(generators):
    live = [g for g in generators if g is not None]
    while live:
        for g in list(live):
            if next(g, StopIteration) is StopIteration:
                live.remove(g)


def _rwkv_kernel(p_ref, w0_ref, wdw_ref, a0_ref, waw_ref, wgw_ref, kk_ref, ka_ref, rk_ref,
                 gng_ref, gnb_ref, o_ref, s_scr, *, width, block_rows):
    t = pl.program_id(1)
    L = RW_CHUNK
    R = block_rows
    n_blocks = p_ref.shape[1] // R
    n_chunks = R // L
    W = width
    G2 = 2 * LANES
    groups = W // LANES
    items = [(c, j) for c in range(n_chunks) for j in range(groups)]
    rsl = lambda c: slice(c * L, (c + 1) * L)
    gsl = lambda j: slice(j * LANES, (j + 1) * LANES)

    @pl.when(t == 0)
    def _():
        s_scr[...] = jnp.zeros_like(s_scr)

    ri = lax.broadcasted_iota(jnp.int32, (MXU_TILE, MXU_TILE), 0)
    ci = lax.broadcasted_iota(jnp.int32, (MXU_TILE, MXU_TILE), 1)
    bd = ((ri // HEAD_DIM) == (ci // HEAD_DIM)).astype(BF16)
    tril = (lax.broadcasted_iota(jnp.int32, (L, L), 1)
            <= lax.broadcasted_iota(jnp.int32, (L, L), 0)).astype(BF16)
    o0 = 3 * W
    lane1 = lax.broadcasted_iota(jnp.int32, (L, LANES), 1)
    lane2 = lax.broadcasted_iota(jnp.int32, (L, G2), 1)
    row2 = lax.broadcasted_iota(jnp.int32, (L, G2), 0)
    first1 = lane1 < HEAD_DIM
    first2 = (lane2 % LANES) < HEAD_DIM
    strict2 = (lane2 % HEAD_DIM) < row2
    incl2 = (lane2 % HEAD_DIM) <= row2
    upper2 = lane2 >= LANES
    eye2 = ((lane1 % HEAD_DIM) == lax.broadcasted_iota(jnp.int32, (L, LANES), 0)).astype(F32)
    zero1 = jnp.zeros((L, LANES), BF16)
    rr = lax.broadcasted_iota(jnp.int32, (LANES, LANES), 0) < HEAD_DIM
    cc = lax.broadcasted_iota(jnp.int32, (LANES, LANES), 1) < HEAD_DIM
    diag_blocks = rr == cc

    def split(m, mask):
        return jnp.concatenate([jnp.where(mask, m, 0), jnp.where(mask, 0, m)], axis=0)

    def prepare(pb, b):
        p = p_ref[0, b * R:(b + 1) * R, :]
        r, k, v = p[:, 0:W], p[:, W:2 * W], p[:, 2 * W:3 * W]
        wd = p[:, o0:o0 + LANES]
        ad = p[:, o0 + LANES:o0 + 2 * LANES]
        gd = p[:, o0 + 2 * LANES:o0 + 4 * LANES]
        z = w0_ref[...] + _dot(_bf(jnp.tanh(wd)), wdw_ref[...])
        softplus = jnp.maximum(-z, 0.0) + jnp.log(1.0 + jnp.exp(-jnp.abs(z)))
        ld = -jnp.exp(-softplus - 0.5)
        yield
        a_sig = _sigmoid(a0_ref[...] + _dot(_bf(ad), waw_ref[...]))
        pb["gate"] = _dot(_bf(_sigmoid(gd)), wgw_ref[...])
        yield
        kk = k * kk_ref[...]
        kk = kk / jnp.maximum(jnp.sqrt(_segsum_lanes(kk * kk)), 1e-12)
        yield
        k2 = k * (1.0 + (a_sig - 1.0) * ka_ref[...])
        b_ = kk * a_sig
        pb["bonus"] = _segsum(r * k2 * rk_ref[...], bd) * v
        yield
        ld_hi = _bf(ld)
        ld_lo = _bf(ld - ld_hi.astype(F32))
        cums = [_dot(tril, ld_hi[rsl(c)]) + _dot(tril, ld_lo[rsl(c)]) for c in range(n_chunks)]
        pb["tots"] = [cum_c[L - 1:L, :] for cum_c in cums]
        cum = jnp.concatenate(cums, axis=0)
        tot_b = jnp.concatenate([jnp.broadcast_to(tc, (L, W)) for tc in pb["tots"]], axis=0)
        yield
        e_neg = jnp.exp(-cum)
        rt = r * jnp.exp(cum)
        pb.update(rt=rt, rt_m=_bf(rt), at=_bf(-kk * jnp.exp(cum - ld)), bt=_bf(b_ * e_neg),
                  kt=_bf(k2 * e_neg))
        yield
        e_rem = jnp.exp(tot_b - cum)
        pb.update(bh=_bf(b_ * e_rem), kh=_bf(k2 * e_rem), vm=_bf(v))

    def chains(pb, cb):
        at, rt_m, bt, kt, vm = pb["at"], pb["rt_m"], pb["bt"], pb["kt"], pb["vm"]
        g_bot, akm, zz = {}, {}, {}
        for c, j in items:
            ar = jnp.concatenate([at[rsl(c), gsl(j)], rt_m[rsl(c), gsl(j)]], axis=0)
            rhs = jnp.concatenate([split(bt[rsl(c), gsl(j)], first1),
                                   split(kt[rsl(c), gsl(j)], first1)], axis=0)
            gm = _dot_nt(ar, rhs)
            top = jnp.where(strict2, gm[:L], 0.0)
            g_bot[c, j] = _bf(jnp.where(incl2, gm[L:], 0.0))
            akm[c, j] = _bf(top[:, LANES:])
            zz[c, j] = jnp.concatenate([top[:, :LANES], eye2], axis=1)
        yield
        for _ in range(int(math.log2(L))):
            for it in items:
                zb = _bf(zz[it])
                out = _dot(zb[:, :LANES], split(zb, first2))
                zz[it] = out + jnp.where(upper2, zz[it], 0.0)
            yield
        vbd = {(c, j): split(vm[rsl(c), gsl(j)], first1) for c, j in items}
        akv = {it: _bf(_dot(akm[it], vbd[it])) for it in items}
        yield
        x = {}
        for c, j in items:
            zed = jnp.concatenate([at[rsl(c), gsl(j)], akv[c, j]], axis=1)
            x[c, j] = _bf(_dot(_bf(zz[c, j][:, LANES:]), split(zed, first2)))
        yield
        cb.update(r_eff={}, y0={}, mt={}, nt={})
        for c, j in items:
            w4 = jnp.concatenate([split(x[c, j], first2),
                                  jnp.concatenate([jnp.concatenate([zero1, zero1], axis=0),
                                                   vbd[c, j]], axis=1)], axis=0)
            out = _dot(g_bot[c, j], w4)
            cb["r_eff"][c, j] = _bf(pb["rt"][rsl(c), gsl(j)] + out[:, :LANES])
            cb["y0"][c, j] = out[:, LANES:]
        yield
        for c, j in items:
            lhs = jnp.concatenate([pb["bh"][rsl(c), gsl(j)], pb["kh"][rsl(c), gsl(j)]], axis=0)
            rhs = jnp.concatenate([x[c, j], jnp.concatenate([zero1, vm[rsl(c), gsl(j)]], axis=1)],
                                  axis=0)
            pt = _dot_tn(lhs, rhs)
            cb["mt"][c, j] = _bf(jnp.where(diag_blocks, pt[:, :LANES], 0.0))
            cb["nt"][c, j] = jnp.where(diag_blocks, pt[:, LANES:], 0.0)
        yield
        cb["e_cols"] = [
            jnp.concatenate([jnp.exp(pb["tots"][c][:, gsl(j)]) for c in range(n_chunks)]
                            + [jnp.zeros((LANES - n_chunks, LANES), F32)], axis=0).T
            for j in range(groups)]

    state = [s_scr[j] for j in range(groups)]

    def tail(pb, cb, b):
        y_rows = []
        for c in range(n_chunks):
            hb = [_bf(s) for s in state]
            y_rows.append(jnp.concatenate(
                [_dot(cb["r_eff"][c, j], hb[j]) + cb["y0"][c, j] for j in range(groups)], axis=1))
            for j in range(groups):
                state[j] = (state[j] * cb["e_cols"][j][:, c:c + 1] + _dot(cb["mt"][c, j], hb[j])
                            + cb["nt"][c, j])
            yield
        y = jnp.concatenate(y_rows, axis=0)
        inv_n = 1.0 / HEAD_DIM
        mean = _segsum_lanes(y) * inv_n
        d = y - mean
        yield
        var = _segsum_lanes(d * d) * inv_n
        yn = d * lax.rsqrt(var + GN_EPS) * gng_ref[...] + gnb_ref[...]
        o_ref[0, b * R:(b + 1) * R, :] = ((yn + pb["bonus"]) * pb["gate"]).astype(o_ref.dtype)

    prep = [dict() for _ in range(n_blocks)]
    chain = [dict() for _ in range(n_blocks)]
    _interleave([prepare(prep[0], 0)])
    for b in range(n_blocks):
        _interleave([chains(prep[b], chain[b]),
                     prepare(prep[b + 1], b + 1) if b + 1 < n_blocks else None,
                     tail(prep[b - 1], chain[b - 1], b - 1) if b > 0 else None])
    _interleave([tail(prep[-1], chain[-1], n_blocks - 1)])
    for j in range(groups):
        s_scr[j] = state[j]


def _rwkv(rw, w0, wdw, a0, waw, wgw, k_k, k_a, r_k, gn_g, gn_b, width, rows=512, block_rows=256):
    B, T, cols = rw.shape
    row = lambda a: a.reshape(1, width)
    vec = _const_spec((1, width))
    return pl.pallas_call(
        functools.partial(_rwkv_kernel, width=width, block_rows=block_rows),
        out_shape=jax.ShapeDtypeStruct((B, T, width), BF16),
        grid=(B, T // rows),
        in_specs=[pl.BlockSpec((1, rows, cols), lambda b, t: (b, t, 0)),
                  vec, _const_spec(wdw.shape), vec, _const_spec(waw.shape),
                  _const_spec(wgw.shape), vec, vec, vec, vec, vec],
        out_specs=pl.BlockSpec((1, rows, width), lambda b, t: (b, t, 0)),
        scratch_shapes=[pltpu.VMEM((width // LANES, LANES, LANES), F32)],
        compiler_params=pltpu.CompilerParams(
            dimension_semantics=("arbitrary", "arbitrary"), vmem_limit_bytes=VMEM_LIMIT),
        name="rwkv",
    )(rw, row(w0), wdw, row(a0), waw, wgw, row(k_k), row(k_a), row(r_k), row(gn_g), row(gn_b))


def _fox_kernel(qtab, ktab, qa_ref, ka_ref, vt_ref, og_ref, g_ref, o_ref,
                m_all, acc_all, s_scr, p_scr, al_scr, *, blk, cb, n_off, hps):
    n_q = qa_ref.shape[2] // blk
    n_el = n_off + n_q
    chains = [(hh, c) for hh in range(hps) for c in range(blk // cb)]
    csl = lambda c: slice(c * cb, (c + 1) * cb)

    m_all[...] = jnp.full_like(m_all, NEG_INF)
    acc_all[...] = jnp.zeros_like(acc_all)

    def element(t):
        if isinstance(t, int) and t >= n_off:
            return t - n_off, t - n_off, True
        return qtab[t], ktab[t], False

    def start(i, size):
        return i * size if isinstance(i, int) else pl.multiple_of(i * size, size)

    def n_keys(c, diag):
        return (c + 1) * cb if diag else blk

    def logits(t, par):
        qi, ki, diag = element(t)
        for hh, c in chains:
            nk = n_keys(c, diag)
            s_scr[par, hh, :nk, csl(c)] = _dot_nt(
                ka_ref[0, hh, pl.ds(start(ki, blk), nk), :],
                qa_ref[0, hh, pl.ds(start(qi, blk) + c * cb, cb), :])
            yield

    def softmax(t, par):
        qi, _, diag = element(t)
        for hh, c in chains:
            nk = n_keys(c, diag)
            s = s_scr[par, hh, :nk, csl(c)]
            if diag:
                key = lax.broadcasted_iota(jnp.int32, (nk, cb), 0)
                qry = c * cb + lax.broadcasted_iota(jnp.int32, (nk, cb), 1)
                s = jnp.where(key <= qry, s, NEG_INF)
            m_prev = m_all[qi, hh, :, csl(c)]
            m_new = jnp.maximum(m_prev, jnp.max(s, axis=0, keepdims=True))
            m_all[qi, hh, :, csl(c)] = m_new
            al_scr[par, hh, :, csl(c)] = jnp.exp2(m_prev - m_new)
            p_scr[par, hh, :nk, csl(c)] = jnp.exp2(s - m_new).astype(BF16)
            yield

    def values(t, par):
        qi, ki, diag = element(t)
        for hh, c in chains:
            nk = n_keys(c, diag)
            pv = _dot(vt_ref[0, hh, ki, :, :nk], p_scr[par, hh, :nk, csl(c)])
            acc_all[qi, hh, :, csl(c)] = (al_scr[par, hh, :, csl(c)] * acc_all[qi, hh, :, csl(c)]
                                          + pv)
            yield
        if diag:
            rows = slice(qi * blk, (qi + 1) * blk)
            for pr in range(hps // 2):
                normed = []
                for hh in (2 * pr, 2 * pr + 1):
                    acc = acc_all[qi, hh]
                    o = acc[:HEAD_DIM] / acc[HEAD_DIM:HEAD_DIM + 1]
                    ms = jnp.mean(o * o, axis=0, keepdims=True)
                    normed.append(o * lax.rsqrt(ms + NORM_EPS))
                on = jnp.concatenate(normed, axis=0).T
                ls = slice(pr * LANES, (pr + 1) * LANES)
                o_ref[0, rows, ls] = (on * g_ref[:, ls]
                                      * _sigmoid(og_ref[0, rows, ls].astype(F32))).astype(o_ref.dtype)

    def step(t, par):
        _interleave([stage(t - lag, (par + lag) % 2)
                     for lag, stage in enumerate((logits, softmax, values))
                     if not isinstance(t, int) or 0 <= t - lag < n_el])

    def two_steps(u, carry):
        step(2 * u + 2, 0)
        step(2 * u + 3, 1)
        return carry

    step(0, 0)
    step(1, 1)
    lax.fori_loop(0, (n_off - 2) // 2, two_steps, 0)
    for t in range(n_off, n_el + 2):
        step(t, t % 2)


def _fox(qa, ka, vt, og, fox_out_g, width, cb=256, hps=4):
    B, H, T, _ = qa.shape
    blk = vt.shape[4]
    n_q = T // blk
    pairs = [(qi, ki) for qi in range(n_q) for ki in range(qi)]
    n_off = len(pairs)
    assert n_off >= 2 and n_off % 2 == 0, "pipeline loop is unrolled by two"
    qtab = jnp.asarray(np.array([p[0] for p in pairs], np.int32))
    ktab = jnp.asarray(np.array([p[1] for p in pairs], np.int32))
    grid_spec = pltpu.PrefetchScalarGridSpec(
        num_scalar_prefetch=2,
        grid=(B, H // hps),
        in_specs=[pl.BlockSpec((1, hps, T, LANES), lambda b, h, qt, kt: (b, h, 0, 0)),
                  pl.BlockSpec((1, hps, T, LANES), lambda b, h, qt, kt: (b, h, 0, 0)),
                  pl.BlockSpec((1, hps, n_q, V_ROWS, blk), lambda b, h, qt, kt: (b, h, 0, 0, 0)),
                  pl.BlockSpec((1, T, hps * HEAD_DIM), lambda b, h, qt, kt: (b, 0, h)),
                  pl.BlockSpec((1, hps * HEAD_DIM), lambda b, h, qt, kt: (0, h))],
        out_specs=pl.BlockSpec((1, T, hps * HEAD_DIM), lambda b, h, qt, kt: (b, 0, h)),
        scratch_shapes=[pltpu.VMEM((n_q, hps, 1, blk), F32),
                        pltpu.VMEM((n_q, hps, V_ROWS, blk), F32),
                        pltpu.VMEM((2, hps, blk, blk), F32), pltpu.VMEM((2, hps, blk, blk), BF16),
                        pltpu.VMEM((2, hps, 1, blk), F32)])
    return pl.pallas_call(
        functools.partial(_fox_kernel, blk=blk, cb=cb, n_off=n_off, hps=hps),
        out_shape=jax.ShapeDtypeStruct((B, T, width), BF16),
        grid_spec=grid_spec,
        compiler_params=pltpu.CompilerParams(
            dimension_semantics=("arbitrary",) * 2, vmem_limit_bytes=VMEM_LIMIT),
        name="fox",
    )(qtab, ktab, qa, ka, vt, og, fox_out_g.reshape(1, width))


def _outmlp_kernel(x_ref, yrw_ref, yfx_ref, mod_ref, wo_ref, g2_ref, w1_ref, w2_ref, gf_ref,
                   o_ref, wo_scr, w1_scr, w2_scr, *, n_wo, n_ff):
    s = pl.program_id(0)
    n_cast = n_wo + 2 * n_ff

    @pl.when(s < n_wo)
    def _():
        wo_scr[s] = wo_ref[...].astype(BF16)

    @pl.when((s >= n_wo) & (s < n_wo + n_ff))
    def _():
        w1_scr[s - n_wo] = w1_ref[...].astype(BF16)

    @pl.when((s >= n_wo + n_ff) & (s < n_cast))
    def _():
        w2_scr[s - n_wo - n_ff] = w2_ref[...].astype(BF16)

    @pl.when(s >= n_cast)
    def _():
        x = x_ref[0]
        gate1 = mod_ref[0, 2:3, :]
        shift2 = mod_ref[0, 3:4, :]
        scale2 = mod_ref[0, 4:5, :]
        gate2 = mod_ref[0, 5:6, :]
        y = _dot(yrw_ref[0], wo_scr[0]) + _dot(yfx_ref[0], wo_scr[1])
        h1 = x + gate1 * y
        ms = jnp.mean(h1 * h1, axis=-1, keepdims=True)
        u = ((h1 * lax.rsqrt(ms + NORM_EPS) * g2_ref[...]) * (1.0 + scale2) + shift2).astype(BF16)
        acc = jnp.zeros_like(x)
        for j in range(n_ff):
            hid = jnp.maximum(_dot(u, w1_scr[j]), 0.0)
            acc = acc + _dot((hid * hid).astype(BF16), w2_scr[j])
        h2 = h1 + gate2 * acc
        ms2 = jnp.mean(h2 * h2, axis=-1, keepdims=True)
        o_ref[0] = h2 * lax.rsqrt(ms2 + NORM_EPS) * gf_ref[...]


def _outmlp(x, y_rw, y_fx, mod3, w_o, norm2_g, w1, w2, final_g, tm=512, ff_tile=512):
    B, T, D = x.shape
    half = y_rw.shape[2]
    n_t = T // tm
    n_wo = w_o.shape[0] // half
    n_ff = w1.shape[1] // ff_tile
    n_cast = n_wo + 2 * n_ff
    assert w_o.shape[0] == n_wo * half and n_wo == 2

    def tile(s):
        r = jnp.maximum(s - n_cast, 0)
        return r // n_t, r % n_t

    rows = lambda s: (*tile(s), 0)
    return pl.pallas_call(
        functools.partial(_outmlp_kernel, n_wo=n_wo, n_ff=n_ff),
        out_shape=jax.ShapeDtypeStruct((B, T, D), F32),
        grid=(n_cast + B * n_t,),
        in_specs=[pl.BlockSpec((1, tm, D), rows),
                  pl.BlockSpec((1, tm, half), rows),
                  pl.BlockSpec((1, tm, half), rows),
                  pl.BlockSpec((1, N_MOD, D), lambda s: (tile(s)[0], 0, 0)),
                  pl.BlockSpec((half, D), lambda s: (jnp.clip(s, 0, n_wo - 1), 0)),
                  _const_spec((1, D)),
                  pl.BlockSpec((D, ff_tile), lambda s: (0, jnp.clip(s - n_wo, 0, n_ff - 1))),
                  pl.BlockSpec((ff_tile, D), lambda s: (jnp.clip(s - n_wo - n_ff, 0, n_ff - 1), 0)),
                  _const_spec((1, D))],
        out_specs=pl.BlockSpec((1, tm, D), rows),
        scratch_shapes=[pltpu.VMEM((n_wo, half, D), BF16), pltpu.VMEM((n_ff, D, ff_tile), BF16),
                        pltpu.VMEM((n_ff, ff_tile, D), BF16)],
        compiler_params=pltpu.CompilerParams(
            dimension_semantics=("arbitrary",), vmem_limit_bytes=VMEM_LIMIT),
        name="outmlp",
    )(x, y_rw, y_fx, mod3, w_o, norm2_g.reshape(1, D), w1, w2, final_g.reshape(1, D))


def _pad_cols(a, n):
    return jnp.pad(a, ((0, 0), (0, n - a.shape[1])))


def _pad_rows(a, n):
    return jnp.pad(a, ((0, n - a.shape[0]), (0, 0)))


def _branches(x, c, w_ada, b_ada, norm1_g, w_in, mu_shift, w0, w_up_decay, a0, w_up_a, w_up_g,
              k_k, k_a, r_k, gn_g, gn_b, b_f, q_norm_g, k_norm_g, fox_out_g):
    B, T, D = x.shape
    W = w0.shape[0]
    heads = b_f.shape[0]
    rw_n = 3 * W + DECAY_LORA + AAA_LORA + GATE_LORA

    def regroup(a):
        rw, fx = a[:, :rw_n], a[:, rw_n:]
        o = 3 * W
        return jnp.concatenate([
            rw[:, :o],
            _pad_cols(rw[:, o:o + DECAY_LORA], LANES),
            _pad_cols(rw[:, o + DECAY_LORA:o + DECAY_LORA + AAA_LORA], LANES),
            _pad_cols(jnp.concatenate([rw[:, o + DECAY_LORA + AAA_LORA:], fx[:, 4 * W:]], axis=1),
                      2 * LANES),
            fx[:, :4 * W]], axis=1)

    rw_cols = 3 * W + 4 * LANES
    w_all = regroup(w_in.astype(BF16))
    mu_p = regroup(jnp.pad(mu_shift.reshape(1, rw_n), ((0, 0), (0, w_in.shape[1] - rw_n))))[:, :rw_cols]
    f_off = GATE_LORA - LANES
    bf_p = jnp.pad(b_f.reshape(1, heads), ((0, 0), (f_off, LANES - f_off - heads)))
    wdw = _pad_rows(w_up_decay, LANES).astype(BF16)
    waw = _pad_rows(w_up_a, LANES).astype(BF16)
    wgw = _pad_rows(w_up_g, 2 * LANES).astype(BF16)

    mod3 = _ada(c, w_ada, b_ada).reshape(B, N_MOD, D)
    rw, qa, ka, vt, og = _inproj(x, mod3, norm1_g, w_all, mu_p, bf_p, q_norm_g, k_norm_g,
                                 rw_cols, f_off)
    y_rw = _rwkv(rw, w0, wdw, a0, waw, wgw, k_k, k_a, r_k.reshape(-1), gn_g, gn_b, W)
    y_fx = _fox(qa, ka, vt, og, fox_out_g, W)
    return y_rw, y_fx, mod3


def kernel(x, c, w_ada, b_ada, norm1_g, w_in, mu_shift, w0, w_up_decay, a0, w_up_a, w_up_g,
           k_k, k_a, r_k, gn_g, gn_b, b_f, q_norm_g, k_norm_g, fox_out_g, w_o, norm2_g,
           w_mlp1, w_mlp2, final_g):
    y_rw, y_fx, mod3 = _branches(x, c, w_ada, b_ada, norm1_g, w_in, mu_shift, w0, w_up_decay,
                                 a0, w_up_a, w_up_g, k_k, k_a, r_k, gn_g, gn_b, b_f,
                                 q_norm_g, k_norm_g, fox_out_g)
    return _outmlp(x, y_rw, y_fx, mod3, w_o, norm2_g, w_mlp1, w_mlp2, final_g)
```

```python
import functools
import math

import jax
import jax.numpy as jnp
import numpy as np
from jax import lax
from jax.experimental import pallas as pl
from jax.experimental.pallas import tpu as pltpu

F32 = jnp.float32
BF16 = jnp.bfloat16

HEAD_DIM = 64
LANES = 128
MXU_TILE = 256
NORM_EPS = 1e-6
GN_EPS = 64e-5
NEG_INF = -1e30
LOG2E = 1.4426950408889634
N_MOD = 6
DECAY_LORA = 64
AAA_LORA = 64
GATE_LORA = 160
RW_CHUNK = 64
V_ROWS = HEAD_DIM + 16
VMEM_LIMIT = 56 * 1024 * 1024


def _dot(a, b):
    return jnp.dot(a, b, preferred_element_type=F32)


def _dot_nt(a, b):
    return lax.dot_general(a, b, (((1,), (1,)), ((), ())), preferred_element_type=F32)


def _dot_tn(a, b):
    return lax.dot_general(a, b, (((0,), (0,)), ((), ())), preferred_element_type=F32)


def _bf(x):
    return x.astype(BF16)


def _sigmoid(x):
    return 1.0 / (1.0 + jnp.exp(-x))


def _const_spec(shape, single_buffer=False):
    n = len(shape)
    mode = pl.Buffered(1) if single_buffer else None
    return pl.BlockSpec(shape, lambda *_: (0,) * n, pipeline_mode=mode)


def _ada_kernel(c_ref, w_ref, b_ref, o_ref):
    c = c_ref[...]
    a = c * _sigmoid(c)
    w = w_ref[...]
    a_hi, w_hi = _bf(a), _bf(w)
    a_lo, w_lo = _bf(a - a_hi.astype(F32)), _bf(w - w_hi.astype(F32))
    o_ref[...] = (_dot(a_hi, w_hi) + (_dot(a_hi, w_lo) + _dot(a_lo, w_hi))) + b_ref[...]


def _ada(c, w_ada, b_ada, tn=2048):
    B, D = c.shape
    n = w_ada.shape[1]
    return pl.pallas_call(
        _ada_kernel,
        out_shape=jax.ShapeDtypeStruct((B, n), F32),
        grid=(n // tn,),
        in_specs=[pl.BlockSpec((B, D), lambda j: (0, 0)),
                  pl.BlockSpec((D, tn), lambda j: (0, j)),
                  pl.BlockSpec((1, tn), lambda j: (0, j))],
        out_specs=pl.BlockSpec((B, tn), lambda j: (0, j)),
        name="ada",
    )(c, w_ada, b_ada.reshape(1, n))


N_PIECES = 3


def _fox_aug_tables(heads, f_off):
    groups = heads // 2
    sel = np.zeros((LANES, 2 * groups * LANES), np.float32)
    bias = np.zeros((1, 2 * groups * LANES), np.float32)
    for h in range(heads):
        base = (h // 2) * LANES + (HEAD_DIM if h % 2 == 0 else 0)
        kbase = groups * LANES + base
        for piece in range(N_PIECES):
            sel[f_off + piece * heads + h, base + piece] = 1.0
            sel[f_off + piece * heads + h, kbase + N_PIECES + piece] = -1.0
            bias[0, base + N_PIECES + piece] = 1.0
            bias[0, kbase + piece] = 1.0
    return jnp.asarray(sel, BF16), jnp.asarray(bias)


def _fox_layout(pf, fcum, f_off, qg_ref, kg_ref, sel_ref, bias_ref, qa_ref, ka_ref, v_ref, og_ref):
    W = pf.shape[1] // 4
    heads = W // HEAD_DIM
    groups = W // LANES
    tm = pf.shape[0]
    lane = lax.broadcasted_iota(jnp.int32, (tm, LANES), 1)
    lane_f = lane - f_off
    fcum = fcum * LOG2E
    f_hi = fcum.astype(BF16).astype(F32)
    r1 = fcum - f_hi
    f_mid = r1.astype(BF16).astype(F32)
    f_lo = r1 - f_mid
    packed = jnp.where(lane_f < 0, 0.0, jnp.where(lane_f < heads, f_hi, jnp.where(
        lane_f < 2 * heads, pltpu.roll(f_mid, heads, axis=1), jnp.where(
            lane_f < 3 * heads, pltpu.roll(f_lo, 2 * heads, axis=1), 0.0))))
    aug = _dot(packed.astype(BF16), sel_ref[...]) + bias_ref[...]
    first = lane < HEAD_DIM
    for side, (gain_ref, out_ref) in enumerate(((qg_ref, qa_ref), (kg_ref, ka_ref))):
        for g in range(groups):
            x = pf[:, side * W + g * LANES:side * W + (g + 1) * LANES]
            sq = x * x
            lo = jnp.sum(jnp.where(first, sq, 0.0), axis=-1, keepdims=True)
            hi = jnp.sum(jnp.where(first, 0.0, sq), axis=-1, keepdims=True)
            ms = jnp.where(first, lo, hi) * (1.0 / HEAD_DIM)
            xn = x * lax.rsqrt(ms + NORM_EPS) * gain_ref[...]
            a = aug[:, (side * groups + g) * LANES:(side * groups + g + 1) * LANES]
            out_ref[0, 2 * g] = jnp.where(first, xn, a).astype(BF16)
            out_ref[0, 2 * g + 1] = jnp.where(first, a, xn).astype(BF16)
    pad = (lax.broadcasted_iota(jnp.int32, (V_ROWS - HEAD_DIM, tm), 0) == 0).astype(BF16)
    for g in range(groups):
        vt = pf[:, 2 * W + g * LANES:2 * W + (g + 1) * LANES].T.astype(BF16)
        for hh in range(2):
            v_ref[0, 2 * g + hh, 0, :HEAD_DIM, :] = vt[hh * HEAD_DIM:(hh + 1) * HEAD_DIM]
            v_ref[0, 2 * g + hh, 0, HEAD_DIM:, :] = pad
    og_ref[0] = pf[:, 3 * W:4 * W].astype(BF16)


def _inproj_kernel(x_ref, mod_ref, g_ref, w_ref, mu_ref, bf_ref, qg_ref, kg_ref, sel_ref, bias_ref,
                   rw_ref, qa_ref, ka_ref, v_ref, og_ref, prev_scr, fcar_scr, *, rw_cols, f_off):
    t = pl.program_id(1)
    tm = x_ref.shape[1]

    @pl.when(t == 0)
    def _():
        prev_scr[...] = jnp.zeros_like(prev_scr)
        fcar_scr[...] = jnp.zeros_like(fcar_scr)

    x = x_ref[0]
    shift = mod_ref[0, 0:1, :]
    scale = mod_ref[0, 1:2, :]
    ms = jnp.mean(x * x, axis=-1, keepdims=True)
    u = (x * lax.rsqrt(ms + NORM_EPS) * g_ref[...]) * (1.0 + scale) + shift
    u = u.astype(BF16)

    p = _dot(u, w_ref[:, :rw_cols])
    row = lax.broadcasted_iota(jnp.int32, p.shape, 0)
    prev = jnp.where(row == 0, prev_scr[0:1, :], pltpu.roll(p, shift=1, axis=0))
    prev_scr[0:1, :] = p[tm - 1:tm, :]
    rw_ref[0] = p + (prev - p) * mu_ref[...]

    pf = _dot(u, w_ref[:, rw_cols:])
    z = p[:, rw_cols - LANES:] + bf_ref[...]
    logf = jnp.minimum(z, 0.0) - jnp.log(1.0 + jnp.exp(-jnp.abs(z)))
    frow = lax.broadcasted_iota(jnp.int32, logf.shape, 0)
    fcum = logf
    shift = 1
    while shift < tm:
        fcum = fcum + jnp.where(frow >= shift, pltpu.roll(fcum, shift, axis=0), 0.0)
        shift *= 2
    fcum = fcum + fcar_scr[0:1, :]
    fcar_scr[0:1, :] = fcum[tm - 1:tm, :]
    _fox_layout(pf, fcum, f_off, qg_ref, kg_ref, sel_ref, bias_ref, qa_ref, ka_ref, v_ref, og_ref)


def _inproj(x, mod3, norm1_g, w_all, mu_p, bf_p, q_norm_g, k_norm_g, rw_cols, f_off, tm=512):
    B, T, D = x.shape
    ncols = w_all.shape[1]
    width = (ncols - rw_cols) // 4
    heads = width // HEAD_DIM
    sel, bias = _fox_aug_tables(heads, f_off)
    scale = LOG2E / math.sqrt(HEAD_DIM)
    qg2 = jnp.tile(q_norm_g * scale, 2).reshape(1, LANES)
    kg2 = jnp.tile(k_norm_g, 2).reshape(1, LANES)
    return pl.pallas_call(
        functools.partial(_inproj_kernel, rw_cols=rw_cols, f_off=f_off),
        out_shape=(jax.ShapeDtypeStruct((B, T, rw_cols), F32),
                   jax.ShapeDtypeStruct((B, heads, T, LANES), BF16),
                   jax.ShapeDtypeStruct((B, heads, T, LANES), BF16),
                   jax.ShapeDtypeStruct((B, heads, T // tm, V_ROWS, tm), BF16),
                   jax.ShapeDtypeStruct((B, T, width), BF16)),
        grid=(B, T // tm),
        in_specs=[pl.BlockSpec((1, tm, D), lambda b, t: (b, t, 0)),
                  pl.BlockSpec((1, N_MOD, D), lambda b, t: (b, 0, 0)),
                  _const_spec((1, D)),
                  _const_spec((D, ncols), single_buffer=True),
                  _const_spec((1, rw_cols)),
                  _const_spec((1, LANES)),
                  _const_spec((1, LANES)), _const_spec((1, LANES)),
                  _const_spec(sel.shape, single_buffer=True), _const_spec(bias.shape)],
        out_specs=(pl.BlockSpec((1, tm, rw_cols), lambda b, t: (b, t, 0)),
                   pl.BlockSpec((1, heads, tm, LANES), lambda b, t: (b, 0, t, 0)),
                   pl.BlockSpec((1, heads, tm, LANES), lambda b, t: (b, 0, t, 0)),
                   pl.BlockSpec((1, heads, 1, V_ROWS, tm), lambda b, t: (b, 0, t, 0, 0)),
                   pl.BlockSpec((1, tm, width), lambda b, t: (b, t, 0))),
        scratch_shapes=[pltpu.VMEM((8, rw_cols), F32), pltpu.VMEM((8, LANES), F32)],
        compiler_params=pltpu.CompilerParams(
            dimension_semantics=("arbitrary", "arbitrary"), vmem_limit_bytes=VMEM_LIMIT),
        name="inproj",
    )(x, mod3, norm1_g.reshape(1, D), w_all, mu_p, bf_p, qg2, kg2, sel, bias)


def _segsum_lanes(x):
    first = lax.broadcasted_iota(jnp.int32, (x.shape[0], LANES), 1) < HEAD_DIM
    cols = []
    for j in range(x.shape[1] // LANES):
        xg = x[:, j * LANES:(j + 1) * LANES]
        lo = jnp.sum(jnp.where(first, xg, 0.0), axis=-1, keepdims=True)
        hi = jnp.sum(jnp.where(first, 0.0, xg), axis=-1, keepdims=True)
        cols.append(jnp.where(first, lo, hi))
    return jnp.concatenate(cols, axis=1)


def _segsum(x, bd):
    tile = bd.shape[0]
    xb = x.astype(BF16)
    return jnp.concatenate([_dot(xb[:, j * tile:(j + 1) * tile], bd)
                            for j in range(x.shape[1] // tile)], axis=1)


def _interleave(generators):
    live = [g for g in generators if g is not None]
    while live:
        for g in list(live):
            if next(g, StopIteration) is StopIteration:
                live.remove(g)


def _rwkv_kernel(p_ref, w0_ref, wdw_ref, a0_ref, waw_ref, wgw_ref, kk_ref, ka_ref, rk_ref,
                 gng_ref, gnb_ref, o_ref, s_scr, *, width, block_rows):
    t = pl.program_id(1)
    L = RW_CHUNK
    R = block_rows
    n_blocks = p_ref.shape[1] // R
    n_chunks = R // L
    W = width
    G2 = 2 * LANES
    groups = W // LANES
    items = [(c, j) for c in range(n_chunks) for j in range(groups)]
    rsl = lambda c: slice(c * L, (c + 1) * L)
    gsl = lambda j: slice(j * LANES, (j + 1) * LANES)

    @pl.when(t == 0)
    def _():
        s_scr[...] = jnp.zeros_like(s_scr)

    ri = lax.broadcasted_iota(jnp.int32, (MXU_TILE, MXU_TILE), 0)
    ci = lax.broadcasted_iota(jnp.int32, (MXU_TILE, MXU_TILE), 1)
    bd = ((ri // HEAD_DIM) == (ci // HEAD_DIM)).astype(BF16)
    tril = (lax.broadcasted_iota(jnp.int32, (L, L), 1)
            <= lax.broadcasted_iota(jnp.int32, (L, L), 0)).astype(BF16)
    o0 = 3 * W
    lane1 = lax.broadcasted_iota(jnp.int32, (L, LANES), 1)
    lane2 = lax.broadcasted_iota(jnp.int32, (L, G2), 1)
    row2 = lax.broadcasted_iota(jnp.int32, (L, G2), 0)
    first1 = lane1 < HEAD_DIM
    first2 = (lane2 % LANES) < HEAD_DIM
    strict2 = (lane2 % HEAD_DIM) < row2
    incl2 = (lane2 % HEAD_DIM) <= row2
    upper2 = lane2 >= LANES
    eye2 = ((lane1 % HEAD_DIM) == lax.broadcasted_iota(jnp.int32, (L, LANES), 0)).astype(F32)
    zero1 = jnp.zeros((L, LANES), BF16)
    rr = lax.broadcasted_iota(jnp.int32, (LANES, LANES), 0) < HEAD_DIM
    cc = lax.broadcasted_iota(jnp.int32, (LANES, LANES), 1) < HEAD_DIM
    diag_blocks = rr == cc

    def split(m, mask):
        return jnp.concatenate([jnp.where(mask, m, 0), jnp.where(mask, 0, m)], axis=0)

    def prepare(pb, b):
        p = p_ref[0, b * R:(b + 1) * R, :]
        r, k, v = p[:, 0:W], p[:, W:2 * W], p[:, 2 * W:3 * W]
        wd = p[:, o0:o0 + LANES]
        ad = p[:, o0 + LANES:o0 + 2 * LANES]
        gd = p[:, o0 + 2 * LANES:o0 + 4 * LANES]
        z = w0_ref[...] + _dot(_bf(jnp.tanh(wd)), wdw_ref[...])
        softplus = jnp.maximum(-z, 0.0) + jnp.log(1.0 + jnp.exp(-jnp.abs(z)))
        ld = -jnp.exp(-softplus - 0.5)
        yield
        a_sig = _sigmoid(a0_ref[...] + _dot(_bf(ad), waw_ref[...]))
        pb["gate"] = _dot(_bf(_sigmoid(gd)), wgw_ref[...])
        yield
        kk = k * kk_ref[...]
        kk = kk / jnp.maximum(jnp.sqrt(_segsum_lanes(kk * kk)), 1e-12)
        yield
        k2 = k * (1.0 + (a_sig - 1.0) * ka_ref[...])
        b_ = kk * a_sig
        pb["bonus"] = _segsum(r * k2 * rk_ref[...], bd) * v
        yield
        ld_hi = _bf(ld)
        ld_lo = _bf(ld - ld_hi.astype(F32))
        cums = [_dot(tril, ld_hi[rsl(c)]) + _dot(tril, ld_lo[rsl(c)]) for c in range(n_chunks)]
        pb["tots"] = [cum_c[L - 1:L, :] for cum_c in cums]
        cum = jnp.concatenate(cums, axis=0)
        tot_b = jnp.concatenate([jnp.broadcast_to(tc, (L, W)) for tc in pb["tots"]], axis=0)
        yield
        e_neg = jnp.exp(-cum)
        rt = r * jnp.exp(cum)
        pb.update(rt=rt, rt_m=_bf(rt), at=_bf(-kk * jnp.exp(cum - ld)), bt=_bf(b_ * e_neg),
                  kt=_bf(k2 * e_neg))
        yield
        e_rem = jnp.exp(tot_b - cum)
        pb.update(bh=_bf(b_ * e_rem), kh=_bf(k2 * e_rem), vm=_bf(v))

    def chains(pb, cb):
        at, rt_m, bt, kt, vm = pb["at"], pb["rt_m"], pb["bt"], pb["kt"], pb["vm"]
        g_bot, akm, zz = {}, {}, {}
        for c, j in items:
            ar = jnp.concatenate([at[rsl(c), gsl(j)], rt_m[rsl(c), gsl(j)]], axis=0)
            rhs = jnp.concatenate([split(bt[rsl(c), gsl(j)], first1),
                                   split(kt[rsl(c), gsl(j)], first1)], axis=0)
            gm = _dot_nt(ar, rhs)
            top = jnp.where(strict2, gm[:L], 0.0)
            g_bot[c, j] = _bf(jnp.where(incl2, gm[L:], 0.0))
            akm[c, j] = _bf(top[:, LANES:])
            zz[c, j] = jnp.concatenate([top[:, :LANES], eye2], axis=1)
        yield
        for _ in range(int(math.log2(L))):
            for it in items:
                zb = _bf(zz[it])
                out = _dot(zb[:, :LANES], split(zb, first2))
                zz[it] = out + jnp.where(upper2, zz[it], 0.0)
            yield
        vbd = {(c, j): split(vm[rsl(c), gsl(j)], first1) for c, j in items}
        akv = {it: _bf(_dot(akm[it], vbd[it])) for it in items}
        yield
        x = {}
        for c, j in items:
            zed = jnp.concatenate([at[rsl(c), gsl(j)], akv[c, j]], axis=1)
            x[c, j] = _bf(_dot(_bf(zz[c, j][:, LANES:]), split(zed, first2)))
        yield
        cb.update(r_eff={}, y0={}, mt={}, nt={})
        for c, j in items:
            w4 = jnp.concatenate([split(x[c, j], first2),
                                  jnp.concatenate([jnp.concatenate([zero1, zero1], axis=0),
                                                   vbd[c, j]], axis=1)], axis=0)
            out = _dot(g_bot[c, j], w4)
            cb["r_eff"][c, j] = _bf(pb["rt"][rsl(c), gsl(j)] + out[:, :LANES])
            cb["y0"][c, j] = out[:, LANES:]
        yield
        for c, j in items:
            lhs = jnp.concatenate([pb["bh"][rsl(c), gsl(j)], pb["kh"][rsl(c), gsl(j)]], axis=0)
            rhs = jnp.concatenate([x[c, j], jnp.concatenate([zero1, vm[rsl(c), gsl(j)]], axis=1)],
                                  axis=0)
            pt = _dot_tn(lhs, rhs)
            cb["mt"][c, j] = _bf(jnp.where(diag_blocks, pt[:, :LANES], 0.0))
            cb["nt"][c, j] = jnp.where(diag_blocks, pt[:, LANES:], 0.0)
        yield
        cb["e_cols"] = [
            jnp.concatenate([jnp.exp(pb["tots"][c][:, gsl(j)]) for c in range(n_chunks)]
                            + [jnp.zeros((LANES - n_chunks, LANES), F32)], axis=0).T
            for j in range(groups)]

    state = [s_scr[j] for j in range(groups)]

    def tail(pb, cb, b):
        y_rows = []
        for c in range(n_chunks):
            hb = [_bf(s) for s in state]
            y_rows.append(jnp.concatenate(
                [_dot(cb["r_eff"][c, j], hb[j]) + cb["y0"][c, j] for j in range(groups)], axis=1))
            for j in range(groups):
                state[j] = (state[j] * cb["e_cols"][j][:, c:c + 1] + _dot(cb["mt"][c, j], hb[j])
                            + cb["nt"][c, j])
            yield
        y = jnp.concatenate(y_rows, axis=0)
        inv_n = 1.0 / HEAD_DIM
        mean = _segsum_lanes(y) * inv_n
        d = y - mean
        yield
        var = _segsum_lanes(d * d) * inv_n
        yn = d * lax.rsqrt(var + GN_EPS) * gng_ref[...] + gnb_ref[...]
        o_ref[0, b * R:(b + 1) * R, :] = ((yn + pb["bonus"]) * pb["gate"]).astype(o_ref.dtype)

    prep = [dict() for _ in range(n_blocks)]
    chain = [dict() for _ in range(n_blocks)]
    _interleave([prepare(prep[0], 0)])
    for b in range(n_blocks):
        _interleave([chains(prep[b], chain[b]),
                     prepare(prep[b + 1], b + 1) if b + 1 < n_blocks else None,
                     tail(prep[b - 1], chain[b - 1], b - 1) if b > 0 else None])
    _interleave([tail(prep[-1], chain[-1], n_blocks - 1)])
    for j in range(groups):
        s_scr[j] = state[j]


def _rwkv(rw, w0, wdw, a0, waw, wgw, k_k, k_a, r_k, gn_g, gn_b, width, rows=512, block_rows=256):
    B, T, cols = rw.shape
    row = lambda a: a.reshape(1, width)
    vec = _const_spec((1, width))
    return pl.pallas_call(
        functools.partial(_rwkv_kernel, width=width, block_rows=block_rows),
        out_shape=jax.ShapeDtypeStruct((B, T, width), BF16),
        grid=(B, T // rows),
        in_specs=[pl.BlockSpec((1, rows, cols), lambda b, t: (b, t, 0)),
                  vec, _const_spec(wdw.shape), vec, _const_spec(waw.shape),
                  _const_spec(wgw.shape), vec, vec, vec, vec, vec],
        out_specs=pl.BlockSpec((1, rows, width), lambda b, t: (b, t, 0)),
        scratch_shapes=[pltpu.VMEM((width // LANES, LANES, LANES), F32)],
        compiler_params=pltpu.CompilerParams(
            dimension_semantics=("arbitrary", "arbitrary"), vmem_limit_bytes=VMEM_LIMIT),
        name="rwkv",
    )(rw, row(w0), wdw, row(a0), waw, wgw, row(k_k), row(k_a), row(r_k), row(gn_g), row(gn_b))


def _fox_kernel(qtab, ktab, qa_ref, ka_ref, vt_ref, og_ref, g_ref, o_ref,
                m_all, acc_all, s_scr, p_scr, al_scr, mx_scr, *, blk, cb, n_off, hps):
    n_q = qa_ref.shape[2] // blk
    n_el = n_off + n_q
    chains = [(hh, c) for hh in range(hps) for c in range(blk // cb)]
    csl = lambda c: slice(c * cb, (c + 1) * cb)

    m_all[...] = jnp.full_like(m_all, NEG_INF)
    acc_all[...] = jnp.zeros_like(acc_all)

    def element(t):
        if isinstance(t, int) and t >= n_off:
            return t - n_off, t - n_off, True
        return qtab[t], ktab[t], False

    def start(i, size):
        return i * size if isinstance(i, int) else pl.multiple_of(i * size, size)

    def n_keys(c, diag):
        return (c + 1) * cb if diag else blk

    def logits(t, par):
        qi, ki, diag = element(t)
        for hh, c in chains:
            nk = n_keys(c, diag)
            s = _dot_nt(ka_ref[0, hh, pl.ds(start(ki, blk), nk), :],
                        qa_ref[0, hh, pl.ds(start(qi, blk) + c * cb, cb), :])
            if diag:
                key = lax.broadcasted_iota(jnp.int32, (nk, cb), 0)
                qry = c * cb + lax.broadcasted_iota(jnp.int32, (nk, cb), 1)
                s = jnp.where(key <= qry, s, NEG_INF)
            s_scr[par, hh, :nk, csl(c)] = s
            mx_scr[par, hh, :, csl(c)] = jnp.max(s, axis=0, keepdims=True)
            yield

    def softmax(t, par):
        qi, _, diag = element(t)
        for hh, c in chains:
            nk = n_keys(c, diag)
            s = s_scr[par, hh, :nk, csl(c)]
            m_prev = m_all[qi, hh, :, csl(c)]
            m_new = jnp.maximum(m_prev, mx_scr[par, hh, :, csl(c)])
            m_all[qi, hh, :, csl(c)] = m_new
            al_scr[par, hh, :, csl(c)] = jnp.exp2(m_prev - m_new)
            p_scr[par, hh, :nk, csl(c)] = jnp.exp2(s - m_new).astype(BF16)
            yield

    def values(t, par):
        qi, ki, diag = element(t)
        for hh, c in chains:
            nk = n_keys(c, diag)
            pv = _dot(vt_ref[0, hh, ki, :, :nk], p_scr[par, hh, :nk, csl(c)])
            acc_all[qi, hh, :, csl(c)] = (al_scr[par, hh, :, csl(c)] * acc_all[qi, hh, :, csl(c)]
                                          + pv)
            yield
        if diag:
            rows = slice(qi * blk, (qi + 1) * blk)
            for pr in range(hps // 2):
                normed = []
                for hh in (2 * pr, 2 * pr + 1):
                    acc = acc_all[qi, hh]
                    o = acc[:HEAD_DIM] / acc[HEAD_DIM:HEAD_DIM + 1]
                    ms = jnp.mean(o * o, axis=0, keepdims=True)
                    normed.append(o * lax.rsqrt(ms + NORM_EPS))
                on = jnp.concatenate(normed, axis=0).T
                ls = slice(pr * LANES, (pr + 1) * LANES)
                o_ref[0, rows, ls] = (on * g_ref[:, ls]
                                      * _sigmoid(og_ref[0, rows, ls].astype(F32))).astype(o_ref.dtype)

    def step(t, par):
        _interleave([stage(t - lag, (par + lag) % 2)
                     for lag, stage in enumerate((logits, softmax, values))
                     if not isinstance(t, int) or 0 <= t - lag < n_el])

    def two_steps(u, carry):
        step(2 * u + 2, 0)
        step(2 * u + 3, 1)
        return carry

    step(0, 0)
    step(1, 1)
    lax.fori_loop(0, (n_off - 2) // 2, two_steps, 0)
    for t in range(n_off, n_el + 2):
        step(t, t % 2)


def _fox(qa, ka, vt, og, fox_out_g, width, cb=256, hps=4):
    B, H, T, _ = qa.shape
    blk = vt.shape[4]
    n_q = T // blk
    pairs = [(qi, ki) for qi in range(n_q) for ki in range(qi)]
    n_off = len(pairs)
    assert n_off >= 2 and n_off % 2 == 0, "pipeline loop is unrolled by two"
    qtab = jnp.asarray(np.array([p[0] for p in pairs], np.int32))
    ktab = jnp.asarray(np.array([p[1] for p in pairs], np.int32))
    grid_spec = pltpu.PrefetchScalarGridSpec(
        num_scalar_prefetch=2,
        grid=(B, H // hps),
        in_specs=[pl.BlockSpec((1, hps, T, LANES), lambda b, h, qt, kt: (b, h, 0, 0)),
                  pl.BlockSpec((1, hps, T, LANES), lambda b, h, qt, kt: (b, h, 0, 0)),
                  pl.BlockSpec((1, hps, n_q, V_ROWS, blk), lambda b, h, qt, kt: (b, h, 0, 0, 0)),
                  pl.BlockSpec((1, T, hps * HEAD_DIM), lambda b, h, qt, kt: (b, 0, h)),
                  pl.BlockSpec((1, hps * HEAD_DIM), lambda b, h, qt, kt: (0, h))],
        out_specs=pl.BlockSpec((1, T, hps * HEAD_DIM), lambda b, h, qt, kt: (b, 0, h)),
        scratch_shapes=[pltpu.VMEM((n_q, hps, 1, blk), F32),
                        pltpu.VMEM((n_q, hps, V_ROWS, blk), F32),
                        pltpu.VMEM((2, hps, blk, blk), F32), pltpu.VMEM((2, hps, blk, blk), BF16),
                        pltpu.VMEM((2, hps, 1, blk), F32), pltpu.VMEM((2, hps, 1, blk), F32)])
    return pl.pallas_call(
        functools.partial(_fox_kernel, blk=blk, cb=cb, n_off=n_off, hps=hps),
        out_shape=jax.ShapeDtypeStruct((B, T, width), BF16),
        grid_spec=grid_spec,
        compiler_params=pltpu.CompilerParams(
            dimension_semantics=("arbitrary",) * 2, vmem_limit_bytes=VMEM_LIMIT),
        name="fox",
    )(qtab, ktab, qa, ka, vt, og, fox_out_g.reshape(1, width))


def _outmlp_kernel(x_ref, yrw_ref, yfx_ref, mod_ref, wo_ref, g2_ref, w1_ref, w2_ref, gf_ref,
                   o_ref, wo_scr, w1_scr, w2_scr, *, n_wo, n_ff):
    s = pl.program_id(0)
    n_cast = n_wo + 2 * n_ff

    @pl.when(s < n_wo)
    def _():
        wo_scr[s] = wo_ref[...].astype(BF16)

    @pl.when((s >= n_wo) & (s < n_wo + n_ff))
    def _():
        w1_scr[s - n_wo] = w1_ref[...].astype(BF16)

    @pl.when((s >= n_wo + n_ff) & (s < n_cast))
    def _():
        w2_scr[s - n_wo - n_ff] = w2_ref[...].astype(BF16)

    @pl.when(s >= n_cast)
    def _():
        x = x_ref[0]
        gate1 = mod_ref[0, 2:3, :]
        shift2 = mod_ref[0, 3:4, :]
        scale2 = mod_ref[0, 4:5, :]
        gate2 = mod_ref[0, 5:6, :]
        y = _dot(yrw_ref[0], wo_scr[0]) + _dot(yfx_ref[0], wo_scr[1])
        h1 = x + gate1 * y
        ms = jnp.mean(h1 * h1, axis=-1, keepdims=True)
        u = ((h1 * lax.rsqrt(ms + NORM_EPS) * g2_ref[...]) * (1.0 + scale2) + shift2).astype(BF16)
        acc = jnp.zeros_like(x)
        for j in range(n_ff):
            hid = jnp.maximum(_dot(u, w1_scr[j]), 0.0)
            acc = acc + _dot((hid * hid).astype(BF16), w2_scr[j])
        h2 = h1 + gate2 * acc
        ms2 = jnp.mean(h2 * h2, axis=-1, keepdims=True)
        o_ref[0] = h2 * lax.rsqrt(ms2 + NORM_EPS) * gf_ref[...]


def _outmlp(x, y_rw, y_fx, mod3, w_o, norm2_g, w1, w2, final_g, tm=512, ff_tile=512):
    B, T, D = x.shape
    half = y_rw.shape[2]
    n_t = T // tm
    n_wo = w_o.shape[0] // half
    n_ff = w1.shape[1] // ff_tile
    n_cast = n_wo + 2 * n_ff
    assert w_o.shape[0] == n_wo * half and n_wo == 2

    def tile(s):
        r = jnp.maximum(s - n_cast, 0)
        return r // n_t, r % n_t

    rows = lambda s: (*tile(s), 0)
    return pl.pallas_call(
        functools.partial(_outmlp_kernel, n_wo=n_wo, n_ff=n_ff),
        out_shape=jax.ShapeDtypeStruct((B, T, D), F32),
        grid=(n_cast + B * n_t,),
        in_specs=[pl.BlockSpec((1, tm, D), rows),
                  pl.BlockSpec((1, tm, half), rows),
                  pl.BlockSpec((1, tm, half), rows),
                  pl.BlockSpec((1, N_MOD, D), lambda s: (tile(s)[0], 0, 0)),
                  pl.BlockSpec((half, D), lambda s: (jnp.clip(s, 0, n_wo - 1), 0)),
                  _const_spec((1, D)),
                  pl.BlockSpec((D, ff_tile), lambda s: (0, jnp.clip(s - n_wo, 0, n_ff - 1))),
                  pl.BlockSpec((ff_tile, D), lambda s: (jnp.clip(s - n_wo - n_ff, 0, n_ff - 1), 0)),
                  _const_spec((1, D))],
        out_specs=pl.BlockSpec((1, tm, D), rows),
        scratch_shapes=[pltpu.VMEM((n_wo, half, D), BF16), pltpu.VMEM((n_ff, D, ff_tile), BF16),
                        pltpu.VMEM((n_ff, ff_tile, D), BF16)],
        compiler_params=pltpu.CompilerParams(
            dimension_semantics=("arbitrary",), vmem_limit_bytes=VMEM_LIMIT),
        name="outmlp",
    )(x, y_rw, y_fx, mod3, w_o, norm2_g.reshape(1, D), w1, w2, final_g.reshape(1, D))


def _pad_cols(a, n):
    return jnp.pad(a, ((0, 0), (0, n - a.shape[1])))


def _pad_rows(a, n):
    return jnp.pad(a, ((0, n - a.shape[0]), (0, 0)))


def _branches(x, c, w_ada, b_ada, norm1_g, w_in, mu_shift, w0, w_up_decay, a0, w_up_a, w_up_g,
              k_k, k_a, r_k, gn_g, gn_b, b_f, q_norm_g, k_norm_g, fox_out_g):
    B, T, D = x.shape
    W = w0.shape[0]
    heads = b_f.shape[0]
    rw_n = 3 * W + DECAY_LORA + AAA_LORA + GATE_LORA

    def regroup(a):
        rw, fx = a[:, :rw_n], a[:, rw_n:]
        o = 3 * W
        return jnp.concatenate([
            rw[:, :o],
            _pad_cols(rw[:, o:o + DECAY_LORA], LANES),
            _pad_cols(rw[:, o + DECAY_LORA:o + DECAY_LORA + AAA_LORA], LANES),
            _pad_cols(jnp.concatenate([rw[:, o + DECAY_LORA + AAA_LORA:], fx[:, 4 * W:]], axis=1),
                      2 * LANES),
            fx[:, :4 * W]], axis=1)

    rw_cols = 3 * W + 4 * LANES
    w_all = regroup(w_in.astype(BF16))
    mu_p = regroup(jnp.pad(mu_shift.reshape(1, rw_n), ((0, 0), (0, w_in.shape[1] - rw_n))))[:, :rw_cols]
    f_off = GATE_LORA - LANES
    bf_p = jnp.pad(b_f.reshape(1, heads), ((0, 0), (f_off, LANES - f_off - heads)))
    wdw = _pad_rows(w_up_decay, LANES).astype(BF16)
    waw = _pad_rows(w_up_a, LANES).astype(BF16)
    wgw = _pad_rows(w_up_g, 2 * LANES).astype(BF16)

    mod3 = _ada(c, w_ada, b_ada).reshape(B, N_MOD, D)
    rw, qa, ka, vt, og = _inproj(x, mod3, norm1_g, w_all, mu_p, bf_p, q_norm_g, k_norm_g,
                                 rw_cols, f_off)
    y_rw = _rwkv(rw, w0, wdw, a0, waw, wgw, k_k, k_a, r_k.reshape(-1), gn_g, gn_b, W)
    y_fx = _fox(qa, ka, vt, og, fox_out_g, W)
    return y_rw, y_fx, mod3


def kernel(x, c, w_ada, b_ada, norm1_g, w_in, mu_shift, w0, w_up_decay, a0, w_up_a, w_up_g,
           k_k, k_a, r_k, gn_g, gn_b, b_f, q_norm_g, k_norm_g, fox_out_g, w_o, norm2_g,
           w_mlp1, w_mlp2, final_g):
    y_rw, y_fx, mod3 = _branches(x, c, w_ada, b_ada, norm1_g, w_in, mu_shift, w0, w_up_decay,
                                 a0, w_up_a, w_up_g, k_k, k_a, r_k, gn_g, gn_b, b_f,
                                 q_norm_g, k_norm_g, fox_out_g)
    return _outmlp(x, y_rw, y_fx, mod3, w_o, norm2_g, w_mlp1, w_mlp2, final_g)
```
